```python
import math
import jax, jax.numpy as jnp
from jax import lax
import numpy as np

D_MODEL = 1024
BATCH = 8
SEQ = 2048
DEPTH = 2
DEC_BATCH = 128
DEC_SEQ = 4
PAST_LEN = 16384
PAGE_SIZE = 128

MIX_WIDTH = D_MODEL
N_MIXERS = 4
GROUP_WIDTH = MIX_WIDTH // N_MIXERS

SSM_WIDTH = GROUP_WIDTH
SSM_CH = 16
SSM_GROUPS = SSM_WIDTH // SSM_CH
SSM_STATE = 64

HGRN_WIDTH = GROUP_WIDTH
HGRN_HEAD = 64
HGRN_HEADS = HGRN_WIDTH // HGRN_HEAD
HGRN_CHUNK = 32
HGRN_NORM_EPS = 1e-5

RWKV_WIDTH = GROUP_WIDTH
RWKV_HEAD = 64
RWKV_HEADS = RWKV_WIDTH // RWKV_HEAD
DECAY_LORA = 64
AAA_LORA = 64
GATE_LORA = 128
RWKV_PROJ = 3 * RWKV_WIDTH + DECAY_LORA + AAA_LORA + GATE_LORA
RWKV_GN_EPS = 64e-5

POOL_WIDTH = MIX_WIDTH - SSM_WIDTH - HGRN_WIDTH - RWKV_WIDTH
POOL_WINDOWS = (2, 4, 8, 16)
POOL_CH = POOL_WIDTH // 4
POOL_BUF = 16 - 1

D_FF = 4 * D_MODEL
NORM_EPS = 1e-6
PROJ_WIDTH = SSM_WIDTH + 4 * HGRN_WIDTH + RWKV_PROJ + POOL_WIDTH
PROJ_SPLITS = (SSM_WIDTH, SSM_WIDTH + HGRN_WIDTH, SSM_WIDTH + 2 * HGRN_WIDTH, SSM_WIDTH + 3 * HGRN_WIDTH,
               SSM_WIDTH + 4 * HGRN_WIDTH, SSM_WIDTH + 4 * HGRN_WIDTH + RWKV_PROJ)
RWKV_SPLITS = (RWKV_WIDTH, 2 * RWKV_WIDTH, 3 * RWKV_WIDTH, 3 * RWKV_WIDTH + DECAY_LORA,
               3 * RWKV_WIDTH + DECAY_LORA + AAA_LORA)

kernel_name = 'hymba_s5_hgrn2_rwkv7_pool_step'


def _f32(a):
    return a.astype(jnp.float32)


def rmsnorm(x, g, eps=NORM_EPS):
    xf = _f32(x)
    y = xf * lax.rsqrt(jnp.mean(xf * xf, axis=-1, keepdims=True) + eps)
    return (y * _f32(g)).astype(x.dtype)


def s5_mixer(u, h0_re, h0_im, lam_re, lam_im, log_dt, b_re, b_im, c_re, c_im, d_skip, glu_w, glu_b):
    bsz, t_len, _ = u.shape
    uf = _f32(u)
    ug = uf.reshape(bsz, t_len, SSM_GROUPS, SSM_CH)
    lr, li = _f32(lam_re), _f32(lam_im)
    dt = jnp.exp(_f32(log_dt))[:, None]
    mag = jnp.exp(lr * dt)
    ab_re, ab_im = mag * jnp.cos(li * dt), mag * jnp.sin(li * dt)
    den = lr * lr + li * li
    zr, zi = ab_re - 1.0, ab_im
    cr = (zr * lr + zi * li) / den
    ci = (zi * lr - zr * li) / den
    br, bi = _f32(b_re), _f32(b_im)
    bb_re = cr[..., None] * br - ci[..., None] * bi
    bb_im = cr[..., None] * bi + ci[..., None] * br
    bu_re = jnp.einsum('gph,btgh->btgp', bb_re, ug)
    bu_im = jnp.einsum('gph,btgh->btgp', bb_im, ug)
    a_re = jnp.broadcast_to(ab_re, bu_re.shape)
    a_im = jnp.broadcast_to(ab_im, bu_im.shape)

    def combine(e1, e2):
        a1r, a1i, b1r, b1i = e1
        a2r, a2i, b2r, b2i = e2
        return (a2r * a1r - a2i * a1i, a2r * a1i + a2i * a1r,
                a2r * b1r - a2i * b1i + b2r, a2r * b1i + a2i * b1r + b2i)

    _, _, hr, hi = lax.associative_scan(combine, (a_re, a_im, bu_re, bu_im), axis=1)
    steps = jnp.arange(1, t_len + 1, dtype=jnp.float32)[:, None, None]
    pmag = jnp.exp(steps * (lr * dt))
    pw_re = pmag * jnp.cos(steps * (li * dt))
    pw_im = pmag * jnp.sin(steps * (li * dt))
    g_re = _f32(h0_re)[:, None]
    g_im = _f32(h0_im)[:, None]
    hr = hr + pw_re * g_re - pw_im * g_im
    hi = hi + pw_re * g_im + pw_im * g_re
    y = jnp.einsum('ghp,btgp->btgh', _f32(c_re), hr) - jnp.einsum('ghp,btgp->btgh', _f32(c_im), hi)
    y = y.reshape(bsz, t_len, SSM_WIDTH) + _f32(d_skip) * uf
    z = jax.nn.gelu(y)
    out = z * jax.nn.sigmoid(z @ _f32(glu_w) + _f32(glu_b))
    return out, hr[:, -1], hi[:, -1]


def gla_chunked(q, k, v, log_f, s0):
    bsz, t_len, n_h, _ = q.shape
    blk = min(HGRN_CHUNK, t_len)
    n_blk = -(-t_len // blk)
    pad = n_blk * blk - t_len

    def prep(a):
        a = jnp.pad(a, ((0, 0), (0, pad), (0, 0), (0, 0)))
        return a.reshape(bsz, n_blk, blk, n_h, a.shape[-1]).transpose(1, 0, 3, 2, 4)

    qc, kc, vc, gc = prep(q), prep(k), prep(v), prep(log_f)
    causal = jnp.tril(jnp.ones((blk, blk), dtype=bool))[:, :, None]

    def step(S, inp):
        qb, kb, vb, gb = inp
        b = jnp.cumsum(gb, axis=2)
        diff = jnp.where(causal, b[:, :, :, None, :] - b[:, :, None, :, :], -jnp.inf)
        att = jnp.einsum('bhtk,bhsk,bhtsk->bhts', qb, kb, jnp.exp(diff))
        o = jnp.einsum('bhts,bhsv->bhtv', att, vb) + jnp.einsum('bhtk,bhkv->bhtv', qb * jnp.exp(b), S)
        b_last = b[:, :, -1:, :]
        S = jnp.exp(b_last[:, :, 0, :])[..., None] * S + jnp.einsum('bhsk,bhsv->bhkv', kb * jnp.exp(b_last - b), vb)
        return S, o

    s_last, o = lax.scan(step, s0, (qc, kc, vc, gc))
    o = o.transpose(1, 0, 3, 2, 4).reshape(bsz, n_blk * blk, n_h, -1)[:, :t_len]
    return o, s_last


def hgrn2_mixer(p_q, p_f, p_i, p_g, s0, lb, norm_g):
    bsz, t_len, _ = p_q.shape
    hs = (bsz, t_len, HGRN_HEADS, HGRN_HEAD)
    zf = _f32(p_f)
    log_f = jnp.logaddexp(jnp.log1p(-lb) + jax.nn.log_sigmoid(zf), jnp.log(lb))
    k = (1.0 - lb) * jax.nn.sigmoid(-zf)
    q = jax.nn.silu(_f32(p_q))
    o, s_last = gla_chunked(q.reshape(hs), k.reshape(hs), _f32(p_i).reshape(hs), log_f.reshape(hs), _f32(s0))
    o = o * lax.rsqrt(jnp.mean(o * o, axis=-1, keepdims=True) + HGRN_NORM_EPS)
    o = o.reshape(bsz, t_len, HGRN_WIDTH) * _f32(norm_g) * jax.nn.silu(_f32(p_g))
    return o, s_last


def rwkv7_mixer(p, shift_buf, s0, mu, w0, w2, a0, a2, g2, k_k, k_a, r_k, ln_g, ln_b):
    bsz, t_len, _ = p.shape
    pf = _f32(p)
    prev = jnp.concatenate([_f32(shift_buf), pf[:, :-1]], axis=1)
    xs = pf + (prev - pf) * _f32(mu)
    xr, xk, xv, xw, xa, xg = jnp.split(xs, RWKV_SPLITS, axis=-1)
    w = -jax.nn.softplus(-(_f32(w0) + jnp.tanh(xw) @ _f32(w2))) - 0.5
    decay = jnp.exp(-jnp.exp(w))
    a = jax.nn.sigmoid(_f32(a0) + xa @ _f32(a2))
    g = jax.nn.sigmoid(xg) @ _f32(g2)
    hs = (bsz, t_len, RWKV_HEADS, RWKV_HEAD)
    kk = (xk * _f32(k_k)).reshape(hs)
    kk = kk / jnp.maximum(jnp.sqrt(jnp.sum(kk * kk, axis=-1, keepdims=True)), 1e-12)
    k = (xk * (1.0 + (a - 1.0) * _f32(k_a))).reshape(hs)
    r = xr.reshape(hs)
    v = xv.reshape(hs)

    def step(S, inp):
        r_t, w_t, k_t, v_t, kk_t, a_t = inp
        sa = jnp.einsum('bhvk,bhk->bhv', S, -kk_t)
        S = S * w_t[:, :, None, :] + sa[..., None] * (kk_t * a_t)[:, :, None, :] + v_t[..., None] * k_t[:, :, None, :]
        return S, jnp.einsum('bhvk,bhk->bhv', S, r_t)

    seq = tuple(jnp.moveaxis(z, 1, 0) for z in (r, decay.reshape(hs), k, v, kk, a.reshape(hs)))
    s_last, y = lax.scan(step, _f32(s0), seq)
    y = jnp.moveaxis(y, 0, 1)
    mean = jnp.mean(y, axis=-1, keepdims=True)
    var = jnp.mean(jnp.square(y - mean), axis=-1, keepdims=True)
    y = ((y - mean) * lax.rsqrt(var + RWKV_GN_EPS)).reshape(bsz, t_len, RWKV_WIDTH) * _f32(ln_g) + _f32(ln_b)
    bonus = jnp.sum(r * k * _f32(r_k).reshape(RWKV_HEADS, RWKV_HEAD), axis=-1, keepdims=True) * v
    y = (y + bonus.reshape(bsz, t_len, RWKV_WIDTH)) * g
    return y, s_last, pf[:, -1:]


def pool_mixer(u, buf, pos0, w_pool, scale):
    bsz, t_len, _ = u.shape
    ext = jnp.concatenate([_f32(buf), _f32(u)], axis=1)
    cs = jnp.pad(jnp.cumsum(ext, axis=1), ((0, 0), (1, 0), (0, 0)))
    end = cs[:, POOL_BUF + 1:]
    pos = pos0 + jnp.arange(t_len)
    means = []
    for gi, win in enumerate(POOL_WINDOWS):
        sl = slice(gi * POOL_CH, (gi + 1) * POOL_CH)
        s = end[..., sl] - cs[:, POOL_BUF + 1 - win: POOL_BUF + 1 - win + t_len, sl]
        cnt = jnp.minimum(pos + 1, win).astype(jnp.float32)[None, :, None]
        means.append(s / cnt)
    pooled = jnp.concatenate(means, axis=-1) - ext[:, POOL_BUF:]
    y = jnp.einsum('btgc,gcd->btgd', pooled.reshape(bsz, t_len, len(POOL_WINDOWS), POOL_CH), _f32(w_pool))
    y = y.reshape(bsz, t_len, POOL_WIDTH) * _f32(scale)
    return y, ext[:, -POOL_BUF:]


def sqrelu_mlp(x, w_up, w_down):
    return jnp.square(jax.nn.relu(x @ w_up)) @ w_down


def trunk(x, ssm_re0, ssm_im0, hgrn0, wkv0, shift0, pool0, pos0, P):
    lb_all = jnp.cumsum(jax.nn.softmax(_f32(P['hgrn_lb_logits']), axis=0), axis=0)
    lb_all = lb_all - lb_all[:1]
    h = x
    n_re, n_im, n_hg, n_wkv, n_sh, n_pool = [], [], [], [], [], []
    for l in range(DEPTH):
        xn = rmsnorm(h, P['norm1_g'][l])
        proj = xn @ P['w_in'][l]
        p_ssm, p_q, p_f, p_i, p_g, p_rwkv, p_pool = jnp.split(proj, PROJ_SPLITS, axis=-1)
        y_a, s_re, s_im = s5_mixer(p_ssm, ssm_re0[l], ssm_im0[l], P['ssm_lambda_re'][l], P['ssm_lambda_im'][l],
                                   P['ssm_log_dt'][l], P['ssm_b_re'][l], P['ssm_b_im'][l], P['ssm_c_re'][l],
                                   P['ssm_c_im'][l], P['ssm_d'][l], P['ssm_glu_w'][l], P['ssm_glu_b'][l])
        y_b, s_hg = hgrn2_mixer(p_q, p_f, p_i, p_g, hgrn0[l], lb_all[l], P['hgrn_norm_g'][l])
        y_c, s_wkv, s_sh = rwkv7_mixer(p_rwkv, shift0[l], wkv0[l], P['rwkv_mu'][l], P['rwkv_w0'][l], P['rwkv_w2'][l],
                                       P['rwkv_a0'][l], P['rwkv_a2'][l], P['rwkv_g2'][l], P['rwkv_k_k'][l],
                                       P['rwkv_k_a'][l], P['rwkv_r_k'][l], P['rwkv_ln_g'][l], P['rwkv_ln_b'][l])
        y_d, s_pool = pool_mixer(p_pool, pool0[l], pos0, P['pool_w'][l], P['pool_scale'][l])
        mixed = jnp.concatenate([y_a, y_b, y_c, y_d], axis=-1).astype(h.dtype)
        h = h + mixed @ P['w_out'][l]
        h = h + sqrelu_mlp(rmsnorm(h, P['norm2_g'][l]), P['mlp_up'][l], P['mlp_down'][l])
        n_re.append(s_re)
        n_im.append(s_im)
        n_hg.append(s_hg)
        n_wkv.append(s_wkv)
        n_sh.append(s_sh)
        n_pool.append(s_pool)
    y = rmsnorm(h, P['norm_f_g'])
    return y, (jnp.stack(n_re), jnp.stack(n_im), jnp.stack(n_hg), jnp.stack(n_wkv), jnp.stack(n_sh), jnp.stack(n_pool))


def setup_inputs(seed: int = 0) -> dict:
    key = jax.random.key(seed)
    keys = list(jax.random.split(key, 48))

    def nrm(shape, std):
        return std * jax.random.normal(keys.pop(), shape, jnp.float32)

    L = DEPTH
    d = {}
    d['x_prompt'] = nrm((BATCH, SEQ, D_MODEL), 1.0)
    d['x_sample'] = nrm((DEC_BATCH, DEC_SEQ, D_MODEL), 1.0)
    d['state_ssm_re'] = nrm((L, DEC_BATCH, SSM_GROUPS, SSM_STATE), 0.5)
    d['state_ssm_im'] = nrm((L, DEC_BATCH, SSM_GROUPS, SSM_STATE), 0.5)
    d['state_hgrn'] = nrm((L, DEC_BATCH, HGRN_HEADS, HGRN_HEAD, HGRN_HEAD), 0.5)
    d['state_wkv'] = nrm((L, DEC_BATCH, RWKV_HEADS, RWKV_HEAD, RWKV_HEAD), 0.3)
    d['state_shift'] = nrm((L, DEC_BATCH, 1, RWKV_PROJ), 1.0)
    d['state_pool'] = nrm((L, DEC_BATCH, POOL_BUF, POOL_WIDTH), 1.0)
    d['norm1_g'] = 1.0 + nrm((L, D_MODEL), 0.02)
    d['w_in'] = nrm((L, D_MODEL, PROJ_WIDTH), D_MODEL ** -0.5)
    n_idx = jnp.arange(SSM_STATE, dtype=jnp.float32)
    d['ssm_lambda_re'] = -0.5 + nrm((L, SSM_GROUPS, SSM_STATE), 0.01)
    d['ssm_lambda_im'] = jnp.pi * n_idx + nrm((L, SSM_GROUPS, SSM_STATE), 0.01)
    d['ssm_log_dt'] = jax.random.uniform(keys.pop(), (L, SSM_GROUPS), jnp.float32,
                                         minval=math.log(1e-3), maxval=math.log(1e-1))
    d['ssm_b_re'] = nrm((L, SSM_GROUPS, SSM_STATE, SSM_CH), (2 * SSM_CH) ** -0.5)
    d['ssm_b_im'] = nrm((L, SSM_GROUPS, SSM_STATE, SSM_CH), (2 * SSM_CH) ** -0.5)
    d['ssm_c_re'] = nrm((L, SSM_GROUPS, SSM_CH, SSM_STATE), 1.0)
    d['ssm_c_im'] = nrm((L, SSM_GROUPS, SSM_CH, SSM_STATE), 1.0)
    d['ssm_d'] = nrm((L, SSM_WIDTH), 1.0)
    d['ssm_glu_w'] = nrm((L, SSM_WIDTH, SSM_WIDTH), SSM_WIDTH ** -0.5)
    d['ssm_glu_b'] = nrm((L, SSM_WIDTH), 0.01)
    d['hgrn_lb_logits'] = nrm((L, HGRN_WIDTH), 0.1)
    d['hgrn_norm_g'] = 1.0 + nrm((L, HGRN_WIDTH), 0.02)
    d['rwkv_mu'] = jax.random.uniform(keys.pop(), (L, RWKV_PROJ), jnp.float32)
    ratio = jnp.arange(RWKV_WIDTH, dtype=jnp.float32) / (RWKV_WIDTH - 1)
    d['rwkv_w0'] = (-7.0 + 5.0 * ratio ** 0.85 + 0.5) + nrm((L, RWKV_WIDTH), 0.1)
    d['rwkv_w2'] = nrm((L, DECAY_LORA, RWKV_WIDTH), 0.1 * DECAY_LORA ** -0.5)
    d['rwkv_a0'] = nrm((L, RWKV_WIDTH), 0.1)
    d['rwkv_a2'] = nrm((L, AAA_LORA, RWKV_WIDTH), AAA_LORA ** -0.5)
    d['rwkv_g2'] = nrm((L, GATE_LORA, RWKV_WIDTH), GATE_LORA ** -0.5)
    d['rwkv_k_k'] = 0.85 + nrm((L, RWKV_WIDTH), 0.02)
    d['rwkv_k_a'] = 1.0 + nrm((L, RWKV_WIDTH), 0.02)
    d['rwkv_r_k'] = nrm((L, RWKV_WIDTH), 0.1)
    d['rwkv_ln_g'] = 1.0 + nrm((L, RWKV_WIDTH), 0.02)
    d['rwkv_ln_b'] = nrm((L, RWKV_WIDTH), 0.01)
    d['pool_w'] = nrm((L, len(POOL_WINDOWS), POOL_CH, POOL_CH), POOL_CH ** -0.5)
    d['pool_scale'] = 1.0 + nrm((L, POOL_WIDTH), 0.1)
    d['w_out'] = nrm((L, MIX_WIDTH, D_MODEL), MIX_WIDTH ** -0.5)
    d['norm2_g'] = 1.0 + nrm((L, D_MODEL), 0.02)
    d['mlp_up'] = nrm((L, D_MODEL, D_FF), D_MODEL ** -0.5)
    d['mlp_down'] = nrm((L, D_FF, D_MODEL), D_FF ** -0.5)
    d['norm_f_g'] = 1.0 + nrm((D_MODEL,), 0.02)
    return d


def reference(x_prompt, x_sample, state_ssm_re, state_ssm_im, state_hgrn, state_wkv, state_shift, state_pool,
              norm1_g, w_in, ssm_lambda_re, ssm_lambda_im, ssm_log_dt, ssm_b_re, ssm_b_im, ssm_c_re, ssm_c_im,
              ssm_d, ssm_glu_w, ssm_glu_b, hgrn_lb_logits, hgrn_norm_g, rwkv_mu, rwkv_w0, rwkv_w2, rwkv_a0,
              rwkv_a2, rwkv_g2, rwkv_k_k, rwkv_k_a, rwkv_r_k, rwkv_ln_g, rwkv_ln_b, pool_w, pool_scale, w_out,
              norm2_g, mlp_up, mlp_down, norm_f_g):
    P = dict(norm1_g=norm1_g, w_in=w_in, ssm_lambda_re=ssm_lambda_re, ssm_lambda_im=ssm_lambda_im,
             ssm_log_dt=ssm_log_dt, ssm_b_re=ssm_b_re, ssm_b_im=ssm_b_im, ssm_c_re=ssm_c_re, ssm_c_im=ssm_c_im,
             ssm_d=ssm_d, ssm_glu_w=ssm_glu_w, ssm_glu_b=ssm_glu_b, hgrn_lb_logits=hgrn_lb_logits,
             hgrn_norm_g=hgrn_norm_g, rwkv_mu=rwkv_mu, rwkv_w0=rwkv_w0, rwkv_w2=rwkv_w2, rwkv_a0=rwkv_a0,
             rwkv_a2=rwkv_a2, rwkv_g2=rwkv_g2, rwkv_k_k=rwkv_k_k, rwkv_k_a=rwkv_k_a, rwkv_r_k=rwkv_r_k,
             rwkv_ln_g=rwkv_ln_g, rwkv_ln_b=rwkv_ln_b, pool_w=pool_w, pool_scale=pool_scale, w_out=w_out,
             norm2_g=norm2_g, mlp_up=mlp_up, mlp_down=mlp_down, norm_f_g=norm_f_g)
    bp = x_prompt.shape[0]
    f32 = jnp.float32
    y_prompt, st_p = trunk(x_prompt,
                           jnp.zeros((DEPTH, bp, SSM_GROUPS, SSM_STATE), f32),
                           jnp.zeros((DEPTH, bp, SSM_GROUPS, SSM_STATE), f32),
                           jnp.zeros((DEPTH, bp, HGRN_HEADS, HGRN_HEAD, HGRN_HEAD), f32),
                           jnp.zeros((DEPTH, bp, RWKV_HEADS, RWKV_HEAD, RWKV_HEAD), f32),
                           jnp.zeros((DEPTH, bp, 1, RWKV_PROJ), f32),
                           jnp.zeros((DEPTH, bp, POOL_BUF, POOL_WIDTH), f32),
                           0, P)
    y_sample, st_s = trunk(x_sample, state_ssm_re, state_ssm_im, state_hgrn, state_wkv, state_shift, state_pool,
                           PAST_LEN, P)
    p_ssm_re, p_ssm_im, p_hgrn, p_wkv, p_shift, p_pool = st_p
    s_ssm_re, s_ssm_im, s_hgrn, s_wkv, s_shift, s_pool = st_s
    return (y_prompt, y_sample, p_ssm_re, p_ssm_im, p_hgrn, p_wkv, p_shift, p_pool,
            s_ssm_re, s_ssm_im, s_hgrn, s_wkv, s_shift, s_pool)
```

```python
import functools

import jax
import jax.numpy as jnp
from jax import lax
from jax.experimental import pallas as pl
from jax.experimental.pallas import tpu as pltpu

F32 = jnp.float32
BF16 = jnp.bfloat16

D_MODEL = 1024
DEPTH = 2
PAST_LEN = 16384
GROUP_WIDTH = 256
HEAD = 64
SSM_GROUPS = 16
SSM_CH = 16
SSM_STATE = 64
SSM_FLAT = SSM_GROUPS * SSM_STATE
POOL_WINDOWS = (2, 4, 8, 16)
POOL_BUF = 15
DECAY_LORA = 64
AAA_LORA = 64
GATE_LORA = 128
RWKV_PROJ = 1024
PROJ_WIDTH = 2560
D_FF = 4096
NORM_EPS = 1e-6
HGRN_NORM_EPS = 1e-5
RWKV_GN_EPS = 64e-5

SEQ_BLK = 8
LANES = 128
VMEM_LIMIT = 48 * 1024 * 1024

COL_SSM, COL_Q, COL_F, COL_I, COL_G, COL_R, COL_K, COL_V, COL_LORA, COL_POOL = range(10)


def _params(sem):
    return pltpu.CompilerParams(dimension_semantics=sem, vmem_limit_bytes=VMEM_LIMIT)


def _dot(a, b):
    return jnp.dot(a, b, preferred_element_type=F32)


def _rms(x, g):
    return x * lax.rsqrt(jnp.mean(x * x, axis=-1, keepdims=True) + NORM_EPS) * g


def _rms_proj_kernel(x_ref, g_ref, w_ref, o_ref):
    o_ref[...] = _dot(_rms(x_ref[...], g_ref[...]).astype(BF16), w_ref[...])


def rms_proj(x, g, w):
    n = x.shape[0]
    tm = min(512, n)
    return pl.pallas_call(
        _rms_proj_kernel,
        grid=(n // tm,),
        in_specs=[pl.BlockSpec((tm, D_MODEL), lambda i: (i, 0)),
                  pl.BlockSpec((1, D_MODEL), lambda i: (0, 0)),
                  pl.BlockSpec((D_MODEL, PROJ_WIDTH), lambda i: (0, 0))],
        out_specs=pl.BlockSpec((tm, PROJ_WIDTH), lambda i: (i, 0)),
        out_shape=jax.ShapeDtypeStruct((n, PROJ_WIDTH), F32),
        compiler_params=_params(("parallel",)),
        name="rms_proj",
    )(x, g, w)


def _row_spec(tc, col, nchunks):
    return pl.BlockSpec((tc * SEQ_BLK, GROUP_WIDTH), lambda s, c: (s * nchunks + c, col))


def _full_spec(shape):
    nd = len(shape)
    return pl.BlockSpec(shape, lambda s, c: (0,) * nd)


def _seq_spec(shape):
    nd = len(shape)
    return pl.BlockSpec((SEQ_BLK,) + shape[1:], lambda s, c: (s,) + (0,) * (nd - 1))


def _head_sums(x):
    lane = lax.broadcasted_iota(jnp.int32, x.shape, 1)
    out = jnp.zeros_like(x)
    for h in range(GROUP_WIDTH // HEAD):
        m = (lane >= h * HEAD) & (lane < (h + 1) * HEAD)
        s = jnp.sum(jnp.where(m, x, 0.0), axis=1, keepdims=True)
        out = jnp.where(m, s, out)
    return out


def _pair_masks():
    lane = lax.broadcasted_iota(jnp.int32, (HEAD, LANES), 1)
    sub = lax.broadcasted_iota(jnp.int32, (HEAD, LANES), 0)
    return lane < HEAD, (lane & (HEAD - 1)) == sub


def _seg_sum(p, lo):
    s0 = jnp.sum(jnp.where(lo, p, 0.0), axis=1, keepdims=True)
    s1 = jnp.sum(jnp.where(lo, 0.0, p), axis=1, keepdims=True)
    return jnp.where(lo, s0, s1)


def _col(row_b, lo, eye2):
    return _seg_sum(jnp.where(eye2, row_b, 0.0), lo)


def _row(col_b, eye2):
    return jnp.sum(jnp.where(eye2, col_b, 0.0), axis=0, keepdims=True)


def _bcast(ref, t, b, p):
    r = ref[t, b:b + 1, p * LANES:(p + 1) * LANES]
    return jnp.broadcast_to(r, (HEAD, LANES))


def _s5_kernel(u_ref, h0r_ref, h0i_ref, lr_ref, li_ref, ldt_ref, bre_ref, bim_ref, ccat_ref, d_ref,
               gw_ref, gb_ref, y_ref, hr_out, hi_out, h_scr, bu_scr, *, tc):
    c = pl.program_id(1)

    @pl.when(c == 0)
    def _():
        h_scr[0] = h0r_ref[...]
        h_scr[1] = h0i_ref[...]

    lr, li = lr_ref[...], li_ref[...]
    dt = jnp.exp(ldt_ref[...])
    mag = jnp.exp(lr * dt)
    ab_re, ab_im = mag * jnp.cos(li * dt), mag * jnp.sin(li * dt)
    den = lr * lr + li * li
    zr, zi = ab_re - 1.0, ab_im
    cr = (zr * lr + zi * li) / den
    ci = (zi * lr - zr * li) / den
    bre, bim = bre_ref[...], bim_ref[...]
    bb_re = (cr * bre - ci * bim).astype(BF16)
    bb_im = (cr * bim + ci * bre).astype(BF16)

    u = u_ref[...]
    ub = u.astype(BF16)
    bu_scr[:, 0:SSM_FLAT] = _dot(ub, bb_re)
    bu_scr[:, SSM_FLAT:2 * SSM_FLAT] = _dot(ub, bb_im)

    ar = jnp.broadcast_to(ab_re, (SEQ_BLK, SSM_FLAT))
    ai = jnp.broadcast_to(ab_im, (SEQ_BLK, SSM_FLAT))

    def step(t, carry):
        hr, hi = carry
        rows = pl.ds(pl.multiple_of(t * SEQ_BLK, SEQ_BLK), SEQ_BLK)
        nhr = ar * hr - ai * hi + bu_scr[rows, 0:SSM_FLAT]
        nhi = ar * hi + ai * hr + bu_scr[rows, SSM_FLAT:2 * SSM_FLAT]
        bu_scr[rows, 0:SSM_FLAT] = nhr
        bu_scr[rows, SSM_FLAT:2 * SSM_FLAT] = nhi
        return nhr, nhi

    hr, hi = lax.fori_loop(0, tc, step, (h_scr[0], h_scr[1]))
    h_scr[0] = hr
    h_scr[1] = hi

    y = _dot(bu_scr[...].astype(BF16), ccat_ref[...]) + d_ref[...] * u
    z = jax.nn.gelu(y)
    out = z * jax.nn.sigmoid(_dot(z.astype(BF16), gw_ref[...]) + gb_ref[...])
    y_ref[...] = out.astype(y_ref.dtype)

    @pl.when(c == pl.num_programs(1) - 1)
    def _():
        hr_out[...] = hr
        hi_out[...] = hi


def s5_mixer(proj, h0_re, h0_im, lam_re, lam_im, log_dt, b_re_bd, b_im_bd, c_cat, d_skip, glu_w, glu_b, *, t_len, tc):
    nseq = h0_re.shape[0] // SEQ_BLK
    nchunks = t_len // tc
    n = proj.shape[0]
    st = jax.ShapeDtypeStruct(h0_re.shape, F32)
    return pl.pallas_call(
        functools.partial(_s5_kernel, tc=tc),
        grid=(nseq, nchunks),
        in_specs=[_row_spec(tc, COL_SSM, nchunks),
                  _seq_spec(h0_re.shape), _seq_spec(h0_im.shape),
                  _full_spec((1, SSM_FLAT)), _full_spec((1, SSM_FLAT)), _full_spec((1, SSM_FLAT)),
                  _full_spec((GROUP_WIDTH, SSM_FLAT)), _full_spec((GROUP_WIDTH, SSM_FLAT)),
                  _full_spec((2 * SSM_FLAT, GROUP_WIDTH)), _full_spec((1, GROUP_WIDTH)),
                  _full_spec((GROUP_WIDTH, GROUP_WIDTH)), _full_spec((1, GROUP_WIDTH))],
        out_specs=[_row_spec(tc, 0, nchunks), _seq_spec(h0_re.shape), _seq_spec(h0_im.shape)],
        out_shape=[jax.ShapeDtypeStruct((n, GROUP_WIDTH), BF16), st, st],
        scratch_shapes=[pltpu.VMEM((2, SEQ_BLK, SSM_FLAT), F32),
                        pltpu.VMEM((tc * SEQ_BLK, 2 * SSM_FLAT), F32)],
        compiler_params=_params(("parallel", "arbitrary")),
        name="s5_mixer",
    )(proj, h0_re, h0_im, lam_re, lam_im, log_dt, b_re_bd, b_im_bd, c_cat, d_skip, glu_w, glu_b)


def _pool_kernel(u_ref, buf_ref, w_ref, sc_ref, y_ref, nbuf_ref, ext_scr, *, tc, pos0):
    c = pl.program_id(1)

    @pl.when(c == 0)
    def _():
        ext_scr[0:POOL_BUF] = buf_ref[...]

    u = u_ref[...].reshape(tc, SEQ_BLK, GROUP_WIDTH)
    ext_scr[POOL_BUF:POOL_BUF + tc] = u
    a1 = ext_scr[...]
    a2 = a1[1:] + a1[:-1]
    a4 = a2[2:] + a2[:-2]
    a8 = a4[4:] + a4[:-4]
    a16 = a8[8:] + a8[:-8]
    sums = (a2[14:], a4[12:], a8[8:], a16)

    shape = (tc, SEQ_BLK, GROUP_WIDTH)
    pos = lax.broadcasted_iota(jnp.int32, shape, 0) + (c * tc + pos0)
    lane = lax.broadcasted_iota(jnp.int32, shape, 2)
    pooled = None
    for gi in reversed(range(len(POOL_WINDOWS))):
        win = POOL_WINDOWS[gi]
        mean = sums[gi] / jnp.minimum(pos + 1, win).astype(F32)
        pooled = mean if pooled is None else jnp.where(lane < (gi + 1) * HEAD, mean, pooled)
    pooled = (pooled - u).reshape(tc * SEQ_BLK, GROUP_WIDTH)
    y_ref[...] = (_dot(pooled.astype(BF16), w_ref[...]) * sc_ref[...]).astype(y_ref.dtype)

    nb = ext_scr[tc:tc + POOL_BUF]
    ext_scr[0:POOL_BUF] = nb

    @pl.when(c == pl.num_programs(1) - 1)
    def _():
        nbuf_ref[...] = nb


def pool_mixer(proj, buf, w_bd, scale, *, t_len, tc, pos0):
    nseq = buf.shape[1] // SEQ_BLK
    nchunks = t_len // tc
    n = proj.shape[0]
    buf_spec = pl.BlockSpec((POOL_BUF, SEQ_BLK, GROUP_WIDTH), lambda s, c: (0, s, 0))
    return pl.pallas_call(
        functools.partial(_pool_kernel, tc=tc, pos0=pos0),
        grid=(nseq, nchunks),
        in_specs=[_row_spec(tc, COL_POOL, nchunks), buf_spec,
                  _full_spec((GROUP_WIDTH, GROUP_WIDTH)), _full_spec((1, GROUP_WIDTH))],
        out_specs=[_row_spec(tc, 0, nchunks), buf_spec],
        out_shape=[jax.ShapeDtypeStruct((n, GROUP_WIDTH), BF16), jax.ShapeDtypeStruct(buf.shape, F32)],
        scratch_shapes=[pltpu.VMEM((tc + POOL_BUF, SEQ_BLK, GROUP_WIDTH), F32)],
        compiler_params=_params(("parallel", "arbitrary")),
        name="pool_mixer",
    )(proj, buf, w_bd, scale)


def _hgrn_lower_bound(logits_ref, layer):
    rows = [logits_ref[l:l + 1, :] for l in range(DEPTH)]
    m = functools.reduce(jnp.maximum, rows)
    es = [jnp.exp(r - m) for r in rows]
    tot = functools.reduce(lambda a, b: a + b, es)
    lb = jnp.zeros_like(m)
    for l in range(1, layer + 1):
        lb = lb + es[l] / tot
    return lb


def _hgrn_kernel(pq_ref, pf_ref, pi_ref, pg_ref, s0_ref, lbl_ref, ng_ref, y_ref, st_ref,
                 s_scr, q_scr, f_scr, k_scr, v_scr, o_scr, *, tc, layer):
    c = pl.program_id(1)
    shape3 = (tc, SEQ_BLK, GROUP_WIDTH)

    @pl.when(c == 0)
    def _():
        s_scr[...] = s0_ref[...]

    lb = _hgrn_lower_bound(lbl_ref, layer)
    zf = pf_ref[...]
    f_scr[...] = (lb + (1.0 - lb) * jax.nn.sigmoid(zf)).reshape(shape3)
    k_scr[...] = ((1.0 - lb) * jax.nn.sigmoid(-zf)).reshape(shape3)
    q_scr[...] = jax.nn.silu(pq_ref[...]).reshape(shape3)
    v_scr[...] = pi_ref[...].reshape(shape3)

    lo, eye2 = _pair_masks()

    def step(t, carry):
        for b in range(SEQ_BLK):
            for p in range(2):
                s = s_scr[b, p]
                vcol = _col(_bcast(v_scr, t, b, p), lo, eye2)
                s = s * _bcast(f_scr, t, b, p) + vcol * _bcast(k_scr, t, b, p)
                s_scr[b, p] = s
                ocol = _seg_sum(s * _bcast(q_scr, t, b, p), lo)
                o_scr[t, b:b + 1, p * LANES:(p + 1) * LANES] = _row(ocol, eye2)
        return carry

    lax.fori_loop(0, tc, step, 0)

    o = o_scr[...].reshape(tc * SEQ_BLK, GROUP_WIDTH)
    ms = _head_sums(o * o) * (1.0 / HEAD)
    out = o * lax.rsqrt(ms + HGRN_NORM_EPS) * ng_ref[...] * jax.nn.silu(pg_ref[...])
    y_ref[...] = out.astype(y_ref.dtype)

    @pl.when(c == pl.num_programs(1) - 1)
    def _():
        st_ref[...] = s_scr[...]


def hgrn_mixer(proj, s0, lb_logits, norm_g, *, t_len, tc, layer):
    nseq = s0.shape[0] // SEQ_BLK
    nchunks = t_len // tc
    n = proj.shape[0]
    tile = pltpu.VMEM((tc, SEQ_BLK, GROUP_WIDTH), F32)
    return pl.pallas_call(
        functools.partial(_hgrn_kernel, tc=tc, layer=layer),
        grid=(nseq, nchunks),
        in_specs=[_row_spec(tc, COL_Q, nchunks), _row_spec(tc, COL_F, nchunks),
                  _row_spec(tc, COL_I, nchunks), _row_spec(tc, COL_G, nchunks),
                  _seq_spec(s0.shape), _full_spec((DEPTH, GROUP_WIDTH)), _full_spec((1, GROUP_WIDTH))],
        out_specs=[_row_spec(tc, 0, nchunks), _seq_spec(s0.shape)],
        out_shape=[jax.ShapeDtypeStruct((n, GROUP_WIDTH), BF16), jax.ShapeDtypeStruct(s0.shape, F32)],
        scratch_shapes=[pltpu.VMEM((SEQ_BLK, 2, HEAD, LANES), F32), tile, tile, tile, tile, tile],
        compiler_params=_params(("parallel", "arbitrary")),
        name="hgrn_mixer",
    )(proj, proj, proj, proj, s0, lb_logits, norm_g)


def _rwkv_kernel(pr_ref, pk_ref, pv_ref, pl_ref, sh0_ref, s0_ref, mu_ref, w0_ref, a0_ref, kk_ref, ka_ref,
                 rk_ref, lng_ref, lnb_ref, w2_ref, a2_ref, g2_ref, y_ref, st_ref, sh_ref,
                 s_scr, prev_scr, r_scr, w_scr, k_scr, v_scr, nkk_scr, kka_scr, o_scr, *, tc):
    c = pl.program_id(1)
    shape3 = (tc, SEQ_BLK, GROUP_WIDTH)
    gw = GROUP_WIDTH

    @pl.when(c == 0)
    def _():
        s_scr[...] = s0_ref[...]
        prev_scr[...] = sh0_ref[...]

    def shifted(ref, j):
        x = ref[...].reshape(shape3)
        first = prev_scr[:, j * gw:(j + 1) * gw].reshape(1, SEQ_BLK, gw)
        prev = first if tc == 1 else jnp.concatenate([first, x[:-1]], axis=0)
        prev_scr[:, j * gw:(j + 1) * gw] = x[tc - 1]
        return (x + (prev - x) * mu_ref[:, j * gw:(j + 1) * gw]).reshape(tc * SEQ_BLK, gw)

    xr, xk, xv, xl = shifted(pr_ref, 0), shifted(pk_ref, 1), shifted(pv_ref, 2), shifted(pl_ref, 3)
    w = -jax.nn.softplus(-(w0_ref[...] + _dot(jnp.tanh(xl).astype(BF16), w2_ref[...]))) - 0.5
    decay = jnp.exp(-jnp.exp(w))
    a = jax.nn.sigmoid(a0_ref[...] + _dot(xl.astype(BF16), a2_ref[...]))
    g = _dot(jax.nn.sigmoid(xl).astype(BF16), g2_ref[...])
    kk = xk * kk_ref[...]
    kk = kk / jnp.maximum(jnp.sqrt(_head_sums(kk * kk)), 1e-12)
    k = xk * (1.0 + (a - 1.0) * ka_ref[...])

    r_scr[...] = xr.reshape(shape3)
    w_scr[...] = decay.reshape(shape3)
    k_scr[...] = k.reshape(shape3)
    v_scr[...] = xv.reshape(shape3)
    nkk_scr[...] = (-kk).reshape(shape3)
    kka_scr[...] = (kk * a).reshape(shape3)

    lo, eye2 = _pair_masks()

    def step(t, carry):
        for b in range(SEQ_BLK):
            for p in range(2):
                s = s_scr[b, p]
                sa = _seg_sum(s * _bcast(nkk_scr, t, b, p), lo)
                vcol = _col(_bcast(v_scr, t, b, p), lo, eye2)
                s = s * _bcast(w_scr, t, b, p) + sa * _bcast(kka_scr, t, b, p) + vcol * _bcast(k_scr, t, b, p)
                s_scr[b, p] = s
                ycol = _seg_sum(s * _bcast(r_scr, t, b, p), lo)
                o_scr[t, b:b + 1, p * LANES:(p + 1) * LANES] = _row(ycol, eye2)
        return carry

    lax.fori_loop(0, tc, step, 0)

    y = o_scr[...].reshape(tc * SEQ_BLK, gw)
    mean = _head_sums(y) * (1.0 / HEAD)
    d = y - mean
    var = _head_sums(d * d) * (1.0 / HEAD)
    yn = d * lax.rsqrt(var + RWKV_GN_EPS) * lng_ref[...] + lnb_ref[...]
    bonus = _head_sums(xr * k * rk_ref[...]) * xv
    y_ref[...] = ((yn + bonus) * g).astype(y_ref.dtype)

    @pl.when(c == pl.num_programs(1) - 1)
    def _():
        st_ref[...] = s_scr[...]
        sh_ref[...] = prev_scr[...]


def rwkv_mixer(proj, shift0, s0, mu, w0, a0, k_k, k_a, r_k, ln_g, ln_b, w2p, a2p, g2p, *, t_len, tc):
    nseq = s0.shape[0] // SEQ_BLK
    nchunks = t_len // tc
    n = proj.shape[0]
    tile = pltpu.VMEM((tc, SEQ_BLK, GROUP_WIDTH), F32)
    vec = _full_spec((1, GROUP_WIDTH))
    mat = _full_spec((GROUP_WIDTH, GROUP_WIDTH))
    return pl.pallas_call(
        functools.partial(_rwkv_kernel, tc=tc),
        grid=(nseq, nchunks),
        in_specs=[_row_spec(tc, COL_R, nchunks), _row_spec(tc, COL_K, nchunks),
                  _row_spec(tc, COL_V, nchunks), _row_spec(tc, COL_LORA, nchunks),
                  _seq_spec(shift0.shape), _seq_spec(s0.shape), _full_spec((1, RWKV_PROJ)),
                  vec, vec, vec, vec, vec, vec, vec, mat, mat, mat],
        out_specs=[_row_spec(tc, 0, nchunks), _seq_spec(s0.shape), _seq_spec(shift0.shape)],
        out_shape=[jax.ShapeDtypeStruct((n, GROUP_WIDTH), BF16), jax.ShapeDtypeStruct(s0.shape, F32),
                   jax.ShapeDtypeStruct(shift0.shape, F32)],
        scratch_shapes=[pltpu.VMEM((SEQ_BLK, 2, HEAD, LANES), F32), pltpu.VMEM((SEQ_BLK, RWKV_PROJ), F32),
                        tile, tile, tile, tile, tile, tile, tile],
        compiler_params=_params(("parallel", "arbitrary")),
        name="rwkv_mixer",
    )(proj, proj, proj, proj, shift0, s0, mu, w0, a0, k_k, k_a, r_k, ln_g, ln_b, w2p, a2p, g2p)


def _mix_mlp_kernel(h_ref, ya_ref, yb_ref, yc_ref, yd_ref, wo_ref, g2_ref, wu_ref, wd_ref, gf_ref, o_ref,
                    h1_scr, xn_scr, acc_scr, *, final_norm):
    j = pl.program_id(1)
    gw = GROUP_WIDTH

    @pl.when(j == 0)
    def _():
        mix = (_dot(ya_ref[...], wo_ref[0:gw]) + _dot(yb_ref[...], wo_ref[gw:2 * gw])
               + _dot(yc_ref[...], wo_ref[2 * gw:3 * gw]) + _dot(yd_ref[...], wo_ref[3 * gw:4 * gw]))
        h1 = h_ref[...] + mix
        h1_scr[...] = h1
        xn_scr[...] = _rms(h1, g2_ref[...]).astype(BF16)
        acc_scr[...] = jnp.zeros_like(acc_scr)

    up = _dot(xn_scr[...], wu_ref[...])
    act = jnp.square(jnp.maximum(up, 0.0)).astype(BF16)
    acc_scr[...] += _dot(act, wd_ref[...])

    @pl.when(j == pl.num_programs(1) - 1)
    def _():
        out = h1_scr[...] + acc_scr[...]
        if final_norm:
            out = _rms(out, gf_ref[...])
        o_ref[...] = out


def mix_mlp(h, ys, w_out, g2, w_up, w_down, g_final, *, final_norm):
    n = h.shape[0]
    tm = min(512, n)
    tf = 1024
    row = lambda w: pl.BlockSpec((tm, w), lambda i, j: (i, 0))
    return pl.pallas_call(
        functools.partial(_mix_mlp_kernel, final_norm=final_norm),
        grid=(n // tm, D_FF // tf),
        in_specs=[row(D_MODEL), row(GROUP_WIDTH), row(GROUP_WIDTH), row(GROUP_WIDTH), row(GROUP_WIDTH),
                  pl.BlockSpec((D_MODEL, D_MODEL), lambda i, j: (0, 0)),
                  pl.BlockSpec((1, D_MODEL), lambda i, j: (0, 0)),
                  pl.BlockSpec((D_MODEL, tf), lambda i, j: (0, j)),
                  pl.BlockSpec((tf, D_MODEL), lambda i, j: (j, 0)),
                  pl.BlockSpec((1, D_MODEL), lambda i, j: (0, 0))],
        out_specs=row(D_MODEL),
        out_shape=jax.ShapeDtypeStruct((n, D_MODEL), F32),
        scratch_shapes=[pltpu.VMEM((tm, D_MODEL), F32), pltpu.VMEM((tm, D_MODEL), BF16),
                        pltpu.VMEM((tm, D_MODEL), F32)],
        compiler_params=_params(("parallel", "arbitrary")),
        name="mix_mlp",
    )(h, *ys, w_out, g2, w_up, w_down, g_final)


def _hgrn_state_in(s):
    b = s.shape[0]
    return s.reshape(b, 2, 2, HEAD, HEAD).transpose(0, 1, 4, 2, 3).reshape(b, 2, HEAD, LANES)


def _hgrn_state_out(s):
    b = s.shape[0]
    return s.reshape(b, 2, HEAD, 2, HEAD).transpose(0, 1, 3, 4, 2).reshape(b, 4, HEAD, HEAD)


def _wkv_state_in(s):
    b = s.shape[0]
    return s.reshape(b, 2, 2, HEAD, HEAD).transpose(0, 1, 3, 2, 4).reshape(b, 2, HEAD, LANES)


def _wkv_state_out(s):
    b = s.shape[0]
    return s.reshape(b, 2, HEAD, 2, HEAD).transpose(0, 1, 3, 2, 4).reshape(b, 4, HEAD, HEAD)


def _block_diag(blocks):
    g, r, c = blocks.shape
    eye = jnp.eye(g, dtype=blocks.dtype)
    return (blocks[:, :, None, :] * eye[:, None, :, None]).reshape(g * r, g * c)


def _pad_rows(w, start):
    return jnp.zeros((GROUP_WIDTH, GROUP_WIDTH), w.dtype).at[start:start + w.shape[0]].set(w)


def _layer_params(l, P):
    row = lambda a: a.reshape(1, -1)
    q = {}
    q["norm1_g"] = row(P["norm1_g"][l])
    q["w_in"] = P["w_in"][l].astype(BF16)
    q["lam_re"] = row(P["ssm_lambda_re"][l])
    q["lam_im"] = row(P["ssm_lambda_im"][l])
    q["log_dt"] = row(jnp.repeat(P["ssm_log_dt"][l], SSM_STATE))
    q["b_re"] = _block_diag(P["ssm_b_re"][l].transpose(0, 2, 1))
    q["b_im"] = _block_diag(P["ssm_b_im"][l].transpose(0, 2, 1))
    q["c_cat"] = jnp.concatenate([_block_diag(P["ssm_c_re"][l].transpose(0, 2, 1)),
                                  -_block_diag(P["ssm_c_im"][l].transpose(0, 2, 1))], axis=0).astype(BF16)
    q["ssm_d"] = row(P["ssm_d"][l])
    q["glu_w"] = P["ssm_glu_w"][l].astype(BF16)
    q["glu_b"] = row(P["ssm_glu_b"][l])
    q["hgrn_norm_g"] = row(P["hgrn_norm_g"][l])
    q["mu"] = row(P["rwkv_mu"][l])
    for name in ("w0", "a0", "k_k", "k_a", "r_k", "ln_g", "ln_b"):
        q[name] = row(P["rwkv_" + name][l])
    q["w2p"] = _pad_rows(P["rwkv_w2"][l], 0).astype(BF16)
    q["a2p"] = _pad_rows(P["rwkv_a2"][l], DECAY_LORA).astype(BF16)
    q["g2p"] = _pad_rows(P["rwkv_g2"][l], DECAY_LORA + AAA_LORA).astype(BF16)
    q["pool_w"] = _block_diag(P["pool_w"][l]).astype(BF16)
    q["pool_scale"] = row(P["pool_scale"][l])
    q["w_out"] = P["w_out"][l].astype(BF16)
    q["norm2_g"] = row(P["norm2_g"][l])
    q["mlp_up"] = P["mlp_up"][l].astype(BF16)
    q["mlp_down"] = P["mlp_down"][l].astype(BF16)
    return q


def _trunk(x_rows, states, pos0, t_len, tc, layer_params, P):
    ssm_re0, ssm_im0, hgrn0, wkv0, shift0, pool0 = states
    h = x_rows
    new = [[] for _ in range(6)]
    g_final = P["norm_f_g"].reshape(1, -1)
    for l in range(DEPTH):
        q = layer_params[l]
        proj = rms_proj(h, q["norm1_g"], q["w_in"])
        y_a, s_re, s_im = s5_mixer(proj, ssm_re0[l], ssm_im0[l], q["lam_re"], q["lam_im"], q["log_dt"], q["b_re"],
                                   q["b_im"], q["c_cat"], q["ssm_d"], q["glu_w"], q["glu_b"], t_len=t_len, tc=tc)
        y_b, s_hg = hgrn_mixer(proj, hgrn0[l], P["hgrn_lb_logits"], q["hgrn_norm_g"], t_len=t_len, tc=tc, layer=l)
        y_c, s_wkv, s_sh = rwkv_mixer(proj, shift0[l], wkv0[l], q["mu"], q["w0"], q["a0"], q["k_k"], q["k_a"],
                                      q["r_k"], q["ln_g"], q["ln_b"], q["w2p"], q["a2p"], q["g2p"],
                                      t_len=t_len, tc=tc)
        y_d, s_pool = pool_mixer(proj, pool0[l], q["pool_w"], q["pool_scale"], t_len=t_len, tc=tc, pos0=pos0)
        h = mix_mlp(h, (y_a, y_b, y_c, y_d), q["w_out"], q["norm2_g"], q["mlp_up"], q["mlp_down"], g_final,
                    final_norm=(l == DEPTH - 1))
        for lst, s in zip(new, (s_re, s_im, s_hg, s_wkv, s_sh, s_pool)):
            lst.append(s)
    return h, new


def _states_out(new, bsz):
    s_re, s_im, s_hg, s_wkv, s_sh, s_pool = new
    return (jnp.stack([s.reshape(bsz, SSM_GROUPS, SSM_STATE) for s in s_re]),
            jnp.stack([s.reshape(bsz, SSM_GROUPS, SSM_STATE) for s in s_im]),
            jnp.stack([_hgrn_state_out(s) for s in s_hg]),
            jnp.stack([_wkv_state_out(s) for s in s_wkv]),
            jnp.stack([s.reshape(bsz, 1, RWKV_PROJ) for s in s_sh]),
            jnp.stack([s.transpose(1, 0, 2) for s in s_pool]))


def kernel(x_prompt, x_sample, state_ssm_re, state_ssm_im, state_hgrn, state_wkv, state_shift, state_pool, norm1_g, w_in, ssm_lambda_re, ssm_lambda_im, ssm_log_dt, ssm_b_re, ssm_b_im, ssm_c_re, ssm_c_im, ssm_d, ssm_glu_w, ssm_glu_b, hgrn_lb_logits, hgrn_norm_g, rwkv_mu, rwkv_w0, rwkv_w2, rwkv_a0, rwkv_a2, rwkv_g2, rwkv_k_k, rwkv_k_a, rwkv_r_k, rwkv_ln_g, rwkv_ln_b, pool_w, pool_scale, w_out, norm2_g, mlp_up, mlp_down, norm_f_g):
    P = dict(norm1_g=norm1_g, w_in=w_in, ssm_lambda_re=ssm_lambda_re, ssm_lambda_im=ssm_lambda_im,
             ssm_log_dt=ssm_log_dt, ssm_b_re=ssm_b_re, ssm_b_im=ssm_b_im, ssm_c_re=ssm_c_re, ssm_c_im=ssm_c_im,
             ssm_d=ssm_d, ssm_glu_w=ssm_glu_w, ssm_glu_b=ssm_glu_b, hgrn_lb_logits=hgrn_lb_logits,
             hgrn_norm_g=hgrn_norm_g, rwkv_mu=rwkv_mu, rwkv_w0=rwkv_w0, rwkv_w2=rwkv_w2, rwkv_a0=rwkv_a0,
             rwkv_a2=rwkv_a2, rwkv_g2=rwkv_g2, rwkv_k_k=rwkv_k_k, rwkv_k_a=rwkv_k_a, rwkv_r_k=rwkv_r_k,
             rwkv_ln_g=rwkv_ln_g, rwkv_ln_b=rwkv_ln_b, pool_w=pool_w, pool_scale=pool_scale, w_out=w_out,
             norm2_g=norm2_g, mlp_up=mlp_up, mlp_down=mlp_down, norm_f_g=norm_f_g)
    layer_params = [_layer_params(l, P) for l in range(DEPTH)]

    bp, t_p, _ = x_prompt.shape
    xp = x_prompt.transpose(1, 0, 2).reshape(t_p * bp, D_MODEL)
    zeros = lambda *shape: [jnp.zeros(shape, F32) for _ in range(DEPTH)]
    st_p = (zeros(bp, SSM_FLAT), zeros(bp, SSM_FLAT), zeros(bp, 2, HEAD, LANES), zeros(bp, 2, HEAD, LANES),
            zeros(bp, RWKV_PROJ), zeros(POOL_BUF, bp, GROUP_WIDTH))
    yp, new_p = _trunk(xp, st_p, 0, t_p, 64, layer_params, P)
    y_prompt = yp.reshape(t_p, bp, D_MODEL).transpose(1, 0, 2)

    bs, t_s, _ = x_sample.shape
    nblk = bs // SEQ_BLK
    xs = x_sample.reshape(nblk, SEQ_BLK, t_s, D_MODEL).transpose(0, 2, 1, 3).reshape(bs * t_s, D_MODEL)
    st_s = ([state_ssm_re[l].reshape(bs, SSM_FLAT) for l in range(DEPTH)],
            [state_ssm_im[l].reshape(bs, SSM_FLAT) for l in range(DEPTH)],
            [_hgrn_state_in(state_hgrn[l]) for l in range(DEPTH)],
            [_wkv_state_in(state_wkv[l]) for l in range(DEPTH)],
            [state_shift[l].reshape(bs, RWKV_PROJ) for l in range(DEPTH)],
            [state_pool[l].transpose(1, 0, 2) for l in range(DEPTH)])
    ys, new_s = _trunk(xs, st_s, PAST_LEN, t_s, t_s, layer_params, P)
    y_sample = ys.reshape(nblk, t_s, SEQ_BLK, D_MODEL).transpose(0, 2, 1, 3).reshape(bs, t_s, D_MODEL)

    return (y_prompt, y_sample) + _states_out(new_p, bp) + _states_out(new_s, bs)
```

```python
import functools

import jax
import jax.numpy as jnp
from jax import lax
from jax.experimental import pallas as pl
from jax.experimental.pallas import tpu as pltpu

F32 = jnp.float32
BF16 = jnp.bfloat16

D_MODEL = 1024
DEPTH = 2
PAST_LEN = 16384
GROUP_WIDTH = 256
HEAD = 64
SSM_GROUPS = 16
SSM_CH = 16
SSM_STATE = 64
SSM_FLAT = SSM_GROUPS * SSM_STATE
POOL_WINDOWS = (2, 4, 8, 16)
POOL_BUF = 15
DECAY_LORA = 64
AAA_LORA = 64
GATE_LORA = 128
RWKV_PROJ = 1024
PROJ_WIDTH = 2560
D_FF = 4096
NORM_EPS = 1e-6
HGRN_NORM_EPS = 1e-5
RWKV_GN_EPS = 64e-5

SEQ_BLK = 8
LANES = 128
VMEM_LIMIT = 48 * 1024 * 1024

COL_SSM, COL_Q, COL_F, COL_I, COL_G, COL_R, COL_K, COL_V, COL_LORA, COL_POOL = range(10)


def _params(sem):
    return pltpu.CompilerParams(dimension_semantics=sem, vmem_limit_bytes=VMEM_LIMIT)


def _dot(a, b):
    return jnp.dot(a, b, preferred_element_type=F32)


def _rms(x, g):
    return x * lax.rsqrt(jnp.mean(x * x, axis=-1, keepdims=True) + NORM_EPS) * g


def _rms_proj_kernel(x_ref, g_ref, w_ref, o_ref):
    o_ref[...] = _dot(_rms(x_ref[...], g_ref[...]).astype(BF16), w_ref[...])


def rms_proj(x, g, w):
    n = x.shape[0]
    tm = min(512, n)
    return pl.pallas_call(
        _rms_proj_kernel,
        grid=(n // tm,),
        in_specs=[pl.BlockSpec((tm, D_MODEL), lambda i: (i, 0)),
                  pl.BlockSpec((1, D_MODEL), lambda i: (0, 0)),
                  pl.BlockSpec((D_MODEL, PROJ_WIDTH), lambda i: (0, 0))],
        out_specs=pl.BlockSpec((tm, PROJ_WIDTH), lambda i: (i, 0)),
        out_shape=jax.ShapeDtypeStruct((n, PROJ_WIDTH), F32),
        compiler_params=_params(("parallel",)),
        name="rms_proj",
    )(x, g, w)


def _row_spec(tc, col, nchunks):
    return pl.BlockSpec((tc * SEQ_BLK, GROUP_WIDTH), lambda s, c: (s * nchunks + c, col))


def _full_spec(shape):
    nd = len(shape)
    return pl.BlockSpec(shape, lambda s, c: (0,) * nd)


def _seq_spec(shape):
    nd = len(shape)
    return pl.BlockSpec((SEQ_BLK,) + shape[1:], lambda s, c: (s,) + (0,) * (nd - 1))


def _head_sums(x):
    lane = lax.broadcasted_iota(jnp.int32, x.shape, 1)
    out = jnp.zeros_like(x)
    for h in range(GROUP_WIDTH // HEAD):
        m = (lane >= h * HEAD) & (lane < (h + 1) * HEAD)
        s = jnp.sum(jnp.where(m, x, 0.0), axis=1, keepdims=True)
        out = jnp.where(m, s, out)
    return out


def _pair_masks():
    lane = lax.broadcasted_iota(jnp.int32, (HEAD, LANES), 1)
    sub = lax.broadcasted_iota(jnp.int32, (HEAD, LANES), 0)
    return lane < HEAD, (lane & (HEAD - 1)) == sub


def _seg_sum(p, lo):
    s0 = jnp.sum(jnp.where(lo, p, 0.0), axis=1, keepdims=True)
    s1 = jnp.sum(jnp.where(lo, 0.0, p), axis=1, keepdims=True)
    return jnp.where(lo, s0, s1)


def _col(row_b, lo, eye2):
    return _seg_sum(jnp.where(eye2, row_b, 0.0), lo)


def _row(col_b, eye2):
    return jnp.sum(jnp.where(eye2, col_b, 0.0), axis=0, keepdims=True)


def _bcast(ref, t, b, p):
    r = ref[t, b:b + 1, p * LANES:(p + 1) * LANES]
    return jnp.broadcast_to(r, (HEAD, LANES))


def _s5_kernel(u_ref, h0r_ref, h0i_ref, lr_ref, li_ref, ldt_ref, bre_ref, bim_ref, ccat_ref, d_ref,
               gw_ref, gb_ref, y_ref, hr_out, hi_out, h_scr, bu_scr, *, tc):
    c = pl.program_id(1)

    @pl.when(c == 0)
    def _():
        h_scr[0] = h0r_ref[...]
        h_scr[1] = h0i_ref[...]

    lr, li = lr_ref[...], li_ref[...]
    dt = jnp.exp(ldt_ref[...])
    mag = jnp.exp(lr * dt)
    ab_re, ab_im = mag * jnp.cos(li * dt), mag * jnp.sin(li * dt)
    den = lr * lr + li * li
    zr, zi = ab_re - 1.0, ab_im
    cr = (zr * lr + zi * li) / den
    ci = (zi * lr - zr * li) / den
    bre, bim = bre_ref[...], bim_ref[...]
    bb_re = (cr * bre - ci * bim).astype(BF16)
    bb_im = (cr * bim + ci * bre).astype(BF16)

    u = u_ref[...]
    ub = u.astype(BF16)
    bu_scr[:, 0:SSM_FLAT] = _dot(ub, bb_re)
    bu_scr[:, SSM_FLAT:2 * SSM_FLAT] = _dot(ub, bb_im)

    ar = jnp.broadcast_to(ab_re, (SEQ_BLK, SSM_FLAT))
    ai = jnp.broadcast_to(ab_im, (SEQ_BLK, SSM_FLAT))

    def step(t, carry):
        hr, hi = carry
        rows = pl.ds(pl.multiple_of(t * SEQ_BLK, SEQ_BLK), SEQ_BLK)
        nhr = ar * hr - ai * hi + bu_scr[rows, 0:SSM_FLAT]
        nhi = ar * hi + ai * hr + bu_scr[rows, SSM_FLAT:2 * SSM_FLAT]
        bu_scr[rows, 0:SSM_FLAT] = nhr
        bu_scr[rows, SSM_FLAT:2 * SSM_FLAT] = nhi
        return nhr, nhi

    hr, hi = lax.fori_loop(0, tc, step, (h_scr[0], h_scr[1]))
    h_scr[0] = hr
    h_scr[1] = hi

    y = _dot(bu_scr[...].astype(BF16), ccat_ref[...]) + d_ref[...] * u
    z = jax.nn.gelu(y)
    out = z * jax.nn.sigmoid(_dot(z.astype(BF16), gw_ref[...]) + gb_ref[...])
    y_ref[...] = out.astype(y_ref.dtype)

    @pl.when(c == pl.num_programs(1) - 1)
    def _():
        hr_out[...] = hr
        hi_out[...] = hi


def s5_mixer(proj, h0_re, h0_im, lam_re, lam_im, log_dt, b_re_bd, b_im_bd, c_cat, d_skip, glu_w, glu_b, *, t_len, tc):
    nseq = h0_re.shape[0] // SEQ_BLK
    nchunks = t_len // tc
    n = proj.shape[0]
    st = jax.ShapeDtypeStruct(h0_re.shape, F32)
    return pl.pallas_call(
        functools.partial(_s5_kernel, tc=tc),
        grid=(nseq, nchunks),
        in_specs=[_row_spec(tc, COL_SSM, nchunks),
                  _seq_spec(h0_re.shape), _seq_spec(h0_im.shape),
                  _full_spec((1, SSM_FLAT)), _full_spec((1, SSM_FLAT)), _full_spec((1, SSM_FLAT)),
                  _full_spec((GROUP_WIDTH, SSM_FLAT)), _full_spec((GROUP_WIDTH, SSM_FLAT)),
                  _full_spec((2 * SSM_FLAT, GROUP_WIDTH)), _full_spec((1, GROUP_WIDTH)),
                  _full_spec((GROUP_WIDTH, GROUP_WIDTH)), _full_spec((1, GROUP_WIDTH))],
        out_specs=[_row_spec(tc, 0, nchunks), _seq_spec(h0_re.shape), _seq_spec(h0_im.shape)],
        out_shape=[jax.ShapeDtypeStruct((n, GROUP_WIDTH), BF16), st, st],
        scratch_shapes=[pltpu.VMEM((2, SEQ_BLK, SSM_FLAT), F32),
                        pltpu.VMEM((tc * SEQ_BLK, 2 * SSM_FLAT), F32)],
        compiler_params=_params(("parallel", "arbitrary")),
        name="s5_mixer",
    )(proj, h0_re, h0_im, lam_re, lam_im, log_dt, b_re_bd, b_im_bd, c_cat, d_skip, glu_w, glu_b)


def _pool_kernel(u_ref, buf_ref, w_ref, sc_ref, y_ref, nbuf_ref, ext_scr, *, tc, pos0):
    c = pl.program_id(1)

    @pl.when(c == 0)
    def _():
        ext_scr[0:POOL_BUF] = buf_ref[...]

    u = u_ref[...].reshape(tc, SEQ_BLK, GROUP_WIDTH)
    ext_scr[POOL_BUF:POOL_BUF + tc] = u
    a1 = ext_scr[...]
    a2 = a1[1:] + a1[:-1]
    a4 = a2[2:] + a2[:-2]
    a8 = a4[4:] + a4[:-4]
    a16 = a8[8:] + a8[:-8]
    sums = (a2[14:], a4[12:], a8[8:], a16)

    shape = (tc, SEQ_BLK, GROUP_WIDTH)
    pos = lax.broadcasted_iota(jnp.int32, shape, 0) + (c * tc + pos0)
    lane = lax.broadcasted_iota(jnp.int32, shape, 2)
    pooled = None
    for gi in reversed(range(len(POOL_WINDOWS))):
        win = POOL_WINDOWS[gi]
        mean = sums[gi] / jnp.minimum(pos + 1, win).astype(F32)
        pooled = mean if pooled is None else jnp.where(lane < (gi + 1) * HEAD, mean, pooled)
    pooled = (pooled - u).reshape(tc * SEQ_BLK, GROUP_WIDTH)
    y_ref[...] = (_dot(pooled.astype(BF16), w_ref[...]) * sc_ref[...]).astype(y_ref.dtype)

    nb = ext_scr[tc:tc + POOL_BUF]
    ext_scr[0:POOL_BUF] = nb

    @pl.when(c == pl.num_programs(1) - 1)
    def _():
        nbuf_ref[...] = nb


def pool_mixer(proj, buf, w_bd, scale, *, t_len, tc, pos0):
    nseq = buf.shape[1] // SEQ_BLK
    nchunks = t_len // tc
    n = proj.shape[0]
    buf_spec = pl.BlockSpec((POOL_BUF, SEQ_BLK, GROUP_WIDTH), lambda s, c: (0, s, 0))
    return pl.pallas_call(
        functools.partial(_pool_kernel, tc=tc, pos0=pos0),
        grid=(nseq, nchunks),
        in_specs=[_row_spec(tc, COL_POOL, nchunks), buf_spec,
                  _full_spec((GROUP_WIDTH, GROUP_WIDTH)), _full_spec((1, GROUP_WIDTH))],
        out_specs=[_row_spec(tc, 0, nchunks), buf_spec],
        out_shape=[jax.ShapeDtypeStruct((n, GROUP_WIDTH), BF16), jax.ShapeDtypeStruct(buf.shape, F32)],
        scratch_shapes=[pltpu.VMEM((tc + POOL_BUF, SEQ_BLK, GROUP_WIDTH), F32)],
        compiler_params=_params(("parallel", "arbitrary")),
        name="pool_mixer",
    )(proj, buf, w_bd, scale)


def _hgrn_lower_bound(logits_ref, layer):
    rows = [logits_ref[l:l + 1, :] for l in range(DEPTH)]
    m = functools.reduce(jnp.maximum, rows)
    es = [jnp.exp(r - m) for r in rows]
    tot = functools.reduce(lambda a, b: a + b, es)
    lb = jnp.zeros_like(m)
    for l in range(1, layer + 1):
        lb = lb + es[l] / tot
    return lb


def _hgrn_kernel(pq_ref, pf_ref, pi_ref, pg_ref, s0_ref, lbl_ref, ng_ref, y_ref, st_ref,
                 s_scr, q_scr, f_scr, k_scr, v_scr, o_scr, *, tc, layer):
    c = pl.program_id(1)
    shape3 = (tc, SEQ_BLK, GROUP_WIDTH)

    @pl.when(c == 0)
    def _():
        s_scr[...] = s0_ref[...]

    lb = _hgrn_lower_bound(lbl_ref, layer)
    zf = pf_ref[...]
    f_scr[...] = (lb + (1.0 - lb) * jax.nn.sigmoid(zf)).reshape(shape3)
    k_scr[...] = ((1.0 - lb) * jax.nn.sigmoid(-zf)).reshape(shape3)
    q_scr[...] = jax.nn.silu(pq_ref[...]).reshape(shape3)
    v_scr[...] = pi_ref[...].reshape(shape3)

    lo, eye2 = _pair_masks()

    def step(t, carry):
        for b in range(SEQ_BLK):
            for p in range(2):
                s = s_scr[b, p]
                vcol = _col(_bcast(v_scr, t, b, p), lo, eye2)
                s = s * _bcast(f_scr, t, b, p) + vcol * _bcast(k_scr, t, b, p)
                s_scr[b, p] = s
                ocol = _seg_sum(s * _bcast(q_scr, t, b, p), lo)
                o_scr[t, b:b + 1, p * LANES:(p + 1) * LANES] = _row(ocol, eye2)
        return carry

    lax.fori_loop(0, tc, step, 0)

    o = o_scr[...].reshape(tc * SEQ_BLK, GROUP_WIDTH)
    ms = _head_sums(o * o) * (1.0 / HEAD)
    out = o * lax.rsqrt(ms + HGRN_NORM_EPS) * ng_ref[...] * jax.nn.silu(pg_ref[...])
    y_ref[...] = out.astype(y_ref.dtype)

    @pl.when(c == pl.num_programs(1) - 1)
    def _():
        st_ref[...] = s_scr[...]


def hgrn_mixer(proj, s0, lb_logits, norm_g, *, t_len, tc, layer):
    nseq = s0.shape[0] // SEQ_BLK
    nchunks = t_len // tc
    n = proj.shape[0]
    tile = pltpu.VMEM((tc, SEQ_BLK, GROUP_WIDTH), F32)
    return pl.pallas_call(
        functools.partial(_hgrn_kernel, tc=tc, layer=layer),
        grid=(nseq, nchunks),
        in_specs=[_row_spec(tc, COL_Q, nchunks), _row_spec(tc, COL_F, nchunks),
                  _row_spec(tc, COL_I, nchunks), _row_spec(tc, COL_G, nchunks),
                  _seq_spec(s0.shape), _full_spec((DEPTH, GROUP_WIDTH)), _full_spec((1, GROUP_WIDTH))],
        out_specs=[_row_spec(tc, 0, nchunks), _seq_spec(s0.shape)],
        out_shape=[jax.ShapeDtypeStruct((n, GROUP_WIDTH), BF16), jax.ShapeDtypeStruct(s0.shape, F32)],
        scratch_shapes=[pltpu.VMEM((SEQ_BLK, 2, HEAD, LANES), F32), tile, tile, tile, tile, tile],
        compiler_params=_params(("parallel", "arbitrary")),
        name="hgrn_mixer",
    )(proj, proj, proj, proj, s0, lb_logits, norm_g)


def _rwkv_kernel(pr_ref, pk_ref, pv_ref, pl_ref, sh0_ref, s0_ref, mu_ref, w0_ref, a0_ref, kk_ref, ka_ref,
                 rk_ref, lng_ref, lnb_ref, w2_ref, a2_ref, g2_ref, y_ref, st_ref, sh_ref,
                 s_scr, prev_scr, r_scr, w_scr, k_scr, v_scr, nkk_scr, kka_scr, o_scr, *, tc):
    c = pl.program_id(1)
    shape3 = (tc, SEQ_BLK, GROUP_WIDTH)
    gw = GROUP_WIDTH

    @pl.when(c == 0)
    def _():
        s_scr[...] = s0_ref[...]
        prev_scr[...] = sh0_ref[...]

    def shifted(ref, j):
        x = ref[...].reshape(shape3)
        first = prev_scr[:, j * gw:(j + 1) * gw].reshape(1, SEQ_BLK, gw)
        prev = first if tc == 1 else jnp.concatenate([first, x[:-1]], axis=0)
        prev_scr[:, j * gw:(j + 1) * gw] = x[tc - 1]
        return (x + (prev - x) * mu_ref[:, j * gw:(j + 1) * gw]).reshape(tc * SEQ_BLK, gw)

    xr, xk, xv, xl = shifted(pr_ref, 0), shifted(pk_ref, 1), shifted(pv_ref, 2), shifted(pl_ref, 3)
    w = -jax.nn.softplus(-(w0_ref[...] + _dot(jnp.tanh(xl).astype(BF16), w2_ref[...]))) - 0.5
    decay = jnp.exp(-jnp.exp(w))
    a = jax.nn.sigmoid(a0_ref[...] + _dot(xl.astype(BF16), a2_ref[...]))
    g = _dot(jax.nn.sigmoid(xl).astype(BF16), g2_ref[...])
    kk = xk * kk_ref[...]
    kk = kk / jnp.maximum(jnp.sqrt(_head_sums(kk * kk)), 1e-12)
    k = xk * (1.0 + (a - 1.0) * ka_ref[...])

    r_scr[...] = xr.reshape(shape3)
    w_scr[...] = decay.reshape(shape3)
    k_scr[...] = k.reshape(shape3)
    v_scr[...] = xv.reshape(shape3)
    nkk_scr[...] = (-kk).reshape(shape3)
    kka_scr[...] = (kk * a).reshape(shape3)

    lo, eye2 = _pair_masks()

    def step(t, carry):
        for b in range(SEQ_BLK):
            for p in range(2):
                s = s_scr[b, p]
                sa = _seg_sum(s * _bcast(nkk_scr, t, b, p), lo)
                vcol = _col(_bcast(v_scr, t, b, p), lo, eye2)
                s = s * _bcast(w_scr, t, b, p) + sa * _bcast(kka_scr, t, b, p) + vcol * _bcast(k_scr, t, b, p)
                s_scr[b, p] = s
                ycol = _seg_sum(s * _bcast(r_scr, t, b, p), lo)
                o_scr[t, b:b + 1, p * LANES:(p + 1) * LANES] = _row(ycol, eye2)
        return carry

    lax.fori_loop(0, tc, step, 0)

    y = o_scr[...].reshape(tc * SEQ_BLK, gw)
    mean = _head_sums(y) * (1.0 / HEAD)
    d = y - mean
    var = _head_sums(d * d) * (1.0 / HEAD)
    yn = d * lax.rsqrt(var + RWKV_GN_EPS) * lng_ref[...] + lnb_ref[...]
    bonus = _head_sums(xr * k * rk_ref[...]) * xv
    y_ref[...] = ((yn + bonus) * g).astype(y_ref.dtype)

    @pl.when(c == pl.num_programs(1) - 1)
    def _():
        st_ref[...] = s_scr[...]
        sh_ref[...] = prev_scr[...]


def rwkv_mixer(proj, shift0, s0, mu, w0, a0, k_k, k_a, r_k, ln_g, ln_b, w2p, a2p, g2p, *, t_len, tc):
    nseq = s0.shape[0] // SEQ_BLK
    nchunks = t_len // tc
    n = proj.shape[0]
    tile = pltpu.VMEM((tc, SEQ_BLK, GROUP_WIDTH), F32)
    vec = _full_spec((1, GROUP_WIDTH))
    mat = _full_spec((GROUP_WIDTH, GROUP_WIDTH))
    return pl.pallas_call(
        functools.partial(_rwkv_kernel, tc=tc),
        grid=(nseq, nchunks),
        in_specs=[_row_spec(tc, COL_R, nchunks), _row_spec(tc, COL_K, nchunks),
                  _row_spec(tc, COL_V, nchunks), _row_spec(tc, COL_LORA, nchunks),
                  _seq_spec(shift0.shape), _seq_spec(s0.shape), _full_spec((1, RWKV_PROJ)),
                  vec, vec, vec, vec, vec, vec, vec, mat, mat, mat],
        out_specs=[_row_spec(tc, 0, nchunks), _seq_spec(s0.shape), _seq_spec(shift0.shape)],
        out_shape=[jax.ShapeDtypeStruct((n, GROUP_WIDTH), BF16), jax.ShapeDtypeStruct(s0.shape, F32),
                   jax.ShapeDtypeStruct(shift0.shape, F32)],
        scratch_shapes=[pltpu.VMEM((SEQ_BLK, 2, HEAD, LANES), F32), pltpu.VMEM((SEQ_BLK, RWKV_PROJ), F32),
                        tile, tile, tile, tile, tile, tile, tile],
        compiler_params=_params(("parallel", "arbitrary")),
        name="rwkv_mixer",
    )(proj, proj, proj, proj, shift0, s0, mu, w0, a0, k_k, k_a, r_k, ln_g, ln_b, w2p, a2p, g2p)


def _dot_nt(a, b):
    return lax.dot_general(a, b, (((1,), (1,)), ((), ())), preferred_element_type=F32)


def _dot_tn(a, b):
    return lax.dot_general(a, b, (((0,), (0,)), ((), ())), preferred_element_type=F32)


def _iota(shape, dim):
    return lax.broadcasted_iota(jnp.int32, shape, dim)


def _head_of(idx):
    return lax.shift_right_logical(idx, HEAD.bit_length() - 1)


def _cumsum_rows(x):
    n = x.shape[0]
    tri = jnp.where(_iota((n, n), 0) >= _iota((n, n), 1), 1.0, 0.0).astype(BF16)
    hi = x.astype(BF16)
    rest = x - hi.astype(F32)
    mid = rest.astype(BF16)
    lo = (rest - mid.astype(F32)).astype(BF16)
    return _dot(tri, hi) + _dot(tri, mid) + _dot(tri, lo)


def _own_head(shape, rows_per_head):
    row_h = lax.shift_right_logical(_iota(shape, 0), rows_per_head.bit_length() - 1)
    return row_h == _head_of(_iota(shape, 1))


def _head_expand(x):
    xx = jnp.concatenate([x] * (GROUP_WIDTH // HEAD), axis=0)
    return jnp.where(_own_head(xx.shape, x.shape[0]), xx, 0.0)


def _head_collapse(xx):
    n = xx.shape[0] // (GROUP_WIDTH // HEAD)
    return xx[0:n] + xx[n:2 * n] + xx[2 * n:3 * n] + xx[3 * n:4 * n]


def _block_diag_mask():
    shape = (GROUP_WIDTH, GROUP_WIDTH)
    return _head_of(_iota(shape, 0)) == _head_of(_iota(shape, 1))


def _seq_col_spec(rows, col, ncols):
    return pl.BlockSpec((rows, GROUP_WIDTH), lambda b, c: (c, b * ncols + col))


HGRN_CHUNK = 128
HGRN_SUB = 16


def _hgrn_chunk_kernel(pq_ref, pf_ref, pi_ref, pg_ref, lbl_ref, ng_ref, y_ref, st_ref,
                       w_scr, k_scr, b_scr, v_scr, p_scr, r_scr, o_scr, *, layer):
    L, n = HGRN_CHUNK, HGRN_SUB
    c = pl.program_id(1)

    @pl.when(c == 0)
    def _():
        w_scr[...] = jnp.zeros_like(w_scr)

    lb = _hgrn_lower_bound(lbl_ref, layer)
    z = pf_ref[...]
    g = jnp.logaddexp(jnp.log1p(-lb) + jax.nn.log_sigmoid(z), jnp.log(lb))
    kg = (1.0 - lb) * jax.nn.sigmoid(-z)
    q = jax.nn.silu(pq_ref[...])
    v = pi_ref[...]
    bc = _cumsum_rows(g)
    k_scr[...] = kg
    b_scr[...] = bc
    v_scr[...] = v

    rid = _iota((n, GROUP_WIDTH), 0)
    for sb in range(L // n):
        base = sb * n
        qs, bs = q[base:base + n], bc[base:base + n]
        for s in range(n):
            e = jnp.where(rid >= s, jnp.exp(bs - b_scr[base + s:base + s + 1, :]), 0.0)
            p_scr[(base + s) * n:(base + s + 1) * n, :] = (qs * k_scr[base + s:base + s + 1, :] * e).astype(BF16)
    bd = _block_diag_mask()
    r_scr[...] = _dot(p_scr[...], jnp.where(bd, 1.0, 0.0).astype(BF16))
    for sb in range(L // n):
        base = sb * n
        acc = jnp.zeros((n, GROUP_WIDTH), F32)
        for s in range(n):
            acc = acc + r_scr[(base + s) * n:(base + s + 1) * n, :] * v_scr[base + s:base + s + 1, :]
        o_scr[base:base + n, :] = acc

    vb = v.astype(BF16)
    rows = _iota((L, GROUP_WIDTH), 0)
    for i in range(1, L // n):
        r0 = i * n
        ref = b_scr[r0 - 1:r0, :]
        qt = q[r0:r0 + n] * jnp.exp(bc[r0:r0 + n] - ref)
        kt = jnp.where(rows < r0, kg * jnp.exp(ref - bc), 0.0)
        att = _dot_nt(_head_expand(qt).astype(BF16), kt.astype(BF16))
        ox = _dot(att.astype(BF16), vb)
        o_scr[r0:r0 + n, :] += _head_collapse(jnp.where(_own_head(ox.shape, n), ox, 0.0))

    w = w_scr[...]
    o = o_scr[...] + _dot_nt((q * jnp.exp(bc)).astype(BF16), w.astype(BF16))
    b_end = b_scr[L - 1:L, :]
    upd = _dot_tn(vb, (kg * jnp.exp(b_end - bc)).astype(BF16))
    w_scr[...] = w * jnp.exp(b_end) + jnp.where(bd, upd, 0.0)

    ms = _head_sums(o * o) * (1.0 / HEAD)
    out = o * lax.rsqrt(ms + HGRN_NORM_EPS) * ng_ref[...] * jax.nn.silu(pg_ref[...])
    y_ref[...] = out.astype(y_ref.dtype)

    @pl.when(c == pl.num_programs(1) - 1)
    def _():
        st_ref[...] = w_scr[...]


def hgrn_chunk_mixer(proj, lb_logits, norm_g, *, bsz, t_len, layer):
    L, n = HGRN_CHUNK, HGRN_SUB
    ncols = PROJ_WIDTH // GROUP_WIDTH
    proj2 = proj.reshape(t_len, bsz * PROJ_WIDTH)
    tile = pltpu.VMEM((L, GROUP_WIDTH), F32)
    y, st = pl.pallas_call(
        functools.partial(_hgrn_chunk_kernel, layer=layer),
        grid=(bsz, t_len // L),
        in_specs=[_seq_col_spec(L, COL_Q, ncols), _seq_col_spec(L, COL_F, ncols),
                  _seq_col_spec(L, COL_I, ncols), _seq_col_spec(L, COL_G, ncols),
                  pl.BlockSpec((DEPTH, GROUP_WIDTH), lambda b, c: (0, 0)),
                  pl.BlockSpec((1, GROUP_WIDTH), lambda b, c: (0, 0))],
        out_specs=[_seq_col_spec(L, 0, 1), pl.BlockSpec((GROUP_WIDTH, GROUP_WIDTH), lambda b, c: (b, 0))],
        out_shape=[jax.ShapeDtypeStruct((t_len, bsz * GROUP_WIDTH), BF16),
                   jax.ShapeDtypeStruct((bsz * GROUP_WIDTH, GROUP_WIDTH), F32)],
        scratch_shapes=[pltpu.VMEM((GROUP_WIDTH, GROUP_WIDTH), F32), tile, tile, tile,
                        pltpu.VMEM((L * n, GROUP_WIDTH), BF16), pltpu.VMEM((L * n, GROUP_WIDTH), F32), tile],
        compiler_params=_params(("parallel", "arbitrary")),
        name="hgrn_chunk_mixer",
    )(proj2, proj2, proj2, proj2, lb_logits, norm_g)
    return y.reshape(t_len * bsz, GROUP_WIDTH), st


RWKV_CHUNK = 64


def _rwkv_chunk_kernel(pr_ref, pk_ref, pv_ref, pl_ref, mu_ref, w0_ref, a0_ref, kk_ref, ka_ref, rk_ref, lng_ref,
                       lnb_ref, w2_ref, a2_ref, g2_ref, y_ref, st_ref, sh_ref, w_scr, prev_scr):
    L = RWKV_CHUNK
    gw = GROUP_WIDTH
    c = pl.program_id(1)

    @pl.when(c == 0)
    def _():
        w_scr[...] = jnp.zeros_like(w_scr)
        prev_scr[...] = jnp.zeros_like(prev_scr)

    rid = _iota((L, gw), 0)

    def shifted(ref, j):
        x = ref[...]
        prev = jnp.where(rid == 0, prev_scr[:, j * gw:(j + 1) * gw], pltpu.roll(x, 1, axis=0))
        prev_scr[:, j * gw:(j + 1) * gw] = x[L - 1:L]
        return x + (prev - x) * mu_ref[:, j * gw:(j + 1) * gw]

    xr, xk, xv, xl = shifted(pr_ref, 0), shifted(pk_ref, 1), shifted(pv_ref, 2), shifted(pl_ref, 3)
    w = -jax.nn.softplus(-(w0_ref[...] + _dot(jnp.tanh(xl).astype(BF16), w2_ref[...]))) - 0.5
    lw = -jnp.exp(w)
    a = jax.nn.sigmoid(a0_ref[...] + _dot(xl.astype(BF16), a2_ref[...]))
    g = _dot(jax.nn.sigmoid(xl).astype(BF16), g2_ref[...])
    kk = xk * kk_ref[...]
    kk = kk / jnp.maximum(jnp.sqrt(_head_sums(kk * kk)), 1e-12)
    k = xk * (1.0 + (a - 1.0) * ka_ref[...])
    beta = kk * a

    cs = _cumsum_rows(lw)
    c_end = cs[L - 1:L]
    e_neg = jnp.exp(-cs)
    e_end = jnp.exp(c_end - cs)
    ar = jnp.concatenate([_head_expand(-kk * jnp.exp(cs - lw)), _head_expand(xr * jnp.exp(cs))], axis=0).astype(BF16)
    bk = jnp.concatenate([_head_expand(beta * e_neg), _head_expand(k * e_neg)], axis=0).astype(BF16)
    vx = _head_expand(xv).astype(BF16)

    nh = 4 * L
    gmat = _dot_nt(ar, bk)
    tt = _iota((nh, nh), 0) & (L - 1)
    ss = _iota((nh, nh), 1) & (L - 1)
    strict, incl = ss < tt, ss <= tt
    nab = jnp.where(strict, gmat[0:nh, 0:nh], 0.0)
    nak = jnp.where(strict, gmat[0:nh, nh:2 * nh], 0.0).astype(BF16)
    nrb = jnp.where(incl, gmat[nh:2 * nh, 0:nh], 0.0).astype(BF16)
    nrk = jnp.where(incl, gmat[nh:2 * nh, nh:2 * nh], 0.0).astype(BF16)

    ri, ci = _iota((nh, nh), 0), _iota((nh, nh), 1)

    def same_block(size):
        sh = size.bit_length() - 1
        return lax.shift_right_logical(ri, sh) == lax.shift_right_logical(ci, sh)

    base = 8
    m = jnp.where(same_block(base), nab, 0.0)
    t_inv = jnp.where(ri == ci, 1.0, 0.0) + m
    m = m.astype(BF16)
    for _ in range(base.bit_length() - 2):
        m = _dot(m, m).astype(BF16)
        t_inv = t_inv + _dot(t_inv.astype(BF16), m)
    size = base
    while size < L:
        off = jnp.where(same_block(2 * size), jnp.where(same_block(size), 0.0, nab), 0.0).astype(BF16)
        tb = t_inv.astype(BF16)
        t_inv = t_inv + _dot(_dot(tb, off).astype(BF16), tb)
        size *= 2

    wst = w_scr[...]
    sw = _dot_nt(ar, wst.astype(BF16))
    x = _dot(t_inv.astype(BF16), (sw[0:nh] + _dot(nak, vx)).astype(BF16))
    ux = x.astype(BF16)
    yx = sw[nh:2 * nh] + _dot(nrb, ux) + _dot(nrk, vx)
    y = _head_collapse(yx)
    u = _head_collapse(x)

    upd = _dot_tn(jnp.concatenate([u, xv], axis=0).astype(BF16),
                  jnp.concatenate([beta * e_end, k * e_end], axis=0).astype(BF16))
    w_scr[...] = wst * jnp.exp(c_end) + jnp.where(_block_diag_mask(), upd, 0.0)

    mean = _head_sums(y) * (1.0 / HEAD)
    d = y - mean
    var = _head_sums(d * d) * (1.0 / HEAD)
    yn = d * lax.rsqrt(var + RWKV_GN_EPS) * lng_ref[...] + lnb_ref[...]
    bonus = _head_sums(xr * k * rk_ref[...]) * xv
    y_ref[...] = ((yn + bonus) * g).astype(y_ref.dtype)

    @pl.when(c == pl.num_programs(1) - 1)
    def _():
        st_ref[...] = w_scr[...]
        sh_ref[...] = prev_scr[...]


def rwkv_chunk_mixer(proj, mu, w0, a0, k_k, k_a, r_k, ln_g, ln_b, w2p, a2p, g2p, *, bsz, t_len):
    L = RWKV_CHUNK
    ncols = PROJ_WIDTH // GROUP_WIDTH
    proj2 = proj.reshape(t_len, bsz * PROJ_WIDTH)
    vec = pl.BlockSpec((1, GROUP_WIDTH), lambda b, c: (0, 0))
    mat = pl.BlockSpec((GROUP_WIDTH, GROUP_WIDTH), lambda b, c: (0, 0))
    y, st, sh = pl.pallas_call(
        _rwkv_chunk_kernel,
        grid=(bsz, t_len // L),
        in_specs=[_seq_col_spec(L, COL_R, ncols), _seq_col_spec(L, COL_K, ncols),
                  _seq_col_spec(L, COL_V, ncols), _seq_col_spec(L, COL_LORA, ncols),
                  pl.BlockSpec((1, RWKV_PROJ), lambda b, c: (0, 0)),
                  vec, vec, vec, vec, vec, vec, vec, mat, mat, mat],
        out_specs=[_seq_col_spec(L, 0, 1), pl.BlockSpec((GROUP_WIDTH, GROUP_WIDTH), lambda b, c: (b, 0)),
                   pl.BlockSpec((None, 1, RWKV_PROJ), lambda b, c: (b, 0, 0))],
        out_shape=[jax.ShapeDtypeStruct((t_len, bsz * GROUP_WIDTH), BF16),
                   jax.ShapeDtypeStruct((bsz * GROUP_WIDTH, GROUP_WIDTH), F32),
                   jax.ShapeDtypeStruct((bsz, 1, RWKV_PROJ), F32)],
        scratch_shapes=[pltpu.VMEM((GROUP_WIDTH, GROUP_WIDTH), F32), pltpu.VMEM((1, RWKV_PROJ), F32)],
        compiler_params=_params(("parallel", "arbitrary")),
        name="rwkv_chunk_mixer",
    )(proj2, proj2, proj2, proj2, mu, w0, a0, k_k, k_a, r_k, ln_g, ln_b, w2p, a2p, g2p)
    return y.reshape(t_len * bsz, GROUP_WIDTH), st, sh


def _diag_heads(st, bsz):
    nh = GROUP_WIDTH // HEAD
    s = st.reshape(bsz, nh, HEAD, nh, HEAD)
    return jnp.stack([s[:, h, :, h, :] for h in range(nh)], axis=1)


def _mix_mlp_kernel(h_ref, ya_ref, yb_ref, yc_ref, yd_ref, wo_ref, g2_ref, wu_ref, wd_ref, gf_ref, o_ref,
                    h1_scr, xn_scr, acc_scr, *, final_norm):
    j = pl.program_id(1)
    gw = GROUP_WIDTH

    @pl.when(j == 0)
    def _():
        mix = (_dot(ya_ref[...], wo_ref[0:gw]) + _dot(yb_ref[...], wo_ref[gw:2 * gw])
               + _dot(yc_ref[...], wo_ref[2 * gw:3 * gw]) + _dot(yd_ref[...], wo_ref[3 * gw:4 * gw]))
        h1 = h_ref[...] + mix
        h1_scr[...] = h1
        xn_scr[...] = _rms(h1, g2_ref[...]).astype(BF16)
        acc_scr[...] = jnp.zeros_like(acc_scr)

    up = _dot(xn_scr[...], wu_ref[...])
    act = jnp.square(jnp.maximum(up, 0.0)).astype(BF16)
    acc_scr[...] += _dot(act, wd_ref[...])

    @pl.when(j == pl.num_programs(1) - 1)
    def _():
        out = h1_scr[...] + acc_scr[...]
        if final_norm:
            out = _rms(out, gf_ref[...])
        o_ref[...] = out


def mix_mlp(h, ys, w_out, g2, w_up, w_down, g_final, *, final_norm):
    n = h.shape[0]
    tm = min(512, n)
    tf = 1024
    row = lambda w: pl.BlockSpec((tm, w), lambda i, j: (i, 0))
    return pl.pallas_call(
        functools.partial(_mix_mlp_kernel, final_norm=final_norm),
        grid=(n // tm, D_FF // tf),
        in_specs=[row(D_MODEL), row(GROUP_WIDTH), row(GROUP_WIDTH), row(GROUP_WIDTH), row(GROUP_WIDTH),
                  pl.BlockSpec((D_MODEL, D_MODEL), lambda i, j: (0, 0)),
                  pl.BlockSpec((1, D_MODEL), lambda i, j: (0, 0)),
                  pl.BlockSpec((D_MODEL, tf), lambda i, j: (0, j)),
                  pl.BlockSpec((tf, D_MODEL), lambda i, j: (j, 0)),
                  pl.BlockSpec((1, D_MODEL), lambda i, j: (0, 0))],
        out_specs=row(D_MODEL),
        out_shape=jax.ShapeDtypeStruct((n, D_MODEL), F32),
        scratch_shapes=[pltpu.VMEM((tm, D_MODEL), F32), pltpu.VMEM((tm, D_MODEL), BF16),
                        pltpu.VMEM((tm, D_MODEL), F32)],
        compiler_params=_params(("parallel", "arbitrary")),
        name="mix_mlp",
    )(h, *ys, w_out, g2, w_up, w_down, g_final)


def _hgrn_state_in(s):
    b = s.shape[0]
    return s.reshape(b, 2, 2, HEAD, HEAD).transpose(0, 1, 4, 2, 3).reshape(b, 2, HEAD, LANES)


def _hgrn_state_out(s):
    b = s.shape[0]
    return s.reshape(b, 2, HEAD, 2, HEAD).transpose(0, 1, 3, 4, 2).reshape(b, 4, HEAD, HEAD)


def _wkv_state_in(s):
    b = s.shape[0]
    return s.reshape(b, 2, 2, HEAD, HEAD).transpose(0, 1, 3, 2, 4).reshape(b, 2, HEAD, LANES)


def _wkv_state_out(s):
    b = s.shape[0]
    return s.reshape(b, 2, HEAD, 2, HEAD).transpose(0, 1, 3, 2, 4).reshape(b, 4, HEAD, HEAD)


def _block_diag(blocks):
    g, r, c = blocks.shape
    eye = jnp.eye(g, dtype=blocks.dtype)
    return (blocks[:, :, None, :] * eye[:, None, :, None]).reshape(g * r, g * c)


def _pad_rows(w, start):
    return jnp.zeros((GROUP_WIDTH, GROUP_WIDTH), w.dtype).at[start:start + w.shape[0]].set(w)


def _layer_params(l, P):
    row = lambda a: a.reshape(1, -1)
    q = {}
    q["norm1_g"] = row(P["norm1_g"][l])
    q["w_in"] = P["w_in"][l].astype(BF16)
    q["lam_re"] = row(P["ssm_lambda_re"][l])
    q["lam_im"] = row(P["ssm_lambda_im"][l])
    q["log_dt"] = row(jnp.repeat(P["ssm_log_dt"][l], SSM_STATE))
    q["b_re"] = _block_diag(P["ssm_b_re"][l].transpose(0, 2, 1))
    q["b_im"] = _block_diag(P["ssm_b_im"][l].transpose(0, 2, 1))
    q["c_cat"] = jnp.concatenate([_block_diag(P["ssm_c_re"][l].transpose(0, 2, 1)),
                                  -_block_diag(P["ssm_c_im"][l].transpose(0, 2, 1))], axis=0).astype(BF16)
    q["ssm_d"] = row(P["ssm_d"][l])
    q["glu_w"] = P["ssm_glu_w"][l].astype(BF16)
    q["glu_b"] = row(P["ssm_glu_b"][l])
    q["hgrn_norm_g"] = row(P["hgrn_norm_g"][l])
    q["mu"] = row(P["rwkv_mu"][l])
    for name in ("w0", "a0", "k_k", "k_a", "r_k", "ln_g", "ln_b"):
        q[name] = row(P["rwkv_" + name][l])
    q["w2p"] = _pad_rows(P["rwkv_w2"][l], 0).astype(BF16)
    q["a2p"] = _pad_rows(P["rwkv_a2"][l], DECAY_LORA).astype(BF16)
    q["g2p"] = _pad_rows(P["rwkv_g2"][l], DECAY_LORA + AAA_LORA).astype(BF16)
    q["pool_w"] = _block_diag(P["pool_w"][l]).astype(BF16)
    q["pool_scale"] = row(P["pool_scale"][l])
    q["w_out"] = P["w_out"][l].astype(BF16)
    q["norm2_g"] = row(P["norm2_g"][l])
    q["mlp_up"] = P["mlp_up"][l].astype(BF16)
    q["mlp_down"] = P["mlp_down"][l].astype(BF16)
    return q


def _trunk(x_rows, states, pos0, t_len, tc, layer_params, P, fresh):
    ssm_re0, ssm_im0, hgrn0, wkv0, shift0, pool0 = states
    bsz = ssm_re0[0].shape[0]
    h = x_rows
    new = [[] for _ in range(6)]
    g_final = P["norm_f_g"].reshape(1, -1)
    for l in range(DEPTH):
        q = layer_params[l]
        proj = rms_proj(h, q["norm1_g"], q["w_in"])
        y_a, s_re, s_im = s5_mixer(proj, ssm_re0[l], ssm_im0[l], q["lam_re"], q["lam_im"], q["log_dt"], q["b_re"],
                                   q["b_im"], q["c_cat"], q["ssm_d"], q["glu_w"], q["glu_b"], t_len=t_len, tc=tc)
        if fresh:
            y_b, s_hg = hgrn_chunk_mixer(proj, P["hgrn_lb_logits"], q["hgrn_norm_g"], bsz=bsz, t_len=t_len, layer=l)
            s_hg = _diag_heads(s_hg, bsz).swapaxes(-1, -2)
            y_c, s_wkv, s_sh = rwkv_chunk_mixer(proj, q["mu"], q["w0"], q["a0"], q["k_k"], q["k_a"], q["r_k"],
                                                q["ln_g"], q["ln_b"], q["w2p"], q["a2p"], q["g2p"],
                                                bsz=bsz, t_len=t_len)
            s_wkv = _diag_heads(s_wkv, bsz)
        else:
            y_b, s_hg = hgrn_mixer(proj, hgrn0[l], P["hgrn_lb_logits"], q["hgrn_norm_g"], t_len=t_len, tc=tc,
                                   layer=l)
            s_hg = _hgrn_state_out(s_hg)
            y_c, s_wkv, s_sh = rwkv_mixer(proj, shift0[l], wkv0[l], q["mu"], q["w0"], q["a0"], q["k_k"], q["k_a"],
                                          q["r_k"], q["ln_g"], q["ln_b"], q["w2p"], q["a2p"], q["g2p"],
                                          t_len=t_len, tc=tc)
            s_wkv = _wkv_state_out(s_wkv)
        y_d, s_pool = pool_mixer(proj, pool0[l], q["pool_w"], q["pool_scale"], t_len=t_len, tc=tc, pos0=pos0)
        h = mix_mlp(h, (y_a, y_b, y_c, y_d), q["w_out"], q["norm2_g"], q["mlp_up"], q["mlp_down"], g_final,
                    final_norm=(l == DEPTH - 1))
        for lst, s in zip(new, (s_re, s_im, s_hg, s_wkv, s_sh, s_pool)):
            lst.append(s)
    return h, new


def _states_out(new, bsz):
    s_re, s_im, s_hg, s_wkv, s_sh, s_pool = new
    return (jnp.stack([s.reshape(bsz, SSM_GROUPS, SSM_STATE) for s in s_re]),
            jnp.stack([s.reshape(bsz, SSM_GROUPS, SSM_STATE) for s in s_im]),
            jnp.stack(s_hg),
            jnp.stack(s_wkv),
            jnp.stack([s.reshape(bsz, 1, RWKV_PROJ) for s in s_sh]),
            jnp.stack([s.transpose(1, 0, 2) for s in s_pool]))


def kernel(x_prompt, x_sample, state_ssm_re, state_ssm_im, state_hgrn, state_wkv, state_shift, state_pool, norm1_g, w_in, ssm_lambda_re, ssm_lambda_im, ssm_log_dt, ssm_b_re, ssm_b_im, ssm_c_re, ssm_c_im, ssm_d, ssm_glu_w, ssm_glu_b, hgrn_lb_logits, hgrn_norm_g, rwkv_mu, rwkv_w0, rwkv_w2, rwkv_a0, rwkv_a2, rwkv_g2, rwkv_k_k, rwkv_k_a, rwkv_r_k, rwkv_ln_g, rwkv_ln_b, pool_w, pool_scale, w_out, norm2_g, mlp_up, mlp_down, norm_f_g):
    P = dict(norm1_g=norm1_g, w_in=w_in, ssm_lambda_re=ssm_lambda_re, ssm_lambda_im=ssm_lambda_im,
             ssm_log_dt=ssm_log_dt, ssm_b_re=ssm_b_re, ssm_b_im=ssm_b_im, ssm_c_re=ssm_c_re, ssm_c_im=ssm_c_im,
             ssm_d=ssm_d, ssm_glu_w=ssm_glu_w, ssm_glu_b=ssm_glu_b, hgrn_lb_logits=hgrn_lb_logits,
             hgrn_norm_g=hgrn_norm_g, rwkv_mu=rwkv_mu, rwkv_w0=rwkv_w0, rwkv_w2=rwkv_w2, rwkv_a0=rwkv_a0,
             rwkv_a2=rwkv_a2, rwkv_g2=rwkv_g2, rwkv_k_k=rwkv_k_k, rwkv_k_a=rwkv_k_a, rwkv_r_k=rwkv_r_k,
             rwkv_ln_g=rwkv_ln_g, rwkv_ln_b=rwkv_ln_b, pool_w=pool_w, pool_scale=pool_scale, w_out=w_out,
             norm2_g=norm2_g, mlp_up=mlp_up, mlp_down=mlp_down, norm_f_g=norm_f_g)
    layer_params = [_layer_params(l, P) for l in range(DEPTH)]

    bp, t_p, _ = x_prompt.shape
    xp = x_prompt.transpose(1, 0, 2).reshape(t_p * bp, D_MODEL)
    zeros = lambda *shape: [jnp.zeros(shape, F32) for _ in range(DEPTH)]
    st_p = (zeros(bp, SSM_FLAT), zeros(bp, SSM_FLAT), zeros(bp, 2, HEAD, LANES), zeros(bp, 2, HEAD, LANES),
            zeros(bp, RWKV_PROJ), zeros(POOL_BUF, bp, GROUP_WIDTH))
    yp, new_p = _trunk(xp, st_p, 0, t_p, 64, layer_params, P, True)
    y_prompt = yp.reshape(t_p, bp, D_MODEL).transpose(1, 0, 2)

    bs, t_s, _ = x_sample.shape
    nblk = bs // SEQ_BLK
    xs = x_sample.reshape(nblk, SEQ_BLK, t_s, D_MODEL).transpose(0, 2, 1, 3).reshape(bs * t_s, D_MODEL)
    st_s = ([state_ssm_re[l].reshape(bs, SSM_FLAT) for l in range(DEPTH)],
            [state_ssm_im[l].reshape(bs, SSM_FLAT) for l in range(DEPTH)],
            [_hgrn_state_in(state_hgrn[l]) for l in range(DEPTH)],
            [_wkv_state_in(state_wkv[l]) for l in range(DEPTH)],
            [state_shift[l].reshape(bs, RWKV_PROJ) for l in range(DEPTH)],
            [state_pool[l].transpose(1, 0, 2) for l in range(DEPTH)])
    ys, new_s = _trunk(xs, st_s, PAST_LEN, t_s, t_s, layer_params, P, False)
    y_sample = ys.reshape(nblk, t_s, SEQ_BLK, D_MODEL).transpose(0, 2, 1, 3).reshape(bs, t_s, D_MODEL)

    return (y_prompt, y_sample) + _states_out(new_p, bp) + _states_out(new_s, bs)
```

```python
import functools

import jax
import jax.numpy as jnp
from jax import lax
from jax.experimental import pallas as pl
from jax.experimental.pallas import tpu as pltpu

F32 = jnp.float32
BF16 = jnp.bfloat16

D_MODEL = 1024
DEPTH = 2
PAST_LEN = 16384
GROUP_WIDTH = 256
HEAD = 64
SSM_GROUPS = 16
SSM_CH = 16
SSM_STATE = 64
SSM_FLAT = SSM_GROUPS * SSM_STATE
POOL_WINDOWS = (2, 4, 8, 16)
POOL_BUF = 15
DECAY_LORA = 64
AAA_LORA = 64
GATE_LORA = 128
RWKV_PROJ = 1024
PROJ_WIDTH = 2560
D_FF = 4096
NORM_EPS = 1e-6
HGRN_NORM_EPS = 1e-5
RWKV_GN_EPS = 64e-5

SEQ_BLK = 8
LANES = 128
VMEM_LIMIT = 48 * 1024 * 1024

COL_SSM, COL_Q, COL_F, COL_I, COL_G, COL_R, COL_K, COL_V, COL_LORA, COL_POOL = range(10)


def _params(sem):
    return pltpu.CompilerParams(dimension_semantics=sem, vmem_limit_bytes=VMEM_LIMIT)


def _dot(a, b):
    return jnp.dot(a, b, preferred_element_type=F32)


def _rms(x, g):
    return x * lax.rsqrt(jnp.mean(x * x, axis=-1, keepdims=True) + NORM_EPS) * g


def _rms_proj_kernel(x_ref, g_ref, w_ref, o_ref):
    o_ref[...] = _dot(_rms(x_ref[...], g_ref[...]).astype(BF16), w_ref[...])


def _tile_map(nseq, nt):
    if nseq == 0:
        return lambda r, *_: (r, 0)
    return lambda r, *_: (r % nt, r // nt)


def rms_proj(x, g, w, *, nseq=0):
    n = x.shape[0]
    tm = min(512, n // max(nseq, 1))
    nt = n // tm // max(nseq, 1)
    out_shape = (n, PROJ_WIDTH) if nseq == 0 else (n // nseq, nseq * PROJ_WIDTH)
    return pl.pallas_call(
        _rms_proj_kernel,
        grid=(n // tm,),
        in_specs=[pl.BlockSpec((tm, D_MODEL), lambda i: (i, 0)),
                  pl.BlockSpec((1, D_MODEL), lambda i: (0, 0)),
                  pl.BlockSpec((D_MODEL, PROJ_WIDTH), lambda i: (0, 0))],
        out_specs=pl.BlockSpec((tm, PROJ_WIDTH), _tile_map(nseq, nt)),
        out_shape=jax.ShapeDtypeStruct(out_shape, F32),
        compiler_params=_params(("parallel",)),
        name="rms_proj",
    )(x, g, w)


def _row_spec(tc, col, nchunks):
    return pl.BlockSpec((tc * SEQ_BLK, GROUP_WIDTH), lambda s, c: (s * nchunks + c, col))


def _full_spec(shape):
    nd = len(shape)
    return pl.BlockSpec(shape, lambda s, c: (0,) * nd)


def _seq_spec(shape):
    nd = len(shape)
    return pl.BlockSpec((SEQ_BLK,) + shape[1:], lambda s, c: (s,) + (0,) * (nd - 1))


def _head_sums(x):
    lane = lax.broadcasted_iota(jnp.int32, x.shape, 1)
    out = jnp.zeros_like(x)
    for h in range(GROUP_WIDTH // HEAD):
        m = (lane >= h * HEAD) & (lane < (h + 1) * HEAD)
        s = jnp.sum(jnp.where(m, x, 0.0), axis=1, keepdims=True)
        out = jnp.where(m, s, out)
    return out


def _pair_masks():
    lane = lax.broadcasted_iota(jnp.int32, (HEAD, LANES), 1)
    sub = lax.broadcasted_iota(jnp.int32, (HEAD, LANES), 0)
    return lane < HEAD, (lane & (HEAD - 1)) == sub


def _seg_sum(p, lo):
    s0 = jnp.sum(jnp.where(lo, p, 0.0), axis=1, keepdims=True)
    s1 = jnp.sum(jnp.where(lo, 0.0, p), axis=1, keepdims=True)
    return jnp.where(lo, s0, s1)


def _col(row_b, lo, eye2):
    return _seg_sum(jnp.where(eye2, row_b, 0.0), lo)


def _row(col_b, eye2):
    return jnp.sum(jnp.where(eye2, col_b, 0.0), axis=0, keepdims=True)


def _bcast(ref, t, b, p):
    r = ref[t, b:b + 1, p * LANES:(p + 1) * LANES]
    return jnp.broadcast_to(r, (HEAD, LANES))


def _s5_kernel(u_ref, h0r_ref, h0i_ref, lr_ref, li_ref, ldt_ref, bre_ref, bim_ref, ccat_ref, d_ref,
               gw_ref, gb_ref, y_ref, hr_out, hi_out, h_scr, bu_scr, *, tc):
    c = pl.program_id(1)

    @pl.when(c == 0)
    def _():
        h_scr[0] = h0r_ref[...]
        h_scr[1] = h0i_ref[...]

    lr, li = lr_ref[...], li_ref[...]
    dt = jnp.exp(ldt_ref[...])
    mag = jnp.exp(lr * dt)
    ab_re, ab_im = mag * jnp.cos(li * dt), mag * jnp.sin(li * dt)
    den = lr * lr + li * li
    zr, zi = ab_re - 1.0, ab_im
    cr = (zr * lr + zi * li) / den
    ci = (zi * lr - zr * li) / den
    bre, bim = bre_ref[...], bim_ref[...]
    bb_re = (cr * bre - ci * bim).astype(BF16)
    bb_im = (cr * bim + ci * bre).astype(BF16)

    u = u_ref[...]
    ub = u.astype(BF16)
    bu_scr[:, 0:SSM_FLAT] = _dot(ub, bb_re)
    bu_scr[:, SSM_FLAT:2 * SSM_FLAT] = _dot(ub, bb_im)

    ar = jnp.broadcast_to(ab_re, (SEQ_BLK, SSM_FLAT))
    ai = jnp.broadcast_to(ab_im, (SEQ_BLK, SSM_FLAT))

    def step(t, carry):
        hr, hi = carry
        rows = pl.ds(pl.multiple_of(t * SEQ_BLK, SEQ_BLK), SEQ_BLK)
        nhr = ar * hr - ai * hi + bu_scr[rows, 0:SSM_FLAT]
        nhi = ar * hi + ai * hr + bu_scr[rows, SSM_FLAT:2 * SSM_FLAT]
        bu_scr[rows, 0:SSM_FLAT] = nhr
        bu_scr[rows, SSM_FLAT:2 * SSM_FLAT] = nhi
        return nhr, nhi

    hr, hi = lax.fori_loop(0, tc, step, (h_scr[0], h_scr[1]))
    h_scr[0] = hr
    h_scr[1] = hi

    y = _dot(bu_scr[...].astype(BF16), ccat_ref[...]) + d_ref[...] * u
    z = jax.nn.gelu(y)
    out = z * jax.nn.sigmoid(_dot(z.astype(BF16), gw_ref[...]) + gb_ref[...])
    y_ref[...] = out.astype(y_ref.dtype)

    @pl.when(c == pl.num_programs(1) - 1)
    def _():
        hr_out[...] = hr
        hi_out[...] = hi


def s5_mixer(proj, h0_re, h0_im, lam_re, lam_im, log_dt, b_re_bd, b_im_bd, c_cat, d_skip, glu_w, glu_b, *, t_len, tc):
    nseq = h0_re.shape[0] // SEQ_BLK
    nchunks = t_len // tc
    n = proj.shape[0]
    st = jax.ShapeDtypeStruct(h0_re.shape, F32)
    return pl.pallas_call(
        functools.partial(_s5_kernel, tc=tc),
        grid=(nseq, nchunks),
        in_specs=[_row_spec(tc, COL_SSM, nchunks),
                  _seq_spec(h0_re.shape), _seq_spec(h0_im.shape),
                  _full_spec((1, SSM_FLAT)), _full_spec((1, SSM_FLAT)), _full_spec((1, SSM_FLAT)),
                  _full_spec((GROUP_WIDTH, SSM_FLAT)), _full_spec((GROUP_WIDTH, SSM_FLAT)),
                  _full_spec((2 * SSM_FLAT, GROUP_WIDTH)), _full_spec((1, GROUP_WIDTH)),
                  _full_spec((GROUP_WIDTH, GROUP_WIDTH)), _full_spec((1, GROUP_WIDTH))],
        out_specs=[_row_spec(tc, 0, nchunks), _seq_spec(h0_re.shape), _seq_spec(h0_im.shape)],
        out_shape=[jax.ShapeDtypeStruct((n, GROUP_WIDTH), BF16), st, st],
        scratch_shapes=[pltpu.VMEM((2, SEQ_BLK, SSM_FLAT), F32),
                        pltpu.VMEM((tc * SEQ_BLK, 2 * SSM_FLAT), F32)],
        compiler_params=_params(("parallel", "arbitrary")),
        name="s5_mixer",
    )(proj, h0_re, h0_im, lam_re, lam_im, log_dt, b_re_bd, b_im_bd, c_cat, d_skip, glu_w, glu_b)


def _pool_kernel(u_ref, buf_ref, w_ref, sc_ref, y_ref, nbuf_ref, ext_scr, *, tc, pos0):
    c = pl.program_id(1)

    @pl.when(c == 0)
    def _():
        ext_scr[0:POOL_BUF] = buf_ref[...]

    u = u_ref[...].reshape(tc, SEQ_BLK, GROUP_WIDTH)
    ext_scr[POOL_BUF:POOL_BUF + tc] = u
    a1 = ext_scr[...]
    a2 = a1[1:] + a1[:-1]
    a4 = a2[2:] + a2[:-2]
    a8 = a4[4:] + a4[:-4]
    a16 = a8[8:] + a8[:-8]
    sums = (a2[14:], a4[12:], a8[8:], a16)

    shape = (tc, SEQ_BLK, GROUP_WIDTH)
    pos = lax.broadcasted_iota(jnp.int32, shape, 0) + (c * tc + pos0)
    lane = lax.broadcasted_iota(jnp.int32, shape, 2)
    pooled = None
    for gi in reversed(range(len(POOL_WINDOWS))):
        win = POOL_WINDOWS[gi]
        mean = sums[gi] / jnp.minimum(pos + 1, win).astype(F32)
        pooled = mean if pooled is None else jnp.where(lane < (gi + 1) * HEAD, mean, pooled)
    pooled = (pooled - u).reshape(tc * SEQ_BLK, GROUP_WIDTH)
    y_ref[...] = (_dot(pooled.astype(BF16), w_ref[...]) * sc_ref[...]).astype(y_ref.dtype)

    nb = ext_scr[tc:tc + POOL_BUF]
    ext_scr[0:POOL_BUF] = nb

    @pl.when(c == pl.num_programs(1) - 1)
    def _():
        nbuf_ref[...] = nb


def pool_mixer(proj, buf, w_bd, scale, *, t_len, tc, pos0):
    nseq = buf.shape[1] // SEQ_BLK
    nchunks = t_len // tc
    n = proj.shape[0]
    buf_spec = pl.BlockSpec((POOL_BUF, SEQ_BLK, GROUP_WIDTH), lambda s, c: (0, s, 0))
    return pl.pallas_call(
        functools.partial(_pool_kernel, tc=tc, pos0=pos0),
        grid=(nseq, nchunks),
        in_specs=[_row_spec(tc, COL_POOL, nchunks), buf_spec,
                  _full_spec((GROUP_WIDTH, GROUP_WIDTH)), _full_spec((1, GROUP_WIDTH))],
        out_specs=[_row_spec(tc, 0, nchunks), buf_spec],
        out_shape=[jax.ShapeDtypeStruct((n, GROUP_WIDTH), BF16), jax.ShapeDtypeStruct(buf.shape, F32)],
        scratch_shapes=[pltpu.VMEM((tc + POOL_BUF, SEQ_BLK, GROUP_WIDTH), F32)],
        compiler_params=_params(("parallel", "arbitrary")),
        name="pool_mixer",
    )(proj, buf, w_bd, scale)


def _hgrn_lower_bound(logits_ref, layer):
    rows = [logits_ref[l:l + 1, :] for l in range(DEPTH)]
    m = functools.reduce(jnp.maximum, rows)
    es = [jnp.exp(r - m) for r in rows]
    tot = functools.reduce(lambda a, b: a + b, es)
    lb = jnp.zeros_like(m)
    for l in range(1, layer + 1):
        lb = lb + es[l] / tot
    return lb


def _hgrn_kernel(pq_ref, pf_ref, pi_ref, pg_ref, s0_ref, lbl_ref, ng_ref, y_ref, st_ref,
                 s_scr, q_scr, f_scr, k_scr, v_scr, o_scr, *, tc, layer):
    c = pl.program_id(1)
    shape3 = (tc, SEQ_BLK, GROUP_WIDTH)

    @pl.when(c == 0)
    def _():
        s_scr[...] = s0_ref[...]

    lb = _hgrn_lower_bound(lbl_ref, layer)
    zf = pf_ref[...]
    f_scr[...] = (lb + (1.0 - lb) * jax.nn.sigmoid(zf)).reshape(shape3)
    k_scr[...] = ((1.0 - lb) * jax.nn.sigmoid(-zf)).reshape(shape3)
    q_scr[...] = jax.nn.silu(pq_ref[...]).reshape(shape3)
    v_scr[...] = pi_ref[...].reshape(shape3)

    lo, eye2 = _pair_masks()

    def step(t, carry):
        for b in range(SEQ_BLK):
            for p in range(2):
                s = s_scr[b, p]
                vcol = _col(_bcast(v_scr, t, b, p), lo, eye2)
                s = s * _bcast(f_scr, t, b, p) + vcol * _bcast(k_scr, t, b, p)
                s_scr[b, p] = s
                ocol = _seg_sum(s * _bcast(q_scr, t, b, p), lo)
                o_scr[t, b:b + 1, p * LANES:(p + 1) * LANES] = _row(ocol, eye2)
        return carry

    lax.fori_loop(0, tc, step, 0)

    o = o_scr[...].reshape(tc * SEQ_BLK, GROUP_WIDTH)
    ms = _head_sums(o * o) * (1.0 / HEAD)
    out = o * lax.rsqrt(ms + HGRN_NORM_EPS) * ng_ref[...] * jax.nn.silu(pg_ref[...])
    y_ref[...] = out.astype(y_ref.dtype)

    @pl.when(c == pl.num_programs(1) - 1)
    def _():
        st_ref[...] = s_scr[...]


def hgrn_mixer(proj, s0, lb_logits, norm_g, *, t_len, tc, layer):
    nseq = s0.shape[0] // SEQ_BLK
    nchunks = t_len // tc
    n = proj.shape[0]
    tile = pltpu.VMEM((tc, SEQ_BLK, GROUP_WIDTH), F32)
    return pl.pallas_call(
        functools.partial(_hgrn_kernel, tc=tc, layer=layer),
        grid=(nseq, nchunks),
        in_specs=[_row_spec(tc, COL_Q, nchunks), _row_spec(tc, COL_F, nchunks),
                  _row_spec(tc, COL_I, nchunks), _row_spec(tc, COL_G, nchunks),
                  _seq_spec(s0.shape), _full_spec((DEPTH, GROUP_WIDTH)), _full_spec((1, GROUP_WIDTH))],
        out_specs=[_row_spec(tc, 0, nchunks), _seq_spec(s0.shape)],
        out_shape=[jax.ShapeDtypeStruct((n, GROUP_WIDTH), BF16), jax.ShapeDtypeStruct(s0.shape, F32)],
        scratch_shapes=[pltpu.VMEM((SEQ_BLK, 2, HEAD, LANES), F32), tile, tile, tile, tile, tile],
        compiler_params=_params(("parallel", "arbitrary")),
        name="hgrn_mixer",
    )(proj, proj, proj, proj, s0, lb_logits, norm_g)


def _rwkv_kernel(pr_ref, pk_ref, pv_ref, pl_ref, sh0_ref, s0_ref, mu_ref, w0_ref, a0_ref, kk_ref, ka_ref,
                 rk_ref, lng_ref, lnb_ref, w2_ref, a2_ref, g2_ref, y_ref, st_ref, sh_ref,
                 s_scr, prev_scr, r_scr, w_scr, k_scr, v_scr, nkk_scr, kka_scr, o_scr, *, tc):
    c = pl.program_id(1)
    shape3 = (tc, SEQ_BLK, GROUP_WIDTH)
    gw = GROUP_WIDTH

    @pl.when(c == 0)
    def _():
        s_scr[...] = s0_ref[...]
        prev_scr[...] = sh0_ref[...]

    def shifted(ref, j):
        x = ref[...].reshape(shape3)
        first = prev_scr[:, j * gw:(j + 1) * gw].reshape(1, SEQ_BLK, gw)
        prev = first if tc == 1 else jnp.concatenate([first, x[:-1]], axis=0)
        prev_scr[:, j * gw:(j + 1) * gw] = x[tc - 1]
        return (x + (prev - x) * mu_ref[:, j * gw:(j + 1) * gw]).reshape(tc * SEQ_BLK, gw)

    xr, xk, xv, xl = shifted(pr_ref, 0), shifted(pk_ref, 1), shifted(pv_ref, 2), shifted(pl_ref, 3)
    w = -jax.nn.softplus(-(w0_ref[...] + _dot(jnp.tanh(xl).astype(BF16), w2_ref[...]))) - 0.5
    decay = jnp.exp(-jnp.exp(w))
    a = jax.nn.sigmoid(a0_ref[...] + _dot(xl.astype(BF16), a2_ref[...]))
    g = _dot(jax.nn.sigmoid(xl).astype(BF16), g2_ref[...])
    kk = xk * kk_ref[...]
    kk = kk / jnp.maximum(jnp.sqrt(_head_sums(kk * kk)), 1e-12)
    k = xk * (1.0 + (a - 1.0) * ka_ref[...])

    r_scr[...] = xr.reshape(shape3)
    w_scr[...] = decay.reshape(shape3)
    k_scr[...] = k.reshape(shape3)
    v_scr[...] = xv.reshape(shape3)
    nkk_scr[...] = (-kk).reshape(shape3)
    kka_scr[...] = (kk * a).reshape(shape3)

    lo, eye2 = _pair_masks()

    def step(t, carry):
        for b in range(SEQ_BLK):
            for p in range(2):
                s = s_scr[b, p]
                sa = _seg_sum(s * _bcast(nkk_scr, t, b, p), lo)
                vcol = _col(_bcast(v_scr, t, b, p), lo, eye2)
                s = s * _bcast(w_scr, t, b, p) + sa * _bcast(kka_scr, t, b, p) + vcol * _bcast(k_scr, t, b, p)
                s_scr[b, p] = s
                ycol = _seg_sum(s * _bcast(r_scr, t, b, p), lo)
                o_scr[t, b:b + 1, p * LANES:(p + 1) * LANES] = _row(ycol, eye2)
        return carry

    lax.fori_loop(0, tc, step, 0)

    y = o_scr[...].reshape(tc * SEQ_BLK, gw)
    mean = _head_sums(y) * (1.0 / HEAD)
    d = y - mean
    var = _head_sums(d * d) * (1.0 / HEAD)
    yn = d * lax.rsqrt(var + RWKV_GN_EPS) * lng_ref[...] + lnb_ref[...]
    bonus = _head_sums(xr * k * rk_ref[...]) * xv
    y_ref[...] = ((yn + bonus) * g).astype(y_ref.dtype)

    @pl.when(c == pl.num_programs(1) - 1)
    def _():
        st_ref[...] = s_scr[...]
        sh_ref[...] = prev_scr[...]


def rwkv_mixer(proj, shift0, s0, mu, w0, a0, k_k, k_a, r_k, ln_g, ln_b, w2p, a2p, g2p, *, t_len, tc):
    nseq = s0.shape[0] // SEQ_BLK
    nchunks = t_len // tc
    n = proj.shape[0]
    tile = pltpu.VMEM((tc, SEQ_BLK, GROUP_WIDTH), F32)
    vec = _full_spec((1, GROUP_WIDTH))
    mat = _full_spec((GROUP_WIDTH, GROUP_WIDTH))
    return pl.pallas_call(
        functools.partial(_rwkv_kernel, tc=tc),
        grid=(nseq, nchunks),
        in_specs=[_row_spec(tc, COL_R, nchunks), _row_spec(tc, COL_K, nchunks),
                  _row_spec(tc, COL_V, nchunks), _row_spec(tc, COL_LORA, nchunks),
                  _seq_spec(shift0.shape), _seq_spec(s0.shape), _full_spec((1, RWKV_PROJ)),
                  vec, vec, vec, vec, vec, vec, vec, mat, mat, mat],
        out_specs=[_row_spec(tc, 0, nchunks), _seq_spec(s0.shape), _seq_spec(shift0.shape)],
        out_shape=[jax.ShapeDtypeStruct((n, GROUP_WIDTH), BF16), jax.ShapeDtypeStruct(s0.shape, F32),
                   jax.ShapeDtypeStruct(shift0.shape, F32)],
        scratch_shapes=[pltpu.VMEM((SEQ_BLK, 2, HEAD, LANES), F32), pltpu.VMEM((SEQ_BLK, RWKV_PROJ), F32),
                        tile, tile, tile, tile, tile, tile, tile],
        compiler_params=_params(("parallel", "arbitrary")),
        name="rwkv_mixer",
    )(proj, proj, proj, proj, shift0, s0, mu, w0, a0, k_k, k_a, r_k, ln_g, ln_b, w2p, a2p, g2p)


def _dot_nt(a, b):
    return lax.dot_general(a, b, (((1,), (1,)), ((), ())), preferred_element_type=F32)


def _dot_tn(a, b):
    return lax.dot_general(a, b, (((0,), (0,)), ((), ())), preferred_element_type=F32)


def _iota(shape, dim):
    return lax.broadcasted_iota(jnp.int32, shape, dim)


def _head_of(idx):
    return lax.shift_right_logical(idx, HEAD.bit_length() - 1)


def _cumsum_rows(x):
    n = x.shape[0]
    tri = jnp.where(_iota((n, n), 0) >= _iota((n, n), 1), 1.0, 0.0).astype(BF16)
    hi = x.astype(BF16)
    rest = x - hi.astype(F32)
    mid = rest.astype(BF16)
    lo = (rest - mid.astype(F32)).astype(BF16)
    return _dot(tri, hi) + _dot(tri, mid) + _dot(tri, lo)


def _own_head(shape, rows_per_head):
    row_h = lax.shift_right_logical(_iota(shape, 0), rows_per_head.bit_length() - 1)
    return row_h == _head_of(_iota(shape, 1))


def _head_expand(x):
    xx = jnp.concatenate([x] * (GROUP_WIDTH // HEAD), axis=0)
    return jnp.where(_own_head(xx.shape, x.shape[0]), xx, 0.0)


def _head_collapse(xx):
    n = xx.shape[0] // (GROUP_WIDTH // HEAD)
    return xx[0:n] + xx[n:2 * n] + xx[2 * n:3 * n] + xx[3 * n:4 * n]


def _block_diag_mask():
    shape = (GROUP_WIDTH, GROUP_WIDTH)
    return _head_of(_iota(shape, 0)) == _head_of(_iota(shape, 1))


def _seq_col_spec(rows, col, ncols):
    return pl.BlockSpec((rows, GROUP_WIDTH), lambda b, c: (c, b * ncols + col))


HGRN_CHUNK = 128
HGRN_SUB = 16


def _hgrn_chunk_kernel(pq_ref, pf_ref, pi_ref, pg_ref, lbl_ref, ng_ref, y_ref, st_ref,
                       w_scr, k_scr, b_scr, v_scr, p_scr, r_scr, o_scr, *, layer):
    L, n = HGRN_CHUNK, HGRN_SUB
    c = pl.program_id(1)

    @pl.when(c == 0)
    def _():
        w_scr[...] = jnp.zeros_like(w_scr)

    lb = _hgrn_lower_bound(lbl_ref, layer)
    z = pf_ref[...]
    g = jnp.logaddexp(jnp.log1p(-lb) + jax.nn.log_sigmoid(z), jnp.log(lb))
    kg = (1.0 - lb) * jax.nn.sigmoid(-z)
    q = jax.nn.silu(pq_ref[...])
    v = pi_ref[...]
    bc = _cumsum_rows(g)
    k_scr[...] = kg
    b_scr[...] = bc
    v_scr[...] = v

    rid = _iota((n, GROUP_WIDTH), 0)
    for sb in range(L // n):
        base = sb * n
        qs, bs = q[base:base + n], bc[base:base + n]
        for s in range(n):
            e = jnp.where(rid >= s, jnp.exp(bs - b_scr[base + s:base + s + 1, :]), 0.0)
            p_scr[(base + s) * n:(base + s + 1) * n, :] = (qs * k_scr[base + s:base + s + 1, :] * e).astype(BF16)
    bd = _block_diag_mask()
    r_scr[...] = _dot(p_scr[...], jnp.where(bd, 1.0, 0.0).astype(BF16))
    for sb in range(L // n):
        base = sb * n
        acc = jnp.zeros((n, GROUP_WIDTH), F32)
        for s in range(n):
            acc = acc + r_scr[(base + s) * n:(base + s + 1) * n, :] * v_scr[base + s:base + s + 1, :]
        o_scr[base:base + n, :] = acc

    vb = v.astype(BF16)
    rows = _iota((L, GROUP_WIDTH), 0)
    for i in range(1, L // n):
        r0 = i * n
        ref = b_scr[r0 - 1:r0, :]
        qt = q[r0:r0 + n] * jnp.exp(bc[r0:r0 + n] - ref)
        kt = jnp.where(rows < r0, kg * jnp.exp(ref - bc), 0.0)
        att = _dot_nt(_head_expand(qt).astype(BF16), kt.astype(BF16))
        ox = _dot(att.astype(BF16), vb)
        o_scr[r0:r0 + n, :] += _head_collapse(jnp.where(_own_head(ox.shape, n), ox, 0.0))

    w = w_scr[...]
    o = o_scr[...] + _dot_nt((q * jnp.exp(bc)).astype(BF16), w.astype(BF16))
    b_end = b_scr[L - 1:L, :]
    upd = _dot_tn(vb, (kg * jnp.exp(b_end - bc)).astype(BF16))
    w_scr[...] = w * jnp.exp(b_end) + jnp.where(bd, upd, 0.0)

    ms = _head_sums(o * o) * (1.0 / HEAD)
    out = o * lax.rsqrt(ms + HGRN_NORM_EPS) * ng_ref[...] * jax.nn.silu(pg_ref[...])
    y_ref[...] = out.astype(y_ref.dtype)

    @pl.when(c == pl.num_programs(1) - 1)
    def _():
        st_ref[...] = w_scr[...]


def hgrn_chunk_mixer(proj2, lb_logits, norm_g, *, bsz, t_len, layer):
    L, n = HGRN_CHUNK, HGRN_SUB
    ncols = PROJ_WIDTH // GROUP_WIDTH
    tile = pltpu.VMEM((L, GROUP_WIDTH), F32)
    y, st = pl.pallas_call(
        functools.partial(_hgrn_chunk_kernel, layer=layer),
        grid=(bsz, t_len // L),
        in_specs=[_seq_col_spec(L, COL_Q, ncols), _seq_col_spec(L, COL_F, ncols),
                  _seq_col_spec(L, COL_I, ncols), _seq_col_spec(L, COL_G, ncols),
                  pl.BlockSpec((DEPTH, GROUP_WIDTH), lambda b, c: (0, 0)),
                  pl.BlockSpec((1, GROUP_WIDTH), lambda b, c: (0, 0))],
        out_specs=[_seq_col_spec(L, 0, 1), pl.BlockSpec((GROUP_WIDTH, GROUP_WIDTH), lambda b, c: (b, 0))],
        out_shape=[jax.ShapeDtypeStruct((t_len, bsz * GROUP_WIDTH), BF16),
                   jax.ShapeDtypeStruct((bsz * GROUP_WIDTH, GROUP_WIDTH), F32)],
        scratch_shapes=[pltpu.VMEM((GROUP_WIDTH, GROUP_WIDTH), F32), tile, tile, tile,
                        pltpu.VMEM((L * n, GROUP_WIDTH), BF16), pltpu.VMEM((L * n, GROUP_WIDTH), F32), tile],
        compiler_params=_params(("parallel", "arbitrary")),
        name="hgrn_chunk_mixer",
    )(proj2, proj2, proj2, proj2, lb_logits, norm_g)
    return y, st


RWKV_CHUNK = 64


def _rwkv_chunk_kernel(pr_ref, pk_ref, pv_ref, pl_ref, mu_ref, w0_ref, a0_ref, kk_ref, ka_ref, rk_ref, lng_ref,
                       lnb_ref, w2_ref, a2_ref, g2_ref, y_ref, st_ref, sh_ref, w_scr, prev_scr):
    L = RWKV_CHUNK
    gw = GROUP_WIDTH
    c = pl.program_id(1)

    @pl.when(c == 0)
    def _():
        w_scr[...] = jnp.zeros_like(w_scr)
        prev_scr[...] = jnp.zeros_like(prev_scr)

    rid = _iota((L, gw), 0)

    def shifted(ref, j):
        x = ref[...]
        prev = jnp.where(rid == 0, prev_scr[:, j * gw:(j + 1) * gw], pltpu.roll(x, 1, axis=0))
        prev_scr[:, j * gw:(j + 1) * gw] = x[L - 1:L]
        return x + (prev - x) * mu_ref[:, j * gw:(j + 1) * gw]

    xr, xk, xv, xl = shifted(pr_ref, 0), shifted(pk_ref, 1), shifted(pv_ref, 2), shifted(pl_ref, 3)
    w = -jax.nn.softplus(-(w0_ref[...] + _dot(jnp.tanh(xl).astype(BF16), w2_ref[...]))) - 0.5
    lw = -jnp.exp(w)
    a = jax.nn.sigmoid(a0_ref[...] + _dot(xl.astype(BF16), a2_ref[...]))
    g = _dot(jax.nn.sigmoid(xl).astype(BF16), g2_ref[...])
    kk = xk * kk_ref[...]
    kk = kk / jnp.maximum(jnp.sqrt(_head_sums(kk * kk)), 1e-12)
    k = xk * (1.0 + (a - 1.0) * ka_ref[...])
    beta = kk * a

    cs = _cumsum_rows(lw)
    c_end = cs[L - 1:L]
    e_neg = jnp.exp(-cs)
    e_end = jnp.exp(c_end - cs)
    ar = jnp.concatenate([_head_expand(-kk * jnp.exp(cs - lw)), _head_expand(xr * jnp.exp(cs))], axis=0).astype(BF16)
    bk = jnp.concatenate([_head_expand(beta * e_neg), _head_expand(k * e_neg)], axis=0).astype(BF16)
    vx = _head_expand(xv).astype(BF16)

    nh = 4 * L
    gmat = _dot_nt(ar, bk)
    tt = _iota((nh, nh), 0) & (L - 1)
    ss = _iota((nh, nh), 1) & (L - 1)
    strict, incl = ss < tt, ss <= tt
    nab = jnp.where(strict, gmat[0:nh, 0:nh], 0.0)
    nak = jnp.where(strict, gmat[0:nh, nh:2 * nh], 0.0).astype(BF16)
    nrb = jnp.where(incl, gmat[nh:2 * nh, 0:nh], 0.0).astype(BF16)
    nrk = jnp.where(incl, gmat[nh:2 * nh, nh:2 * nh], 0.0).astype(BF16)

    ri, ci = _iota((nh, nh), 0), _iota((nh, nh), 1)

    def same_block(size):
        sh = size.bit_length() - 1
        return lax.shift_right_logical(ri, sh) == lax.shift_right_logical(ci, sh)

    base = 8
    m = jnp.where(same_block(base), nab, 0.0)
    t_inv = jnp.where(ri == ci, 1.0, 0.0) + m
    m = m.astype(BF16)
    for _ in range(base.bit_length() - 2):
        m = _dot(m, m).astype(BF16)
        t_inv = t_inv + _dot(t_inv.astype(BF16), m)
    size = base
    while size < L:
        off = jnp.where(same_block(2 * size), jnp.where(same_block(size), 0.0, nab), 0.0).astype(BF16)
        tb = t_inv.astype(BF16)
        t_inv = t_inv + _dot(_dot(tb, off).astype(BF16), tb)
        size *= 2

    wst = w_scr[...]
    sw = _dot_nt(ar, wst.astype(BF16))
    x = _dot(t_inv.astype(BF16), (sw[0:nh] + _dot(nak, vx)).astype(BF16))
    ux = x.astype(BF16)
    yx = sw[nh:2 * nh] + _dot(nrb, ux) + _dot(nrk, vx)
    y = _head_collapse(yx)
    u = _head_collapse(x)

    upd = _dot_tn(jnp.concatenate([u, xv], axis=0).astype(BF16),
                  jnp.concatenate([beta * e_end, k * e_end], axis=0).astype(BF16))
    w_scr[...] = wst * jnp.exp(c_end) + jnp.where(_block_diag_mask(), upd, 0.0)

    mean = _head_sums(y) * (1.0 / HEAD)
    d = y - mean
    var = _head_sums(d * d) * (1.0 / HEAD)
    yn = d * lax.rsqrt(var + RWKV_GN_EPS) * lng_ref[...] + lnb_ref[...]
    bonus = _head_sums(xr * k * rk_ref[...]) * xv
    y_ref[...] = ((yn + bonus) * g).astype(y_ref.dtype)

    @pl.when(c == pl.num_programs(1) - 1)
    def _():
        st_ref[...] = w_scr[...]
        sh_ref[...] = prev_scr[...]


def rwkv_chunk_mixer(proj2, mu, w0, a0, k_k, k_a, r_k, ln_g, ln_b, w2p, a2p, g2p, *, bsz, t_len):
    L = RWKV_CHUNK
    ncols = PROJ_WIDTH // GROUP_WIDTH
    vec = pl.BlockSpec((1, GROUP_WIDTH), lambda b, c: (0, 0))
    mat = pl.BlockSpec((GROUP_WIDTH, GROUP_WIDTH), lambda b, c: (0, 0))
    y, st, sh = pl.pallas_call(
        _rwkv_chunk_kernel,
        grid=(bsz, t_len // L),
        in_specs=[_seq_col_spec(L, COL_R, ncols), _seq_col_spec(L, COL_K, ncols),
                  _seq_col_spec(L, COL_V, ncols), _seq_col_spec(L, COL_LORA, ncols),
                  pl.BlockSpec((1, RWKV_PROJ), lambda b, c: (0, 0)),
                  vec, vec, vec, vec, vec, vec, vec, mat, mat, mat],
        out_specs=[_seq_col_spec(L, 0, 1), pl.BlockSpec((GROUP_WIDTH, GROUP_WIDTH), lambda b, c: (b, 0)),
                   pl.BlockSpec((None, 1, RWKV_PROJ), lambda b, c: (b, 0, 0))],
        out_shape=[jax.ShapeDtypeStruct((t_len, bsz * GROUP_WIDTH), BF16),
                   jax.ShapeDtypeStruct((bsz * GROUP_WIDTH, GROUP_WIDTH), F32),
                   jax.ShapeDtypeStruct((bsz, 1, RWKV_PROJ), F32)],
        scratch_shapes=[pltpu.VMEM((GROUP_WIDTH, GROUP_WIDTH), F32), pltpu.VMEM((1, RWKV_PROJ), F32)],
        compiler_params=_params(("parallel", "arbitrary")),
        name="rwkv_chunk_mixer",
    )(proj2, proj2, proj2, proj2, mu, w0, a0, k_k, k_a, r_k, ln_g, ln_b, w2p, a2p, g2p)
    return y, st, sh


S5_CHUNK = 64


def _s5_seq_kernel(*refs, nseq):
    u_refs = refs[:nseq]
    lr_ref, li_ref, ldt_ref, bre_ref, bim_ref, ccat_ref, d_ref, gw_ref, gb_ref = refs[nseq:nseq + 9]
    y_ref, hr_out, hi_out, h_scr, bu_scr = refs[nseq + 9:]
    tc = S5_CHUNK
    n = nseq * tc
    c = pl.program_id(0)

    @pl.when(c == 0)
    def _():
        h_scr[...] = jnp.zeros_like(h_scr)

    lr, li = lr_ref[...], li_ref[...]
    dt = jnp.exp(ldt_ref[...])
    mag = jnp.exp(lr * dt)
    ab_re, ab_im = mag * jnp.cos(li * dt), mag * jnp.sin(li * dt)
    den = lr * lr + li * li
    zr, zi = ab_re - 1.0, ab_im
    cr = (zr * lr + zi * li) / den
    ci = (zi * lr - zr * li) / den
    bre, bim = bre_ref[...], bim_ref[...]
    bb_re = (cr * bre - ci * bim).astype(BF16)
    bb_im = (cr * bim + ci * bre).astype(BF16)

    u = jnp.concatenate([r[...] for r in u_refs], axis=0)
    ri, cj = _iota((n, n), 0), _iota((n, n), 1)
    lseq, lt = nseq.bit_length() - 1, tc.bit_length() - 1
    to_time = jnp.where(cj == (ri & (nseq - 1)) * tc + lax.shift_right_logical(ri, lseq), 1.0, 0.0).astype(BF16)
    to_seq = jnp.where(cj == (ri & (tc - 1)) * nseq + lax.shift_right_logical(ri, lt), 1.0, 0.0).astype(BF16)
    u_t = _dot(to_time, u.astype(BF16)).astype(BF16)
    bu_scr[:, 0:SSM_FLAT] = _dot(u_t, bb_re)
    bu_scr[:, SSM_FLAT:2 * SSM_FLAT] = _dot(u_t, bb_im)

    ar = jnp.broadcast_to(ab_re, (nseq, SSM_FLAT))
    ai = jnp.broadcast_to(ab_im, (nseq, SSM_FLAT))

    def step(t, carry):
        hr, hi = carry
        rows = pl.ds(pl.multiple_of(t * nseq, nseq), nseq)
        nhr = ar * hr - ai * hi + bu_scr[rows, 0:SSM_FLAT]
        nhi = ar * hi + ai * hr + bu_scr[rows, SSM_FLAT:2 * SSM_FLAT]
        bu_scr[rows, 0:SSM_FLAT] = nhr
        bu_scr[rows, SSM_FLAT:2 * SSM_FLAT] = nhi
        return nhr, nhi

    hr, hi = lax.fori_loop(0, tc, step, (h_scr[0], h_scr[1]))
    h_scr[0] = hr
    h_scr[1] = hi

    y_t = _dot(bu_scr[...].astype(BF16), ccat_ref[...])
    hi_p = y_t.astype(BF16)
    rest = y_t - hi_p.astype(F32)
    mid_p = rest.astype(BF16)
    lo_p = (rest - mid_p.astype(F32)).astype(BF16)
    y = _dot(to_seq, hi_p) + _dot(to_seq, mid_p) + _dot(to_seq, lo_p) + d_ref[...] * u
    z = jax.nn.gelu(y)
    out = z * jax.nn.sigmoid(_dot(z.astype(BF16), gw_ref[...]) + gb_ref[...])
    for b in range(nseq):
        y_ref[:, b * GROUP_WIDTH:(b + 1) * GROUP_WIDTH] = out[b * tc:(b + 1) * tc].astype(y_ref.dtype)

    @pl.when(c == pl.num_programs(0) - 1)
    def _():
        hr_out[...] = hr
        hi_out[...] = hi


def s5_seq_mixer(proj2, lam_re, lam_im, log_dt, b_re_bd, b_im_bd, c_cat, d_skip, glu_w, glu_b, *, bsz, t_len):
    tc = S5_CHUNK
    ncols = PROJ_WIDTH // GROUP_WIDTH
    full = lambda shape: pl.BlockSpec(shape, lambda c: (0,) * len(shape))
    st = jax.ShapeDtypeStruct((bsz, SSM_FLAT), F32)
    u_specs = [pl.BlockSpec((tc, GROUP_WIDTH), functools.partial(lambda c, b: (c, b * ncols + COL_SSM), b=b))
               for b in range(bsz)]
    return pl.pallas_call(
        functools.partial(_s5_seq_kernel, nseq=bsz),
        grid=(t_len // tc,),
        in_specs=u_specs + [full((1, SSM_FLAT)), full((1, SSM_FLAT)), full((1, SSM_FLAT)),
                            full((GROUP_WIDTH, SSM_FLAT)), full((GROUP_WIDTH, SSM_FLAT)),
                            full((2 * SSM_FLAT, GROUP_WIDTH)), full((1, GROUP_WIDTH)),
                            full((GROUP_WIDTH, GROUP_WIDTH)), full((1, GROUP_WIDTH))],
        out_specs=[pl.BlockSpec((tc, bsz * GROUP_WIDTH), lambda c: (c, 0)), full((bsz, SSM_FLAT)),
                   full((bsz, SSM_FLAT))],
        out_shape=[jax.ShapeDtypeStruct((t_len, bsz * GROUP_WIDTH), BF16), st, st],
        scratch_shapes=[pltpu.VMEM((2, bsz, SSM_FLAT), F32), pltpu.VMEM((bsz * tc, 2 * SSM_FLAT), F32)],
        compiler_params=_params(("arbitrary",)),
        name="s5_seq_mixer",
    )(*([proj2] * bsz), lam_re, lam_im, log_dt, b_re_bd, b_im_bd, c_cat, d_skip, glu_w, glu_b)


POOL_CHUNK = 256
POOL_HIST = 16


def _pool_seq_kernel(u_ref, w_ref, sc_ref, y_ref, nbuf_ref, ext_scr):
    L, hist = POOL_CHUNK, POOL_HIST
    c = pl.program_id(1)

    @pl.when(c == 0)
    def _():
        ext_scr[0:hist] = jnp.zeros((hist, GROUP_WIDTH), F32)

    u = u_ref[...]
    ext_scr[hist:hist + L] = u
    e = ext_scr[...]
    a2 = e + pltpu.roll(e, 1, axis=0)
    a4 = a2 + pltpu.roll(a2, 2, axis=0)
    a8 = a4 + pltpu.roll(a4, 4, axis=0)
    a16 = a8 + pltpu.roll(a8, 8, axis=0)
    sums = (a2[hist:], a4[hist:], a8[hist:], a16[hist:])

    shape = (L, GROUP_WIDTH)
    pos = _iota(shape, 0) + c * L
    lane = _iota(shape, 1)
    pooled = None
    for gi in reversed(range(len(POOL_WINDOWS))):
        win = POOL_WINDOWS[gi]
        mean = sums[gi] / jnp.minimum(pos + 1, win).astype(F32)
        pooled = mean if pooled is None else jnp.where(lane < (gi + 1) * HEAD, mean, pooled)
    y_ref[...] = (_dot((pooled - u).astype(BF16), w_ref[...]) * sc_ref[...]).astype(y_ref.dtype)

    nb = ext_scr[L:L + hist]
    ext_scr[0:hist] = nb

    @pl.when(c == pl.num_programs(1) - 1)
    def _():
        nbuf_ref[...] = nb


def pool_seq_mixer(proj2, w_bd, scale, *, bsz, t_len):
    L, hist = POOL_CHUNK, POOL_HIST
    ncols = PROJ_WIDTH // GROUP_WIDTH
    return pl.pallas_call(
        _pool_seq_kernel,
        grid=(bsz, t_len // L),
        in_specs=[_seq_col_spec(L, COL_POOL, ncols),
                  pl.BlockSpec((GROUP_WIDTH, GROUP_WIDTH), lambda b, c: (0, 0)),
                  pl.BlockSpec((1, GROUP_WIDTH), lambda b, c: (0, 0))],
        out_specs=[_seq_col_spec(L, 0, 1), pl.BlockSpec((hist, GROUP_WIDTH), lambda b, c: (b, 0))],
        out_shape=[jax.ShapeDtypeStruct((t_len, bsz * GROUP_WIDTH), BF16),
                   jax.ShapeDtypeStruct((bsz * hist, GROUP_WIDTH), F32)],
        scratch_shapes=[pltpu.VMEM((hist + L, GROUP_WIDTH), F32)],
        compiler_params=_params(("parallel", "arbitrary")),
        name="pool_seq_mixer",
    )(proj2, w_bd, scale)


def _diag_heads(st, bsz):
    nh = GROUP_WIDTH // HEAD
    s = st.reshape(bsz, nh, HEAD, nh, HEAD)
    return jnp.stack([s[:, h, :, h, :] for h in range(nh)], axis=1)


def _mix_mlp_kernel(h_ref, ya_ref, yb_ref, yc_ref, yd_ref, wo_ref, g2_ref, wu_ref, wd_ref, gf_ref, o_ref,
                    h1_scr, xn_scr, acc_scr, *, final_norm):
    j = pl.program_id(1)
    gw = GROUP_WIDTH

    @pl.when(j == 0)
    def _():
        mix = (_dot(ya_ref[...], wo_ref[0:gw]) + _dot(yb_ref[...], wo_ref[gw:2 * gw])
               + _dot(yc_ref[...], wo_ref[2 * gw:3 * gw]) + _dot(yd_ref[...], wo_ref[3 * gw:4 * gw]))
        h1 = h_ref[...] + mix
        h1_scr[...] = h1
        xn_scr[...] = _rms(h1, g2_ref[...]).astype(BF16)
        acc_scr[...] = jnp.zeros_like(acc_scr)

    up = _dot(xn_scr[...], wu_ref[...])
    act = jnp.square(jnp.maximum(up, 0.0)).astype(BF16)
    acc_scr[...] += _dot(act, wd_ref[...])

    @pl.when(j == pl.num_programs(1) - 1)
    def _():
        out = h1_scr[...] + acc_scr[...]
        if final_norm:
            out = _rms(out, gf_ref[...])
        o_ref[...] = out


def mix_mlp(h, ys, w_out, g2, w_up, w_down, g_final, *, final_norm, nseq=0):
    n = h.shape[0]
    tm = min(512, n // max(nseq, 1))
    tf = 1024
    nt = n // tm // max(nseq, 1)
    row = lambda w: pl.BlockSpec((tm, w), lambda i, j: (i, 0))
    mix = pl.BlockSpec((tm, GROUP_WIDTH), _tile_map(nseq, nt))
    return pl.pallas_call(
        functools.partial(_mix_mlp_kernel, final_norm=final_norm),
        grid=(n // tm, D_FF // tf),
        in_specs=[row(D_MODEL), mix, mix, mix, mix,
                  pl.BlockSpec((D_MODEL, D_MODEL), lambda i, j: (0, 0)),
                  pl.BlockSpec((1, D_MODEL), lambda i, j: (0, 0)),
                  pl.BlockSpec((D_MODEL, tf), lambda i, j: (0, j)),
                  pl.BlockSpec((tf, D_MODEL), lambda i, j: (j, 0)),
                  pl.BlockSpec((1, D_MODEL), lambda i, j: (0, 0))],
        out_specs=row(D_MODEL),
        out_shape=jax.ShapeDtypeStruct((n, D_MODEL), F32),
        scratch_shapes=[pltpu.VMEM((tm, D_MODEL), F32), pltpu.VMEM((tm, D_MODEL), BF16),
                        pltpu.VMEM((tm, D_MODEL), F32)],
        compiler_params=_params(("parallel", "arbitrary")),
        name="mix_mlp",
    )(h, *ys, w_out, g2, w_up, w_down, g_final)


def _hgrn_state_in(s):
    b = s.shape[0]
    return s.reshape(b, 2, 2, HEAD, HEAD).transpose(0, 1, 4, 2, 3).reshape(b, 2, HEAD, LANES)


def _hgrn_state_out(s):
    b = s.shape[0]
    return s.reshape(b, 2, HEAD, 2, HEAD).transpose(0, 1, 3, 4, 2).reshape(b, 4, HEAD, HEAD)


def _wkv_state_in(s):
    b = s.shape[0]
    return s.reshape(b, 2, 2, HEAD, HEAD).transpose(0, 1, 3, 2, 4).reshape(b, 2, HEAD, LANES)


def _wkv_state_out(s):
    b = s.shape[0]
    return s.reshape(b, 2, HEAD, 2, HEAD).transpose(0, 1, 3, 2, 4).reshape(b, 4, HEAD, HEAD)


def _block_diag(blocks):
    g, r, c = blocks.shape
    eye = jnp.eye(g, dtype=blocks.dtype)
    return (blocks[:, :, None, :] * eye[:, None, :, None]).reshape(g * r, g * c)


def _pad_rows(w, start):
    return jnp.zeros((GROUP_WIDTH, GROUP_WIDTH), w.dtype).at[start:start + w.shape[0]].set(w)


def _layer_params(l, P):
    row = lambda a: a.reshape(1, -1)
    q = {}
    q["norm1_g"] = row(P["norm1_g"][l])
    q["w_in"] = P["w_in"][l].astype(BF16)
    q["lam_re"] = row(P["ssm_lambda_re"][l])
    q["lam_im"] = row(P["ssm_lambda_im"][l])
    q["log_dt"] = row(jnp.repeat(P["ssm_log_dt"][l], SSM_STATE))
    q["b_re"] = _block_diag(P["ssm_b_re"][l].transpose(0, 2, 1))
    q["b_im"] = _block_diag(P["ssm_b_im"][l].transpose(0, 2, 1))
    q["c_cat"] = jnp.concatenate([_block_diag(P["ssm_c_re"][l].transpose(0, 2, 1)),
                                  -_block_diag(P["ssm_c_im"][l].transpose(0, 2, 1))], axis=0).astype(BF16)
    q["ssm_d"] = row(P["ssm_d"][l])
    q["glu_w"] = P["ssm_glu_w"][l].astype(BF16)
    q["glu_b"] = row(P["ssm_glu_b"][l])
    q["hgrn_norm_g"] = row(P["hgrn_norm_g"][l])
    q["mu"] = row(P["rwkv_mu"][l])
    for name in ("w0", "a0", "k_k", "k_a", "r_k", "ln_g", "ln_b"):
        q[name] = row(P["rwkv_" + name][l])
    q["w2p"] = _pad_rows(P["rwkv_w2"][l], 0).astype(BF16)
    q["a2p"] = _pad_rows(P["rwkv_a2"][l], DECAY_LORA).astype(BF16)
    q["g2p"] = _pad_rows(P["rwkv_g2"][l], DECAY_LORA + AAA_LORA).astype(BF16)
    q["pool_w"] = _block_diag(P["pool_w"][l]).astype(BF16)
    q["pool_scale"] = row(P["pool_scale"][l])
    q["w_out"] = P["w_out"][l].astype(BF16)
    q["norm2_g"] = row(P["norm2_g"][l])
    q["mlp_up"] = P["mlp_up"][l].astype(BF16)
    q["mlp_down"] = P["mlp_down"][l].astype(BF16)
    return q


def _trunk_fresh(x_rows, bsz, t_len, layer_params, P):
    h = x_rows
    new = [[] for _ in range(6)]
    g_final = P["norm_f_g"].reshape(1, -1)
    for l in range(DEPTH):
        q = layer_params[l]
        proj2 = rms_proj(h, q["norm1_g"], q["w_in"], nseq=bsz)
        y_a, s_re, s_im = s5_seq_mixer(proj2, q["lam_re"], q["lam_im"], q["log_dt"], q["b_re"], q["b_im"],
                                       q["c_cat"], q["ssm_d"], q["glu_w"], q["glu_b"], bsz=bsz, t_len=t_len)
        y_b, s_hg = hgrn_chunk_mixer(proj2, P["hgrn_lb_logits"], q["hgrn_norm_g"], bsz=bsz, t_len=t_len, layer=l)
        y_c, s_wkv, s_sh = rwkv_chunk_mixer(proj2, q["mu"], q["w0"], q["a0"], q["k_k"], q["k_a"], q["r_k"],
                                            q["ln_g"], q["ln_b"], q["w2p"], q["a2p"], q["g2p"], bsz=bsz, t_len=t_len)
        y_d, s_pool = pool_seq_mixer(proj2, q["pool_w"], q["pool_scale"], bsz=bsz, t_len=t_len)
        h = mix_mlp(h, (y_a, y_b, y_c, y_d), q["w_out"], q["norm2_g"], q["mlp_up"], q["mlp_down"], g_final,
                    final_norm=(l == DEPTH - 1), nseq=bsz)
        s_hg = _diag_heads(s_hg, bsz).swapaxes(-1, -2)
        s_wkv = _diag_heads(s_wkv, bsz)
        s_pool = s_pool.reshape(bsz, POOL_HIST, GROUP_WIDTH)[:, POOL_HIST - POOL_BUF:]
        for lst, s in zip(new, (s_re, s_im, s_hg, s_wkv, s_sh, s_pool)):
            lst.append(s)
    return h, new


def _trunk_carry(x_rows, states, pos0, t_len, layer_params, P):
    ssm_re0, ssm_im0, hgrn0, wkv0, shift0, pool0 = states
    h = x_rows
    new = [[] for _ in range(6)]
    g_final = P["norm_f_g"].reshape(1, -1)
    tc = t_len
    for l in range(DEPTH):
        q = layer_params[l]
        proj = rms_proj(h, q["norm1_g"], q["w_in"])
        y_a, s_re, s_im = s5_mixer(proj, ssm_re0[l], ssm_im0[l], q["lam_re"], q["lam_im"], q["log_dt"], q["b_re"],
                                   q["b_im"], q["c_cat"], q["ssm_d"], q["glu_w"], q["glu_b"], t_len=t_len, tc=tc)
        y_b, s_hg = hgrn_mixer(proj, hgrn0[l], P["hgrn_lb_logits"], q["hgrn_norm_g"], t_len=t_len, tc=tc, layer=l)
        y_c, s_wkv, s_sh = rwkv_mixer(proj, shift0[l], wkv0[l], q["mu"], q["w0"], q["a0"], q["k_k"], q["k_a"],
                                      q["r_k"], q["ln_g"], q["ln_b"], q["w2p"], q["a2p"], q["g2p"],
                                      t_len=t_len, tc=tc)
        y_d, s_pool = pool_mixer(proj, pool0[l], q["pool_w"], q["pool_scale"], t_len=t_len, tc=tc, pos0=pos0)
        h = mix_mlp(h, (y_a, y_b, y_c, y_d), q["w_out"], q["norm2_g"], q["mlp_up"], q["mlp_down"], g_final,
                    final_norm=(l == DEPTH - 1))
        for lst, s in zip(new, (s_re, s_im, _hgrn_state_out(s_hg), _wkv_state_out(s_wkv), s_sh,
                                s_pool.transpose(1, 0, 2))):
            lst.append(s)
    return h, new


def _states_out(new, bsz):
    s_re, s_im, s_hg, s_wkv, s_sh, s_pool = new
    return (jnp.stack([s.reshape(bsz, SSM_GROUPS, SSM_STATE) for s in s_re]),
            jnp.stack([s.reshape(bsz, SSM_GROUPS, SSM_STATE) for s in s_im]),
            jnp.stack(s_hg),
            jnp.stack(s_wkv),
            jnp.stack([s.reshape(bsz, 1, RWKV_PROJ) for s in s_sh]),
            jnp.stack(s_pool))


def kernel(x_prompt, x_sample, state_ssm_re, state_ssm_im, state_hgrn, state_wkv, state_shift, state_pool, norm1_g, w_in, ssm_lambda_re, ssm_lambda_im, ssm_log_dt, ssm_b_re, ssm_b_im, ssm_c_re, ssm_c_im, ssm_d, ssm_glu_w, ssm_glu_b, hgrn_lb_logits, hgrn_norm_g, rwkv_mu, rwkv_w0, rwkv_w2, rwkv_a0, rwkv_a2, rwkv_g2, rwkv_k_k, rwkv_k_a, rwkv_r_k, rwkv_ln_g, rwkv_ln_b, pool_w, pool_scale, w_out, norm2_g, mlp_up, mlp_down, norm_f_g):
    P = dict(norm1_g=norm1_g, w_in=w_in, ssm_lambda_re=ssm_lambda_re, ssm_lambda_im=ssm_lambda_im,
             ssm_log_dt=ssm_log_dt, ssm_b_re=ssm_b_re, ssm_b_im=ssm_b_im, ssm_c_re=ssm_c_re, ssm_c_im=ssm_c_im,
             ssm_d=ssm_d, ssm_glu_w=ssm_glu_w, ssm_glu_b=ssm_glu_b, hgrn_lb_logits=hgrn_lb_logits,
             hgrn_norm_g=hgrn_norm_g, rwkv_mu=rwkv_mu, rwkv_w0=rwkv_w0, rwkv_w2=rwkv_w2, rwkv_a0=rwkv_a0,
             rwkv_a2=rwkv_a2, rwkv_g2=rwkv_g2, rwkv_k_k=rwkv_k_k, rwkv_k_a=rwkv_k_a, rwkv_r_k=rwkv_r_k,
             rwkv_ln_g=rwkv_ln_g, rwkv_ln_b=rwkv_ln_b, pool_w=pool_w, pool_scale=pool_scale, w_out=w_out,
             norm2_g=norm2_g, mlp_up=mlp_up, mlp_down=mlp_down, norm_f_g=norm_f_g)
    layer_params = [_layer_params(l, P) for l in range(DEPTH)]

    bp, t_p, _ = x_prompt.shape
    yp, new_p = _trunk_fresh(x_prompt.reshape(bp * t_p, D_MODEL), bp, t_p, layer_params, P)
    y_prompt = yp.reshape(bp, t_p, D_MODEL)

    bs, t_s, _ = x_sample.shape
    nblk = bs // SEQ_BLK
    xs = x_sample.reshape(nblk, SEQ_BLK, t_s, D_MODEL).transpose(0, 2, 1, 3).reshape(bs * t_s, D_MODEL)
    st_s = ([state_ssm_re[l].reshape(bs, SSM_FLAT) for l in range(DEPTH)],
            [state_ssm_im[l].reshape(bs, SSM_FLAT) for l in range(DEPTH)],
            [_hgrn_state_in(state_hgrn[l]) for l in range(DEPTH)],
            [_wkv_state_in(state_wkv[l]) for l in range(DEPTH)],
            [state_shift[l].reshape(bs, RWKV_PROJ) for l in range(DEPTH)],
            [state_pool[l].transpose(1, 0, 2) for l in range(DEPTH)])
    ys, new_s = _trunk_carry(xs, st_s, PAST_LEN, t_s, layer_params, P)
    y_sample = ys.reshape(nblk, t_s, SEQ_BLK, D_MODEL).transpose(0, 2, 1, 3).reshape(bs, t_s, D_MODEL)

    return (y_prompt, y_sample) + _states_out(new_p, bp) + _states_out(new_s, bs)
```

```python
import functools
import itertools

import jax
import jax.numpy as jnp
from jax import lax
from jax.experimental import pallas as pl
from jax.experimental.pallas import tpu as pltpu

F32 = jnp.float32
BF16 = jnp.bfloat16

D_MODEL = 1024
DEPTH = 2
PAST_LEN = 16384
GROUP_WIDTH = 256
HEAD = 64
SSM_GROUPS = 16
SSM_CH = 16
SSM_STATE = 64
SSM_FLAT = SSM_GROUPS * SSM_STATE
POOL_WINDOWS = (2, 4, 8, 16)
POOL_BUF = 15
DECAY_LORA = 64
AAA_LORA = 64
GATE_LORA = 128
RWKV_PROJ = 1024
PROJ_WIDTH = 2560
D_FF = 4096
NORM_EPS = 1e-6
HGRN_NORM_EPS = 1e-5
RWKV_GN_EPS = 64e-5

SEQ_BLK = 8
LANES = 128
VMEM_LIMIT = 48 * 1024 * 1024

COL_SSM, COL_Q, COL_F, COL_I, COL_G, COL_R, COL_K, COL_V, COL_LORA, COL_POOL = range(10)


def _params(sem):
    return pltpu.CompilerParams(dimension_semantics=sem, vmem_limit_bytes=VMEM_LIMIT)


def _dot(a, b):
    return jnp.dot(a, b, preferred_element_type=F32)


def _rms(x, g):
    return x * lax.rsqrt(jnp.mean(x * x, axis=-1, keepdims=True) + NORM_EPS) * g


def _rms_proj_kernel(x_ref, g_ref, w_ref, o_ref):
    o_ref[...] = _dot(_rms(x_ref[...], g_ref[...]).astype(BF16), w_ref[...])


def _tile_map(nseq, nt):
    if nseq == 0:
        return lambda r, *_: (r, 0)
    return lambda r, *_: (r % nt, r // nt)


def rms_proj(x, g, w, *, nseq=0):
    n = x.shape[0]
    tm = min(512, n // max(nseq, 1))
    nt = n // tm // max(nseq, 1)
    out_shape = (n, PROJ_WIDTH) if nseq == 0 else (n // nseq, nseq * PROJ_WIDTH)
    return pl.pallas_call(
        _rms_proj_kernel,
        grid=(n // tm,),
        in_specs=[pl.BlockSpec((tm, D_MODEL), lambda i: (i, 0)),
                  pl.BlockSpec((1, D_MODEL), lambda i: (0, 0)),
                  pl.BlockSpec((D_MODEL, PROJ_WIDTH), lambda i: (0, 0))],
        out_specs=pl.BlockSpec((tm, PROJ_WIDTH), _tile_map(nseq, nt)),
        out_shape=jax.ShapeDtypeStruct(out_shape, F32),
        compiler_params=_params(("parallel",)),
        name="rms_proj",
    )(x, g, w)


def _row_spec(tc, col, nchunks):
    return pl.BlockSpec((tc * SEQ_BLK, GROUP_WIDTH), lambda s, c: (s * nchunks + c, col))


def _full_spec(shape):
    nd = len(shape)
    return pl.BlockSpec(shape, lambda s, c: (0,) * nd)


def _seq_spec(shape):
    nd = len(shape)
    return pl.BlockSpec((SEQ_BLK,) + shape[1:], lambda s, c: (s,) + (0,) * (nd - 1))


def _head_sums(x):
    lane = lax.broadcasted_iota(jnp.int32, x.shape, 1)
    out = jnp.zeros_like(x)
    for h in range(GROUP_WIDTH // HEAD):
        m = (lane >= h * HEAD) & (lane < (h + 1) * HEAD)
        s = jnp.sum(jnp.where(m, x, 0.0), axis=1, keepdims=True)
        out = jnp.where(m, s, out)
    return out


def _pair_masks():
    lane = lax.broadcasted_iota(jnp.int32, (HEAD, LANES), 1)
    sub = lax.broadcasted_iota(jnp.int32, (HEAD, LANES), 0)
    return lane < HEAD, (lane & (HEAD - 1)) == sub


def _seg_sum(p, lo):
    s0 = jnp.sum(jnp.where(lo, p, 0.0), axis=1, keepdims=True)
    s1 = jnp.sum(jnp.where(lo, 0.0, p), axis=1, keepdims=True)
    return jnp.where(lo, s0, s1)


def _col(row_b, lo, eye2):
    return _seg_sum(jnp.where(eye2, row_b, 0.0), lo)


def _row(col_b, eye2):
    return jnp.sum(jnp.where(eye2, col_b, 0.0), axis=0, keepdims=True)


def _bcast(ref, t, b, p):
    r = ref[t, b:b + 1, p * LANES:(p + 1) * LANES]
    return jnp.broadcast_to(r, (HEAD, LANES))


def _s5_kernel(u_ref, h0r_ref, h0i_ref, lr_ref, li_ref, ldt_ref, bre_ref, bim_ref, ccat_ref, d_ref,
               gw_ref, gb_ref, y_ref, hr_out, hi_out, h_scr, bu_scr, *, tc):
    c = pl.program_id(1)

    @pl.when(c == 0)
    def _():
        h_scr[0] = h0r_ref[...]
        h_scr[1] = h0i_ref[...]

    lr, li = lr_ref[...], li_ref[...]
    dt = jnp.exp(ldt_ref[...])
    mag = jnp.exp(lr * dt)
    ab_re, ab_im = mag * jnp.cos(li * dt), mag * jnp.sin(li * dt)
    den = lr * lr + li * li
    zr, zi = ab_re - 1.0, ab_im
    cr = (zr * lr + zi * li) / den
    ci = (zi * lr - zr * li) / den
    bre, bim = bre_ref[...], bim_ref[...]
    bb_re = (cr * bre - ci * bim).astype(BF16)
    bb_im = (cr * bim + ci * bre).astype(BF16)

    u = u_ref[...]
    ub = u.astype(BF16)
    bu_scr[:, 0:SSM_FLAT] = _dot(ub, bb_re)
    bu_scr[:, SSM_FLAT:2 * SSM_FLAT] = _dot(ub, bb_im)

    ar = jnp.broadcast_to(ab_re, (SEQ_BLK, SSM_FLAT))
    ai = jnp.broadcast_to(ab_im, (SEQ_BLK, SSM_FLAT))

    def step(t, carry):
        hr, hi = carry
        rows = pl.ds(pl.multiple_of(t * SEQ_BLK, SEQ_BLK), SEQ_BLK)
        nhr = ar * hr - ai * hi + bu_scr[rows, 0:SSM_FLAT]
        nhi = ar * hi + ai * hr + bu_scr[rows, SSM_FLAT:2 * SSM_FLAT]
        bu_scr[rows, 0:SSM_FLAT] = nhr
        bu_scr[rows, SSM_FLAT:2 * SSM_FLAT] = nhi
        return nhr, nhi

    hr, hi = lax.fori_loop(0, tc, step, (h_scr[0], h_scr[1]))
    h_scr[0] = hr
    h_scr[1] = hi

    y = _dot(bu_scr[...].astype(BF16), ccat_ref[...]) + d_ref[...] * u
    z = jax.nn.gelu(y)
    out = z * jax.nn.sigmoid(_dot(z.astype(BF16), gw_ref[...]) + gb_ref[...])
    y_ref[...] = out.astype(y_ref.dtype)

    @pl.when(c == pl.num_programs(1) - 1)
    def _():
        hr_out[...] = hr
        hi_out[...] = hi


def s5_mixer(proj, h0_re, h0_im, lam_re, lam_im, log_dt, b_re_bd, b_im_bd, c_cat, d_skip, glu_w, glu_b, *, t_len, tc):
    nseq = h0_re.shape[0] // SEQ_BLK
    nchunks = t_len // tc
    n = proj.shape[0]
    st = jax.ShapeDtypeStruct(h0_re.shape, F32)
    return pl.pallas_call(
        functools.partial(_s5_kernel, tc=tc),
        grid=(nseq, nchunks),
        in_specs=[_row_spec(tc, COL_SSM, nchunks),
                  _seq_spec(h0_re.shape), _seq_spec(h0_im.shape),
                  _full_spec((1, SSM_FLAT)), _full_spec((1, SSM_FLAT)), _full_spec((1, SSM_FLAT)),
                  _full_spec((GROUP_WIDTH, SSM_FLAT)), _full_spec((GROUP_WIDTH, SSM_FLAT)),
                  _full_spec((2 * SSM_FLAT, GROUP_WIDTH)), _full_spec((1, GROUP_WIDTH)),
                  _full_spec((GROUP_WIDTH, GROUP_WIDTH)), _full_spec((1, GROUP_WIDTH))],
        out_specs=[_row_spec(tc, 0, nchunks), _seq_spec(h0_re.shape), _seq_spec(h0_im.shape)],
        out_shape=[jax.ShapeDtypeStruct((n, GROUP_WIDTH), BF16), st, st],
        scratch_shapes=[pltpu.VMEM((2, SEQ_BLK, SSM_FLAT), F32),
                        pltpu.VMEM((tc * SEQ_BLK, 2 * SSM_FLAT), F32)],
        compiler_params=_params(("parallel", "arbitrary")),
        name="s5_mixer",
    )(proj, h0_re, h0_im, lam_re, lam_im, log_dt, b_re_bd, b_im_bd, c_cat, d_skip, glu_w, glu_b)


def _pool_kernel(u_ref, buf_ref, w_ref, sc_ref, y_ref, nbuf_ref, ext_scr, *, tc, pos0):
    c = pl.program_id(1)

    @pl.when(c == 0)
    def _():
        ext_scr[0:POOL_BUF] = buf_ref[...]

    u = u_ref[...].reshape(tc, SEQ_BLK, GROUP_WIDTH)
    ext_scr[POOL_BUF:POOL_BUF + tc] = u
    a1 = ext_scr[...]
    a2 = a1[1:] + a1[:-1]
    a4 = a2[2:] + a2[:-2]
    a8 = a4[4:] + a4[:-4]
    a16 = a8[8:] + a8[:-8]
    sums = (a2[14:], a4[12:], a8[8:], a16)

    shape = (tc, SEQ_BLK, GROUP_WIDTH)
    pos = lax.broadcasted_iota(jnp.int32, shape, 0) + (c * tc + pos0)
    lane = lax.broadcasted_iota(jnp.int32, shape, 2)
    pooled = None
    for gi in reversed(range(len(POOL_WINDOWS))):
        win = POOL_WINDOWS[gi]
        mean = sums[gi] / jnp.minimum(pos + 1, win).astype(F32)
        pooled = mean if pooled is None else jnp.where(lane < (gi + 1) * HEAD, mean, pooled)
    pooled = (pooled - u).reshape(tc * SEQ_BLK, GROUP_WIDTH)
    y_ref[...] = (_dot(pooled.astype(BF16), w_ref[...]) * sc_ref[...]).astype(y_ref.dtype)

    nb = ext_scr[tc:tc + POOL_BUF]
    ext_scr[0:POOL_BUF] = nb

    @pl.when(c == pl.num_programs(1) - 1)
    def _():
        nbuf_ref[...] = nb


def pool_mixer(proj, buf, w_bd, scale, *, t_len, tc, pos0):
    nseq = buf.shape[1] // SEQ_BLK
    nchunks = t_len // tc
    n = proj.shape[0]
    buf_spec = pl.BlockSpec((POOL_BUF, SEQ_BLK, GROUP_WIDTH), lambda s, c: (0, s, 0))
    return pl.pallas_call(
        functools.partial(_pool_kernel, tc=tc, pos0=pos0),
        grid=(nseq, nchunks),
        in_specs=[_row_spec(tc, COL_POOL, nchunks), buf_spec,
                  _full_spec((GROUP_WIDTH, GROUP_WIDTH)), _full_spec((1, GROUP_WIDTH))],
        out_specs=[_row_spec(tc, 0, nchunks), buf_spec],
        out_shape=[jax.ShapeDtypeStruct((n, GROUP_WIDTH), BF16), jax.ShapeDtypeStruct(buf.shape, F32)],
        scratch_shapes=[pltpu.VMEM((tc + POOL_BUF, SEQ_BLK, GROUP_WIDTH), F32)],
        compiler_params=_params(("parallel", "arbitrary")),
        name="pool_mixer",
    )(proj, buf, w_bd, scale)


def _hgrn_lower_bound(logits_ref, layer):
    rows = [logits_ref[l:l + 1, :] for l in range(DEPTH)]
    m = functools.reduce(jnp.maximum, rows)
    es = [jnp.exp(r - m) for r in rows]
    tot = functools.reduce(lambda a, b: a + b, es)
    lb = jnp.zeros_like(m)
    for l in range(1, layer + 1):
        lb = lb + es[l] / tot
    return lb


def _hgrn_kernel(pq_ref, pf_ref, pi_ref, pg_ref, s0_ref, lbl_ref, ng_ref, y_ref, st_ref,
                 s_scr, q_scr, f_scr, k_scr, v_scr, o_scr, *, tc, layer):
    c = pl.program_id(1)
    shape3 = (tc, SEQ_BLK, GROUP_WIDTH)

    @pl.when(c == 0)
    def _():
        s_scr[...] = s0_ref[...]

    lb = _hgrn_lower_bound(lbl_ref, layer)
    zf = pf_ref[...]
    f_scr[...] = (lb + (1.0 - lb) * jax.nn.sigmoid(zf)).reshape(shape3)
    k_scr[...] = ((1.0 - lb) * jax.nn.sigmoid(-zf)).reshape(shape3)
    q_scr[...] = jax.nn.silu(pq_ref[...]).reshape(shape3)
    v_scr[...] = pi_ref[...].reshape(shape3)

    lo, eye2 = _pair_masks()

    def step(t, carry):
        for b in range(SEQ_BLK):
            for p in range(2):
                s = s_scr[b, p]
                vcol = _col(_bcast(v_scr, t, b, p), lo, eye2)
                s = s * _bcast(f_scr, t, b, p) + vcol * _bcast(k_scr, t, b, p)
                s_scr[b, p] = s
                ocol = _seg_sum(s * _bcast(q_scr, t, b, p), lo)
                o_scr[t, b:b + 1, p * LANES:(p + 1) * LANES] = _row(ocol, eye2)
        return carry

    lax.fori_loop(0, tc, step, 0)

    o = o_scr[...].reshape(tc * SEQ_BLK, GROUP_WIDTH)
    ms = _head_sums(o * o) * (1.0 / HEAD)
    out = o * lax.rsqrt(ms + HGRN_NORM_EPS) * ng_ref[...] * jax.nn.silu(pg_ref[...])
    y_ref[...] = out.astype(y_ref.dtype)

    @pl.when(c == pl.num_programs(1) - 1)
    def _():
        st_ref[...] = s_scr[...]


def hgrn_mixer(proj, s0, lb_logits, norm_g, *, t_len, tc, layer):
    nseq = s0.shape[0] // SEQ_BLK
    nchunks = t_len // tc
    n = proj.shape[0]
    tile = pltpu.VMEM((tc, SEQ_BLK, GROUP_WIDTH), F32)
    return pl.pallas_call(
        functools.partial(_hgrn_kernel, tc=tc, layer=layer),
        grid=(nseq, nchunks),
        in_specs=[_row_spec(tc, COL_Q, nchunks), _row_spec(tc, COL_F, nchunks),
                  _row_spec(tc, COL_I, nchunks), _row_spec(tc, COL_G, nchunks),
                  _seq_spec(s0.shape), _full_spec((DEPTH, GROUP_WIDTH)), _full_spec((1, GROUP_WIDTH))],
        out_specs=[_row_spec(tc, 0, nchunks), _seq_spec(s0.shape)],
        out_shape=[jax.ShapeDtypeStruct((n, GROUP_WIDTH), BF16), jax.ShapeDtypeStruct(s0.shape, F32)],
        scratch_shapes=[pltpu.VMEM((SEQ_BLK, 2, HEAD, LANES), F32), tile, tile, tile, tile, tile],
        compiler_params=_params(("parallel", "arbitrary")),
        name="hgrn_mixer",
    )(proj, proj, proj, proj, s0, lb_logits, norm_g)


def _rwkv_kernel(pr_ref, pk_ref, pv_ref, pl_ref, sh0_ref, s0_ref, mu_ref, w0_ref, a0_ref, kk_ref, ka_ref,
                 rk_ref, lng_ref, lnb_ref, w2_ref, a2_ref, g2_ref, y_ref, st_ref, sh_ref,
                 s_scr, prev_scr, r_scr, w_scr, k_scr, v_scr, nkk_scr, kka_scr, o_scr, *, tc):
    c = pl.program_id(1)
    shape3 = (tc, SEQ_BLK, GROUP_WIDTH)
    gw = GROUP_WIDTH

    @pl.when(c == 0)
    def _():
        s_scr[...] = s0_ref[...]
        prev_scr[...] = sh0_ref[...]

    def shifted(ref, j):
        x = ref[...].reshape(shape3)
        first = prev_scr[:, j * gw:(j + 1) * gw].reshape(1, SEQ_BLK, gw)
        prev = first if tc == 1 else jnp.concatenate([first, x[:-1]], axis=0)
        prev_scr[:, j * gw:(j + 1) * gw] = x[tc - 1]
        return (x + (prev - x) * mu_ref[:, j * gw:(j + 1) * gw]).reshape(tc * SEQ_BLK, gw)

    xr, xk, xv, xl = shifted(pr_ref, 0), shifted(pk_ref, 1), shifted(pv_ref, 2), shifted(pl_ref, 3)
    w = -jax.nn.softplus(-(w0_ref[...] + _dot(jnp.tanh(xl).astype(BF16), w2_ref[...]))) - 0.5
    decay = jnp.exp(-jnp.exp(w))
    a = jax.nn.sigmoid(a0_ref[...] + _dot(xl.astype(BF16), a2_ref[...]))
    g = _dot(jax.nn.sigmoid(xl).astype(BF16), g2_ref[...])
    kk = xk * kk_ref[...]
    kk = kk / jnp.maximum(jnp.sqrt(_head_sums(kk * kk)), 1e-12)
    k = xk * (1.0 + (a - 1.0) * ka_ref[...])

    r_scr[...] = xr.reshape(shape3)
    w_scr[...] = decay.reshape(shape3)
    k_scr[...] = k.reshape(shape3)
    v_scr[...] = xv.reshape(shape3)
    nkk_scr[...] = (-kk).reshape(shape3)
    kka_scr[...] = (kk * a).reshape(shape3)

    lo, eye2 = _pair_masks()

    def step(t, carry):
        for b in range(SEQ_BLK):
            for p in range(2):
                s = s_scr[b, p]
                sa = _seg_sum(s * _bcast(nkk_scr, t, b, p), lo)
                vcol = _col(_bcast(v_scr, t, b, p), lo, eye2)
                s = s * _bcast(w_scr, t, b, p) + sa * _bcast(kka_scr, t, b, p) + vcol * _bcast(k_scr, t, b, p)
                s_scr[b, p] = s
                ycol = _seg_sum(s * _bcast(r_scr, t, b, p), lo)
                o_scr[t, b:b + 1, p * LANES:(p + 1) * LANES] = _row(ycol, eye2)
        return carry

    lax.fori_loop(0, tc, step, 0)

    y = o_scr[...].reshape(tc * SEQ_BLK, gw)
    mean = _head_sums(y) * (1.0 / HEAD)
    d = y - mean
    var = _head_sums(d * d) * (1.0 / HEAD)
    yn = d * lax.rsqrt(var + RWKV_GN_EPS) * lng_ref[...] + lnb_ref[...]
    bonus = _head_sums(xr * k * rk_ref[...]) * xv
    y_ref[...] = ((yn + bonus) * g).astype(y_ref.dtype)

    @pl.when(c == pl.num_programs(1) - 1)
    def _():
        st_ref[...] = s_scr[...]
        sh_ref[...] = prev_scr[...]


def rwkv_mixer(proj, shift0, s0, mu, w0, a0, k_k, k_a, r_k, ln_g, ln_b, w2p, a2p, g2p, *, t_len, tc):
    nseq = s0.shape[0] // SEQ_BLK
    nchunks = t_len // tc
    n = proj.shape[0]
    tile = pltpu.VMEM((tc, SEQ_BLK, GROUP_WIDTH), F32)
    vec = _full_spec((1, GROUP_WIDTH))
    mat = _full_spec((GROUP_WIDTH, GROUP_WIDTH))
    return pl.pallas_call(
        functools.partial(_rwkv_kernel, tc=tc),
        grid=(nseq, nchunks),
        in_specs=[_row_spec(tc, COL_R, nchunks), _row_spec(tc, COL_K, nchunks),
                  _row_spec(tc, COL_V, nchunks), _row_spec(tc, COL_LORA, nchunks),
                  _seq_spec(shift0.shape), _seq_spec(s0.shape), _full_spec((1, RWKV_PROJ)),
                  vec, vec, vec, vec, vec, vec, vec, mat, mat, mat],
        out_specs=[_row_spec(tc, 0, nchunks), _seq_spec(s0.shape), _seq_spec(shift0.shape)],
        out_shape=[jax.ShapeDtypeStruct((n, GROUP_WIDTH), BF16), jax.ShapeDtypeStruct(s0.shape, F32),
                   jax.ShapeDtypeStruct(shift0.shape, F32)],
        scratch_shapes=[pltpu.VMEM((SEQ_BLK, 2, HEAD, LANES), F32), pltpu.VMEM((SEQ_BLK, RWKV_PROJ), F32),
                        tile, tile, tile, tile, tile, tile, tile],
        compiler_params=_params(("parallel", "arbitrary")),
        name="rwkv_mixer",
    )(proj, proj, proj, proj, shift0, s0, mu, w0, a0, k_k, k_a, r_k, ln_g, ln_b, w2p, a2p, g2p)


def _dot_nt(a, b):
    return lax.dot_general(a, b, (((1,), (1,)), ((), ())), preferred_element_type=F32)


def _dot_tn(a, b):
    return lax.dot_general(a, b, (((0,), (0,)), ((), ())), preferred_element_type=F32)


def _iota(shape, dim):
    return lax.broadcasted_iota(jnp.int32, shape, dim)


def _head_of(idx):
    return lax.shift_right_logical(idx, HEAD.bit_length() - 1)


def _cumsum_rows(x):
    n = x.shape[0]
    tri = jnp.where(_iota((n, n), 0) >= _iota((n, n), 1), 1.0, 0.0).astype(BF16)
    hi = x.astype(BF16)
    rest = x - hi.astype(F32)
    mid = rest.astype(BF16)
    lo = (rest - mid.astype(F32)).astype(BF16)
    return _dot(tri, hi) + _dot(tri, mid) + _dot(tri, lo)


def _own_head(shape, rows_per_head):
    row_h = lax.shift_right_logical(_iota(shape, 0), rows_per_head.bit_length() - 1)
    return row_h == _head_of(_iota(shape, 1))


def _head_expand(x):
    xx = jnp.concatenate([x] * (GROUP_WIDTH // HEAD), axis=0)
    return jnp.where(_own_head(xx.shape, x.shape[0]), xx, 0.0)


def _head_collapse(xx):
    n = xx.shape[0] // (GROUP_WIDTH // HEAD)
    return xx[0:n] + xx[n:2 * n] + xx[2 * n:3 * n] + xx[3 * n:4 * n]


def _block_diag_mask():
    shape = (GROUP_WIDTH, GROUP_WIDTH)
    return _head_of(_iota(shape, 0)) == _head_of(_iota(shape, 1))


def _seq_col_spec(rows, col, ncols):
    return pl.BlockSpec((rows, GROUP_WIDTH), lambda b, c: (c, b * ncols + col))


HGRN_CHUNK = 128
HGRN_SUB = 16


def _hgrn_chunk_kernel(pq_ref, pf_ref, pi_ref, pg_ref, lbl_ref, ng_ref, y_ref, st_ref,
                       w_scr, k_scr, b_scr, v_scr, p_scr, r_scr, o_scr, *, layer):
    L, n = HGRN_CHUNK, HGRN_SUB
    c = pl.program_id(1)

    @pl.when(c == 0)
    def _():
        w_scr[...] = jnp.zeros_like(w_scr)

    lb = _hgrn_lower_bound(lbl_ref, layer)
    z = pf_ref[...]
    g = jnp.logaddexp(jnp.log1p(-lb) + jax.nn.log_sigmoid(z), jnp.log(lb))
    kg = (1.0 - lb) * jax.nn.sigmoid(-z)
    q = jax.nn.silu(pq_ref[...])
    v = pi_ref[...]
    bc = _cumsum_rows(g)
    k_scr[...] = kg
    b_scr[...] = bc
    v_scr[...] = v

    rid = _iota((n, GROUP_WIDTH), 0)
    for sb in range(L // n):
        base = sb * n
        qs, bs = q[base:base + n], bc[base:base + n]
        for s in range(n):
            e = jnp.where(rid >= s, jnp.exp(bs - b_scr[base + s:base + s + 1, :]), 0.0)
            p_scr[(base + s) * n:(base + s + 1) * n, :] = (qs * k_scr[base + s:base + s + 1, :] * e).astype(BF16)
    bd = _block_diag_mask()
    r_scr[...] = _dot(p_scr[...], jnp.where(bd, 1.0, 0.0).astype(BF16))
    for sb in range(L // n):
        base = sb * n
        acc = jnp.zeros((n, GROUP_WIDTH), F32)
        for s in range(n):
            acc = acc + r_scr[(base + s) * n:(base + s + 1) * n, :] * v_scr[base + s:base + s + 1, :]
        o_scr[base:base + n, :] = acc

    vb = v.astype(BF16)
    rows = _iota((L, GROUP_WIDTH), 0)
    for i in range(1, L // n):
        r0 = i * n
        ref = b_scr[r0 - 1:r0, :]
        qt = q[r0:r0 + n] * jnp.exp(bc[r0:r0 + n] - ref)
        kt = jnp.where(rows < r0, kg * jnp.exp(ref - bc), 0.0)
        att = _dot_nt(_head_expand(qt).astype(BF16), kt.astype(BF16))
        ox = _dot(att.astype(BF16), vb)
        o_scr[r0:r0 + n, :] += _head_collapse(jnp.where(_own_head(ox.shape, n), ox, 0.0))

    w = w_scr[...]
    o = o_scr[...] + _dot_nt((q * jnp.exp(bc)).astype(BF16), w.astype(BF16))
    b_end = b_scr[L - 1:L, :]
    upd = _dot_tn(vb, (kg * jnp.exp(b_end - bc)).astype(BF16))
    w_scr[...] = w * jnp.exp(b_end) + jnp.where(bd, upd, 0.0)

    ms = _head_sums(o * o) * (1.0 / HEAD)
    out = o * lax.rsqrt(ms + HGRN_NORM_EPS) * ng_ref[...] * jax.nn.silu(pg_ref[...])
    y_ref[...] = out.astype(y_ref.dtype)

    @pl.when(c == pl.num_programs(1) - 1)
    def _():
        st_ref[...] = w_scr[...]


def hgrn_chunk_mixer(proj2, lb_logits, norm_g, *, bsz, t_len, layer):
    L, n = HGRN_CHUNK, HGRN_SUB
    ncols = PROJ_WIDTH // GROUP_WIDTH
    tile = pltpu.VMEM((L, GROUP_WIDTH), F32)
    y, st = pl.pallas_call(
        functools.partial(_hgrn_chunk_kernel, layer=layer),
        grid=(bsz, t_len // L),
        in_specs=[_seq_col_spec(L, COL_Q, ncols), _seq_col_spec(L, COL_F, ncols),
                  _seq_col_spec(L, COL_I, ncols), _seq_col_spec(L, COL_G, ncols),
                  pl.BlockSpec((DEPTH, GROUP_WIDTH), lambda b, c: (0, 0)),
                  pl.BlockSpec((1, GROUP_WIDTH), lambda b, c: (0, 0))],
        out_specs=[_seq_col_spec(L, 0, 1), pl.BlockSpec((GROUP_WIDTH, GROUP_WIDTH), lambda b, c: (b, 0))],
        out_shape=[jax.ShapeDtypeStruct((t_len, bsz * GROUP_WIDTH), BF16),
                   jax.ShapeDtypeStruct((bsz * GROUP_WIDTH, GROUP_WIDTH), F32)],
        scratch_shapes=[pltpu.VMEM((GROUP_WIDTH, GROUP_WIDTH), F32), tile, tile, tile,
                        pltpu.VMEM((L * n, GROUP_WIDTH), BF16), pltpu.VMEM((L * n, GROUP_WIDTH), F32), tile],
        compiler_params=_params(("parallel", "arbitrary")),
        name="hgrn_chunk_mixer",
    )(proj2, proj2, proj2, proj2, lb_logits, norm_g)
    return y, st


RWKV_CHUNK = 64


def _rwkv_chunk_kernel(*refs, nseq):
    params = refs[4 * nseq:4 * nseq + 11]
    y_ref, st_ref, sh_ref, w_scr, prev_scr = refs[4 * nseq + 11:]
    gw = GROUP_WIDTH
    c = pl.program_id(1)

    @pl.when(c == 0)
    def _():
        w_scr[...] = jnp.zeros_like(w_scr)
        prev_scr[...] = jnp.zeros_like(prev_scr)

    stages = [_rwkv_chunk_one(*refs[4 * s:4 * s + 4], *params, y_ref.at[:, s * gw:(s + 1) * gw], w_scr.at[s],
                              prev_scr.at[s]) for s in range(nseq)]
    for _ in itertools.zip_longest(*stages):
        pass

    @pl.when(c == pl.num_programs(1) - 1)
    def _():
        for s in range(nseq):
            st_ref[s * gw:(s + 1) * gw, :] = w_scr[s]
        sh_ref[...] = prev_scr[...]


def _rwkv_chunk_one(pr_ref, pk_ref, pv_ref, pl_ref, mu_ref, w0_ref, a0_ref, kk_ref, ka_ref, rk_ref, lng_ref,
                    lnb_ref, w2_ref, a2_ref, g2_ref, y_ref, w_scr, prev_scr):
    L = RWKV_CHUNK
    gw = GROUP_WIDTH
    rid = _iota((L, gw), 0)

    def shifted(ref, j):
        x = ref[...]
        prev = jnp.where(rid == 0, prev_scr[:, j * gw:(j + 1) * gw], pltpu.roll(x, 1, axis=0))
        prev_scr[:, j * gw:(j + 1) * gw] = x[L - 1:L]
        return x + (prev - x) * mu_ref[:, j * gw:(j + 1) * gw]

    xr, xk, xv, xl = shifted(pr_ref, 0), shifted(pk_ref, 1), shifted(pv_ref, 2), shifted(pl_ref, 3)
    w = -jax.nn.softplus(-(w0_ref[...] + _dot(jnp.tanh(xl).astype(BF16), w2_ref[...]))) - 0.5
    lw = -jnp.exp(w)
    a = jax.nn.sigmoid(a0_ref[...] + _dot(xl.astype(BF16), a2_ref[...]))
    g = _dot(jax.nn.sigmoid(xl).astype(BF16), g2_ref[...])
    kk = xk * kk_ref[...]
    kk = kk / jnp.maximum(jnp.sqrt(_head_sums(kk * kk)), 1e-12)
    k = xk * (1.0 + (a - 1.0) * ka_ref[...])
    beta = kk * a
    yield

    cs = _cumsum_rows(lw)
    c_end = cs[L - 1:L]
    e_neg = jnp.exp(-cs)
    e_end = jnp.exp(c_end - cs)
    ar = jnp.concatenate([_head_expand(-kk * jnp.exp(cs - lw)), _head_expand(xr * jnp.exp(cs))], axis=0).astype(BF16)
    bk = jnp.concatenate([_head_expand(beta * e_neg), _head_expand(k * e_neg)], axis=0).astype(BF16)
    vx = _head_expand(xv).astype(BF16)
    yield

    nh = 4 * L
    gmat = _dot_nt(ar, bk)
    tt = _iota((nh, nh), 0) & (L - 1)
    ss = _iota((nh, nh), 1) & (L - 1)
    strict, incl = ss < tt, ss <= tt
    nab = jnp.where(strict, gmat[0:nh, 0:nh], 0.0)
    nak = jnp.where(strict, gmat[0:nh, nh:2 * nh], 0.0).astype(BF16)
    nrb = jnp.where(incl, gmat[nh:2 * nh, 0:nh], 0.0).astype(BF16)
    nrk = jnp.where(incl, gmat[nh:2 * nh, nh:2 * nh], 0.0).astype(BF16)
    yield

    ri, ci = _iota((nh, nh), 0), _iota((nh, nh), 1)

    def same_block(size):
        sh = size.bit_length() - 1
        return lax.shift_right_logical(ri, sh) == lax.shift_right_logical(ci, sh)

    base = 8
    m = jnp.where(same_block(base), nab, 0.0)
    t_inv = jnp.where(ri == ci, 1.0, 0.0) + m
    m = m.astype(BF16)
    for _ in range(base.bit_length() - 2):
        m = _dot(m, m).astype(BF16)
        yield
        t_inv = t_inv + _dot(t_inv.astype(BF16), m)
        yield
    size = base
    while size < L:
        off = jnp.where(same_block(2 * size), jnp.where(same_block(size), 0.0, nab), 0.0).astype(BF16)
        tb = t_inv.astype(BF16)
        half = _dot(tb, off).astype(BF16)
        yield
        t_inv = t_inv + _dot(half, tb)
        yield
        size *= 2

    wst = w_scr[...]
    sw = _dot_nt(ar, wst.astype(BF16))
    rhs = (sw[0:nh] + _dot(nak, vx)).astype(BF16)
    yield
    x = _dot(t_inv.astype(BF16), rhs)
    yield
    ux = x.astype(BF16)
    yx = sw[nh:2 * nh] + _dot(nrb, ux) + _dot(nrk, vx)
    y = _head_collapse(yx)
    u = _head_collapse(x)
    yield

    upd = _dot_tn(jnp.concatenate([u, xv], axis=0).astype(BF16),
                  jnp.concatenate([beta * e_end, k * e_end], axis=0).astype(BF16))
    w_scr[...] = wst * jnp.exp(c_end) + jnp.where(_block_diag_mask(), upd, 0.0)
    yield

    mean = _head_sums(y) * (1.0 / HEAD)
    d = y - mean
    var = _head_sums(d * d) * (1.0 / HEAD)
    yn = d * lax.rsqrt(var + RWKV_GN_EPS) * lng_ref[...] + lnb_ref[...]
    bonus = _head_sums(xr * k * rk_ref[...]) * xv
    y_ref[...] = ((yn + bonus) * g).astype(y_ref.dtype)


RWKV_SEQS_PER_STEP = 8


def _seq_group_col_spec(rows, col, ncols, s, nseq):
    return pl.BlockSpec((rows, GROUP_WIDTH), lambda i, c: (c, (i * nseq + s) * ncols + col))


def rwkv_chunk_mixer(proj2, mu, w0, a0, k_k, k_a, r_k, ln_g, ln_b, w2p, a2p, g2p, *, bsz, t_len):
    L, nseq = RWKV_CHUNK, RWKV_SEQS_PER_STEP
    gw = GROUP_WIDTH
    ncols = PROJ_WIDTH // gw
    vec = pl.BlockSpec((1, gw), lambda i, c: (0, 0))
    mat = pl.BlockSpec((gw, gw), lambda i, c: (0, 0))
    seq_specs = [_seq_group_col_spec(L, col, ncols, s, nseq)
                 for s in range(nseq) for col in (COL_R, COL_K, COL_V, COL_LORA)]
    y, st, sh = pl.pallas_call(
        functools.partial(_rwkv_chunk_kernel, nseq=nseq),
        grid=(bsz // nseq, t_len // L),
        in_specs=seq_specs + [pl.BlockSpec((1, RWKV_PROJ), lambda i, c: (0, 0)),
                              vec, vec, vec, vec, vec, vec, vec, mat, mat, mat],
        out_specs=[pl.BlockSpec((L, nseq * gw), lambda i, c: (c, i)),
                   pl.BlockSpec((nseq * gw, gw), lambda i, c: (i, 0)),
                   pl.BlockSpec((nseq, 1, RWKV_PROJ), lambda i, c: (i, 0, 0))],
        out_shape=[jax.ShapeDtypeStruct((t_len, bsz * gw), BF16),
                   jax.ShapeDtypeStruct((bsz * gw, gw), F32),
                   jax.ShapeDtypeStruct((bsz, 1, RWKV_PROJ), F32)],
        scratch_shapes=[pltpu.VMEM((nseq, gw, gw), F32), pltpu.VMEM((nseq, 1, RWKV_PROJ), F32)],
        compiler_params=_params(("parallel", "arbitrary")),
        name="rwkv_chunk_mixer",
    )(*([proj2] * (4 * nseq)), mu, w0, a0, k_k, k_a, r_k, ln_g, ln_b, w2p, a2p, g2p)
    return y, st, sh


S5_CHUNK = 64


def _s5_seq_kernel(*refs, nseq):
    u_refs = refs[:nseq]
    lr_ref, li_ref, ldt_ref, bre_ref, bim_ref, ccat_ref, d_ref, gw_ref, gb_ref = refs[nseq:nseq + 9]
    y_ref, hr_out, hi_out, h_scr, bu_scr = refs[nseq + 9:]
    tc = S5_CHUNK
    n = nseq * tc
    c = pl.program_id(0)

    @pl.when(c == 0)
    def _():
        h_scr[...] = jnp.zeros_like(h_scr)

    lr, li = lr_ref[...], li_ref[...]
    dt = jnp.exp(ldt_ref[...])
    mag = jnp.exp(lr * dt)
    ab_re, ab_im = mag * jnp.cos(li * dt), mag * jnp.sin(li * dt)
    den = lr * lr + li * li
    zr, zi = ab_re - 1.0, ab_im
    cr = (zr * lr + zi * li) / den
    ci = (zi * lr - zr * li) / den
    bre, bim = bre_ref[...], bim_ref[...]
    bb_re = (cr * bre - ci * bim).astype(BF16)
    bb_im = (cr * bim + ci * bre).astype(BF16)

    u = jnp.concatenate([r[...] for r in u_refs], axis=0)
    ri, cj = _iota((n, n), 0), _iota((n, n), 1)
    lseq, lt = nseq.bit_length() - 1, tc.bit_length() - 1
    to_time = jnp.where(cj == (ri & (nseq - 1)) * tc + lax.shift_right_logical(ri, lseq), 1.0, 0.0).astype(BF16)
    to_seq = jnp.where(cj == (ri & (tc - 1)) * nseq + lax.shift_right_logical(ri, lt), 1.0, 0.0).astype(BF16)
    u_t = _dot(to_time, u.astype(BF16)).astype(BF16)
    bu_scr[:, 0:SSM_FLAT] = _dot(u_t, bb_re)
    bu_scr[:, SSM_FLAT:2 * SSM_FLAT] = _dot(u_t, bb_im)

    ar = jnp.broadcast_to(ab_re, (nseq, SSM_FLAT))
    ai = jnp.broadcast_to(ab_im, (nseq, SSM_FLAT))

    def step(t, carry):
        hr, hi = carry
        rows = pl.ds(pl.multiple_of(t * nseq, nseq), nseq)
        nhr = ar * hr - ai * hi + bu_scr[rows, 0:SSM_FLAT]
        nhi = ar * hi + ai * hr + bu_scr[rows, SSM_FLAT:2 * SSM_FLAT]
        bu_scr[rows, 0:SSM_FLAT] = nhr
        bu_scr[rows, SSM_FLAT:2 * SSM_FLAT] = nhi
        return nhr, nhi

    hr, hi = lax.fori_loop(0, tc, step, (h_scr[0], h_scr[1]))
    h_scr[0] = hr
    h_scr[1] = hi

    y_t = _dot(bu_scr[...].astype(BF16), ccat_ref[...])
    hi_p = y_t.astype(BF16)
    rest = y_t - hi_p.astype(F32)
    mid_p = rest.astype(BF16)
    lo_p = (rest - mid_p.astype(F32)).astype(BF16)
    y = _dot(to_seq, hi_p) + _dot(to_seq, mid_p) + _dot(to_seq, lo_p) + d_ref[...] * u
    z = jax.nn.gelu(y)
    out = z * jax.nn.sigmoid(_dot(z.astype(BF16), gw_ref[...]) + gb_ref[...])
    for b in range(nseq):
        y_ref[:, b * GROUP_WIDTH:(b + 1) * GROUP_WIDTH] = out[b * tc:(b + 1) * tc].astype(y_ref.dtype)

    @pl.when(c == pl.num_programs(0) - 1)
    def _():
        hr_out[...] = hr
        hi_out[...] = hi


def s5_seq_mixer(proj2, lam_re, lam_im, log_dt, b_re_bd, b_im_bd, c_cat, d_skip, glu_w, glu_b, *, bsz, t_len):
    tc = S5_CHUNK
    ncols = PROJ_WIDTH // GROUP_WIDTH
    full = lambda shape: pl.BlockSpec(shape, lambda c: (0,) * len(shape))
    st = jax.ShapeDtypeStruct((bsz, SSM_FLAT), F32)
    u_specs = [pl.BlockSpec((tc, GROUP_WIDTH), functools.partial(lambda c, b: (c, b * ncols + COL_SSM), b=b))
               for b in range(bsz)]
    return pl.pallas_call(
        functools.partial(_s5_seq_kernel, nseq=bsz),
        grid=(t_len // tc,),
        in_specs=u_specs + [full((1, SSM_FLAT)), full((1, SSM_FLAT)), full((1, SSM_FLAT)),
                            full((GROUP_WIDTH, SSM_FLAT)), full((GROUP_WIDTH, SSM_FLAT)),
                            full((2 * SSM_FLAT, GROUP_WIDTH)), full((1, GROUP_WIDTH)),
                            full((GROUP_WIDTH, GROUP_WIDTH)), full((1, GROUP_WIDTH))],
        out_specs=[pl.BlockSpec((tc, bsz * GROUP_WIDTH), lambda c: (c, 0)), full((bsz, SSM_FLAT)),
                   full((bsz, SSM_FLAT))],
        out_shape=[jax.ShapeDtypeStruct((t_len, bsz * GROUP_WIDTH), BF16), st, st],
        scratch_shapes=[pltpu.VMEM((2, bsz, SSM_FLAT), F32), pltpu.VMEM((bsz * tc, 2 * SSM_FLAT), F32)],
        compiler_params=_params(("arbitrary",)),
        name="s5_seq_mixer",
    )(*([proj2] * bsz), lam_re, lam_im, log_dt, b_re_bd, b_im_bd, c_cat, d_skip, glu_w, glu_b)


POOL_CHUNK = 256
POOL_HIST = 16


def _pool_seq_kernel(u_ref, w_ref, sc_ref, y_ref, nbuf_ref, ext_scr):
    L, hist = POOL_CHUNK, POOL_HIST
    c = pl.program_id(1)

    @pl.when(c == 0)
    def _():
        ext_scr[0:hist] = jnp.zeros((hist, GROUP_WIDTH), F32)

    u = u_ref[...]
    ext_scr[hist:hist + L] = u
    e = ext_scr[...]
    a2 = e + pltpu.roll(e, 1, axis=0)
    a4 = a2 + pltpu.roll(a2, 2, axis=0)
    a8 = a4 + pltpu.roll(a4, 4, axis=0)
    a16 = a8 + pltpu.roll(a8, 8, axis=0)
    sums = (a2[hist:], a4[hist:], a8[hist:], a16[hist:])

    shape = (L, GROUP_WIDTH)
    pos = _iota(shape, 0) + c * L
    lane = _iota(shape, 1)
    pooled = None
    for gi in reversed(range(len(POOL_WINDOWS))):
        win = POOL_WINDOWS[gi]
        mean = sums[gi] / jnp.minimum(pos + 1, win).astype(F32)
        pooled = mean if pooled is None else jnp.where(lane < (gi + 1) * HEAD, mean, pooled)
    y_ref[...] = (_dot((pooled - u).astype(BF16), w_ref[...]) * sc_ref[...]).astype(y_ref.dtype)

    nb = ext_scr[L:L + hist]
    ext_scr[0:hist] = nb

    @pl.when(c == pl.num_programs(1) - 1)
    def _():
        nbuf_ref[...] = nb


def pool_seq_mixer(proj2, w_bd, scale, *, bsz, t_len):
    L, hist = POOL_CHUNK, POOL_HIST
    ncols = PROJ_WIDTH // GROUP_WIDTH
    return pl.pallas_call(
        _pool_seq_kernel,
        grid=(bsz, t_len // L),
        in_specs=[_seq_col_spec(L, COL_POOL, ncols),
                  pl.BlockSpec((GROUP_WIDTH, GROUP_WIDTH), lambda b, c: (0, 0)),
                  pl.BlockSpec((1, GROUP_WIDTH), lambda b, c: (0, 0))],
        out_specs=[_seq_col_spec(L, 0, 1), pl.BlockSpec((hist, GROUP_WIDTH), lambda b, c: (b, 0))],
        out_shape=[jax.ShapeDtypeStruct((t_len, bsz * GROUP_WIDTH), BF16),
                   jax.ShapeDtypeStruct((bsz * hist, GROUP_WIDTH), F32)],
        scratch_shapes=[pltpu.VMEM((hist + L, GROUP_WIDTH), F32)],
        compiler_params=_params(("parallel", "arbitrary")),
        name="pool_seq_mixer",
    )(proj2, w_bd, scale)


def _diag_heads(st, bsz):
    nh = GROUP_WIDTH // HEAD
    s = st.reshape(bsz, nh, HEAD, nh, HEAD)
    return jnp.stack([s[:, h, :, h, :] for h in range(nh)], axis=1)


def _mix_mlp_kernel(h_ref, ya_ref, yb_ref, yc_ref, yd_ref, wo_ref, g2_ref, wu_ref, wd_ref, gf_ref, o_ref,
                    h1_scr, xn_scr, acc_scr, *, final_norm):
    j = pl.program_id(1)
    gw = GROUP_WIDTH

    @pl.when(j == 0)
    def _():
        mix = (_dot(ya_ref[...], wo_ref[0:gw]) + _dot(yb_ref[...], wo_ref[gw:2 * gw])
               + _dot(yc_ref[...], wo_ref[2 * gw:3 * gw]) + _dot(yd_ref[...], wo_ref[3 * gw:4 * gw]))
        h1 = h_ref[...] + mix
        h1_scr[...] = h1
        xn_scr[...] = _rms(h1, g2_ref[...]).astype(BF16)
        acc_scr[...] = jnp.zeros_like(acc_scr)

    up = _dot(xn_scr[...], wu_ref[...])
    act = jnp.square(jnp.maximum(up, 0.0)).astype(BF16)
    acc_scr[...] += _dot(act, wd_ref[...])

    @pl.when(j == pl.num_programs(1) - 1)
    def _():
        out = h1_scr[...] + acc_scr[...]
        if final_norm:
            out = _rms(out, gf_ref[...])
        o_ref[...] = out


def mix_mlp(h, ys, w_out, g2, w_up, w_down, g_final, *, final_norm, nseq=0):
    n = h.shape[0]
    tm = min(512, n // max(nseq, 1))
    tf = 1024
    nt = n // tm // max(nseq, 1)
    row = lambda w: pl.BlockSpec((tm, w), lambda i, j: (i, 0))
    mix = pl.BlockSpec((tm, GROUP_WIDTH), _tile_map(nseq, nt))
    return pl.pallas_call(
        functools.partial(_mix_mlp_kernel, final_norm=final_norm),
        grid=(n // tm, D_FF // tf),
        in_specs=[row(D_MODEL), mix, mix, mix, mix,
                  pl.BlockSpec((D_MODEL, D_MODEL), lambda i, j: (0, 0)),
                  pl.BlockSpec((1, D_MODEL), lambda i, j: (0, 0)),
                  pl.BlockSpec((D_MODEL, tf), lambda i, j: (0, j)),
                  pl.BlockSpec((tf, D_MODEL), lambda i, j: (j, 0)),
                  pl.BlockSpec((1, D_MODEL), lambda i, j: (0, 0))],
        out_specs=row(D_MODEL),
        out_shape=jax.ShapeDtypeStruct((n, D_MODEL), F32),
        scratch_shapes=[pltpu.VMEM((tm, D_MODEL), F32), pltpu.VMEM((tm, D_MODEL), BF16),
                        pltpu.VMEM((tm, D_MODEL), F32)],
        compiler_params=_params(("parallel", "arbitrary")),
        name="mix_mlp",
    )(h, *ys, w_out, g2, w_up, w_down, g_final)


def _hgrn_state_in(s):
    b = s.shape[0]
    return s.reshape(b, 2, 2, HEAD, HEAD).transpose(0, 1, 4, 2, 3).reshape(b, 2, HEAD, LANES)


def _hgrn_state_out(s):
    b = s.shape[0]
    return s.reshape(b, 2, HEAD, 2, HEAD).transpose(0, 1, 3, 4, 2).reshape(b, 4, HEAD, HEAD)


def _wkv_state_in(s):
    b = s.shape[0]
    return s.reshape(b, 2, 2, HEAD, HEAD).transpose(0, 1, 3, 2, 4).reshape(b, 2, HEAD, LANES)


def _wkv_state_out(s):
    b = s.shape[0]
    return s.reshape(b, 2, HEAD, 2, HEAD).transpose(0, 1, 3, 2, 4).reshape(b, 4, HEAD, HEAD)


def _block_diag(blocks):
    g, r, c = blocks.shape
    eye = jnp.eye(g, dtype=blocks.dtype)
    return (blocks[:, :, None, :] * eye[:, None, :, None]).reshape(g * r, g * c)


def _pad_rows(w, start):
    return jnp.zeros((GROUP_WIDTH, GROUP_WIDTH), w.dtype).at[start:start + w.shape[0]].set(w)


def _layer_params(l, P):
    row = lambda a: a.reshape(1, -1)
    q = {}
    q["norm1_g"] = row(P["norm1_g"][l])
    q["w_in"] = P["w_in"][l].astype(BF16)
    q["lam_re"] = row(P["ssm_lambda_re"][l])
    q["lam_im"] = row(P["ssm_lambda_im"][l])
    q["log_dt"] = row(jnp.repeat(P["ssm_log_dt"][l], SSM_STATE))
    q["b_re"] = _block_diag(P["ssm_b_re"][l].transpose(0, 2, 1))
    q["b_im"] = _block_diag(P["ssm_b_im"][l].transpose(0, 2, 1))
    q["c_cat"] = jnp.concatenate([_block_diag(P["ssm_c_re"][l].transpose(0, 2, 1)),
                                  -_block_diag(P["ssm_c_im"][l].transpose(0, 2, 1))], axis=0).astype(BF16)
    q["ssm_d"] = row(P["ssm_d"][l])
    q["glu_w"] = P["ssm_glu_w"][l].astype(BF16)
    q["glu_b"] = row(P["ssm_glu_b"][l])
    q["hgrn_norm_g"] = row(P["hgrn_norm_g"][l])
    q["mu"] = row(P["rwkv_mu"][l])
    for name in ("w0", "a0", "k_k", "k_a", "r_k", "ln_g", "ln_b"):
        q[name] = row(P["rwkv_" + name][l])
    q["w2p"] = _pad_rows(P["rwkv_w2"][l], 0).astype(BF16)
    q["a2p"] = _pad_rows(P["rwkv_a2"][l], DECAY_LORA).astype(BF16)
    q["g2p"] = _pad_rows(P["rwkv_g2"][l], DECAY_LORA + AAA_LORA).astype(BF16)
    q["pool_w"] = _block_diag(P["pool_w"][l]).astype(BF16)
    q["pool_scale"] = row(P["pool_scale"][l])
    q["w_out"] = P["w_out"][l].astype(BF16)
    q["norm2_g"] = row(P["norm2_g"][l])
    q["mlp_up"] = P["mlp_up"][l].astype(BF16)
    q["mlp_down"] = P["mlp_down"][l].astype(BF16)
    return q


def _trunk_fresh(x_rows, bsz, t_len, layer_params, P):
    h = x_rows
    new = [[] for _ in range(6)]
    g_final = P["norm_f_g"].reshape(1, -1)
    for l in range(DEPTH):
        q = layer_params[l]
        proj2 = rms_proj(h, q["norm1_g"], q["w_in"], nseq=bsz)
        y_a, s_re, s_im = s5_seq_mixer(proj2, q["lam_re"], q["lam_im"], q["log_dt"], q["b_re"], q["b_im"],
                                       q["c_cat"], q["ssm_d"], q["glu_w"], q["glu_b"], bsz=bsz, t_len=t_len)
        y_b, s_hg = hgrn_chunk_mixer(proj2, P["hgrn_lb_logits"], q["hgrn_norm_g"], bsz=bsz, t_len=t_len, layer=l)
        y_c, s_wkv, s_sh = rwkv_chunk_mixer(proj2, q["mu"], q["w0"], q["a0"], q["k_k"], q["k_a"], q["r_k"],
                                            q["ln_g"], q["ln_b"], q["w2p"], q["a2p"], q["g2p"], bsz=bsz, t_len=t_len)
        y_d, s_pool = pool_seq_mixer(proj2, q["pool_w"], q["pool_scale"], bsz=bsz, t_len=t_len)
        h = mix_mlp(h, (y_a, y_b, y_c, y_d), q["w_out"], q["norm2_g"], q["mlp_up"], q["mlp_down"], g_final,
                    final_norm=(l == DEPTH - 1), nseq=bsz)
        s_hg = _diag_heads(s_hg, bsz).swapaxes(-1, -2)
        s_wkv = _diag_heads(s_wkv, bsz)
        s_pool = s_pool.reshape(bsz, POOL_HIST, GROUP_WIDTH)[:, POOL_HIST - POOL_BUF:]
        for lst, s in zip(new, (s_re, s_im, s_hg, s_wkv, s_sh, s_pool)):
            lst.append(s)
    return h, new


def _trunk_carry(x_rows, states, pos0, t_len, layer_params, P):
    ssm_re0, ssm_im0, hgrn0, wkv0, shift0, pool0 = states
    h = x_rows
    new = [[] for _ in range(6)]
    g_final = P["norm_f_g"].reshape(1, -1)
    tc = t_len
    for l in range(DEPTH):
        q = layer_params[l]
        proj = rms_proj(h, q["norm1_g"], q["w_in"])
        y_a, s_re, s_im = s5_mixer(proj, ssm_re0[l], ssm_im0[l], q["lam_re"], q["lam_im"], q["log_dt"], q["b_re"],
                                   q["b_im"], q["c_cat"], q["ssm_d"], q["glu_w"], q["glu_b"], t_len=t_len, tc=tc)
        y_b, s_hg = hgrn_mixer(proj, hgrn0[l], P["hgrn_lb_logits"], q["hgrn_norm_g"], t_len=t_len, tc=tc, layer=l)
        y_c, s_wkv, s_sh = rwkv_mixer(proj, shift0[l], wkv0[l], q["mu"], q["w0"], q["a0"], q["k_k"], q["k_a"],
                                      q["r_k"], q["ln_g"], q["ln_b"], q["w2p"], q["a2p"], q["g2p"],
                                      t_len=t_len, tc=tc)
        y_d, s_pool = pool_mixer(proj, pool0[l], q["pool_w"], q["pool_scale"], t_len=t_len, tc=tc, pos0=pos0)
        h = mix_mlp(h, (y_a, y_b, y_c, y_d), q["w_out"], q["norm2_g"], q["mlp_up"], q["mlp_down"], g_final,
                    final_norm=(l == DEPTH - 1))
        for lst, s in zip(new, (s_re, s_im, _hgrn_state_out(s_hg), _wkv_state_out(s_wkv), s_sh,
                                s_pool.transpose(1, 0, 2))):
            lst.append(s)
    return h, new


def _states_out(new, bsz):
    s_re, s_im, s_hg, s_wkv, s_sh, s_pool = new
    return (jnp.stack([s.reshape(bsz, SSM_GROUPS, SSM_STATE) for s in s_re]),
            jnp.stack([s.reshape(bsz, SSM_GROUPS, SSM_STATE) for s in s_im]),
            jnp.stack(s_hg),
            jnp.stack(s_wkv),
            jnp.stack([s.reshape(bsz, 1, RWKV_PROJ) for s in s_sh]),
            jnp.stack(s_pool))


def kernel(x_prompt, x_sample, state_ssm_re, state_ssm_im, state_hgrn, state_wkv, state_shift, state_pool, norm1_g, w_in, ssm_lambda_re, ssm_lambda_im, ssm_log_dt, ssm_b_re, ssm_b_im, ssm_c_re, ssm_c_im, ssm_d, ssm_glu_w, ssm_glu_b, hgrn_lb_logits, hgrn_norm_g, rwkv_mu, rwkv_w0, rwkv_w2, rwkv_a0, rwkv_a2, rwkv_g2, rwkv_k_k, rwkv_k_a, rwkv_r_k, rwkv_ln_g, rwkv_ln_b, pool_w, pool_scale, w_out, norm2_g, mlp_up, mlp_down, norm_f_g):
    P = dict(norm1_g=norm1_g, w_in=w_in, ssm_lambda_re=ssm_lambda_re, ssm_lambda_im=ssm_lambda_im,
             ssm_log_dt=ssm_log_dt, ssm_b_re=ssm_b_re, ssm_b_im=ssm_b_im, ssm_c_re=ssm_c_re, ssm_c_im=ssm_c_im,
             ssm_d=ssm_d, ssm_glu_w=ssm_glu_w, ssm_glu_b=ssm_glu_b, hgrn_lb_logits=hgrn_lb_logits,
             hgrn_norm_g=hgrn_norm_g, rwkv_mu=rwkv_mu, rwkv_w0=rwkv_w0, rwkv_w2=rwkv_w2, rwkv_a0=rwkv_a0,
             rwkv_a2=rwkv_a2, rwkv_g2=rwkv_g2, rwkv_k_k=rwkv_k_k, rwkv_k_a=rwkv_k_a, rwkv_r_k=rwkv_r_k,
             rwkv_ln_g=rwkv_ln_g, rwkv_ln_b=rwkv_ln_b, pool_w=pool_w, pool_scale=pool_scale, w_out=w_out,
             norm2_g=norm2_g, mlp_up=mlp_up, mlp_down=mlp_down, norm_f_g=norm_f_g)
    layer_params = [_layer_params(l, P) for l in range(DEPTH)]

    bp, t_p, _ = x_prompt.shape
    yp, new_p = _trunk_fresh(x_prompt.reshape(bp * t_p, D_MODEL), bp, t_p, layer_params, P)
    y_prompt = yp.reshape(bp, t_p, D_MODEL)

    bs, t_s, _ = x_sample.shape
    nblk = bs // SEQ_BLK
    xs = x_sample.reshape(nblk, SEQ_BLK, t_s, D_MODEL).transpose(0, 2, 1, 3).reshape(bs * t_s, D_MODEL)
    st_s = ([state_ssm_re[l].reshape(bs, SSM_FLAT) for l in range(DEPTH)],
            [state_ssm_im[l].reshape(bs, SSM_FLAT) for l in range(DEPTH)],
            [_hgrn_state_in(state_hgrn[l]) for l in range(DEPTH)],
            [_wkv_state_in(state_wkv[l]) for l in range(DEPTH)],
            [state_shift[l].reshape(bs, RWKV_PROJ) for l in range(DEPTH)],
            [state_pool[l].transpose(1, 0, 2) for l in range(DEPTH)])
    ys, new_s = _trunk_carry(xs, st_s, PAST_LEN, t_s, layer_params, P)
    y_sample = ys.reshape(nblk, t_s, SEQ_BLK, D_MODEL).transpose(0, 2, 1, 3).reshape(bs, t_s, D_MODEL)

    return (y_prompt, y_sample) + _states_out(new_p, bp) + _states_out(new_s, bs)
```

```python
import functools
import itertools

import jax
import jax.numpy as jnp
from jax import lax
from jax.experimental import pallas as pl
from jax.experimental.pallas import tpu as pltpu

F32 = jnp.float32
BF16 = jnp.bfloat16

D_MODEL = 1024
DEPTH = 2
PAST_LEN = 16384
GROUP_WIDTH = 256
HEAD = 64
SSM_GROUPS = 16
SSM_CH = 16
SSM_STATE = 64
SSM_FLAT = SSM_GROUPS * SSM_STATE
POOL_WINDOWS = (2, 4, 8, 16)
POOL_BUF = 15
DECAY_LORA = 64
AAA_LORA = 64
GATE_LORA = 128
RWKV_PROJ = 1024
PROJ_WIDTH = 2560
D_FF = 4096
NORM_EPS = 1e-6
HGRN_NORM_EPS = 1e-5
RWKV_GN_EPS = 64e-5

SEQ_BLK = 8
LANES = 128
VMEM_LIMIT = 48 * 1024 * 1024

COL_SSM, COL_Q, COL_F, COL_I, COL_G, COL_R, COL_K, COL_V, COL_LORA, COL_POOL = range(10)


def _params(sem):
    return pltpu.CompilerParams(dimension_semantics=sem, vmem_limit_bytes=VMEM_LIMIT)


def _dot(a, b):
    return jnp.dot(a, b, preferred_element_type=F32)


def _rms(x, g):
    return x * lax.rsqrt(jnp.mean(x * x, axis=-1, keepdims=True) + NORM_EPS) * g


def _rms_proj_kernel(x_ref, g_ref, w_ref, o_ref):
    o_ref[...] = _dot(_rms(x_ref[...], g_ref[...]).astype(BF16), w_ref[...])


def _tile_map(nseq, nt):
    if nseq == 0:
        return lambda r, *_: (r, 0)
    return lambda r, *_: (r % nt, r // nt)


def rms_proj(x, g, w, *, nseq=0):
    n = x.shape[0]
    tm = min(512, n // max(nseq, 1))
    nt = n // tm // max(nseq, 1)
    out_shape = (n, PROJ_WIDTH) if nseq == 0 else (n // nseq, nseq * PROJ_WIDTH)
    return pl.pallas_call(
        _rms_proj_kernel,
        grid=(n // tm,),
        in_specs=[pl.BlockSpec((tm, D_MODEL), lambda i: (i, 0)),
                  pl.BlockSpec((1, D_MODEL), lambda i: (0, 0)),
                  pl.BlockSpec((D_MODEL, PROJ_WIDTH), lambda i: (0, 0))],
        out_specs=pl.BlockSpec((tm, PROJ_WIDTH), _tile_map(nseq, nt)),
        out_shape=jax.ShapeDtypeStruct(out_shape, F32),
        compiler_params=_params(("parallel",)),
        name="rms_proj",
    )(x, g, w)


def _row_spec(tc, col, nchunks):
    return pl.BlockSpec((tc * SEQ_BLK, GROUP_WIDTH), lambda s, c: (s * nchunks + c, col))


def _full_spec(shape):
    nd = len(shape)
    return pl.BlockSpec(shape, lambda s, c: (0,) * nd)


def _seq_spec(shape):
    nd = len(shape)
    return pl.BlockSpec((SEQ_BLK,) + shape[1:], lambda s, c: (s,) + (0,) * (nd - 1))


def _head_sums(x):
    lane = lax.broadcasted_iota(jnp.int32, x.shape, 1)
    out = jnp.zeros_like(x)
    for h in range(GROUP_WIDTH // HEAD):
        m = (lane >= h * HEAD) & (lane < (h + 1) * HEAD)
        s = jnp.sum(jnp.where(m, x, 0.0), axis=1, keepdims=True)
        out = jnp.where(m, s, out)
    return out


def _pair_masks():
    lane = lax.broadcasted_iota(jnp.int32, (HEAD, LANES), 1)
    sub = lax.broadcasted_iota(jnp.int32, (HEAD, LANES), 0)
    return lane < HEAD, (lane & (HEAD - 1)) == sub


def _seg_sum(p, lo):
    s0 = jnp.sum(jnp.where(lo, p, 0.0), axis=1, keepdims=True)
    s1 = jnp.sum(jnp.where(lo, 0.0, p), axis=1, keepdims=True)
    return jnp.where(lo, s0, s1)


def _col(row_b, lo, eye2):
    return _seg_sum(jnp.where(eye2, row_b, 0.0), lo)


def _row(col_b, eye2):
    return jnp.sum(jnp.where(eye2, col_b, 0.0), axis=0, keepdims=True)


def _bcast(ref, t, b, p):
    r = ref[t, b:b + 1, p * LANES:(p + 1) * LANES]
    return jnp.broadcast_to(r, (HEAD, LANES))


def _s5_kernel(u_ref, h0r_ref, h0i_ref, lr_ref, li_ref, ldt_ref, bre_ref, bim_ref, ccat_ref, d_ref,
               gw_ref, gb_ref, y_ref, hr_out, hi_out, h_scr, bu_scr, *, tc):
    c = pl.program_id(1)

    @pl.when(c == 0)
    def _():
        h_scr[0] = h0r_ref[...]
        h_scr[1] = h0i_ref[...]

    lr, li = lr_ref[...], li_ref[...]
    dt = jnp.exp(ldt_ref[...])
    mag = jnp.exp(lr * dt)
    ab_re, ab_im = mag * jnp.cos(li * dt), mag * jnp.sin(li * dt)
    den = lr * lr + li * li
    zr, zi = ab_re - 1.0, ab_im
    cr = (zr * lr + zi * li) / den
    ci = (zi * lr - zr * li) / den
    bre, bim = bre_ref[...], bim_ref[...]
    bb_re = (cr * bre - ci * bim).astype(BF16)
    bb_im = (cr * bim + ci * bre).astype(BF16)

    u = u_ref[...]
    ub = u.astype(BF16)
    bu_scr[:, 0:SSM_FLAT] = _dot(ub, bb_re)
    bu_scr[:, SSM_FLAT:2 * SSM_FLAT] = _dot(ub, bb_im)

    ar = jnp.broadcast_to(ab_re, (SEQ_BLK, SSM_FLAT))
    ai = jnp.broadcast_to(ab_im, (SEQ_BLK, SSM_FLAT))

    def step(t, carry):
        hr, hi = carry
        rows = pl.ds(pl.multiple_of(t * SEQ_BLK, SEQ_BLK), SEQ_BLK)
        nhr = ar * hr - ai * hi + bu_scr[rows, 0:SSM_FLAT]
        nhi = ar * hi + ai * hr + bu_scr[rows, SSM_FLAT:2 * SSM_FLAT]
        bu_scr[rows, 0:SSM_FLAT] = nhr
        bu_scr[rows, SSM_FLAT:2 * SSM_FLAT] = nhi
        return nhr, nhi

    hr, hi = lax.fori_loop(0, tc, step, (h_scr[0], h_scr[1]))
    h_scr[0] = hr
    h_scr[1] = hi

    y = _dot(bu_scr[...].astype(BF16), ccat_ref[...]) + d_ref[...] * u
    z = jax.nn.gelu(y)
    out = z * jax.nn.sigmoid(_dot(z.astype(BF16), gw_ref[...]) + gb_ref[...])
    y_ref[...] = out.astype(y_ref.dtype)

    @pl.when(c == pl.num_programs(1) - 1)
    def _():
        hr_out[...] = hr
        hi_out[...] = hi


def s5_mixer(proj, h0_re, h0_im, lam_re, lam_im, log_dt, b_re_bd, b_im_bd, c_cat, d_skip, glu_w, glu_b, *, t_len, tc):
    nseq = h0_re.shape[0] // SEQ_BLK
    nchunks = t_len // tc
    n = proj.shape[0]
    st = jax.ShapeDtypeStruct(h0_re.shape, F32)
    return pl.pallas_call(
        functools.partial(_s5_kernel, tc=tc),
        grid=(nseq, nchunks),
        in_specs=[_row_spec(tc, COL_SSM, nchunks),
                  _seq_spec(h0_re.shape), _seq_spec(h0_im.shape),
                  _full_spec((1, SSM_FLAT)), _full_spec((1, SSM_FLAT)), _full_spec((1, SSM_FLAT)),
                  _full_spec((GROUP_WIDTH, SSM_FLAT)), _full_spec((GROUP_WIDTH, SSM_FLAT)),
                  _full_spec((2 * SSM_FLAT, GROUP_WIDTH)), _full_spec((1, GROUP_WIDTH)),
                  _full_spec((GROUP_WIDTH, GROUP_WIDTH)), _full_spec((1, GROUP_WIDTH))],
        out_specs=[_row_spec(tc, 0, nchunks), _seq_spec(h0_re.shape), _seq_spec(h0_im.shape)],
        out_shape=[jax.ShapeDtypeStruct((n, GROUP_WIDTH), BF16), st, st],
        scratch_shapes=[pltpu.VMEM((2, SEQ_BLK, SSM_FLAT), F32),
                        pltpu.VMEM((tc * SEQ_BLK, 2 * SSM_FLAT), F32)],
        compiler_params=_params(("parallel", "arbitrary")),
        name="s5_mixer",
    )(proj, h0_re, h0_im, lam_re, lam_im, log_dt, b_re_bd, b_im_bd, c_cat, d_skip, glu_w, glu_b)


def _pool_kernel(u_ref, buf_ref, w_ref, sc_ref, y_ref, nbuf_ref, ext_scr, *, tc, pos0):
    c = pl.program_id(1)

    @pl.when(c == 0)
    def _():
        ext_scr[0:POOL_BUF] = buf_ref[...]

    u = u_ref[...].reshape(tc, SEQ_BLK, GROUP_WIDTH)
    ext_scr[POOL_BUF:POOL_BUF + tc] = u
    a1 = ext_scr[...]
    a2 = a1[1:] + a1[:-1]
    a4 = a2[2:] + a2[:-2]
    a8 = a4[4:] + a4[:-4]
    a16 = a8[8:] + a8[:-8]
    sums = (a2[14:], a4[12:], a8[8:], a16)

    shape = (tc, SEQ_BLK, GROUP_WIDTH)
    pos = lax.broadcasted_iota(jnp.int32, shape, 0) + (c * tc + pos0)
    lane = lax.broadcasted_iota(jnp.int32, shape, 2)
    pooled = None
    for gi in reversed(range(len(POOL_WINDOWS))):
        win = POOL_WINDOWS[gi]
        mean = sums[gi] / jnp.minimum(pos + 1, win).astype(F32)
        pooled = mean if pooled is None else jnp.where(lane < (gi + 1) * HEAD, mean, pooled)
    pooled = (pooled - u).reshape(tc * SEQ_BLK, GROUP_WIDTH)
    y_ref[...] = (_dot(pooled.astype(BF16), w_ref[...]) * sc_ref[...]).astype(y_ref.dtype)

    nb = ext_scr[tc:tc + POOL_BUF]
    ext_scr[0:POOL_BUF] = nb

    @pl.when(c == pl.num_programs(1) - 1)
    def _():
        nbuf_ref[...] = nb


def pool_mixer(proj, buf, w_bd, scale, *, t_len, tc, pos0):
    nseq = buf.shape[1] // SEQ_BLK
    nchunks = t_len // tc
    n = proj.shape[0]
    buf_spec = pl.BlockSpec((POOL_BUF, SEQ_BLK, GROUP_WIDTH), lambda s, c: (0, s, 0))
    return pl.pallas_call(
        functools.partial(_pool_kernel, tc=tc, pos0=pos0),
        grid=(nseq, nchunks),
        in_specs=[_row_spec(tc, COL_POOL, nchunks), buf_spec,
                  _full_spec((GROUP_WIDTH, GROUP_WIDTH)), _full_spec((1, GROUP_WIDTH))],
        out_specs=[_row_spec(tc, 0, nchunks), buf_spec],
        out_shape=[jax.ShapeDtypeStruct((n, GROUP_WIDTH), BF16), jax.ShapeDtypeStruct(buf.shape, F32)],
        scratch_shapes=[pltpu.VMEM((tc + POOL_BUF, SEQ_BLK, GROUP_WIDTH), F32)],
        compiler_params=_params(("parallel", "arbitrary")),
        name="pool_mixer",
    )(proj, buf, w_bd, scale)


def _hgrn_lower_bound(logits_ref, layer):
    rows = [logits_ref[l:l + 1, :] for l in range(DEPTH)]
    m = functools.reduce(jnp.maximum, rows)
    es = [jnp.exp(r - m) for r in rows]
    tot = functools.reduce(lambda a, b: a + b, es)
    lb = jnp.zeros_like(m)
    for l in range(1, layer + 1):
        lb = lb + es[l] / tot
    return lb


def _hgrn_kernel(pq_ref, pf_ref, pi_ref, pg_ref, s0_ref, lbl_ref, ng_ref, y_ref, st_ref,
                 s_scr, q_scr, f_scr, k_scr, v_scr, o_scr, *, tc, layer):
    c = pl.program_id(1)
    shape3 = (tc, SEQ_BLK, GROUP_WIDTH)

    @pl.when(c == 0)
    def _():
        s_scr[...] = s0_ref[...]

    lb = _hgrn_lower_bound(lbl_ref, layer)
    zf = pf_ref[...]
    f_scr[...] = (lb + (1.0 - lb) * jax.nn.sigmoid(zf)).reshape(shape3)
    k_scr[...] = ((1.0 - lb) * jax.nn.sigmoid(-zf)).reshape(shape3)
    q_scr[...] = jax.nn.silu(pq_ref[...]).reshape(shape3)
    v_scr[...] = pi_ref[...].reshape(shape3)

    lo, eye2 = _pair_masks()

    def step(t, carry):
        for b in range(SEQ_BLK):
            for p in range(2):
                s = s_scr[b, p]
                vcol = _col(_bcast(v_scr, t, b, p), lo, eye2)
                s = s * _bcast(f_scr, t, b, p) + vcol * _bcast(k_scr, t, b, p)
                s_scr[b, p] = s
                ocol = _seg_sum(s * _bcast(q_scr, t, b, p), lo)
                o_scr[t, b:b + 1, p * LANES:(p + 1) * LANES] = _row(ocol, eye2)
        return carry

    lax.fori_loop(0, tc, step, 0)

    o = o_scr[...].reshape(tc * SEQ_BLK, GROUP_WIDTH)
    ms = _head_sums(o * o) * (1.0 / HEAD)
    out = o * lax.rsqrt(ms + HGRN_NORM_EPS) * ng_ref[...] * jax.nn.silu(pg_ref[...])
    y_ref[...] = out.astype(y_ref.dtype)

    @pl.when(c == pl.num_programs(1) - 1)
    def _():
        st_ref[...] = s_scr[...]


def hgrn_mixer(proj, s0, lb_logits, norm_g, *, t_len, tc, layer):
    nseq = s0.shape[0] // SEQ_BLK
    nchunks = t_len // tc
    n = proj.shape[0]
    tile = pltpu.VMEM((tc, SEQ_BLK, GROUP_WIDTH), F32)
    return pl.pallas_call(
        functools.partial(_hgrn_kernel, tc=tc, layer=layer),
        grid=(nseq, nchunks),
        in_specs=[_row_spec(tc, COL_Q, nchunks), _row_spec(tc, COL_F, nchunks),
                  _row_spec(tc, COL_I, nchunks), _row_spec(tc, COL_G, nchunks),
                  _seq_spec(s0.shape), _full_spec((DEPTH, GROUP_WIDTH)), _full_spec((1, GROUP_WIDTH))],
        out_specs=[_row_spec(tc, 0, nchunks), _seq_spec(s0.shape)],
        out_shape=[jax.ShapeDtypeStruct((n, GROUP_WIDTH), BF16), jax.ShapeDtypeStruct(s0.shape, F32)],
        scratch_shapes=[pltpu.VMEM((SEQ_BLK, 2, HEAD, LANES), F32), tile, tile, tile, tile, tile],
        compiler_params=_params(("parallel", "arbitrary")),
        name="hgrn_mixer",
    )(proj, proj, proj, proj, s0, lb_logits, norm_g)


def _rwkv_kernel(pr_ref, pk_ref, pv_ref, pl_ref, sh0_ref, s0_ref, mu_ref, w0_ref, a0_ref, kk_ref, ka_ref,
                 rk_ref, lng_ref, lnb_ref, w2_ref, a2_ref, g2_ref, y_ref, st_ref, sh_ref,
                 s_scr, prev_scr, r_scr, w_scr, k_scr, v_scr, nkk_scr, kka_scr, o_scr, *, tc):
    c = pl.program_id(1)
    shape3 = (tc, SEQ_BLK, GROUP_WIDTH)
    gw = GROUP_WIDTH

    @pl.when(c == 0)
    def _():
        s_scr[...] = s0_ref[...]
        prev_scr[...] = sh0_ref[...]

    def shifted(ref, j):
        x = ref[...].reshape(shape3)
        first = prev_scr[:, j * gw:(j + 1) * gw].reshape(1, SEQ_BLK, gw)
        prev = first if tc == 1 else jnp.concatenate([first, x[:-1]], axis=0)
        prev_scr[:, j * gw:(j + 1) * gw] = x[tc - 1]
        return (x + (prev - x) * mu_ref[:, j * gw:(j + 1) * gw]).reshape(tc * SEQ_BLK, gw)

    xr, xk, xv, xl = shifted(pr_ref, 0), shifted(pk_ref, 1), shifted(pv_ref, 2), shifted(pl_ref, 3)
    w = -jax.nn.softplus(-(w0_ref[...] + _dot(jnp.tanh(xl).astype(BF16), w2_ref[...]))) - 0.5
    decay = jnp.exp(-jnp.exp(w))
    a = jax.nn.sigmoid(a0_ref[...] + _dot(xl.astype(BF16), a2_ref[...]))
    g = _dot(jax.nn.sigmoid(xl).astype(BF16), g2_ref[...])
    kk = xk * kk_ref[...]
    kk = kk / jnp.maximum(jnp.sqrt(_head_sums(kk * kk)), 1e-12)
    k = xk * (1.0 + (a - 1.0) * ka_ref[...])

    r_scr[...] = xr.reshape(shape3)
    w_scr[...] = decay.reshape(shape3)
    k_scr[...] = k.reshape(shape3)
    v_scr[...] = xv.reshape(shape3)
    nkk_scr[...] = (-kk).reshape(shape3)
    kka_scr[...] = (kk * a).reshape(shape3)

    lo, eye2 = _pair_masks()

    def step(t, carry):
        for b in range(SEQ_BLK):
            for p in range(2):
                s = s_scr[b, p]
                sa = _seg_sum(s * _bcast(nkk_scr, t, b, p), lo)
                vcol = _col(_bcast(v_scr, t, b, p), lo, eye2)
                s = s * _bcast(w_scr, t, b, p) + sa * _bcast(kka_scr, t, b, p) + vcol * _bcast(k_scr, t, b, p)
                s_scr[b, p] = s
                ycol = _seg_sum(s * _bcast(r_scr, t, b, p), lo)
                o_scr[t, b:b + 1, p * LANES:(p + 1) * LANES] = _row(ycol, eye2)
        return carry

    lax.fori_loop(0, tc, step, 0)

    y = o_scr[...].reshape(tc * SEQ_BLK, gw)
    mean = _head_sums(y) * (1.0 / HEAD)
    d = y - mean
    var = _head_sums(d * d) * (1.0 / HEAD)
    yn = d * lax.rsqrt(var + RWKV_GN_EPS) * lng_ref[...] + lnb_ref[...]
    bonus = _head_sums(xr * k * rk_ref[...]) * xv
    y_ref[...] = ((yn + bonus) * g).astype(y_ref.dtype)

    @pl.when(c == pl.num_programs(1) - 1)
    def _():
        st_ref[...] = s_scr[...]
        sh_ref[...] = prev_scr[...]


def rwkv_mixer(proj, shift0, s0, mu, w0, a0, k_k, k_a, r_k, ln_g, ln_b, w2p, a2p, g2p, *, t_len, tc):
    nseq = s0.shape[0] // SEQ_BLK
    nchunks = t_len // tc
    n = proj.shape[0]
    tile = pltpu.VMEM((tc, SEQ_BLK, GROUP_WIDTH), F32)
    vec = _full_spec((1, GROUP_WIDTH))
    mat = _full_spec((GROUP_WIDTH, GROUP_WIDTH))
    return pl.pallas_call(
        functools.partial(_rwkv_kernel, tc=tc),
        grid=(nseq, nchunks),
        in_specs=[_row_spec(tc, COL_R, nchunks), _row_spec(tc, COL_K, nchunks),
                  _row_spec(tc, COL_V, nchunks), _row_spec(tc, COL_LORA, nchunks),
                  _seq_spec(shift0.shape), _seq_spec(s0.shape), _full_spec((1, RWKV_PROJ)),
                  vec, vec, vec, vec, vec, vec, vec, mat, mat, mat],
        out_specs=[_row_spec(tc, 0, nchunks), _seq_spec(s0.shape), _seq_spec(shift0.shape)],
        out_shape=[jax.ShapeDtypeStruct((n, GROUP_WIDTH), BF16), jax.ShapeDtypeStruct(s0.shape, F32),
                   jax.ShapeDtypeStruct(shift0.shape, F32)],
        scratch_shapes=[pltpu.VMEM((SEQ_BLK, 2, HEAD, LANES), F32), pltpu.VMEM((SEQ_BLK, RWKV_PROJ), F32),
                        tile, tile, tile, tile, tile, tile, tile],
        compiler_params=_params(("parallel", "arbitrary")),
        name="rwkv_mixer",
    )(proj, proj, proj, proj, shift0, s0, mu, w0, a0, k_k, k_a, r_k, ln_g, ln_b, w2p, a2p, g2p)


def _dot_nt(a, b):
    return lax.dot_general(a, b, (((1,), (1,)), ((), ())), preferred_element_type=F32)


def _dot_tn(a, b):
    return lax.dot_general(a, b, (((0,), (0,)), ((), ())), preferred_element_type=F32)


def _iota(shape, dim):
    return lax.broadcasted_iota(jnp.int32, shape, dim)


def _head_of(idx):
    return lax.shift_right_logical(idx, HEAD.bit_length() - 1)


def _cumsum_rows(x):
    n = x.shape[0]
    tri = jnp.where(_iota((n, n), 0) >= _iota((n, n), 1), 1.0, 0.0).astype(BF16)
    hi = x.astype(BF16)
    rest = x - hi.astype(F32)
    mid = rest.astype(BF16)
    lo = (rest - mid.astype(F32)).astype(BF16)
    return _dot(tri, hi) + _dot(tri, mid) + _dot(tri, lo)


def _own_head(shape, rows_per_head):
    row_h = lax.shift_right_logical(_iota(shape, 0), rows_per_head.bit_length() - 1)
    return row_h == _head_of(_iota(shape, 1))


def _head_expand(x):
    xx = jnp.concatenate([x] * (GROUP_WIDTH // HEAD), axis=0)
    return jnp.where(_own_head(xx.shape, x.shape[0]), xx, 0.0)


def _head_collapse(xx):
    n = xx.shape[0] // (GROUP_WIDTH // HEAD)
    return xx[0:n] + xx[n:2 * n] + xx[2 * n:3 * n] + xx[3 * n:4 * n]


def _block_diag_mask():
    shape = (GROUP_WIDTH, GROUP_WIDTH)
    return _head_of(_iota(shape, 0)) == _head_of(_iota(shape, 1))


def _seq_col_spec(rows, col, ncols):
    return pl.BlockSpec((rows, GROUP_WIDTH), lambda b, c: (c, b * ncols + col))


HGRN_CHUNK = 128
HGRN_SUB = 16


HGRN_SEQS_PER_STEP = 4


def _run_staged(stages):
    for _ in itertools.zip_longest(*stages):
        pass


def _hgrn_chunk_kernel(*refs, nseq, layer):
    lbl_ref, ng_ref, y_ref, st_ref, w_scr, k_scr, b_scr, v_scr, p_scr, o_scr = refs[4 * nseq:]
    gw = GROUP_WIDTH
    c = pl.program_id(1)

    @pl.when(c == 0)
    def _():
        w_scr[...] = jnp.zeros_like(w_scr)
        p_scr[...] = jnp.zeros_like(p_scr)

    _run_staged([_hgrn_chunk_one(*refs[4 * s:4 * s + 4], lbl_ref, ng_ref, y_ref.at[:, s * gw:(s + 1) * gw],
                                 w_scr.at[s], k_scr.at[s], b_scr.at[s], v_scr.at[s], p_scr.at[s], o_scr.at[s],
                                 layer=layer) for s in range(nseq)])

    @pl.when(c == pl.num_programs(1) - 1)
    def _():
        for s in range(nseq):
            st_ref[s * gw:(s + 1) * gw, :] = w_scr[s]


def _hgrn_chunk_one(pq_ref, pf_ref, pi_ref, pg_ref, lbl_ref, ng_ref, y_ref, w_scr, k_scr, b_scr, v_scr, p_scr,
                    o_scr, *, layer):
    L, n = HGRN_CHUNK, HGRN_SUB
    half = n // 2
    lb = _hgrn_lower_bound(lbl_ref, layer)
    z = pf_ref[...]
    g = jnp.logaddexp(jnp.log1p(-lb) + jax.nn.log_sigmoid(z), jnp.log(lb))
    kg = (1.0 - lb) * jax.nn.sigmoid(-z)
    q = jax.nn.silu(pq_ref[...])
    v = pi_ref[...]
    bc = _cumsum_rows(g)
    k_scr[...] = kg
    b_scr[...] = bc
    v_scr[...] = v
    yield

    bd = _block_diag_mask()
    ones_bd = jnp.where(bd, 1.0, 0.0).astype(BF16)
    rid = _iota((n, GROUP_WIDTH), 0)
    rid_lo = _iota((half, GROUP_WIDTH), 0) + half
    for sb in range(L // n):
        base = sb * n
        qs, bs = q[base:base + n], bc[base:base + n]
        q_lo, b_lo = q[base + half:base + n], bc[base + half:base + n]
        for s in range(n):
            ks, bsrow = k_scr[base + s:base + s + 1, :], b_scr[base + s:base + s + 1, :]
            if s < half:
                p_scr[s * n:(s + 1) * n, :] = qs * ks * jnp.where(rid >= s, jnp.exp(bs - bsrow), 0.0)
            else:
                p_scr[s * n + half:(s + 1) * n, :] = q_lo * ks * jnp.where(rid_lo >= s, jnp.exp(b_lo - bsrow), 0.0)
        r = _dot(p_scr[...].astype(BF16), ones_bd)
        acc = jnp.zeros((n, GROUP_WIDTH), F32)
        for s in range(n):
            acc = acc + r[s * n:(s + 1) * n] * v_scr[base + s:base + s + 1, :]
        o_scr[base:base + n, :] = acc
        yield

    vb = v.astype(BF16)
    for i in range(1, L // n):
        r0 = i * n
        ref = b_scr[r0 - 1:r0, :]
        qt = q[r0:r0 + n] * jnp.exp(bc[r0:r0 + n] - ref)
        kt = jnp.concatenate([kg[:r0] * jnp.exp(ref - bc[:r0]), jnp.zeros((L - r0, GROUP_WIDTH), F32)], axis=0)
        att = _dot_nt(_head_expand(qt).astype(BF16), kt.astype(BF16))
        yield
        ox = _dot(att.astype(BF16), vb)
        o_scr[r0:r0 + n, :] += _head_collapse(jnp.where(_own_head(ox.shape, n), ox, 0.0))
        yield

    w = w_scr[...]
    o = o_scr[...] + _dot_nt((q * jnp.exp(bc)).astype(BF16), w.astype(BF16))
    b_end = b_scr[L - 1:L, :]
    upd = _dot_tn(vb, (kg * jnp.exp(b_end - bc)).astype(BF16))
    w_scr[...] = w * jnp.exp(b_end) + jnp.where(bd, upd, 0.0)
    yield

    ms = _head_sums(o * o) * (1.0 / HEAD)
    out = o * lax.rsqrt(ms + HGRN_NORM_EPS) * ng_ref[...] * jax.nn.silu(pg_ref[...])
    y_ref[...] = out.astype(y_ref.dtype)


def _seq_group_col_spec(rows, col, ncols, s, nseq):
    return pl.BlockSpec((rows, GROUP_WIDTH), lambda i, c: (c, (i * nseq + s) * ncols + col))


def hgrn_chunk_mixer(proj2, lb_logits, norm_g, *, bsz, t_len, layer):
    L, n, nseq = HGRN_CHUNK, HGRN_SUB, HGRN_SEQS_PER_STEP
    gw = GROUP_WIDTH
    ncols = PROJ_WIDTH // gw
    tile = pltpu.VMEM((nseq, L, gw), F32)
    seq_specs = [_seq_group_col_spec(L, col, ncols, s, nseq)
                 for s in range(nseq) for col in (COL_Q, COL_F, COL_I, COL_G)]
    y, st = pl.pallas_call(
        functools.partial(_hgrn_chunk_kernel, nseq=nseq, layer=layer),
        grid=(bsz // nseq, t_len // L),
        in_specs=seq_specs + [pl.BlockSpec((DEPTH, gw), lambda i, c: (0, 0)),
                              pl.BlockSpec((1, gw), lambda i, c: (0, 0))],
        out_specs=[pl.BlockSpec((L, nseq * gw), lambda i, c: (c, i)),
                   pl.BlockSpec((nseq * gw, gw), lambda i, c: (i, 0))],
        out_shape=[jax.ShapeDtypeStruct((t_len, bsz * gw), BF16),
                   jax.ShapeDtypeStruct((bsz * gw, gw), F32)],
        scratch_shapes=[pltpu.VMEM((nseq, gw, gw), F32), tile, tile, tile,
                        pltpu.VMEM((nseq, n * n, gw), F32), tile],
        compiler_params=_params(("parallel", "arbitrary")),
        name="hgrn_chunk_mixer",
    )(*([proj2] * (4 * nseq)), lb_logits, norm_g)
    return y, st


RWKV_CHUNK = 64


def _rwkv_chunk_kernel(*refs, nseq):
    params = refs[4 * nseq:4 * nseq + 11]
    y_ref, st_ref, sh_ref, w_scr, prev_scr = refs[4 * nseq + 11:]
    gw = GROUP_WIDTH
    c = pl.program_id(1)

    @pl.when(c == 0)
    def _():
        w_scr[...] = jnp.zeros_like(w_scr)
        prev_scr[...] = jnp.zeros_like(prev_scr)

    _run_staged([_rwkv_chunk_one(*refs[4 * s:4 * s + 4], *params, y_ref.at[:, s * gw:(s + 1) * gw], w_scr.at[s],
                                 prev_scr.at[s]) for s in range(nseq)])

    @pl.when(c == pl.num_programs(1) - 1)
    def _():
        for s in range(nseq):
            st_ref[s * gw:(s + 1) * gw, :] = w_scr[s]
        sh_ref[...] = prev_scr[...]


def _rwkv_chunk_one(pr_ref, pk_ref, pv_ref, pl_ref, mu_ref, w0_ref, a0_ref, kk_ref, ka_ref, rk_ref, lng_ref,
                    lnb_ref, w2_ref, a2_ref, g2_ref, y_ref, w_scr, prev_scr):
    L = RWKV_CHUNK
    gw = GROUP_WIDTH
    rid = _iota((L, gw), 0)

    def shifted(ref, j):
        x = ref[...]
        prev = jnp.where(rid == 0, prev_scr[:, j * gw:(j + 1) * gw], pltpu.roll(x, 1, axis=0))
        prev_scr[:, j * gw:(j + 1) * gw] = x[L - 1:L]
        return x + (prev - x) * mu_ref[:, j * gw:(j + 1) * gw]

    xr, xk, xv, xl = shifted(pr_ref, 0), shifted(pk_ref, 1), shifted(pv_ref, 2), shifted(pl_ref, 3)
    w = -jax.nn.softplus(-(w0_ref[...] + _dot(jnp.tanh(xl).astype(BF16), w2_ref[...]))) - 0.5
    lw = -jnp.exp(w)
    a = jax.nn.sigmoid(a0_ref[...] + _dot(xl.astype(BF16), a2_ref[...]))
    g = _dot(jax.nn.sigmoid(xl).astype(BF16), g2_ref[...])
    kk = xk * kk_ref[...]
    kk = kk / jnp.maximum(jnp.sqrt(_head_sums(kk * kk)), 1e-12)
    k = xk * (1.0 + (a - 1.0) * ka_ref[...])
    beta = kk * a
    yield

    cs = _cumsum_rows(lw)
    c_end = cs[L - 1:L]
    e_neg = jnp.exp(-cs)
    e_end = jnp.exp(c_end - cs)
    ar = jnp.concatenate([_head_expand(-kk * jnp.exp(cs - lw)), _head_expand(xr * jnp.exp(cs))], axis=0).astype(BF16)
    bk = jnp.concatenate([_head_expand(beta * e_neg), _head_expand(k * e_neg)], axis=0).astype(BF16)
    vx = _head_expand(xv).astype(BF16)
    yield

    nh = 4 * L
    gmat = _dot_nt(ar, bk)
    tt = _iota((nh, nh), 0) & (L - 1)
    ss = _iota((nh, nh), 1) & (L - 1)
    strict, incl = ss < tt, ss <= tt
    nab = jnp.where(strict, gmat[0:nh, 0:nh], 0.0)
    nak = jnp.where(strict, gmat[0:nh, nh:2 * nh], 0.0).astype(BF16)
    nrb = jnp.where(incl, gmat[nh:2 * nh, 0:nh], 0.0).astype(BF16)
    nrk = jnp.where(incl, gmat[nh:2 * nh, nh:2 * nh], 0.0).astype(BF16)
    yield

    ri, ci = _iota((nh, nh), 0), _iota((nh, nh), 1)

    def same_block(size):
        sh = size.bit_length() - 1
        return lax.shift_right_logical(ri, sh) == lax.shift_right_logical(ci, sh)

    base = 8
    m = jnp.where(same_block(base), nab, 0.0)
    t_inv = jnp.where(ri == ci, 1.0, 0.0) + m
    m = m.astype(BF16)
    for _ in range(base.bit_length() - 2):
        m = _dot(m, m).astype(BF16)
        yield
        t_inv = t_inv + _dot(t_inv.astype(BF16), m)
        yield
    size = base
    while size < L:
        off = jnp.where(same_block(2 * size), jnp.where(same_block(size), 0.0, nab), 0.0).astype(BF16)
        tb = t_inv.astype(BF16)
        half = _dot(tb, off).astype(BF16)
        yield
        t_inv = t_inv + _dot(half, tb)
        yield
        size *= 2

    wst = w_scr[...]
    sw = _dot_nt(ar, wst.astype(BF16))
    rhs = (sw[0:nh] + _dot(nak, vx)).astype(BF16)
    yield
    x = _dot(t_inv.astype(BF16), rhs)
    yield
    ux = x.astype(BF16)
    yx = sw[nh:2 * nh] + _dot(nrb, ux) + _dot(nrk, vx)
    y = _head_collapse(yx)
    u = _head_collapse(x)
    yield

    upd = _dot_tn(jnp.concatenate([u, xv], axis=0).astype(BF16),
                  jnp.concatenate([beta * e_end, k * e_end], axis=0).astype(BF16))
    w_scr[...] = wst * jnp.exp(c_end) + jnp.where(_block_diag_mask(), upd, 0.0)
    yield

    mean = _head_sums(y) * (1.0 / HEAD)
    d = y - mean
    var = _head_sums(d * d) * (1.0 / HEAD)
    yn = d * lax.rsqrt(var + RWKV_GN_EPS) * lng_ref[...] + lnb_ref[...]
    bonus = _head_sums(xr * k * rk_ref[...]) * xv
    y_ref[...] = ((yn + bonus) * g).astype(y_ref.dtype)


RWKV_SEQS_PER_STEP = 8


def rwkv_chunk_mixer(proj2, mu, w0, a0, k_k, k_a, r_k, ln_g, ln_b, w2p, a2p, g2p, *, bsz, t_len):
    L, nseq = RWKV_CHUNK, RWKV_SEQS_PER_STEP
    gw = GROUP_WIDTH
    ncols = PROJ_WIDTH // gw
    vec = pl.BlockSpec((1, gw), lambda i, c: (0, 0))
    mat = pl.BlockSpec((gw, gw), lambda i, c: (0, 0))
    seq_specs = [_seq_group_col_spec(L, col, ncols, s, nseq)
                 for s in range(nseq) for col in (COL_R, COL_K, COL_V, COL_LORA)]
    y, st, sh = pl.pallas_call(
        functools.partial(_rwkv_chunk_kernel, nseq=nseq),
        grid=(bsz // nseq, t_len // L),
        in_specs=seq_specs + [pl.BlockSpec((1, RWKV_PROJ), lambda i, c: (0, 0)),
                              vec, vec, vec, vec, vec, vec, vec, mat, mat, mat],
        out_specs=[pl.BlockSpec((L, nseq * gw), lambda i, c: (c, i)),
                   pl.BlockSpec((nseq * gw, gw), lambda i, c: (i, 0)),
                   pl.BlockSpec((nseq, 1, RWKV_PROJ), lambda i, c: (i, 0, 0))],
        out_shape=[jax.ShapeDtypeStruct((t_len, bsz * gw), BF16),
                   jax.ShapeDtypeStruct((bsz * gw, gw), F32),
                   jax.ShapeDtypeStruct((bsz, 1, RWKV_PROJ), F32)],
        scratch_shapes=[pltpu.VMEM((nseq, gw, gw), F32), pltpu.VMEM((nseq, 1, RWKV_PROJ), F32)],
        compiler_params=_params(("parallel", "arbitrary")),
        name="rwkv_chunk_mixer",
    )(*([proj2] * (4 * nseq)), mu, w0, a0, k_k, k_a, r_k, ln_g, ln_b, w2p, a2p, g2p)
    return y, st, sh


S5_CHUNK = 64


def _s5_seq_kernel(*refs, nseq):
    u_refs = refs[:nseq]
    lr_ref, li_ref, ldt_ref, bre_ref, bim_ref, ccat_ref, d_ref, gw_ref, gb_ref = refs[nseq:nseq + 9]
    y_ref, hr_out, hi_out, h_scr, bu_scr, bb_scr, ab_scr, perm_scr = refs[nseq + 9:]
    tc = S5_CHUNK
    n = nseq * tc
    c = pl.program_id(0)

    @pl.when(c == 0)
    def _():
        h_scr[...] = jnp.zeros_like(h_scr)
        lr, li = lr_ref[...], li_ref[...]
        dt = jnp.exp(ldt_ref[...])
        mag = jnp.exp(lr * dt)
        ab_re, ab_im = mag * jnp.cos(li * dt), mag * jnp.sin(li * dt)
        den = lr * lr + li * li
        zr, zi = ab_re - 1.0, ab_im
        cr = (zr * lr + zi * li) / den
        ci = (zi * lr - zr * li) / den
        bre, bim = bre_ref[...], bim_ref[...]
        bb_scr[:, 0:SSM_FLAT] = (cr * bre - ci * bim).astype(BF16)
        bb_scr[:, SSM_FLAT:2 * SSM_FLAT] = (cr * bim + ci * bre).astype(BF16)
        ab_scr[0] = jnp.broadcast_to(ab_re, (nseq, SSM_FLAT))
        ab_scr[1] = jnp.broadcast_to(ab_im, (nseq, SSM_FLAT))
        ri, cj = _iota((n, n), 0), _iota((n, n), 1)
        lseq, lt = nseq.bit_length() - 1, tc.bit_length() - 1
        perm_scr[0] = jnp.where(cj == (ri & (nseq - 1)) * tc + lax.shift_right_logical(ri, lseq), 1.0, 0.0).astype(BF16)
        perm_scr[1] = jnp.where(cj == (ri & (tc - 1)) * nseq + lax.shift_right_logical(ri, lt), 1.0, 0.0).astype(BF16)

    u = jnp.concatenate([r[...] for r in u_refs], axis=0)
    to_seq = perm_scr[1]
    u_t = _dot(perm_scr[0], u.astype(BF16)).astype(BF16)
    bu_scr[...] = _dot(u_t, bb_scr[...])

    ar, ai = ab_scr[0], ab_scr[1]

    def step(t, carry):
        hr, hi = carry
        rows = pl.ds(pl.multiple_of(t * nseq, nseq), nseq)
        nhr = ar * hr - ai * hi + bu_scr[rows, 0:SSM_FLAT]
        nhi = ar * hi + ai * hr + bu_scr[rows, SSM_FLAT:2 * SSM_FLAT]
        bu_scr[rows, 0:SSM_FLAT] = nhr
        bu_scr[rows, SSM_FLAT:2 * SSM_FLAT] = nhi
        return nhr, nhi

    hr, hi = lax.fori_loop(0, tc, step, (h_scr[0], h_scr[1]))
    h_scr[0] = hr
    h_scr[1] = hi

    y_t = _dot(bu_scr[...].astype(BF16), ccat_ref[...])
    hi_p = y_t.astype(BF16)
    rest = y_t - hi_p.astype(F32)
    mid_p = rest.astype(BF16)
    lo_p = (rest - mid_p.astype(F32)).astype(BF16)
    y = _dot(to_seq, hi_p) + _dot(to_seq, mid_p) + _dot(to_seq, lo_p) + d_ref[...] * u
    z = jax.nn.gelu(y)
    out = z * jax.nn.sigmoid(_dot(z.astype(BF16), gw_ref[...]) + gb_ref[...])
    for b in range(nseq):
        y_ref[:, b * GROUP_WIDTH:(b + 1) * GROUP_WIDTH] = out[b * tc:(b + 1) * tc].astype(y_ref.dtype)

    @pl.when(c == pl.num_programs(0) - 1)
    def _():
        hr_out[...] = hr
        hi_out[...] = hi


def s5_seq_mixer(proj2, lam_re, lam_im, log_dt, b_re_bd, b_im_bd, c_cat, d_skip, glu_w, glu_b, *, bsz, t_len):
    tc = S5_CHUNK
    ncols = PROJ_WIDTH // GROUP_WIDTH
    full = lambda shape: pl.BlockSpec(shape, lambda c: (0,) * len(shape))
    st = jax.ShapeDtypeStruct((bsz, SSM_FLAT), F32)
    u_specs = [pl.BlockSpec((tc, GROUP_WIDTH), functools.partial(lambda c, b: (c, b * ncols + COL_SSM), b=b))
               for b in range(bsz)]
    return pl.pallas_call(
        functools.partial(_s5_seq_kernel, nseq=bsz),
        grid=(t_len // tc,),
        in_specs=u_specs + [full((1, SSM_FLAT)), full((1, SSM_FLAT)), full((1, SSM_FLAT)),
                            full((GROUP_WIDTH, SSM_FLAT)), full((GROUP_WIDTH, SSM_FLAT)),
                            full((2 * SSM_FLAT, GROUP_WIDTH)), full((1, GROUP_WIDTH)),
                            full((GROUP_WIDTH, GROUP_WIDTH)), full((1, GROUP_WIDTH))],
        out_specs=[pl.BlockSpec((tc, bsz * GROUP_WIDTH), lambda c: (c, 0)), full((bsz, SSM_FLAT)),
                   full((bsz, SSM_FLAT))],
        out_shape=[jax.ShapeDtypeStruct((t_len, bsz * GROUP_WIDTH), BF16), st, st],
        scratch_shapes=[pltpu.VMEM((2, bsz, SSM_FLAT), F32), pltpu.VMEM((bsz * tc, 2 * SSM_FLAT), F32),
                        pltpu.VMEM((GROUP_WIDTH, 2 * SSM_FLAT), BF16), pltpu.VMEM((2, bsz, SSM_FLAT), F32),
                        pltpu.VMEM((2, bsz * tc, bsz * tc), BF16)],
        compiler_params=_params(("arbitrary",)),
        name="s5_seq_mixer",
    )(*([proj2] * bsz), lam_re, lam_im, log_dt, b_re_bd, b_im_bd, c_cat, d_skip, glu_w, glu_b)


POOL_CHUNK = 256
POOL_HIST = 16


def _pool_seq_kernel(u_ref, w_ref, sc_ref, y_ref, nbuf_ref, ext_scr):
    L, hist = POOL_CHUNK, POOL_HIST
    c = pl.program_id(1)

    @pl.when(c == 0)
    def _():
        ext_scr[0:hist] = jnp.zeros((hist, GROUP_WIDTH), F32)

    u = u_ref[...]
    ext_scr[hist:hist + L] = u
    e = ext_scr[...]
    a2 = e + pltpu.roll(e, 1, axis=0)
    a4 = a2 + pltpu.roll(a2, 2, axis=0)
    a8 = a4 + pltpu.roll(a4, 4, axis=0)
    a16 = a8 + pltpu.roll(a8, 8, axis=0)
    sums = (a2[hist:], a4[hist:], a8[hist:], a16[hist:])

    shape = (L, GROUP_WIDTH)
    pos = _iota(shape, 0) + c * L
    lane = _iota(shape, 1)
    pooled = None
    for gi in reversed(range(len(POOL_WINDOWS))):
        win = POOL_WINDOWS[gi]
        mean = sums[gi] / jnp.minimum(pos + 1, win).astype(F32)
        pooled = mean if pooled is None else jnp.where(lane < (gi + 1) * HEAD, mean, pooled)
    y_ref[...] = (_dot((pooled - u).astype(BF16), w_ref[...]) * sc_ref[...]).astype(y_ref.dtype)

    nb = ext_scr[L:L + hist]
    ext_scr[0:hist] = nb

    @pl.when(c == pl.num_programs(1) - 1)
    def _():
        nbuf_ref[...] = nb


def pool_seq_mixer(proj2, w_bd, scale, *, bsz, t_len):
    L, hist = POOL_CHUNK, POOL_HIST
    ncols = PROJ_WIDTH // GROUP_WIDTH
    return pl.pallas_call(
        _pool_seq_kernel,
        grid=(bsz, t_len // L),
        in_specs=[_seq_col_spec(L, COL_POOL, ncols),
                  pl.BlockSpec((GROUP_WIDTH, GROUP_WIDTH), lambda b, c: (0, 0)),
                  pl.BlockSpec((1, GROUP_WIDTH), lambda b, c: (0, 0))],
        out_specs=[_seq_col_spec(L, 0, 1), pl.BlockSpec((hist, GROUP_WIDTH), lambda b, c: (b, 0))],
        out_shape=[jax.ShapeDtypeStruct((t_len, bsz * GROUP_WIDTH), BF16),
                   jax.ShapeDtypeStruct((bsz * hist, GROUP_WIDTH), F32)],
        scratch_shapes=[pltpu.VMEM((hist + L, GROUP_WIDTH), F32)],
        compiler_params=_params(("parallel", "arbitrary")),
        name="pool_seq_mixer",
    )(proj2, w_bd, scale)


def _diag_heads(st, bsz):
    nh = GROUP_WIDTH // HEAD
    s = st.reshape(bsz, nh, HEAD, nh, HEAD)
    return jnp.stack([s[:, h, :, h, :] for h in range(nh)], axis=1)


def _mix_mlp_kernel(h_ref, ya_ref, yb_ref, yc_ref, yd_ref, wo_ref, g2_ref, wu_ref, wd_ref, gf_ref, o_ref,
                    h1_scr, xn_scr, acc_scr, *, final_norm):
    j = pl.program_id(1)
    gw = GROUP_WIDTH

    @pl.when(j == 0)
    def _():
        mix = (_dot(ya_ref[...], wo_ref[0:gw]) + _dot(yb_ref[...], wo_ref[gw:2 * gw])
               + _dot(yc_ref[...], wo_ref[2 * gw:3 * gw]) + _dot(yd_ref[...], wo_ref[3 * gw:4 * gw]))
        h1 = h_ref[...] + mix
        h1_scr[...] = h1
        xn_scr[...] = _rms(h1, g2_ref[...]).astype(BF16)
        acc_scr[...] = jnp.zeros_like(acc_scr)

    up = _dot(xn_scr[...], wu_ref[...])
    act = jnp.square(jnp.maximum(up, 0.0)).astype(BF16)
    acc_scr[...] += _dot(act, wd_ref[...])

    @pl.when(j == pl.num_programs(1) - 1)
    def _():
        out = h1_scr[...] + acc_scr[...]
        if final_norm:
            out = _rms(out, gf_ref[...])
        o_ref[...] = out


def mix_mlp(h, ys, w_out, g2, w_up, w_down, g_final, *, final_norm, nseq=0):
    n = h.shape[0]
    tm = min(512, n // max(nseq, 1))
    tf = 1024
    nt = n // tm // max(nseq, 1)
    row = lambda w: pl.BlockSpec((tm, w), lambda i, j: (i, 0))
    mix = pl.BlockSpec((tm, GROUP_WIDTH), _tile_map(nseq, nt))
    return pl.pallas_call(
        functools.partial(_mix_mlp_kernel, final_norm=final_norm),
        grid=(n // tm, D_FF // tf),
        in_specs=[row(D_MODEL), mix, mix, mix, mix,
                  pl.BlockSpec((D_MODEL, D_MODEL), lambda i, j: (0, 0)),
                  pl.BlockSpec((1, D_MODEL), lambda i, j: (0, 0)),
                  pl.BlockSpec((D_MODEL, tf), lambda i, j: (0, j)),
                  pl.BlockSpec((tf, D_MODEL), lambda i, j: (j, 0)),
                  pl.BlockSpec((1, D_MODEL), lambda i, j: (0, 0))],
        out_specs=row(D_MODEL),
        out_shape=jax.ShapeDtypeStruct((n, D_MODEL), F32),
        scratch_shapes=[pltpu.VMEM((tm, D_MODEL), F32), pltpu.VMEM((tm, D_MODEL), BF16),
                        pltpu.VMEM((tm, D_MODEL), F32)],
        compiler_params=_params(("parallel", "arbitrary")),
        name="mix_mlp",
    )(h, *ys, w_out, g2, w_up, w_down, g_final)


def _hgrn_state_in(s):
    b = s.shape[0]
    return s.reshape(b, 2, 2, HEAD, HEAD).transpose(0, 1, 4, 2, 3).reshape(b, 2, HEAD, LANES)


def _hgrn_state_out(s):
    b = s.shape[0]
    return s.reshape(b, 2, HEAD, 2, HEAD).transpose(0, 1, 3, 4, 2).reshape(b, 4, HEAD, HEAD)


def _wkv_state_in(s):
    b = s.shape[0]
    return s.reshape(b, 2, 2, HEAD, HEAD).transpose(0, 1, 3, 2, 4).reshape(b, 2, HEAD, LANES)


def _wkv_state_out(s):
    b = s.shape[0]
    return s.reshape(b, 2, HEAD, 2, HEAD).transpose(0, 1, 3, 2, 4).reshape(b, 4, HEAD, HEAD)


def _block_diag(blocks):
    g, r, c = blocks.shape
    eye = jnp.eye(g, dtype=blocks.dtype)
    return (blocks[:, :, None, :] * eye[:, None, :, None]).reshape(g * r, g * c)


def _pad_rows(w, start):
    return jnp.zeros((GROUP_WIDTH, GROUP_WIDTH), w.dtype).at[start:start + w.shape[0]].set(w)


def _layer_params(l, P):
    row = lambda a: a.reshape(1, -1)
    q = {}
    q["norm1_g"] = row(P["norm1_g"][l])
    q["w_in"] = P["w_in"][l].astype(BF16)
    q["lam_re"] = row(P["ssm_lambda_re"][l])
    q["lam_im"] = row(P["ssm_lambda_im"][l])
    q["log_dt"] = row(jnp.repeat(P["ssm_log_dt"][l], SSM_STATE))
    q["b_re"] = _block_diag(P["ssm_b_re"][l].transpose(0, 2, 1))
    q["b_im"] = _block_diag(P["ssm_b_im"][l].transpose(0, 2, 1))
    q["c_cat"] = jnp.concatenate([_block_diag(P["ssm_c_re"][l].transpose(0, 2, 1)),
                                  -_block_diag(P["ssm_c_im"][l].transpose(0, 2, 1))], axis=0).astype(BF16)
    q["ssm_d"] = row(P["ssm_d"][l])
    q["glu_w"] = P["ssm_glu_w"][l].astype(BF16)
    q["glu_b"] = row(P["ssm_glu_b"][l])
    q["hgrn_norm_g"] = row(P["hgrn_norm_g"][l])
    q["mu"] = row(P["rwkv_mu"][l])
    for name in ("w0", "a0", "k_k", "k_a", "r_k", "ln_g", "ln_b"):
        q[name] = row(P["rwkv_" + name][l])
    q["w2p"] = _pad_rows(P["rwkv_w2"][l], 0).astype(BF16)
    q["a2p"] = _pad_rows(P["rwkv_a2"][l], DECAY_LORA).astype(BF16)
    q["g2p"] = _pad_rows(P["rwkv_g2"][l], DECAY_LORA + AAA_LORA).astype(BF16)
    q["pool_w"] = _block_diag(P["pool_w"][l]).astype(BF16)
    q["pool_scale"] = row(P["pool_scale"][l])
    q["w_out"] = P["w_out"][l].astype(BF16)
    q["norm2_g"] = row(P["norm2_g"][l])
    q["mlp_up"] = P["mlp_up"][l].astype(BF16)
    q["mlp_down"] = P["mlp_down"][l].astype(BF16)
    return q


def _trunk_fresh(x_rows, bsz, t_len, layer_params, P):
    h = x_rows
    new = [[] for _ in range(6)]
    g_final = P["norm_f_g"].reshape(1, -1)
    for l in range(DEPTH):
        q = layer_params[l]
        proj2 = rms_proj(h, q["norm1_g"], q["w_in"], nseq=bsz)
        y_a, s_re, s_im = s5_seq_mixer(proj2, q["lam_re"], q["lam_im"], q["log_dt"], q["b_re"], q["b_im"],
                                       q["c_cat"], q["ssm_d"], q["glu_w"], q["glu_b"], bsz=bsz, t_len=t_len)
        y_b, s_hg = hgrn_chunk_mixer(proj2, P["hgrn_lb_logits"], q["hgrn_norm_g"], bsz=bsz, t_len=t_len, layer=l)
        y_c, s_wkv, s_sh = rwkv_chunk_mixer(proj2, q["mu"], q["w0"], q["a0"], q["k_k"], q["k_a"], q["r_k"],
                                            q["ln_g"], q["ln_b"], q["w2p"], q["a2p"], q["g2p"], bsz=bsz, t_len=t_len)
        y_d, s_pool = pool_seq_mixer(proj2, q["pool_w"], q["pool_scale"], bsz=bsz, t_len=t_len)
        h = mix_mlp(h, (y_a, y_b, y_c, y_d), q["w_out"], q["norm2_g"], q["mlp_up"], q["mlp_down"], g_final,
                    final_norm=(l == DEPTH - 1), nseq=bsz)
        s_hg = _diag_heads(s_hg, bsz).swapaxes(-1, -2)
        s_wkv = _diag_heads(s_wkv, bsz)
        s_pool = s_pool.reshape(bsz, POOL_HIST, GROUP_WIDTH)[:, POOL_HIST - POOL_BUF:]
        for lst, s in zip(new, (s_re, s_im, s_hg, s_wkv, s_sh, s_pool)):
            lst.append(s)
    return h, new


def _trunk_carry(x_rows, states, pos0, t_len, layer_params, P):
    ssm_re0, ssm_im0, hgrn0, wkv0, shift0, pool0 = states
    h = x_rows
    new = [[] for _ in range(6)]
    g_final = P["norm_f_g"].reshape(1, -1)
    tc = t_len
    for l in range(DEPTH):
        q = layer_params[l]
        proj = rms_proj(h, q["norm1_g"], q["w_in"])
        y_a, s_re, s_im = s5_mixer(proj, ssm_re0[l], ssm_im0[l], q["lam_re"], q["lam_im"], q["log_dt"], q["b_re"],
                                   q["b_im"], q["c_cat"], q["ssm_d"], q["glu_w"], q["glu_b"], t_len=t_len, tc=tc)
        y_b, s_hg = hgrn_mixer(proj, hgrn0[l], P["hgrn_lb_logits"], q["hgrn_norm_g"], t_len=t_len, tc=tc, layer=l)
        y_c, s_wkv, s_sh = rwkv_mixer(proj, shift0[l], wkv0[l], q["mu"], q["w0"], q["a0"], q["k_k"], q["k_a"],
                                      q["r_k"], q["ln_g"], q["ln_b"], q["w2p"], q["a2p"], q["g2p"],
                                      t_len=t_len, tc=tc)
        y_d, s_pool = pool_mixer(proj, pool0[l], q["pool_w"], q["pool_scale"], t_len=t_len, tc=tc, pos0=pos0)
        h = mix_mlp(h, (y_a, y_b, y_c, y_d), q["w_out"], q["norm2_g"], q["mlp_up"], q["mlp_down"], g_final,
                    final_norm=(l == DEPTH - 1))
        for lst, s in zip(new, (s_re, s_im, _hgrn_state_out(s_hg), _wkv_state_out(s_wkv), s_sh,
                                s_pool.transpose(1, 0, 2))):
            lst.append(s)
    return h, new


def _states_out(new, bsz):
    s_re, s_im, s_hg, s_wkv, s_sh, s_pool = new
    return (jnp.stack([s.reshape(bsz, SSM_GROUPS, SSM_STATE) for s in s_re]),
            jnp.stack([s.reshape(bsz, SSM_GROUPS, SSM_STATE) for s in s_im]),
            jnp.stack(s_hg),
            jnp.stack(s_wkv),
            jnp.stack([s.reshape(bsz, 1, RWKV_PROJ) for s in s_sh]),
            jnp.stack(s_pool))


def kernel(x_prompt, x_sample, state_ssm_re, state_ssm_im, state_hgrn, state_wkv, state_shift, state_pool, norm1_g, w_in, ssm_lambda_re, ssm_lambda_im, ssm_log_dt, ssm_b_re, ssm_b_im, ssm_c_re, ssm_c_im, ssm_d, ssm_glu_w, ssm_glu_b, hgrn_lb_logits, hgrn_norm_g, rwkv_mu, rwkv_w0, rwkv_w2, rwkv_a0, rwkv_a2, rwkv_g2, rwkv_k_k, rwkv_k_a, rwkv_r_k, rwkv_ln_g, rwkv_ln_b, pool_w, pool_scale, w_out, norm2_g, mlp_up, mlp_down, norm_f_g):
    P = dict(norm1_g=norm1_g, w_in=w_in, ssm_lambda_re=ssm_lambda_re, ssm_lambda_im=ssm_lambda_im,
             ssm_log_dt=ssm_log_dt, ssm_b_re=ssm_b_re, ssm_b_im=ssm_b_im, ssm_c_re=ssm_c_re, ssm_c_im=ssm_c_im,
             ssm_d=ssm_d, ssm_glu_w=ssm_glu_w, ssm_glu_b=ssm_glu_b, hgrn_lb_logits=hgrn_lb_logits,
             hgrn_norm_g=hgrn_norm_g, rwkv_mu=rwkv_mu, rwkv_w0=rwkv_w0, rwkv_w2=rwkv_w2, rwkv_a0=rwkv_a0,
             rwkv_a2=rwkv_a2, rwkv_g2=rwkv_g2, rwkv_k_k=rwkv_k_k, rwkv_k_a=rwkv_k_a, rwkv_r_k=rwkv_r_k,
             rwkv_ln_g=rwkv_ln_g, rwkv_ln_b=rwkv_ln_b, pool_w=pool_w, pool_scale=pool_scale, w_out=w_out,
             norm2_g=norm2_g, mlp_up=mlp_up, mlp_down=mlp_down, norm_f_g=norm_f_g)
    layer_params = [_layer_params(l, P) for l in range(DEPTH)]

    bp, t_p, _ = x_prompt.shape
    yp, new_p = _trunk_fresh(x_prompt.reshape(bp * t_p, D_MODEL), bp, t_p, layer_params, P)
    y_prompt = yp.reshape(bp, t_p, D_MODEL)

    bs, t_s, _ = x_sample.shape
    nblk = bs // SEQ_BLK
    xs = x_sample.reshape(nblk, SEQ_BLK, t_s, D_MODEL).transpose(0, 2, 1, 3).reshape(bs * t_s, D_MODEL)
    st_s = ([state_ssm_re[l].reshape(bs, SSM_FLAT) for l in range(DEPTH)],
            [state_ssm_im[l].reshape(bs, SSM_FLAT) for l in range(DEPTH)],
            [_hgrn_state_in(state_hgrn[l]) for l in range(DEPTH)],
            [_wkv_state_in(state_wkv[l]) for l in range(DEPTH)],
            [state_shift[l].reshape(bs, RWKV_PROJ) for l in range(DEPTH)],
            [state_pool[l].transpose(1, 0, 2) for l in range(DEPTH)])
    ys, new_s = _trunk_carry(xs, st_s, PAST_LEN, t_s, layer_params, P)
    y_sample = ys.reshape(nblk, t_s, SEQ_BLK, D_MODEL).transpose(0, 2, 1, 3).reshape(bs, t_s, D_MODEL)

    return (y_prompt, y_sample) + _states_out(new_p, bp) + _states_out(new_s, bs)
```

```python
import functools
import itertools

import jax
import jax.numpy as jnp
from jax import lax
from jax.experimental import pallas as pl
from jax.experimental.pallas import tpu as pltpu

F32 = jnp.float32
BF16 = jnp.bfloat16

D_MODEL = 1024
DEPTH = 2
PAST_LEN = 16384
GROUP_WIDTH = 256
HEAD = 64
SSM_GROUPS = 16
SSM_CH = 16
SSM_STATE = 64
SSM_FLAT = SSM_GROUPS * SSM_STATE
POOL_WINDOWS = (2, 4, 8, 16)
POOL_BUF = 15
DECAY_LORA = 64
AAA_LORA = 64
GATE_LORA = 128
RWKV_PROJ = 1024
PROJ_WIDTH = 2560
D_FF = 4096
NORM_EPS = 1e-6
HGRN_NORM_EPS = 1e-5
RWKV_GN_EPS = 64e-5

SEQ_BLK = 8
LANES = 128
VMEM_LIMIT = 48 * 1024 * 1024

COL_SSM, COL_Q, COL_F, COL_I, COL_G, COL_R, COL_K, COL_V, COL_LORA, COL_POOL = range(10)


def _params(sem):
    return pltpu.CompilerParams(dimension_semantics=sem, vmem_limit_bytes=VMEM_LIMIT)


def _dot(a, b):
    return jnp.dot(a, b, preferred_element_type=F32)


def _rms(x, g):
    return x * lax.rsqrt(jnp.mean(x * x, axis=-1, keepdims=True) + NORM_EPS) * g


def _rms_proj_kernel(x_ref, g_ref, w_ref, o_ref):
    o_ref[...] = _dot(_rms(x_ref[...], g_ref[...]).astype(BF16), w_ref[...])


def _tile_map(nseq, nt):
    if nseq == 0:
        return lambda r, *_: (r, 0)
    return lambda r, *_: (r % nt, r // nt)


def rms_proj(x, g, w, *, nseq=0):
    n = x.shape[0]
    tm = min(512, n // max(nseq, 1))
    nt = n // tm // max(nseq, 1)
    out_shape = (n, PROJ_WIDTH) if nseq == 0 else (n // nseq, nseq * PROJ_WIDTH)
    return pl.pallas_call(
        _rms_proj_kernel,
        grid=(n // tm,),
        in_specs=[pl.BlockSpec((tm, D_MODEL), lambda i: (i, 0)),
                  pl.BlockSpec((1, D_MODEL), lambda i: (0, 0)),
                  pl.BlockSpec((D_MODEL, PROJ_WIDTH), lambda i: (0, 0))],
        out_specs=pl.BlockSpec((tm, PROJ_WIDTH), _tile_map(nseq, nt)),
        out_shape=jax.ShapeDtypeStruct(out_shape, F32),
        compiler_params=_params(("parallel",)),
        name="rms_proj",
    )(x, g, w)


def _row_spec(tc, col, nchunks):
    return pl.BlockSpec((tc * SEQ_BLK, GROUP_WIDTH), lambda s, c: (s * nchunks + c, col))


def _full_spec(shape):
    nd = len(shape)
    return pl.BlockSpec(shape, lambda s, c: (0,) * nd)


def _seq_spec(shape):
    nd = len(shape)
    return pl.BlockSpec((SEQ_BLK,) + shape[1:], lambda s, c: (s,) + (0,) * (nd - 1))


def _head_sums(x):
    lane = lax.broadcasted_iota(jnp.int32, x.shape, 1)
    out = jnp.zeros_like(x)
    for h in range(GROUP_WIDTH // HEAD):
        m = (lane >= h * HEAD) & (lane < (h + 1) * HEAD)
        s = jnp.sum(jnp.where(m, x, 0.0), axis=1, keepdims=True)
        out = jnp.where(m, s, out)
    return out


UNITS = SEQ_BLK * 2


def _unit_masks():
    r, c = _iota((2 * LANES, LANES), 0), _iota((2 * LANES, LANES), 1)
    ones2 = jnp.where(_head_of(r & (LANES - 1)) == _head_of(c), 1.0, 0.0).astype(BF16)
    shape = (UNITS * HEAD, LANES)
    eye = (_iota(shape, 1) & (HEAD - 1)) == (_iota(shape, 0) & (HEAD - 1))
    return ones2, eye


def _seg_sum_mxu(p, ones2):
    hi = p.astype(BF16)
    lo = (p - hi.astype(F32)).astype(BF16)
    return _dot(jnp.concatenate([hi, lo], axis=1), ones2)


def _unit_rows(ref, t):
    return jnp.concatenate([jnp.broadcast_to(ref[t, b:b + 1, p * LANES:(p + 1) * LANES], (HEAD, LANES))
                            for b in range(SEQ_BLK) for p in range(2)], axis=0)


def _store_unit_rows(ref, t, cols, eye):
    picked = jnp.where(eye, cols, 0.0)
    for b in range(SEQ_BLK):
        for p in range(2):
            u = b * 2 + p
            ref[t, b:b + 1, p * LANES:(p + 1) * LANES] = jnp.sum(picked[u * HEAD:(u + 1) * HEAD], axis=0, keepdims=True)


def _s5_kernel(u_ref, h0r_ref, h0i_ref, lr_ref, li_ref, ldt_ref, bre_ref, bim_ref, ccat_ref, d_ref,
               gw_ref, gb_ref, y_ref, hr_out, hi_out, h_scr, bu_scr, *, tc):
    c = pl.program_id(1)

    @pl.when(c == 0)
    def _():
        h_scr[0] = h0r_ref[...]
        h_scr[1] = h0i_ref[...]

    lr, li = lr_ref[...], li_ref[...]
    dt = jnp.exp(ldt_ref[...])
    mag = jnp.exp(lr * dt)
    ab_re, ab_im = mag * jnp.cos(li * dt), mag * jnp.sin(li * dt)
    den = lr * lr + li * li
    zr, zi = ab_re - 1.0, ab_im
    cr = (zr * lr + zi * li) / den
    ci = (zi * lr - zr * li) / den
    bre, bim = bre_ref[...], bim_ref[...]
    bb_re = (cr * bre - ci * bim).astype(BF16)
    bb_im = (cr * bim + ci * bre).astype(BF16)

    u = u_ref[...]
    ub = u.astype(BF16)
    bu_scr[:, 0:SSM_FLAT] = _dot(ub, bb_re)
    bu_scr[:, SSM_FLAT:2 * SSM_FLAT] = _dot(ub, bb_im)

    ar = jnp.broadcast_to(ab_re, (SEQ_BLK, SSM_FLAT))
    ai = jnp.broadcast_to(ab_im, (SEQ_BLK, SSM_FLAT))

    def step(t, carry):
        hr, hi = carry
        rows = pl.ds(pl.multiple_of(t * SEQ_BLK, SEQ_BLK), SEQ_BLK)
        nhr = ar * hr - ai * hi + bu_scr[rows, 0:SSM_FLAT]
        nhi = ar * hi + ai * hr + bu_scr[rows, SSM_FLAT:2 * SSM_FLAT]
        bu_scr[rows, 0:SSM_FLAT] = nhr
        bu_scr[rows, SSM_FLAT:2 * SSM_FLAT] = nhi
        return nhr, nhi

    hr, hi = lax.fori_loop(0, tc, step, (h_scr[0], h_scr[1]))
    h_scr[0] = hr
    h_scr[1] = hi

    y = _dot(bu_scr[...].astype(BF16), ccat_ref[...]) + d_ref[...] * u
    z = jax.nn.gelu(y)
    out = z * jax.nn.sigmoid(_dot(z.astype(BF16), gw_ref[...]) + gb_ref[...])
    y_ref[...] = out.astype(y_ref.dtype)

    @pl.when(c == pl.num_programs(1) - 1)
    def _():
        hr_out[...] = hr
        hi_out[...] = hi


def s5_mixer(proj, h0_re, h0_im, lam_re, lam_im, log_dt, b_re_bd, b_im_bd, c_cat, d_skip, glu_w, glu_b, *, t_len, tc):
    nseq = h0_re.shape[0] // SEQ_BLK
    nchunks = t_len // tc
    n = proj.shape[0]
    st = jax.ShapeDtypeStruct(h0_re.shape, F32)
    return pl.pallas_call(
        functools.partial(_s5_kernel, tc=tc),
        grid=(nseq, nchunks),
        in_specs=[_row_spec(tc, COL_SSM, nchunks),
                  _seq_spec(h0_re.shape), _seq_spec(h0_im.shape),
                  _full_spec((1, SSM_FLAT)), _full_spec((1, SSM_FLAT)), _full_spec((1, SSM_FLAT)),
                  _full_spec((GROUP_WIDTH, SSM_FLAT)), _full_spec((GROUP_WIDTH, SSM_FLAT)),
                  _full_spec((2 * SSM_FLAT, GROUP_WIDTH)), _full_spec((1, GROUP_WIDTH)),
                  _full_spec((GROUP_WIDTH, GROUP_WIDTH)), _full_spec((1, GROUP_WIDTH))],
        out_specs=[_row_spec(tc, 0, nchunks), _seq_spec(h0_re.shape), _seq_spec(h0_im.shape)],
        out_shape=[jax.ShapeDtypeStruct((n, GROUP_WIDTH), BF16), st, st],
        scratch_shapes=[pltpu.VMEM((2, SEQ_BLK, SSM_FLAT), F32),
                        pltpu.VMEM((tc * SEQ_BLK, 2 * SSM_FLAT), F32)],
        compiler_params=_params(("parallel", "arbitrary")),
        name="s5_mixer",
    )(proj, h0_re, h0_im, lam_re, lam_im, log_dt, b_re_bd, b_im_bd, c_cat, d_skip, glu_w, glu_b)


def _pool_kernel(u_ref, buf_ref, w_ref, sc_ref, y_ref, nbuf_ref, ext_scr, *, tc, pos0):
    c = pl.program_id(1)

    @pl.when(c == 0)
    def _():
        ext_scr[0:POOL_BUF] = buf_ref[...]

    u = u_ref[...].reshape(tc, SEQ_BLK, GROUP_WIDTH)
    ext_scr[POOL_BUF:POOL_BUF + tc] = u
    a1 = ext_scr[...]
    a2 = a1[1:] + a1[:-1]
    a4 = a2[2:] + a2[:-2]
    a8 = a4[4:] + a4[:-4]
    a16 = a8[8:] + a8[:-8]
    sums = (a2[14:], a4[12:], a8[8:], a16)

    shape = (tc, SEQ_BLK, GROUP_WIDTH)
    pos = lax.broadcasted_iota(jnp.int32, shape, 0) + (c * tc + pos0)
    lane = lax.broadcasted_iota(jnp.int32, shape, 2)
    pooled = None
    for gi in reversed(range(len(POOL_WINDOWS))):
        win = POOL_WINDOWS[gi]
        mean = sums[gi] / jnp.minimum(pos + 1, win).astype(F32)
        pooled = mean if pooled is None else jnp.where(lane < (gi + 1) * HEAD, mean, pooled)
    pooled = (pooled - u).reshape(tc * SEQ_BLK, GROUP_WIDTH)
    y_ref[...] = (_dot(pooled.astype(BF16), w_ref[...]) * sc_ref[...]).astype(y_ref.dtype)

    nb = ext_scr[tc:tc + POOL_BUF]
    ext_scr[0:POOL_BUF] = nb

    @pl.when(c == pl.num_programs(1) - 1)
    def _():
        nbuf_ref[...] = nb


def pool_mixer(proj, buf, w_bd, scale, *, t_len, tc, pos0):
    nseq = buf.shape[1] // SEQ_BLK
    nchunks = t_len // tc
    n = proj.shape[0]
    buf_spec = pl.BlockSpec((POOL_BUF, SEQ_BLK, GROUP_WIDTH), lambda s, c: (0, s, 0))
    return pl.pallas_call(
        functools.partial(_pool_kernel, tc=tc, pos0=pos0),
        grid=(nseq, nchunks),
        in_specs=[_row_spec(tc, COL_POOL, nchunks), buf_spec,
                  _full_spec((GROUP_WIDTH, GROUP_WIDTH)), _full_spec((1, GROUP_WIDTH))],
        out_specs=[_row_spec(tc, 0, nchunks), buf_spec],
        out_shape=[jax.ShapeDtypeStruct((n, GROUP_WIDTH), BF16), jax.ShapeDtypeStruct(buf.shape, F32)],
        scratch_shapes=[pltpu.VMEM((tc + POOL_BUF, SEQ_BLK, GROUP_WIDTH), F32)],
        compiler_params=_params(("parallel", "arbitrary")),
        name="pool_mixer",
    )(proj, buf, w_bd, scale)


def _hgrn_lower_bound(logits_ref, layer):
    rows = [logits_ref[l:l + 1, :] for l in range(DEPTH)]
    m = functools.reduce(jnp.maximum, rows)
    es = [jnp.exp(r - m) for r in rows]
    tot = functools.reduce(lambda a, b: a + b, es)
    lb = jnp.zeros_like(m)
    for l in range(1, layer + 1):
        lb = lb + es[l] / tot
    return lb


def _hgrn_kernel(pq_ref, pf_ref, pi_ref, pg_ref, s0_ref, lbl_ref, ng_ref, y_ref, st_ref,
                 s_scr, q_scr, f_scr, k_scr, v_scr, o_scr, *, tc, layer):
    c = pl.program_id(1)
    shape3 = (tc, SEQ_BLK, GROUP_WIDTH)

    @pl.when(c == 0)
    def _():
        s_scr[...] = s0_ref[...]

    lb = _hgrn_lower_bound(lbl_ref, layer)
    zf = pf_ref[...]
    f_scr[...] = (lb + (1.0 - lb) * jax.nn.sigmoid(zf)).reshape(shape3)
    k_scr[...] = ((1.0 - lb) * jax.nn.sigmoid(-zf)).reshape(shape3)
    q_scr[...] = jax.nn.silu(pq_ref[...]).reshape(shape3)
    v_scr[...] = pi_ref[...].reshape(shape3)

    ones2, eye = _unit_masks()
    s = s_scr[...].reshape(UNITS * HEAD, LANES)
    for t in range(tc):
        vcol = _seg_sum_mxu(jnp.where(eye, _unit_rows(v_scr, t), 0.0), ones2)
        s = s * _unit_rows(f_scr, t) + vcol * _unit_rows(k_scr, t)
        _store_unit_rows(o_scr, t, _seg_sum_mxu(s * _unit_rows(q_scr, t), ones2), eye)
    s_scr[...] = s.reshape(s_scr.shape)

    o = o_scr[...].reshape(tc * SEQ_BLK, GROUP_WIDTH)
    ms = _head_sums(o * o) * (1.0 / HEAD)
    out = o * lax.rsqrt(ms + HGRN_NORM_EPS) * ng_ref[...] * jax.nn.silu(pg_ref[...])
    y_ref[...] = out.astype(y_ref.dtype)

    @pl.when(c == pl.num_programs(1) - 1)
    def _():
        st_ref[...] = s_scr[...]


def hgrn_mixer(proj, s0, lb_logits, norm_g, *, t_len, tc, layer):
    nseq = s0.shape[0] // SEQ_BLK
    nchunks = t_len // tc
    n = proj.shape[0]
    tile = pltpu.VMEM((tc, SEQ_BLK, GROUP_WIDTH), F32)
    return pl.pallas_call(
        functools.partial(_hgrn_kernel, tc=tc, layer=layer),
        grid=(nseq, nchunks),
        in_specs=[_row_spec(tc, COL_Q, nchunks), _row_spec(tc, COL_F, nchunks),
                  _row_spec(tc, COL_I, nchunks), _row_spec(tc, COL_G, nchunks),
                  _seq_spec(s0.shape), _full_spec((DEPTH, GROUP_WIDTH)), _full_spec((1, GROUP_WIDTH))],
        out_specs=[_row_spec(tc, 0, nchunks), _seq_spec(s0.shape)],
        out_shape=[jax.ShapeDtypeStruct((n, GROUP_WIDTH), BF16), jax.ShapeDtypeStruct(s0.shape, F32)],
        scratch_shapes=[pltpu.VMEM((SEQ_BLK, 2, HEAD, LANES), F32), tile, tile, tile, tile, tile],
        compiler_params=_params(("parallel", "arbitrary")),
        name="hgrn_mixer",
    )(proj, proj, proj, proj, s0, lb_logits, norm_g)


def _rwkv_kernel(pr_ref, pk_ref, pv_ref, pl_ref, sh0_ref, s0_ref, mu_ref, w0_ref, a0_ref, kk_ref, ka_ref,
                 rk_ref, lng_ref, lnb_ref, w2_ref, a2_ref, g2_ref, y_ref, st_ref, sh_ref,
                 s_scr, prev_scr, r_scr, w_scr, k_scr, v_scr, nkk_scr, kka_scr, o_scr, *, tc):
    c = pl.program_id(1)
    shape3 = (tc, SEQ_BLK, GROUP_WIDTH)
    gw = GROUP_WIDTH

    @pl.when(c == 0)
    def _():
        s_scr[...] = s0_ref[...]
        prev_scr[...] = sh0_ref[...]

    def shifted(ref, j):
        x = ref[...].reshape(shape3)
        first = prev_scr[:, j * gw:(j + 1) * gw].reshape(1, SEQ_BLK, gw)
        prev = first if tc == 1 else jnp.concatenate([first, x[:-1]], axis=0)
        prev_scr[:, j * gw:(j + 1) * gw] = x[tc - 1]
        return (x + (prev - x) * mu_ref[:, j * gw:(j + 1) * gw]).reshape(tc * SEQ_BLK, gw)

    xr, xk, xv, xl = shifted(pr_ref, 0), shifted(pk_ref, 1), shifted(pv_ref, 2), shifted(pl_ref, 3)
    w = -jax.nn.softplus(-(w0_ref[...] + _dot(jnp.tanh(xl).astype(BF16), w2_ref[...]))) - 0.5
    decay = jnp.exp(-jnp.exp(w))
    a = jax.nn.sigmoid(a0_ref[...] + _dot(xl.astype(BF16), a2_ref[...]))
    g = _dot(jax.nn.sigmoid(xl).astype(BF16), g2_ref[...])
    kk = xk * kk_ref[...]
    kk = kk / jnp.maximum(jnp.sqrt(_head_sums(kk * kk)), 1e-12)
    k = xk * (1.0 + (a - 1.0) * ka_ref[...])

    r_scr[...] = xr.reshape(shape3)
    w_scr[...] = decay.reshape(shape3)
    k_scr[...] = k.reshape(shape3)
    v_scr[...] = xv.reshape(shape3)
    nkk_scr[...] = (-kk).reshape(shape3)
    kka_scr[...] = (kk * a).reshape(shape3)

    ones2, eye = _unit_masks()
    s = s_scr[...].reshape(UNITS * HEAD, LANES)
    for t in range(tc):
        sa = _seg_sum_mxu(s * _unit_rows(nkk_scr, t), ones2)
        vcol = _seg_sum_mxu(jnp.where(eye, _unit_rows(v_scr, t), 0.0), ones2)
        s = s * _unit_rows(w_scr, t) + sa * _unit_rows(kka_scr, t) + vcol * _unit_rows(k_scr, t)
        _store_unit_rows(o_scr, t, _seg_sum_mxu(s * _unit_rows(r_scr, t), ones2), eye)
    s_scr[...] = s.reshape(s_scr.shape)

    y = o_scr[...].reshape(tc * SEQ_BLK, gw)
    mean = _head_sums(y) * (1.0 / HEAD)
    d = y - mean
    var = _head_sums(d * d) * (1.0 / HEAD)
    yn = d * lax.rsqrt(var + RWKV_GN_EPS) * lng_ref[...] + lnb_ref[...]
    bonus = _head_sums(xr * k * rk_ref[...]) * xv
    y_ref[...] = ((yn + bonus) * g).astype(y_ref.dtype)

    @pl.when(c == pl.num_programs(1) - 1)
    def _():
        st_ref[...] = s_scr[...]
        sh_ref[...] = prev_scr[...]


def rwkv_mixer(proj, shift0, s0, mu, w0, a0, k_k, k_a, r_k, ln_g, ln_b, w2p, a2p, g2p, *, t_len, tc):
    nseq = s0.shape[0] // SEQ_BLK
    nchunks = t_len // tc
    n = proj.shape[0]
    tile = pltpu.VMEM((tc, SEQ_BLK, GROUP_WIDTH), F32)
    vec = _full_spec((1, GROUP_WIDTH))
    mat = _full_spec((GROUP_WIDTH, GROUP_WIDTH))
    return pl.pallas_call(
        functools.partial(_rwkv_kernel, tc=tc),
        grid=(nseq, nchunks),
        in_specs=[_row_spec(tc, COL_R, nchunks), _row_spec(tc, COL_K, nchunks),
                  _row_spec(tc, COL_V, nchunks), _row_spec(tc, COL_LORA, nchunks),
                  _seq_spec(shift0.shape), _seq_spec(s0.shape), _full_spec((1, RWKV_PROJ)),
                  vec, vec, vec, vec, vec, vec, vec, mat, mat, mat],
        out_specs=[_row_spec(tc, 0, nchunks), _seq_spec(s0.shape), _seq_spec(shift0.shape)],
        out_shape=[jax.ShapeDtypeStruct((n, GROUP_WIDTH), BF16), jax.ShapeDtypeStruct(s0.shape, F32),
                   jax.ShapeDtypeStruct(shift0.shape, F32)],
        scratch_shapes=[pltpu.VMEM((SEQ_BLK, 2, HEAD, LANES), F32), pltpu.VMEM((SEQ_BLK, RWKV_PROJ), F32),
                        tile, tile, tile, tile, tile, tile, tile],
        compiler_params=_params(("parallel", "arbitrary")),
        name="rwkv_mixer",
    )(proj, proj, proj, proj, shift0, s0, mu, w0, a0, k_k, k_a, r_k, ln_g, ln_b, w2p, a2p, g2p)


def _dot_nt(a, b):
    return lax.dot_general(a, b, (((1,), (1,)), ((), ())), preferred_element_type=F32)


def _dot_tn(a, b):
    return lax.dot_general(a, b, (((0,), (0,)), ((), ())), preferred_element_type=F32)


def _iota(shape, dim):
    return lax.broadcasted_iota(jnp.int32, shape, dim)


def _head_of(idx):
    return lax.shift_right_logical(idx, HEAD.bit_length() - 1)


def _cumsum_rows(x):
    n = x.shape[0]
    tri = jnp.where(_iota((n, n), 0) >= _iota((n, n), 1), 1.0, 0.0).astype(BF16)
    hi = x.astype(BF16)
    rest = x - hi.astype(F32)
    mid = rest.astype(BF16)
    lo = (rest - mid.astype(F32)).astype(BF16)
    return _dot(tri, hi) + _dot(tri, mid) + _dot(tri, lo)


def _own_head(shape, rows_per_head):
    row_h = lax.shift_right_logical(_iota(shape, 0), rows_per_head.bit_length() - 1)
    return row_h == _head_of(_iota(shape, 1))


def _head_expand(x):
    xx = jnp.concatenate([x] * (GROUP_WIDTH // HEAD), axis=0)
    return jnp.where(_own_head(xx.shape, x.shape[0]), xx, 0.0)


def _head_collapse(xx):
    n = xx.shape[0] // (GROUP_WIDTH // HEAD)
    return xx[0:n] + xx[n:2 * n] + xx[2 * n:3 * n] + xx[3 * n:4 * n]


def _block_diag_mask():
    shape = (GROUP_WIDTH, GROUP_WIDTH)
    return _head_of(_iota(shape, 0)) == _head_of(_iota(shape, 1))


def _seq_col_spec(rows, col, ncols):
    return pl.BlockSpec((rows, GROUP_WIDTH), lambda b, c: (c, b * ncols + col))


HGRN_CHUNK = 128
HGRN_SUB = 16


HGRN_SEQS_PER_STEP = 4


def _run_staged(stages):
    for _ in itertools.zip_longest(*stages):
        pass


def _hgrn_chunk_kernel(*refs, nseq, layer):
    lbl_ref, ng_ref, y_ref, st_ref, w_scr, k_scr, b_scr, v_scr, p_scr, o_scr = refs[4 * nseq:]
    gw = GROUP_WIDTH
    c = pl.program_id(1)

    @pl.when(c == 0)
    def _():
        w_scr[...] = jnp.zeros_like(w_scr)
        p_scr[...] = jnp.zeros_like(p_scr)

    _run_staged([_hgrn_chunk_one(*refs[4 * s:4 * s + 4], lbl_ref, ng_ref, y_ref.at[:, s * gw:(s + 1) * gw],
                                 w_scr.at[s], k_scr.at[s], b_scr.at[s], v_scr.at[s], p_scr.at[s], o_scr.at[s],
                                 layer=layer) for s in range(nseq)])

    @pl.when(c == pl.num_programs(1) - 1)
    def _():
        for s in range(nseq):
            st_ref[s * gw:(s + 1) * gw, :] = w_scr[s]


def _hgrn_chunk_one(pq_ref, pf_ref, pi_ref, pg_ref, lbl_ref, ng_ref, y_ref, w_scr, k_scr, b_scr, v_scr, p_scr,
                    o_scr, *, layer):
    L, n = HGRN_CHUNK, HGRN_SUB
    half = n // 2
    lb = _hgrn_lower_bound(lbl_ref, layer)
    z = pf_ref[...]
    g = jnp.logaddexp(jnp.log1p(-lb) + jax.nn.log_sigmoid(z), jnp.log(lb))
    kg = (1.0 - lb) * jax.nn.sigmoid(-z)
    q = jax.nn.silu(pq_ref[...])
    v = pi_ref[...]
    bc = _cumsum_rows(g)
    k_scr[...] = kg
    b_scr[...] = bc
    v_scr[...] = v
    yield

    bd = _block_diag_mask()
    ones_bd = jnp.where(bd, 1.0, 0.0).astype(BF16)
    rid = _iota((n, GROUP_WIDTH), 0)
    rid_lo = _iota((half, GROUP_WIDTH), 0) + half
    for sb in range(L // n):
        base = sb * n
        qs, bs = q[base:base + n], bc[base:base + n]
        q_lo, b_lo = q[base + half:base + n], bc[base + half:base + n]
        for s in range(n):
            ks, bsrow = k_scr[base + s:base + s + 1, :], b_scr[base + s:base + s + 1, :]
            if s < half:
                p_scr[s * n:(s + 1) * n, :] = qs * ks * jnp.where(rid >= s, jnp.exp(bs - bsrow), 0.0)
            else:
                p_scr[s * n + half:(s + 1) * n, :] = q_lo * ks * jnp.where(rid_lo >= s, jnp.exp(b_lo - bsrow), 0.0)
        r = _dot(p_scr[...].astype(BF16), ones_bd)
        acc = jnp.zeros((n, GROUP_WIDTH), F32)
        for s in range(n):
            acc = acc + r[s * n:(s + 1) * n] * v_scr[base + s:base + s + 1, :]
        o_scr[base:base + n, :] = acc
        yield

    vb = v.astype(BF16)
    for i in range(1, L // n):
        r0 = i * n
        ref = b_scr[r0 - 1:r0, :]
        qt = q[r0:r0 + n] * jnp.exp(bc[r0:r0 + n] - ref)
        kt = jnp.concatenate([kg[:r0] * jnp.exp(ref - bc[:r0]), jnp.zeros((L - r0, GROUP_WIDTH), F32)], axis=0)
        att = _dot_nt(_head_expand(qt).astype(BF16), kt.astype(BF16))
        yield
        ox = _dot(att.astype(BF16), vb)
        o_scr[r0:r0 + n, :] += _head_collapse(jnp.where(_own_head(ox.shape, n), ox, 0.0))
        yield

    w = w_scr[...]
    o = o_scr[...] + _dot_nt((q * jnp.exp(bc)).astype(BF16), w.astype(BF16))
    b_end = b_scr[L - 1:L, :]
    upd = _dot_tn(vb, (kg * jnp.exp(b_end - bc)).astype(BF16))
    w_scr[...] = w * jnp.exp(b_end) + jnp.where(bd, upd, 0.0)
    yield

    ms = _head_sums(o * o) * (1.0 / HEAD)
    out = o * lax.rsqrt(ms + HGRN_NORM_EPS) * ng_ref[...] * jax.nn.silu(pg_ref[...])
    y_ref[...] = out.astype(y_ref.dtype)


def _seq_group_col_spec(rows, col, ncols, s, nseq):
    return pl.BlockSpec((rows, GROUP_WIDTH), lambda i, c: (c, (i * nseq + s) * ncols + col))


def hgrn_chunk_mixer(proj2, lb_logits, norm_g, *, bsz, t_len, layer):
    L, n, nseq = HGRN_CHUNK, HGRN_SUB, HGRN_SEQS_PER_STEP
    gw = GROUP_WIDTH
    ncols = PROJ_WIDTH // gw
    tile = pltpu.VMEM((nseq, L, gw), F32)
    seq_specs = [_seq_group_col_spec(L, col, ncols, s, nseq)
                 for s in range(nseq) for col in (COL_Q, COL_F, COL_I, COL_G)]
    y, st = pl.pallas_call(
        functools.partial(_hgrn_chunk_kernel, nseq=nseq, layer=layer),
        grid=(bsz // nseq, t_len // L),
        in_specs=seq_specs + [pl.BlockSpec((DEPTH, gw), lambda i, c: (0, 0)),
                              pl.BlockSpec((1, gw), lambda i, c: (0, 0))],
        out_specs=[pl.BlockSpec((L, nseq * gw), lambda i, c: (c, i)),
                   pl.BlockSpec((nseq * gw, gw), lambda i, c: (i, 0))],
        out_shape=[jax.ShapeDtypeStruct((t_len, bsz * gw), BF16),
                   jax.ShapeDtypeStruct((bsz * gw, gw), F32)],
        scratch_shapes=[pltpu.VMEM((nseq, gw, gw), F32), tile, tile, tile,
                        pltpu.VMEM((nseq, n * n, gw), F32), tile],
        compiler_params=_params(("parallel", "arbitrary")),
        name="hgrn_chunk_mixer",
    )(*([proj2] * (4 * nseq)), lb_logits, norm_g)
    return y, st


RWKV_CHUNK = 64


def _rwkv_chunk_kernel(*refs, nseq):
    params = refs[4 * nseq:4 * nseq + 11]
    y_ref, st_ref, sh_ref, w_scr, prev_scr = refs[4 * nseq + 11:]
    gw = GROUP_WIDTH
    c = pl.program_id(1)

    @pl.when(c == 0)
    def _():
        w_scr[...] = jnp.zeros_like(w_scr)
        prev_scr[...] = jnp.zeros_like(prev_scr)

    _run_staged([_rwkv_chunk_one(*refs[4 * s:4 * s + 4], *params, y_ref.at[:, s * gw:(s + 1) * gw], w_scr.at[s],
                                 prev_scr.at[s]) for s in range(nseq)])

    @pl.when(c == pl.num_programs(1) - 1)
    def _():
        for s in range(nseq):
            st_ref[s * gw:(s + 1) * gw, :] = w_scr[s]
        sh_ref[...] = prev_scr[...]


def _rwkv_chunk_one(pr_ref, pk_ref, pv_ref, pl_ref, mu_ref, w0_ref, a0_ref, kk_ref, ka_ref, rk_ref, lng_ref,
                    lnb_ref, w2_ref, a2_ref, g2_ref, y_ref, w_scr, prev_scr):
    L = RWKV_CHUNK
    gw = GROUP_WIDTH
    rid = _iota((L, gw), 0)

    def shifted(ref, j):
        x = ref[...]
        prev = jnp.where(rid == 0, prev_scr[:, j * gw:(j + 1) * gw], pltpu.roll(x, 1, axis=0))
        prev_scr[:, j * gw:(j + 1) * gw] = x[L - 1:L]
        return x + (prev - x) * mu_ref[:, j * gw:(j + 1) * gw]

    xr, xk, xv, xl = shifted(pr_ref, 0), shifted(pk_ref, 1), shifted(pv_ref, 2), shifted(pl_ref, 3)
    w = -jax.nn.softplus(-(w0_ref[...] + _dot(jnp.tanh(xl).astype(BF16), w2_ref[...]))) - 0.5
    lw = -jnp.exp(w)
    a = jax.nn.sigmoid(a0_ref[...] + _dot(xl.astype(BF16), a2_ref[...]))
    g = _dot(jax.nn.sigmoid(xl).astype(BF16), g2_ref[...])
    kk = xk * kk_ref[...]
    kk = kk / jnp.maximum(jnp.sqrt(_head_sums(kk * kk)), 1e-12)
    k = xk * (1.0 + (a - 1.0) * ka_ref[...])
    beta = kk * a
    yield

    cs = _cumsum_rows(lw)
    c_end = cs[L - 1:L]
    e_neg = jnp.exp(-cs)
    e_end = jnp.exp(c_end - cs)
    ar = jnp.concatenate([_head_expand(-kk * jnp.exp(cs - lw)), _head_expand(xr * jnp.exp(cs))], axis=0).astype(BF16)
    bk = jnp.concatenate([_head_expand(beta * e_neg), _head_expand(k * e_neg)], axis=0).astype(BF16)
    vx = _head_expand(xv).astype(BF16)
    yield

    nh = 4 * L
    gmat = _dot_nt(ar, bk)
    tt = _iota((nh, nh), 0) & (L - 1)
    ss = _iota((nh, nh), 1) & (L - 1)
    strict, incl = ss < tt, ss <= tt
    nab = jnp.where(strict, gmat[0:nh, 0:nh], 0.0)
    nak = jnp.where(strict, gmat[0:nh, nh:2 * nh], 0.0).astype(BF16)
    nrb = jnp.where(incl, gmat[nh:2 * nh, 0:nh], 0.0).astype(BF16)
    nrk = jnp.where(incl, gmat[nh:2 * nh, nh:2 * nh], 0.0).astype(BF16)
    yield

    ri, ci = _iota((nh, nh), 0), _iota((nh, nh), 1)

    def same_block(size):
        sh = size.bit_length() - 1
        return lax.shift_right_logical(ri, sh) == lax.shift_right_logical(ci, sh)

    base = 8
    m = jnp.where(same_block(base), nab, 0.0)
    t_inv = jnp.where(ri == ci, 1.0, 0.0) + m
    m = m.astype(BF16)
    for _ in range(base.bit_length() - 2):
        m = _dot(m, m).astype(BF16)
        yield
        t_inv = t_inv + _dot(t_inv.astype(BF16), m)
        yield
    size = base
    while size < L:
        off = jnp.where(same_block(2 * size), jnp.where(same_block(size), 0.0, nab), 0.0).astype(BF16)
        tb = t_inv.astype(BF16)
        half = _dot(tb, off).astype(BF16)
        yield
        t_inv = t_inv + _dot(half, tb)
        yield
        size *= 2

    wst = w_scr[...]
    sw = _dot_nt(ar, wst.astype(BF16))
    rhs = (sw[0:nh] + _dot(nak, vx)).astype(BF16)
    yield
    x = _dot(t_inv.astype(BF16), rhs)
    yield
    ux = x.astype(BF16)
    yx = sw[nh:2 * nh] + _dot(nrb, ux) + _dot(nrk, vx)
    y = _head_collapse(yx)
    u = _head_collapse(x)
    yield

    upd = _dot_tn(jnp.concatenate([u, xv], axis=0).astype(BF16),
                  jnp.concatenate([beta * e_end, k * e_end], axis=0).astype(BF16))
    w_scr[...] = wst * jnp.exp(c_end) + jnp.where(_block_diag_mask(), upd, 0.0)
    yield

    mean = _head_sums(y) * (1.0 / HEAD)
    d = y - mean
    var = _head_sums(d * d) * (1.0 / HEAD)
    yn = d * lax.rsqrt(var + RWKV_GN_EPS) * lng_ref[...] + lnb_ref[...]
    bonus = _head_sums(xr * k * rk_ref[...]) * xv
    y_ref[...] = ((yn + bonus) * g).astype(y_ref.dtype)


RWKV_SEQS_PER_STEP = 8


def rwkv_chunk_mixer(proj2, mu, w0, a0, k_k, k_a, r_k, ln_g, ln_b, w2p, a2p, g2p, *, bsz, t_len):
    L, nseq = RWKV_CHUNK, RWKV_SEQS_PER_STEP
    gw = GROUP_WIDTH
    ncols = PROJ_WIDTH // gw
    vec = pl.BlockSpec((1, gw), lambda i, c: (0, 0))
    mat = pl.BlockSpec((gw, gw), lambda i, c: (0, 0))
    seq_specs = [_seq_group_col_spec(L, col, ncols, s, nseq)
                 for s in range(nseq) for col in (COL_R, COL_K, COL_V, COL_LORA)]
    y, st, sh = pl.pallas_call(
        functools.partial(_rwkv_chunk_kernel, nseq=nseq),
        grid=(bsz // nseq, t_len // L),
        in_specs=seq_specs + [pl.BlockSpec((1, RWKV_PROJ), lambda i, c: (0, 0)),
                              vec, vec, vec, vec, vec, vec, vec, mat, mat, mat],
        out_specs=[pl.BlockSpec((L, nseq * gw), lambda i, c: (c, i)),
                   pl.BlockSpec((nseq * gw, gw), lambda i, c: (i, 0)),
                   pl.BlockSpec((nseq, 1, RWKV_PROJ), lambda i, c: (i, 0, 0))],
        out_shape=[jax.ShapeDtypeStruct((t_len, bsz * gw), BF16),
                   jax.ShapeDtypeStruct((bsz * gw, gw), F32),
                   jax.ShapeDtypeStruct((bsz, 1, RWKV_PROJ), F32)],
        scratch_shapes=[pltpu.VMEM((nseq, gw, gw), F32), pltpu.VMEM((nseq, 1, RWKV_PROJ), F32)],
        compiler_params=_params(("parallel", "arbitrary")),
        name="rwkv_chunk_mixer",
    )(*([proj2] * (4 * nseq)), mu, w0, a0, k_k, k_a, r_k, ln_g, ln_b, w2p, a2p, g2p)
    return y, st, sh


S5_CHUNK = 64


def _s5_seq_kernel(*refs, nseq):
    u_refs = refs[:nseq]
    lr_ref, li_ref, ldt_ref, bre_ref, bim_ref, ccat_ref, d_ref, gw_ref, gb_ref = refs[nseq:nseq + 9]
    y_ref, hr_out, hi_out, h_scr, bu_scr, bb_scr, ab_scr, perm_scr = refs[nseq + 9:]
    tc = S5_CHUNK
    n = nseq * tc
    c = pl.program_id(0)

    @pl.when(c == 0)
    def _():
        h_scr[...] = jnp.zeros_like(h_scr)
        lr, li = lr_ref[...], li_ref[...]
        dt = jnp.exp(ldt_ref[...])
        mag = jnp.exp(lr * dt)
        ab_re, ab_im = mag * jnp.cos(li * dt), mag * jnp.sin(li * dt)
        den = lr * lr + li * li
        zr, zi = ab_re - 1.0, ab_im
        cr = (zr * lr + zi * li) / den
        ci = (zi * lr - zr * li) / den
        bre, bim = bre_ref[...], bim_ref[...]
        bb_scr[:, 0:SSM_FLAT] = (cr * bre - ci * bim).astype(BF16)
        bb_scr[:, SSM_FLAT:2 * SSM_FLAT] = (cr * bim + ci * bre).astype(BF16)
        ab_scr[0] = jnp.broadcast_to(ab_re, (nseq, SSM_FLAT))
        ab_scr[1] = jnp.broadcast_to(ab_im, (nseq, SSM_FLAT))
        ri, cj = _iota((n, n), 0), _iota((n, n), 1)
        lseq, lt = nseq.bit_length() - 1, tc.bit_length() - 1
        perm_scr[0] = jnp.where(cj == (ri & (nseq - 1)) * tc + lax.shift_right_logical(ri, lseq), 1.0, 0.0).astype(BF16)
        perm_scr[1] = jnp.where(cj == (ri & (tc - 1)) * nseq + lax.shift_right_logical(ri, lt), 1.0, 0.0).astype(BF16)

    u = jnp.concatenate([r[...] for r in u_refs], axis=0)
    to_seq = perm_scr[1]
    u_t = _dot(perm_scr[0], u.astype(BF16)).astype(BF16)
    bu_scr[...] = _dot(u_t, bb_scr[...])

    ar, ai = ab_scr[0], ab_scr[1]

    def step(t, carry):
        hr, hi = carry
        rows = pl.ds(pl.multiple_of(t * nseq, nseq), nseq)
        nhr = ar * hr - ai * hi + bu_scr[rows, 0:SSM_FLAT]
        nhi = ar * hi + ai * hr + bu_scr[rows, SSM_FLAT:2 * SSM_FLAT]
        bu_scr[rows, 0:SSM_FLAT] = nhr
        bu_scr[rows, SSM_FLAT:2 * SSM_FLAT] = nhi
        return nhr, nhi

    hr, hi = lax.fori_loop(0, tc, step, (h_scr[0], h_scr[1]))
    h_scr[0] = hr
    h_scr[1] = hi

    y_t = _dot(bu_scr[...].astype(BF16), ccat_ref[...])
    hi_p = y_t.astype(BF16)
    rest = y_t - hi_p.astype(F32)
    mid_p = rest.astype(BF16)
    lo_p = (rest - mid_p.astype(F32)).astype(BF16)
    y = _dot(to_seq, hi_p) + _dot(to_seq, mid_p) + _dot(to_seq, lo_p) + d_ref[...] * u
    z = jax.nn.gelu(y)
    out = z * jax.nn.sigmoid(_dot(z.astype(BF16), gw_ref[...]) + gb_ref[...])
    for b in range(nseq):
        y_ref[:, b * GROUP_WIDTH:(b + 1) * GROUP_WIDTH] = out[b * tc:(b + 1) * tc].astype(y_ref.dtype)

    @pl.when(c == pl.num_programs(0) - 1)
    def _():
        hr_out[...] = hr
        hi_out[...] = hi


def s5_seq_mixer(proj2, lam_re, lam_im, log_dt, b_re_bd, b_im_bd, c_cat, d_skip, glu_w, glu_b, *, bsz, t_len):
    tc = S5_CHUNK
    ncols = PROJ_WIDTH // GROUP_WIDTH
    full = lambda shape: pl.BlockSpec(shape, lambda c: (0,) * len(shape))
    st = jax.ShapeDtypeStruct((bsz, SSM_FLAT), F32)
    u_specs = [pl.BlockSpec((tc, GROUP_WIDTH), functools.partial(lambda c, b: (c, b * ncols + COL_SSM), b=b))
               for b in range(bsz)]
    return pl.pallas_call(
        functools.partial(_s5_seq_kernel, nseq=bsz),
        grid=(t_len // tc,),
        in_specs=u_specs + [full((1, SSM_FLAT)), full((1, SSM_FLAT)), full((1, SSM_FLAT)),
                            full((GROUP_WIDTH, SSM_FLAT)), full((GROUP_WIDTH, SSM_FLAT)),
                            full((2 * SSM_FLAT, GROUP_WIDTH)), full((1, GROUP_WIDTH)),
                            full((GROUP_WIDTH, GROUP_WIDTH)), full((1, GROUP_WIDTH))],
        out_specs=[pl.BlockSpec((tc, bsz * GROUP_WIDTH), lambda c: (c, 0)), full((bsz, SSM_FLAT)),
                   full((bsz, SSM_FLAT))],
        out_shape=[jax.ShapeDtypeStruct((t_len, bsz * GROUP_WIDTH), BF16), st, st],
        scratch_shapes=[pltpu.VMEM((2, bsz, SSM_FLAT), F32), pltpu.VMEM((bsz * tc, 2 * SSM_FLAT), F32),
                        pltpu.VMEM((GROUP_WIDTH, 2 * SSM_FLAT), BF16), pltpu.VMEM((2, bsz, SSM_FLAT), F32),
                        pltpu.VMEM((2, bsz * tc, bsz * tc), BF16)],
        compiler_params=_params(("arbitrary",)),
        name="s5_seq_mixer",
    )(*([proj2] * bsz), lam_re, lam_im, log_dt, b_re_bd, b_im_bd, c_cat, d_skip, glu_w, glu_b)


POOL_CHUNK = 1024
POOL_HIST = 16


def _pool_seq_kernel(u_ref, w_ref, sc_ref, y_ref, nbuf_ref, ext_scr):
    L, hist = POOL_CHUNK, POOL_HIST
    c = pl.program_id(1)

    @pl.when(c == 0)
    def _():
        ext_scr[0:hist] = jnp.zeros((hist, GROUP_WIDTH), F32)

    u = u_ref[...]
    ext_scr[hist:hist + L] = u
    e = ext_scr[...]
    a2 = e + pltpu.roll(e, 1, axis=0)
    a4 = a2 + pltpu.roll(a2, 2, axis=0)
    a8 = a4 + pltpu.roll(a4, 4, axis=0)
    a16 = a8 + pltpu.roll(a8, 8, axis=0)
    sums = (a2[hist:], a4[hist:], a8[hist:], a16[hist:])

    shape = (L, GROUP_WIDTH)
    pos = _iota(shape, 0) + c * L
    lane = _iota(shape, 1)
    pooled = None
    for gi in reversed(range(len(POOL_WINDOWS))):
        win = POOL_WINDOWS[gi]
        mean = sums[gi] / jnp.minimum(pos + 1, win).astype(F32)
        pooled = mean if pooled is None else jnp.where(lane < (gi + 1) * HEAD, mean, pooled)
    y_ref[...] = (_dot((pooled - u).astype(BF16), w_ref[...]) * sc_ref[...]).astype(y_ref.dtype)

    nb = ext_scr[L:L + hist]
    ext_scr[0:hist] = nb

    @pl.when(c == pl.num_programs(1) - 1)
    def _():
        nbuf_ref[...] = nb


def pool_seq_mixer(proj2, w_bd, scale, *, bsz, t_len):
    L, hist = POOL_CHUNK, POOL_HIST
    ncols = PROJ_WIDTH // GROUP_WIDTH
    return pl.pallas_call(
        _pool_seq_kernel,
        grid=(bsz, t_len // L),
        in_specs=[_seq_col_spec(L, COL_POOL, ncols),
                  pl.BlockSpec((GROUP_WIDTH, GROUP_WIDTH), lambda b, c: (0, 0)),
                  pl.BlockSpec((1, GROUP_WIDTH), lambda b, c: (0, 0))],
        out_specs=[_seq_col_spec(L, 0, 1), pl.BlockSpec((hist, GROUP_WIDTH), lambda b, c: (b, 0))],
        out_shape=[jax.ShapeDtypeStruct((t_len, bsz * GROUP_WIDTH), BF16),
                   jax.ShapeDtypeStruct((bsz * hist, GROUP_WIDTH), F32)],
        scratch_shapes=[pltpu.VMEM((hist + L, GROUP_WIDTH), F32)],
        compiler_params=_params(("parallel", "arbitrary")),
        name="pool_seq_mixer",
    )(proj2, w_bd, scale)


def _diag_heads(st, bsz):
    nh = GROUP_WIDTH // HEAD
    s = st.reshape(bsz, nh, HEAD, nh, HEAD)
    return jnp.stack([s[:, h, :, h, :] for h in range(nh)], axis=1)


def _mix_mlp_kernel(h_ref, ya_ref, yb_ref, yc_ref, yd_ref, wo_ref, g2_ref, wu_ref, wd_ref, gf_ref, o_ref,
                    h1_scr, xn_scr, acc_scr, *, final_norm):
    j = pl.program_id(1)
    gw = GROUP_WIDTH

    @pl.when(j == 0)
    def _():
        mix = (_dot(ya_ref[...], wo_ref[0:gw]) + _dot(yb_ref[...], wo_ref[gw:2 * gw])
               + _dot(yc_ref[...], wo_ref[2 * gw:3 * gw]) + _dot(yd_ref[...], wo_ref[3 * gw:4 * gw]))
        h1 = h_ref[...] + mix
        h1_scr[...] = h1
        xn_scr[...] = _rms(h1, g2_ref[...]).astype(BF16)
        acc_scr[...] = jnp.zeros_like(acc_scr)

    up = _dot(xn_scr[...], wu_ref[...])
    act = jnp.square(jnp.maximum(up, 0.0)).astype(BF16)
    acc_scr[...] += _dot(act, wd_ref[...])

    @pl.when(j == pl.num_programs(1) - 1)
    def _():
        out = h1_scr[...] + acc_scr[...]
        if final_norm:
            out = _rms(out, gf_ref[...])
        o_ref[...] = out


def mix_mlp(h, ys, w_out, g2, w_up, w_down, g_final, *, final_norm, nseq=0):
    n = h.shape[0]
    tm = min(512, n // max(nseq, 1))
    tf = 1024
    nt = n // tm // max(nseq, 1)
    row = lambda w: pl.BlockSpec((tm, w), lambda i, j: (i, 0))
    mix = pl.BlockSpec((tm, GROUP_WIDTH), _tile_map(nseq, nt))
    return pl.pallas_call(
        functools.partial(_mix_mlp_kernel, final_norm=final_norm),
        grid=(n // tm, D_FF // tf),
        in_specs=[row(D_MODEL), mix, mix, mix, mix,
                  pl.BlockSpec((D_MODEL, D_MODEL), lambda i, j: (0, 0)),
                  pl.BlockSpec((1, D_MODEL), lambda i, j: (0, 0)),
                  pl.BlockSpec((D_MODEL, tf), lambda i, j: (0, j)),
                  pl.BlockSpec((tf, D_MODEL), lambda i, j: (j, 0)),
                  pl.BlockSpec((1, D_MODEL), lambda i, j: (0, 0))],
        out_specs=row(D_MODEL),
        out_shape=jax.ShapeDtypeStruct((n, D_MODEL), F32),
        scratch_shapes=[pltpu.VMEM((tm, D_MODEL), F32), pltpu.VMEM((tm, D_MODEL), BF16),
                        pltpu.VMEM((tm, D_MODEL), F32)],
        compiler_params=_params(("parallel", "arbitrary")),
        name="mix_mlp",
    )(h, *ys, w_out, g2, w_up, w_down, g_final)


def _hgrn_state_in(s):
    b = s.shape[0]
    return s.reshape(b, 2, 2, HEAD, HEAD).transpose(0, 1, 4, 2, 3).reshape(b, 2, HEAD, LANES)


def _hgrn_state_out(s):
    b = s.shape[0]
    return s.reshape(b, 2, HEAD, 2, HEAD).transpose(0, 1, 3, 4, 2).reshape(b, 4, HEAD, HEAD)


def _wkv_state_in(s):
    b = s.shape[0]
    return s.reshape(b, 2, 2, HEAD, HEAD).transpose(0, 1, 3, 2, 4).reshape(b, 2, HEAD, LANES)


def _wkv_state_out(s):
    b = s.shape[0]
    return s.reshape(b, 2, HEAD, 2, HEAD).transpose(0, 1, 3, 2, 4).reshape(b, 4, HEAD, HEAD)


def _block_diag(blocks):
    g, r, c = blocks.shape
    eye = jnp.eye(g, dtype=blocks.dtype)
    return (blocks[:, :, None, :] * eye[:, None, :, None]).reshape(g * r, g * c)


def _pad_rows(w, start):
    return jnp.zeros((GROUP_WIDTH, GROUP_WIDTH), w.dtype).at[start:start + w.shape[0]].set(w)


def _layer_params(l, P):
    row = lambda a: a.reshape(1, -1)
    q = {}
    q["norm1_g"] = row(P["norm1_g"][l])
    q["w_in"] = P["w_in"][l].astype(BF16)
    q["lam_re"] = row(P["ssm_lambda_re"][l])
    q["lam_im"] = row(P["ssm_lambda_im"][l])
    q["log_dt"] = row(jnp.repeat(P["ssm_log_dt"][l], SSM_STATE))
    q["b_re"] = _block_diag(P["ssm_b_re"][l].transpose(0, 2, 1))
    q["b_im"] = _block_diag(P["ssm_b_im"][l].transpose(0, 2, 1))
    q["c_cat"] = jnp.concatenate([_block_diag(P["ssm_c_re"][l].transpose(0, 2, 1)),
                                  -_block_diag(P["ssm_c_im"][l].transpose(0, 2, 1))], axis=0).astype(BF16)
    q["ssm_d"] = row(P["ssm_d"][l])
    q["glu_w"] = P["ssm_glu_w"][l].astype(BF16)
    q["glu_b"] = row(P["ssm_glu_b"][l])
    q["hgrn_norm_g"] = row(P["hgrn_norm_g"][l])
    q["mu"] = row(P["rwkv_mu"][l])
    for name in ("w0", "a0", "k_k", "k_a", "r_k", "ln_g", "ln_b"):
        q[name] = row(P["rwkv_" + name][l])
    q["w2p"] = _pad_rows(P["rwkv_w2"][l], 0).astype(BF16)
    q["a2p"] = _pad_rows(P["rwkv_a2"][l], DECAY_LORA).astype(BF16)
    q["g2p"] = _pad_rows(P["rwkv_g2"][l], DECAY_LORA + AAA_LORA).astype(BF16)
    q["pool_w"] = _block_diag(P["pool_w"][l]).astype(BF16)
    q["pool_scale"] = row(P["pool_scale"][l])
    q["w_out"] = P["w_out"][l].astype(BF16)
    q["norm2_g"] = row(P["norm2_g"][l])
    q["mlp_up"] = P["mlp_up"][l].astype(BF16)
    q["mlp_down"] = P["mlp_down"][l].astype(BF16)
    return q


def _trunk_fresh(x_rows, bsz, t_len, layer_params, P):
    h = x_rows
    new = [[] for _ in range(6)]
    g_final = P["norm_f_g"].reshape(1, -1)
    for l in range(DEPTH):
        q = layer_params[l]
        proj2 = rms_proj(h, q["norm1_g"], q["w_in"], nseq=bsz)
        y_a, s_re, s_im = s5_seq_mixer(proj2, q["lam_re"], q["lam_im"], q["log_dt"], q["b_re"], q["b_im"],
                                       q["c_cat"], q["ssm_d"], q["glu_w"], q["glu_b"], bsz=bsz, t_len=t_len)
        y_b, s_hg = hgrn_chunk_mixer(proj2, P["hgrn_lb_logits"], q["hgrn_norm_g"], bsz=bsz, t_len=t_len, layer=l)
        y_c, s_wkv, s_sh = rwkv_chunk_mixer(proj2, q["mu"], q["w0"], q["a0"], q["k_k"], q["k_a"], q["r_k"],
                                            q["ln_g"], q["ln_b"], q["w2p"], q["a2p"], q["g2p"], bsz=bsz, t_len=t_len)
        y_d, s_pool = pool_seq_mixer(proj2, q["pool_w"], q["pool_scale"], bsz=bsz, t_len=t_len)
        h = mix_mlp(h, (y_a, y_b, y_c, y_d), q["w_out"], q["norm2_g"], q["mlp_up"], q["mlp_down"], g_final,
                    final_norm=(l == DEPTH - 1), nseq=bsz)
        s_hg = _diag_heads(s_hg, bsz).swapaxes(-1, -2)
        s_wkv = _diag_heads(s_wkv, bsz)
        s_pool = s_pool.reshape(bsz, POOL_HIST, GROUP_WIDTH)[:, POOL_HIST - POOL_BUF:]
        for lst, s in zip(new, (s_re, s_im, s_hg, s_wkv, s_sh, s_pool)):
            lst.append(s)
    return h, new


def _trunk_carry(x_rows, states, pos0, t_len, layer_params, P):
    ssm_re0, ssm_im0, hgrn0, wkv0, shift0, pool0 = states
    h = x_rows
    new = [[] for _ in range(6)]
    g_final = P["norm_f_g"].reshape(1, -1)
    tc = t_len
    for l in range(DEPTH):
        q = layer_params[l]
        proj = rms_proj(h, q["norm1_g"], q["w_in"])
        y_a, s_re, s_im = s5_mixer(proj, ssm_re0[l], ssm_im0[l], q["lam_re"], q["lam_im"], q["log_dt"], q["b_re"],
                                   q["b_im"], q["c_cat"], q["ssm_d"], q["glu_w"], q["glu_b"], t_len=t_len, tc=tc)
        y_b, s_hg = hgrn_mixer(proj, hgrn0[l], P["hgrn_lb_logits"], q["hgrn_norm_g"], t_len=t_len, tc=tc, layer=l)
        y_c, s_wkv, s_sh = rwkv_mixer(proj, shift0[l], wkv0[l], q["mu"], q["w0"], q["a0"], q["k_k"], q["k_a"],
                                      q["r_k"], q["ln_g"], q["ln_b"], q["w2p"], q["a2p"], q["g2p"],
                                      t_len=t_len, tc=tc)
        y_d, s_pool = pool_mixer(proj, pool0[l], q["pool_w"], q["pool_scale"], t_len=t_len, tc=tc, pos0=pos0)
        h = mix_mlp(h, (y_a, y_b, y_c, y_d), q["w_out"], q["norm2_g"], q["mlp_up"], q["mlp_down"], g_final,
                    final_norm=(l == DEPTH - 1))
        for lst, s in zip(new, (s_re, s_im, _hgrn_state_out(s_hg), _wkv_state_out(s_wkv), s_sh,
                                s_pool.transpose(1, 0, 2))):
            lst.append(s)
    return h, new


def _states_out(new, bsz):
    s_re, s_im, s_hg, s_wkv, s_sh, s_pool = new
    return (jnp.stack([s.reshape(bsz, SSM_GROUPS, SSM_STATE) for s in s_re]),
            jnp.stack([s.reshape(bsz, SSM_GROUPS, SSM_STATE) for s in s_im]),
            jnp.stack(s_hg),
            jnp.stack(s_wkv),
            jnp.stack([s.reshape(bsz, 1, RWKV_PROJ) for s in s_sh]),
            jnp.stack(s_pool))


def kernel(x_prompt, x_sample, state_ssm_re, state_ssm_im, state_hgrn, state_wkv, state_shift, state_pool, norm1_g, w_in, ssm_lambda_re, ssm_lambda_im, ssm_log_dt, ssm_b_re, ssm_b_im, ssm_c_re, ssm_c_im, ssm_d, ssm_glu_w, ssm_glu_b, hgrn_lb_logits, hgrn_norm_g, rwkv_mu, rwkv_w0, rwkv_w2, rwkv_a0, rwkv_a2, rwkv_g2, rwkv_k_k, rwkv_k_a, rwkv_r_k, rwkv_ln_g, rwkv_ln_b, pool_w, pool_scale, w_out, norm2_g, mlp_up, mlp_down, norm_f_g):
    P = dict(norm1_g=norm1_g, w_in=w_in, ssm_lambda_re=ssm_lambda_re, ssm_lambda_im=ssm_lambda_im,
             ssm_log_dt=ssm_log_dt, ssm_b_re=ssm_b_re, ssm_b_im=ssm_b_im, ssm_c_re=ssm_c_re, ssm_c_im=ssm_c_im,
             ssm_d=ssm_d, ssm_glu_w=ssm_glu_w, ssm_glu_b=ssm_glu_b, hgrn_lb_logits=hgrn_lb_logits,
             hgrn_norm_g=hgrn_norm_g, rwkv_mu=rwkv_mu, rwkv_w0=rwkv_w0, rwkv_w2=rwkv_w2, rwkv_a0=rwkv_a0,
             rwkv_a2=rwkv_a2, rwkv_g2=rwkv_g2, rwkv_k_k=rwkv_k_k, rwkv_k_a=rwkv_k_a, rwkv_r_k=rwkv_r_k,
             rwkv_ln_g=rwkv_ln_g, rwkv_ln_b=rwkv_ln_b, pool_w=pool_w, pool_scale=pool_scale, w_out=w_out,
             norm2_g=norm2_g, mlp_up=mlp_up, mlp_down=mlp_down, norm_f_g=norm_f_g)
    layer_params = [_layer_params(l, P) for l in range(DEPTH)]

    bp, t_p, _ = x_prompt.shape
    yp, new_p = _trunk_fresh(x_prompt.reshape(bp * t_p, D_MODEL), bp, t_p, layer_params, P)
    y_prompt = yp.reshape(bp, t_p, D_MODEL)

    bs, t_s, _ = x_sample.shape
    nblk = bs // SEQ_BLK
    xs = x_sample.reshape(nblk, SEQ_BLK, t_s, D_MODEL).transpose(0, 2, 1, 3).reshape(bs * t_s, D_MODEL)
    st_s = ([state_ssm_re[l].reshape(bs, SSM_FLAT) for l in range(DEPTH)],
            [state_ssm_im[l].reshape(bs, SSM_FLAT) for l in range(DEPTH)],
            [_hgrn_state_in(state_hgrn[l]) for l in range(DEPTH)],
            [_wkv_state_in(state_wkv[l]) for l in range(DEPTH)],
            [state_shift[l].reshape(bs, RWKV_PROJ) for l in range(DEPTH)],
            [state_pool[l].transpose(1, 0, 2) for l in range(DEPTH)])
    ys, new_s = _trunk_carry(xs, st_s, PAST_LEN, t_s, layer_params, P)
    y_sample = ys.reshape(nblk, t_s, SEQ_BLK, D_MODEL).transpose(0, 2, 1, 3).reshape(bs, t_s, D_MODEL)

    return (y_prompt, y_sample) + _states_out(new_p, bp) + _states_out(new_s, bs)
```

```python
import functools
import itertools

import jax
import jax.numpy as jnp
from jax import lax
from jax.experimental import pallas as pl
from jax.experimental.pallas import tpu as pltpu

F32 = jnp.float32
BF16 = jnp.bfloat16

D_MODEL = 1024
DEPTH = 2
PAST_LEN = 16384
GROUP_WIDTH = 256
HEAD = 64
SSM_GROUPS = 16
SSM_CH = 16
SSM_STATE = 64
SSM_FLAT = SSM_GROUPS * SSM_STATE
POOL_WINDOWS = (2, 4, 8, 16)
POOL_BUF = 15
DECAY_LORA = 64
AAA_LORA = 64
GATE_LORA = 128
RWKV_PROJ = 1024
PROJ_WIDTH = 2560
D_FF = 4096
NORM_EPS = 1e-6
HGRN_NORM_EPS = 1e-5
RWKV_GN_EPS = 64e-5

SEQ_BLK = 8
LANES = 128
VMEM_LIMIT = 48 * 1024 * 1024

COL_SSM, COL_Q, COL_F, COL_I, COL_G, COL_R, COL_K, COL_V, COL_LORA, COL_POOL = range(10)


def _params(sem):
    return pltpu.CompilerParams(dimension_semantics=sem, vmem_limit_bytes=VMEM_LIMIT)


def _dot(a, b):
    return jnp.dot(a, b, preferred_element_type=F32)


def _rms(x, g):
    return x * lax.rsqrt(jnp.mean(x * x, axis=-1, keepdims=True) + NORM_EPS) * g


def _rms_proj_kernel(x_ref, g_ref, w_ref, o_ref):
    o_ref[...] = _dot(_rms(x_ref[...], g_ref[...]).astype(BF16), w_ref[...])


def _tile_map(nseq, nt):
    if nseq == 0:
        return lambda r, *_: (r, 0)
    return lambda r, *_: (r % nt, r // nt)


def rms_proj(x, g, w, *, nseq=0):
    n = x.shape[0]
    tm = min(512, n // max(nseq, 1))
    nt = n // tm // max(nseq, 1)
    out_shape = (n, PROJ_WIDTH) if nseq == 0 else (n // nseq, nseq * PROJ_WIDTH)
    return pl.pallas_call(
        _rms_proj_kernel,
        grid=(n // tm,),
        in_specs=[pl.BlockSpec((tm, D_MODEL), lambda i: (i, 0)),
                  pl.BlockSpec((1, D_MODEL), lambda i: (0, 0)),
                  pl.BlockSpec((D_MODEL, PROJ_WIDTH), lambda i: (0, 0))],
        out_specs=pl.BlockSpec((tm, PROJ_WIDTH), _tile_map(nseq, nt)),
        out_shape=jax.ShapeDtypeStruct(out_shape, F32),
        compiler_params=_params(("parallel",)),
        name="rms_proj",
    )(x, g, w)


def _row_spec(tc, col, nchunks):
    return pl.BlockSpec((tc * SEQ_BLK, GROUP_WIDTH), lambda s, c: (s * nchunks + c, col))


def _full_spec(shape):
    nd = len(shape)
    return pl.BlockSpec(shape, lambda s, c: (0,) * nd)


def _seq_spec(shape):
    nd = len(shape)
    return pl.BlockSpec((SEQ_BLK,) + shape[1:], lambda s, c: (s,) + (0,) * (nd - 1))


def _head_sums(x):
    lane = lax.broadcasted_iota(jnp.int32, x.shape, 1)
    out = jnp.zeros_like(x)
    for h in range(GROUP_WIDTH // HEAD):
        m = (lane >= h * HEAD) & (lane < (h + 1) * HEAD)
        s = jnp.sum(jnp.where(m, x, 0.0), axis=1, keepdims=True)
        out = jnp.where(m, s, out)
    return out


UNITS = SEQ_BLK * 2


def _unit_masks():
    r, c = _iota((2 * LANES, LANES), 0), _iota((2 * LANES, LANES), 1)
    ones2 = jnp.where(_head_of(r & (LANES - 1)) == _head_of(c), 1.0, 0.0).astype(BF16)
    shape = (UNITS * HEAD, LANES)
    eye = (_iota(shape, 1) & (HEAD - 1)) == (_iota(shape, 0) & (HEAD - 1))
    return ones2, eye


def _seg_sum_mxu(p, ones2):
    hi = p.astype(BF16)
    lo = (p - hi.astype(F32)).astype(BF16)
    return _dot(jnp.concatenate([hi, lo], axis=1), ones2)


def _unit_rows(ref, t):
    return jnp.concatenate([jnp.broadcast_to(ref[t, b:b + 1, p * LANES:(p + 1) * LANES], (HEAD, LANES))
                            for b in range(SEQ_BLK) for p in range(2)], axis=0)


def _store_unit_rows(ref, t, cols, eye):
    picked = jnp.where(eye, cols, 0.0)
    for b in range(SEQ_BLK):
        for p in range(2):
            u = b * 2 + p
            ref[t, b:b + 1, p * LANES:(p + 1) * LANES] = jnp.sum(picked[u * HEAD:(u + 1) * HEAD], axis=0, keepdims=True)


def _s5_kernel(u_ref, h0r_ref, h0i_ref, lr_ref, li_ref, ldt_ref, bre_ref, bim_ref, ccat_ref, d_ref,
               gw_ref, gb_ref, y_ref, hr_out, hi_out, h_scr, bu_scr, *, tc):
    c = pl.program_id(1)

    @pl.when(c == 0)
    def _():
        h_scr[0] = h0r_ref[...]
        h_scr[1] = h0i_ref[...]

    lr, li = lr_ref[...], li_ref[...]
    dt = jnp.exp(ldt_ref[...])
    mag = jnp.exp(lr * dt)
    ab_re, ab_im = mag * jnp.cos(li * dt), mag * jnp.sin(li * dt)
    den = lr * lr + li * li
    zr, zi = ab_re - 1.0, ab_im
    cr = (zr * lr + zi * li) / den
    ci = (zi * lr - zr * li) / den
    bre, bim = bre_ref[...], bim_ref[...]
    bb_re = (cr * bre - ci * bim).astype(BF16)
    bb_im = (cr * bim + ci * bre).astype(BF16)

    u = u_ref[...]
    ub = u.astype(BF16)
    bu_scr[:, 0:SSM_FLAT] = _dot(ub, bb_re)
    bu_scr[:, SSM_FLAT:2 * SSM_FLAT] = _dot(ub, bb_im)

    ar = jnp.broadcast_to(ab_re, (SEQ_BLK, SSM_FLAT))
    ai = jnp.broadcast_to(ab_im, (SEQ_BLK, SSM_FLAT))

    def step(t, carry):
        hr, hi = carry
        rows = pl.ds(pl.multiple_of(t * SEQ_BLK, SEQ_BLK), SEQ_BLK)
        nhr = ar * hr - ai * hi + bu_scr[rows, 0:SSM_FLAT]
        nhi = ar * hi + ai * hr + bu_scr[rows, SSM_FLAT:2 * SSM_FLAT]
        bu_scr[rows, 0:SSM_FLAT] = nhr
        bu_scr[rows, SSM_FLAT:2 * SSM_FLAT] = nhi
        return nhr, nhi

    hr, hi = lax.fori_loop(0, tc, step, (h_scr[0], h_scr[1]))
    h_scr[0] = hr
    h_scr[1] = hi

    y = _dot(bu_scr[...].astype(BF16), ccat_ref[...]) + d_ref[...] * u
    z = jax.nn.gelu(y)
    out = z * jax.nn.sigmoid(_dot(z.astype(BF16), gw_ref[...]) + gb_ref[...])
    y_ref[...] = out.astype(y_ref.dtype)

    @pl.when(c == pl.num_programs(1) - 1)
    def _():
        hr_out[...] = hr
        hi_out[...] = hi


def s5_mixer(proj, h0_re, h0_im, lam_re, lam_im, log_dt, b_re_bd, b_im_bd, c_cat, d_skip, glu_w, glu_b, *, t_len, tc):
    nseq = h0_re.shape[0] // SEQ_BLK
    nchunks = t_len // tc
    n = proj.shape[0]
    st = jax.ShapeDtypeStruct(h0_re.shape, F32)
    return pl.pallas_call(
        functools.partial(_s5_kernel, tc=tc),
        grid=(nseq, nchunks),
        in_specs=[_row_spec(tc, COL_SSM, nchunks),
                  _seq_spec(h0_re.shape), _seq_spec(h0_im.shape),
                  _full_spec((1, SSM_FLAT)), _full_spec((1, SSM_FLAT)), _full_spec((1, SSM_FLAT)),
                  _full_spec((GROUP_WIDTH, SSM_FLAT)), _full_spec((GROUP_WIDTH, SSM_FLAT)),
                  _full_spec((2 * SSM_FLAT, GROUP_WIDTH)), _full_spec((1, GROUP_WIDTH)),
                  _full_spec((GROUP_WIDTH, GROUP_WIDTH)), _full_spec((1, GROUP_WIDTH))],
        out_specs=[_row_spec(tc, 0, nchunks), _seq_spec(h0_re.shape), _seq_spec(h0_im.shape)],
        out_shape=[jax.ShapeDtypeStruct((n, GROUP_WIDTH), BF16), st, st],
        scratch_shapes=[pltpu.VMEM((2, SEQ_BLK, SSM_FLAT), F32),
                        pltpu.VMEM((tc * SEQ_BLK, 2 * SSM_FLAT), F32)],
        compiler_params=_params(("parallel", "arbitrary")),
        name="s5_mixer",
    )(proj, h0_re, h0_im, lam_re, lam_im, log_dt, b_re_bd, b_im_bd, c_cat, d_skip, glu_w, glu_b)


def _pool_kernel(u_ref, buf_ref, w_ref, sc_ref, y_ref, nbuf_ref, ext_scr, *, tc, pos0):
    c = pl.program_id(1)

    @pl.when(c == 0)
    def _():
        ext_scr[0:POOL_BUF] = buf_ref[...]

    u = u_ref[...].reshape(tc, SEQ_BLK, GROUP_WIDTH)
    ext_scr[POOL_BUF:POOL_BUF + tc] = u
    a1 = ext_scr[...]
    a2 = a1[1:] + a1[:-1]
    a4 = a2[2:] + a2[:-2]
    a8 = a4[4:] + a4[:-4]
    a16 = a8[8:] + a8[:-8]
    sums = (a2[14:], a4[12:], a8[8:], a16)

    shape = (tc, SEQ_BLK, GROUP_WIDTH)
    pos = lax.broadcasted_iota(jnp.int32, shape, 0) + (c * tc + pos0)
    lane = lax.broadcasted_iota(jnp.int32, shape, 2)
    pooled = None
    for gi in reversed(range(len(POOL_WINDOWS))):
        win = POOL_WINDOWS[gi]
        mean = sums[gi] / jnp.minimum(pos + 1, win).astype(F32)
        pooled = mean if pooled is None else jnp.where(lane < (gi + 1) * HEAD, mean, pooled)
    pooled = (pooled - u).reshape(tc * SEQ_BLK, GROUP_WIDTH)
    y_ref[...] = (_dot(pooled.astype(BF16), w_ref[...]) * sc_ref[...]).astype(y_ref.dtype)

    nb = ext_scr[tc:tc + POOL_BUF]
    ext_scr[0:POOL_BUF] = nb

    @pl.when(c == pl.num_programs(1) - 1)
    def _():
        nbuf_ref[...] = nb


def pool_mixer(proj, buf, w_bd, scale, *, t_len, tc, pos0):
    nseq = buf.shape[1] // SEQ_BLK
    nchunks = t_len // tc
    n = proj.shape[0]
    buf_spec = pl.BlockSpec((POOL_BUF, SEQ_BLK, GROUP_WIDTH), lambda s, c: (0, s, 0))
    return pl.pallas_call(
        functools.partial(_pool_kernel, tc=tc, pos0=pos0),
        grid=(nseq, nchunks),
        in_specs=[_row_spec(tc, COL_POOL, nchunks), buf_spec,
                  _full_spec((GROUP_WIDTH, GROUP_WIDTH)), _full_spec((1, GROUP_WIDTH))],
        out_specs=[_row_spec(tc, 0, nchunks), buf_spec],
        out_shape=[jax.ShapeDtypeStruct((n, GROUP_WIDTH), BF16), jax.ShapeDtypeStruct(buf.shape, F32)],
        scratch_shapes=[pltpu.VMEM((tc + POOL_BUF, SEQ_BLK, GROUP_WIDTH), F32)],
        compiler_params=_params(("parallel", "arbitrary")),
        name="pool_mixer",
    )(proj, buf, w_bd, scale)


def _hgrn_lower_bound(logits_ref, layer):
    rows = [logits_ref[l:l + 1, :] for l in range(DEPTH)]
    m = functools.reduce(jnp.maximum, rows)
    es = [jnp.exp(r - m) for r in rows]
    tot = functools.reduce(lambda a, b: a + b, es)
    lb = jnp.zeros_like(m)
    for l in range(1, layer + 1):
        lb = lb + es[l] / tot
    return lb


def _hgrn_kernel(pq_ref, pf_ref, pi_ref, pg_ref, s0_ref, lbl_ref, ng_ref, y_ref, st_ref,
                 s_scr, q_scr, f_scr, k_scr, v_scr, o_scr, *, tc, layer):
    c = pl.program_id(1)
    shape3 = (tc, SEQ_BLK, GROUP_WIDTH)

    @pl.when(c == 0)
    def _():
        s_scr[...] = s0_ref[...]

    lb = _hgrn_lower_bound(lbl_ref, layer)
    zf = pf_ref[...]
    f_scr[...] = (lb + (1.0 - lb) * jax.nn.sigmoid(zf)).reshape(shape3)
    k_scr[...] = ((1.0 - lb) * jax.nn.sigmoid(-zf)).reshape(shape3)
    q_scr[...] = jax.nn.silu(pq_ref[...]).reshape(shape3)
    v_scr[...] = pi_ref[...].reshape(shape3)

    ones2, eye = _unit_masks()
    s = s_scr[...].reshape(UNITS * HEAD, LANES)
    for t in range(tc):
        vcol = _seg_sum_mxu(jnp.where(eye, _unit_rows(v_scr, t), 0.0), ones2)
        s = s * _unit_rows(f_scr, t) + vcol * _unit_rows(k_scr, t)
        _store_unit_rows(o_scr, t, _seg_sum_mxu(s * _unit_rows(q_scr, t), ones2), eye)
    s_scr[...] = s.reshape(s_scr.shape)

    o = o_scr[...].reshape(tc * SEQ_BLK, GROUP_WIDTH)
    ms = _head_sums(o * o) * (1.0 / HEAD)
    out = o * lax.rsqrt(ms + HGRN_NORM_EPS) * ng_ref[...] * jax.nn.silu(pg_ref[...])
    y_ref[...] = out.astype(y_ref.dtype)

    @pl.when(c == pl.num_programs(1) - 1)
    def _():
        st_ref[...] = s_scr[...]


def hgrn_mixer(proj, s0, lb_logits, norm_g, *, t_len, tc, layer):
    nseq = s0.shape[0] // SEQ_BLK
    nchunks = t_len // tc
    n = proj.shape[0]
    tile = pltpu.VMEM((tc, SEQ_BLK, GROUP_WIDTH), F32)
    return pl.pallas_call(
        functools.partial(_hgrn_kernel, tc=tc, layer=layer),
        grid=(nseq, nchunks),
        in_specs=[_row_spec(tc, COL_Q, nchunks), _row_spec(tc, COL_F, nchunks),
                  _row_spec(tc, COL_I, nchunks), _row_spec(tc, COL_G, nchunks),
                  _seq_spec(s0.shape), _full_spec((DEPTH, GROUP_WIDTH)), _full_spec((1, GROUP_WIDTH))],
        out_specs=[_row_spec(tc, 0, nchunks), _seq_spec(s0.shape)],
        out_shape=[jax.ShapeDtypeStruct((n, GROUP_WIDTH), BF16), jax.ShapeDtypeStruct(s0.shape, F32)],
        scratch_shapes=[pltpu.VMEM((SEQ_BLK, 2, HEAD, LANES), F32), tile, tile, tile, tile, tile],
        compiler_params=_params(("parallel", "arbitrary")),
        name="hgrn_mixer",
    )(proj, proj, proj, proj, s0, lb_logits, norm_g)


def _rwkv_kernel(pr_ref, pk_ref, pv_ref, pl_ref, sh0_ref, s0_ref, mu_ref, w0_ref, a0_ref, kk_ref, ka_ref,
                 rk_ref, lng_ref, lnb_ref, w2_ref, a2_ref, g2_ref, y_ref, st_ref, sh_ref,
                 s_scr, prev_scr, r_scr, w_scr, k_scr, v_scr, nkk_scr, kka_scr, o_scr, *, tc):
    c = pl.program_id(1)
    shape3 = (tc, SEQ_BLK, GROUP_WIDTH)
    gw = GROUP_WIDTH

    @pl.when(c == 0)
    def _():
        s_scr[...] = s0_ref[...]
        prev_scr[...] = sh0_ref[...]

    def shifted(ref, j):
        x = ref[...].reshape(shape3)
        first = prev_scr[:, j * gw:(j + 1) * gw].reshape(1, SEQ_BLK, gw)
        prev = first if tc == 1 else jnp.concatenate([first, x[:-1]], axis=0)
        prev_scr[:, j * gw:(j + 1) * gw] = x[tc - 1]
        return (x + (prev - x) * mu_ref[:, j * gw:(j + 1) * gw]).reshape(tc * SEQ_BLK, gw)

    xr, xk, xv, xl = shifted(pr_ref, 0), shifted(pk_ref, 1), shifted(pv_ref, 2), shifted(pl_ref, 3)
    w = -jax.nn.softplus(-(w0_ref[...] + _dot(jnp.tanh(xl).astype(BF16), w2_ref[...]))) - 0.5
    decay = jnp.exp(-jnp.exp(w))
    a = jax.nn.sigmoid(a0_ref[...] + _dot(xl.astype(BF16), a2_ref[...]))
    g = _dot(jax.nn.sigmoid(xl).astype(BF16), g2_ref[...])
    kk = xk * kk_ref[...]
    kk = kk / jnp.maximum(jnp.sqrt(_head_sums(kk * kk)), 1e-12)
    k = xk * (1.0 + (a - 1.0) * ka_ref[...])

    r_scr[...] = xr.reshape(shape3)
    w_scr[...] = decay.reshape(shape3)
    k_scr[...] = k.reshape(shape3)
    v_scr[...] = xv.reshape(shape3)
    nkk_scr[...] = (-kk).reshape(shape3)
    kka_scr[...] = (kk * a).reshape(shape3)

    ones2, eye = _unit_masks()
    s = s_scr[...].reshape(UNITS * HEAD, LANES)
    for t in range(tc):
        sa = _seg_sum_mxu(s * _unit_rows(nkk_scr, t), ones2)
        vcol = _seg_sum_mxu(jnp.where(eye, _unit_rows(v_scr, t), 0.0), ones2)
        s = s * _unit_rows(w_scr, t) + sa * _unit_rows(kka_scr, t) + vcol * _unit_rows(k_scr, t)
        _store_unit_rows(o_scr, t, _seg_sum_mxu(s * _unit_rows(r_scr, t), ones2), eye)
    s_scr[...] = s.reshape(s_scr.shape)

    y = o_scr[...].reshape(tc * SEQ_BLK, gw)
    mean = _head_sums(y) * (1.0 / HEAD)
    d = y - mean
    var = _head_sums(d * d) * (1.0 / HEAD)
    yn = d * lax.rsqrt(var + RWKV_GN_EPS) * lng_ref[...] + lnb_ref[...]
    bonus = _head_sums(xr * k * rk_ref[...]) * xv
    y_ref[...] = ((yn + bonus) * g).astype(y_ref.dtype)

    @pl.when(c == pl.num_programs(1) - 1)
    def _():
        st_ref[...] = s_scr[...]
        sh_ref[...] = prev_scr[...]


def rwkv_mixer(proj, shift0, s0, mu, w0, a0, k_k, k_a, r_k, ln_g, ln_b, w2p, a2p, g2p, *, t_len, tc):
    nseq = s0.shape[0] // SEQ_BLK
    nchunks = t_len // tc
    n = proj.shape[0]
    tile = pltpu.VMEM((tc, SEQ_BLK, GROUP_WIDTH), F32)
    vec = _full_spec((1, GROUP_WIDTH))
    mat = _full_spec((GROUP_WIDTH, GROUP_WIDTH))
    return pl.pallas_call(
        functools.partial(_rwkv_kernel, tc=tc),
        grid=(nseq, nchunks),
        in_specs=[_row_spec(tc, COL_R, nchunks), _row_spec(tc, COL_K, nchunks),
                  _row_spec(tc, COL_V, nchunks), _row_spec(tc, COL_LORA, nchunks),
                  _seq_spec(shift0.shape), _seq_spec(s0.shape), _full_spec((1, RWKV_PROJ)),
                  vec, vec, vec, vec, vec, vec, vec, mat, mat, mat],
        out_specs=[_row_spec(tc, 0, nchunks), _seq_spec(s0.shape), _seq_spec(shift0.shape)],
        out_shape=[jax.ShapeDtypeStruct((n, GROUP_WIDTH), BF16), jax.ShapeDtypeStruct(s0.shape, F32),
                   jax.ShapeDtypeStruct(shift0.shape, F32)],
        scratch_shapes=[pltpu.VMEM((SEQ_BLK, 2, HEAD, LANES), F32), pltpu.VMEM((SEQ_BLK, RWKV_PROJ), F32),
                        tile, tile, tile, tile, tile, tile, tile],
        compiler_params=_params(("parallel", "arbitrary")),
        name="rwkv_mixer",
    )(proj, proj, proj, proj, shift0, s0, mu, w0, a0, k_k, k_a, r_k, ln_g, ln_b, w2p, a2p, g2p)


def _dot_nt(a, b):
    return lax.dot_general(a, b, (((1,), (1,)), ((), ())), preferred_element_type=F32)


def _dot_tn(a, b):
    return lax.dot_general(a, b, (((0,), (0,)), ((), ())), preferred_element_type=F32)


def _iota(shape, dim):
    return lax.broadcasted_iota(jnp.int32, shape, dim)


def _head_of(idx):
    return lax.shift_right_logical(idx, HEAD.bit_length() - 1)


def _cumsum_rows(x):
    n = x.shape[0]
    tri = jnp.where(_iota((n, n), 0) >= _iota((n, n), 1), 1.0, 0.0).astype(BF16)
    hi = x.astype(BF16)
    rest = x - hi.astype(F32)
    mid = rest.astype(BF16)
    lo = (rest - mid.astype(F32)).astype(BF16)
    return _dot(tri, hi) + _dot(tri, mid) + _dot(tri, lo)


def _own_head(shape, rows_per_head):
    row_h = lax.shift_right_logical(_iota(shape, 0), rows_per_head.bit_length() - 1)
    return row_h == _head_of(_iota(shape, 1))


def _head_expand(x):
    xx = jnp.concatenate([x] * (GROUP_WIDTH // HEAD), axis=0)
    return jnp.where(_own_head(xx.shape, x.shape[0]), xx, 0.0)


def _head_collapse(xx):
    n = xx.shape[0] // (GROUP_WIDTH // HEAD)
    return xx[0:n] + xx[n:2 * n] + xx[2 * n:3 * n] + xx[3 * n:4 * n]


def _block_diag_mask():
    shape = (GROUP_WIDTH, GROUP_WIDTH)
    return _head_of(_iota(shape, 0)) == _head_of(_iota(shape, 1))


def _seq_col_spec(rows, col, ncols):
    return pl.BlockSpec((rows, GROUP_WIDTH), lambda b, c: (c, b * ncols + col))


HGRN_CHUNK = 128
HGRN_SUB = 16


HGRN_SEQS_PER_STEP = 4


def _run_staged(stages):
    for _ in itertools.zip_longest(*stages):
        pass


def _hgrn_chunk_kernel(*refs, nseq, layer):
    lbl_ref, ng_ref, y_ref, st_ref, w_scr, k_scr, b_scr, v_scr, p_scr, o_scr = refs[4 * nseq:]
    gw = GROUP_WIDTH
    c = pl.program_id(1)

    @pl.when(c == 0)
    def _():
        w_scr[...] = jnp.zeros_like(w_scr)
        p_scr[...] = jnp.zeros_like(p_scr)

    _run_staged([_hgrn_chunk_one(*refs[4 * s:4 * s + 4], lbl_ref, ng_ref, y_ref.at[:, s * gw:(s + 1) * gw],
                                 w_scr.at[s], k_scr.at[s], b_scr.at[s], v_scr.at[s], p_scr.at[s], o_scr.at[s],
                                 layer=layer) for s in range(nseq)])

    @pl.when(c == pl.num_programs(1) - 1)
    def _():
        for s in range(nseq):
            st_ref[s * gw:(s + 1) * gw, :] = w_scr[s]


def _hgrn_chunk_one(pq_ref, pf_ref, pi_ref, pg_ref, lbl_ref, ng_ref, y_ref, w_scr, k_scr, b_scr, v_scr, p_scr,
                    o_scr, *, layer):
    L, n = HGRN_CHUNK, HGRN_SUB
    half = n // 2
    lb = _hgrn_lower_bound(lbl_ref, layer)
    z = pf_ref[...]
    g = jnp.logaddexp(jnp.log1p(-lb) + jax.nn.log_sigmoid(z), jnp.log(lb))
    kg = (1.0 - lb) * jax.nn.sigmoid(-z)
    q = jax.nn.silu(pq_ref[...])
    v = pi_ref[...]
    bc = _cumsum_rows(g)
    k_scr[...] = kg
    b_scr[...] = bc
    v_scr[...] = v
    yield

    bd = _block_diag_mask()
    ones_bd = jnp.where(bd, 1.0, 0.0).astype(BF16)
    rid = _iota((n, GROUP_WIDTH), 0)
    rid_lo = _iota((half, GROUP_WIDTH), 0) + half
    for sb in range(L // n):
        base = sb * n
        qs, bs = q[base:base + n], bc[base:base + n]
        q_lo, b_lo = q[base + half:base + n], bc[base + half:base + n]
        for s in range(n):
            ks, bsrow = k_scr[base + s:base + s + 1, :], b_scr[base + s:base + s + 1, :]
            if s < half:
                p_scr[s * n:(s + 1) * n, :] = qs * ks * jnp.where(rid >= s, jnp.exp(bs - bsrow), 0.0)
            else:
                p_scr[s * n + half:(s + 1) * n, :] = q_lo * ks * jnp.where(rid_lo >= s, jnp.exp(b_lo - bsrow), 0.0)
        r = _dot(p_scr[...].astype(BF16), ones_bd)
        acc = jnp.zeros((n, GROUP_WIDTH), F32)
        for s in range(n):
            acc = acc + r[s * n:(s + 1) * n] * v_scr[base + s:base + s + 1, :]
        o_scr[base:base + n, :] = acc
        yield

    vb = v.astype(BF16)
    for i in range(1, L // n):
        r0 = i * n
        ref = b_scr[r0 - 1:r0, :]
        qt = q[r0:r0 + n] * jnp.exp(bc[r0:r0 + n] - ref)
        kt = jnp.concatenate([kg[:r0] * jnp.exp(ref - bc[:r0]), jnp.zeros((L - r0, GROUP_WIDTH), F32)], axis=0)
        att = _dot_nt(_head_expand(qt).astype(BF16), kt.astype(BF16))
        yield
        ox = _dot(att.astype(BF16), vb)
        o_scr[r0:r0 + n, :] += _head_collapse(jnp.where(_own_head(ox.shape, n), ox, 0.0))
        yield

    w = w_scr[...]
    o = o_scr[...] + _dot_nt((q * jnp.exp(bc)).astype(BF16), w.astype(BF16))
    b_end = b_scr[L - 1:L, :]
    upd = _dot_tn(vb, (kg * jnp.exp(b_end - bc)).astype(BF16))
    w_scr[...] = w * jnp.exp(b_end) + jnp.where(bd, upd, 0.0)
    yield

    ms = _head_sums(o * o) * (1.0 / HEAD)
    out = o * lax.rsqrt(ms + HGRN_NORM_EPS) * ng_ref[...] * jax.nn.silu(pg_ref[...])
    y_ref[...] = out.astype(y_ref.dtype)


def _seq_group_col_spec(rows, col, ncols, s, nseq):
    return pl.BlockSpec((rows, GROUP_WIDTH), lambda i, c: (c, (i * nseq + s) * ncols + col))


def hgrn_chunk_mixer(proj2, lb_logits, norm_g, *, bsz, t_len, layer):
    L, n, nseq = HGRN_CHUNK, HGRN_SUB, HGRN_SEQS_PER_STEP
    gw = GROUP_WIDTH
    ncols = PROJ_WIDTH // gw
    tile = pltpu.VMEM((nseq, L, gw), F32)
    seq_specs = [_seq_group_col_spec(L, col, ncols, s, nseq)
                 for s in range(nseq) for col in (COL_Q, COL_F, COL_I, COL_G)]
    y, st = pl.pallas_call(
        functools.partial(_hgrn_chunk_kernel, nseq=nseq, layer=layer),
        grid=(bsz // nseq, t_len // L),
        in_specs=seq_specs + [pl.BlockSpec((DEPTH, gw), lambda i, c: (0, 0)),
                              pl.BlockSpec((1, gw), lambda i, c: (0, 0))],
        out_specs=[pl.BlockSpec((L, nseq * gw), lambda i, c: (c, i)),
                   pl.BlockSpec((nseq * gw, gw), lambda i, c: (i, 0))],
        out_shape=[jax.ShapeDtypeStruct((t_len, bsz * gw), BF16),
                   jax.ShapeDtypeStruct((bsz * gw, gw), F32)],
        scratch_shapes=[pltpu.VMEM((nseq, gw, gw), F32), tile, tile, tile,
                        pltpu.VMEM((nseq, n * n, gw), F32), tile],
        compiler_params=_params(("parallel", "arbitrary")),
        name="hgrn_chunk_mixer",
    )(*([proj2] * (4 * nseq)), lb_logits, norm_g)
    return y, st


RWKV_CHUNK = 64


def _rwkv_chunk_kernel(*refs, nseq):
    params = refs[4 * nseq:4 * nseq + 11]
    y_ref, st_ref, sh_ref, w_scr, prev_scr = refs[4 * nseq + 11:]
    gw = GROUP_WIDTH
    c = pl.program_id(1)

    @pl.when(c == 0)
    def _():
        w_scr[...] = jnp.zeros_like(w_scr)
        prev_scr[...] = jnp.zeros_like(prev_scr)

    _run_staged([_rwkv_chunk_one(*refs[4 * s:4 * s + 4], *params, y_ref.at[:, s * gw:(s + 1) * gw], w_scr.at[s],
                                 prev_scr.at[s]) for s in range(nseq)])

    @pl.when(c == pl.num_programs(1) - 1)
    def _():
        for s in range(nseq):
            st_ref[s * gw:(s + 1) * gw, :] = w_scr[s]
        sh_ref[...] = prev_scr[...]


def _rwkv_chunk_one(pr_ref, pk_ref, pv_ref, pl_ref, mu_ref, w0_ref, a0_ref, kk_ref, ka_ref, rk_ref, lng_ref,
                    lnb_ref, w2_ref, a2_ref, g2_ref, y_ref, w_scr, prev_scr):
    L = RWKV_CHUNK
    gw = GROUP_WIDTH
    rid = _iota((L, gw), 0)

    def shifted(ref, j):
        x = ref[...]
        prev = jnp.where(rid == 0, prev_scr[:, j * gw:(j + 1) * gw], pltpu.roll(x, 1, axis=0))
        prev_scr[:, j * gw:(j + 1) * gw] = x[L - 1:L]
        return x + (prev - x) * mu_ref[:, j * gw:(j + 1) * gw]

    xr, xk, xv, xl = shifted(pr_ref, 0), shifted(pk_ref, 1), shifted(pv_ref, 2), shifted(pl_ref, 3)
    w = -jax.nn.softplus(-(w0_ref[...] + _dot(jnp.tanh(xl).astype(BF16), w2_ref[...]))) - 0.5
    lw = -jnp.exp(w)
    a = jax.nn.sigmoid(a0_ref[...] + _dot(xl.astype(BF16), a2_ref[...]))
    g = _dot(jax.nn.sigmoid(xl).astype(BF16), g2_ref[...])
    kk = xk * kk_ref[...]
    kk = kk / jnp.maximum(jnp.sqrt(_head_sums(kk * kk)), 1e-12)
    k = xk * (1.0 + (a - 1.0) * ka_ref[...])
    beta = kk * a
    yield

    cs = _cumsum_rows(lw)
    c_end = cs[L - 1:L]
    e_neg = jnp.exp(-cs)
    e_end = jnp.exp(c_end - cs)
    ar = jnp.concatenate([_head_expand(-kk * jnp.exp(cs - lw)), _head_expand(xr * jnp.exp(cs))], axis=0).astype(BF16)
    bk = jnp.concatenate([_head_expand(beta * e_neg), _head_expand(k * e_neg)], axis=0).astype(BF16)
    vx = _head_expand(xv).astype(BF16)
    yield

    nh = 4 * L
    gmat = _dot_nt(ar, bk)
    tt = _iota((nh, nh), 0) & (L - 1)
    ss = _iota((nh, nh), 1) & (L - 1)
    strict, incl = ss < tt, ss <= tt
    nab = jnp.where(strict, gmat[0:nh, 0:nh], 0.0)
    nak = jnp.where(strict, gmat[0:nh, nh:2 * nh], 0.0).astype(BF16)
    nrb = jnp.where(incl, gmat[nh:2 * nh, 0:nh], 0.0).astype(BF16)
    nrk = jnp.where(incl, gmat[nh:2 * nh, nh:2 * nh], 0.0).astype(BF16)
    yield

    ri, ci = _iota((nh, nh), 0), _iota((nh, nh), 1)

    def same_block(size):
        sh = size.bit_length() - 1
        return lax.shift_right_logical(ri, sh) == lax.shift_right_logical(ci, sh)

    base = 8
    m = jnp.where(same_block(base), nab, 0.0)
    t_inv = jnp.where(ri == ci, 1.0, 0.0) + m
    m = m.astype(BF16)
    for _ in range(base.bit_length() - 2):
        m = _dot(m, m).astype(BF16)
        yield
        t_inv = t_inv + _dot(t_inv.astype(BF16), m)
        yield
    size = base
    while size < L:
        off = jnp.where(same_block(2 * size), jnp.where(same_block(size), 0.0, nab), 0.0).astype(BF16)
        tb = t_inv.astype(BF16)
        half = _dot(tb, off).astype(BF16)
        yield
        t_inv = t_inv + _dot(half, tb)
        yield
        size *= 2

    wst = w_scr[...]
    sw = _dot_nt(ar, wst.astype(BF16))
    rhs = (sw[0:nh] + _dot(nak, vx)).astype(BF16)
    yield
    x = _dot(t_inv.astype(BF16), rhs)
    yield
    ux = x.astype(BF16)
    yx = sw[nh:2 * nh] + _dot(nrb, ux) + _dot(nrk, vx)
    y = _head_collapse(yx)
    u = _head_collapse(x)
    yield

    upd = _dot_tn(jnp.concatenate([u, xv], axis=0).astype(BF16),
                  jnp.concatenate([beta * e_end, k * e_end], axis=0).astype(BF16))
    w_scr[...] = wst * jnp.exp(c_end) + jnp.where(_block_diag_mask(), upd, 0.0)
    yield

    mean = _head_sums(y) * (1.0 / HEAD)
    d = y - mean
    var = _head_sums(d * d) * (1.0 / HEAD)
    yn = d * lax.rsqrt(var + RWKV_GN_EPS) * lng_ref[...] + lnb_ref[...]
    bonus = _head_sums(xr * k * rk_ref[...]) * xv
    y_ref[...] = ((yn + bonus) * g).astype(y_ref.dtype)


RWKV_SEQS_PER_STEP = 8


def rwkv_chunk_mixer(proj2, mu, w0, a0, k_k, k_a, r_k, ln_g, ln_b, w2p, a2p, g2p, *, bsz, t_len):
    L, nseq = RWKV_CHUNK, RWKV_SEQS_PER_STEP
    gw = GROUP_WIDTH
    ncols = PROJ_WIDTH // gw
    vec = pl.BlockSpec((1, gw), lambda i, c: (0, 0))
    mat = pl.BlockSpec((gw, gw), lambda i, c: (0, 0))
    seq_specs = [_seq_group_col_spec(L, col, ncols, s, nseq)
                 for s in range(nseq) for col in (COL_R, COL_K, COL_V, COL_LORA)]
    y, st, sh = pl.pallas_call(
        functools.partial(_rwkv_chunk_kernel, nseq=nseq),
        grid=(bsz // nseq, t_len // L),
        in_specs=seq_specs + [pl.BlockSpec((1, RWKV_PROJ), lambda i, c: (0, 0)),
                              vec, vec, vec, vec, vec, vec, vec, mat, mat, mat],
        out_specs=[pl.BlockSpec((L, nseq * gw), lambda i, c: (c, i)),
                   pl.BlockSpec((nseq * gw, gw), lambda i, c: (i, 0)),
                   pl.BlockSpec((nseq, 1, RWKV_PROJ), lambda i, c: (i, 0, 0))],
        out_shape=[jax.ShapeDtypeStruct((t_len, bsz * gw), BF16),
                   jax.ShapeDtypeStruct((bsz * gw, gw), F32),
                   jax.ShapeDtypeStruct((bsz, 1, RWKV_PROJ), F32)],
        scratch_shapes=[pltpu.VMEM((nseq, gw, gw), F32), pltpu.VMEM((nseq, 1, RWKV_PROJ), F32)],
        compiler_params=_params(("parallel", "arbitrary")),
        name="rwkv_chunk_mixer",
    )(*([proj2] * (4 * nseq)), mu, w0, a0, k_k, k_a, r_k, ln_g, ln_b, w2p, a2p, g2p)
    return y, st, sh


def _hgrn_rwkv_chunk_kernel(*refs, layer, hgrn_steps):
    nr, nh = RWKV_SEQS_PER_STEP, HGRN_SEQS_PER_STEP
    it = iter(refs)
    take = lambda n: [next(it) for _ in range(n)]
    r_seq, r_par, h_seq = take(4 * nr), take(11), take(4 * nh)
    lbl_ref, ng_ref = take(2)
    yr_ref, str_ref, shr_ref, yh_ref, sth_ref = take(5)
    wr_scr, prev_scr, wh_scr, k_scr, b_scr, v_scr, p_scr, o_scr = take(8)
    gw = GROUP_WIDTH
    g = pl.program_id(0)
    ch = g % hgrn_steps

    @pl.when(g == 0)
    def _():
        wr_scr[...] = jnp.zeros_like(wr_scr)
        prev_scr[...] = jnp.zeros_like(prev_scr)

    @pl.when(ch == 0)
    def _():
        wh_scr[...] = jnp.zeros_like(wh_scr)
        p_scr[...] = jnp.zeros_like(p_scr)

    rwkv = [_rwkv_chunk_one(*r_seq[4 * s:4 * s + 4], *r_par, yr_ref.at[:, s * gw:(s + 1) * gw], wr_scr.at[s],
                            prev_scr.at[s]) for s in range(nr)]
    hgrn = [_hgrn_chunk_one(*h_seq[4 * s:4 * s + 4], lbl_ref, ng_ref, yh_ref.at[:, s * gw:(s + 1) * gw],
                            wh_scr.at[s], k_scr.at[s], b_scr.at[s], v_scr.at[s], p_scr.at[s], o_scr.at[s],
                            layer=layer) for s in range(nh)]
    order = []
    for s in range(max(nr, nh)):
        order += rwkv[s:s + 1] + hgrn[s:s + 1]
    _run_staged(order)

    @pl.when(g == pl.num_programs(0) - 1)
    def _():
        for s in range(nr):
            str_ref[s * gw:(s + 1) * gw, :] = wr_scr[s]
        shr_ref[...] = prev_scr[...]

    @pl.when(ch == hgrn_steps - 1)
    def _():
        for s in range(nh):
            sth_ref[s * gw:(s + 1) * gw, :] = wh_scr[s]


def hgrn_rwkv_chunk_mixer(proj2, lb_logits, norm_g, mu, w0, a0, k_k, k_a, r_k, ln_g, ln_b, w2p, a2p, g2p, *,
                          bsz, t_len, layer):
    nr, nh, lr, lh, n = RWKV_SEQS_PER_STEP, HGRN_SEQS_PER_STEP, RWKV_CHUNK, HGRN_CHUNK, HGRN_SUB
    gw = GROUP_WIDTH
    ncols = PROJ_WIDTH // gw
    steps = t_len // lr
    hgrn_steps = t_len // lh
    assert nr == bsz and (bsz // nh) * hgrn_steps == steps
    const = lambda shape: pl.BlockSpec(shape, lambda g: (0,) * len(shape))
    r_specs = [pl.BlockSpec((lr, gw), functools.partial(lambda g, j: (g, j), j=s * ncols + col))
               for s in range(nr) for col in (COL_R, COL_K, COL_V, COL_LORA)]
    h_specs = [pl.BlockSpec((lh, gw), functools.partial(
                   lambda g, s, col: (g % hgrn_steps, ((g // hgrn_steps) * nh + s) * ncols + col), s=s, col=col))
               for s in range(nh) for col in (COL_Q, COL_F, COL_I, COL_G)]
    vec, mat = const((1, gw)), const((gw, gw))
    tile = pltpu.VMEM((nh, lh, gw), F32)
    yr, st_r, sh_r, yh, st_h = pl.pallas_call(
        functools.partial(_hgrn_rwkv_chunk_kernel, layer=layer, hgrn_steps=hgrn_steps),
        grid=(steps,),
        in_specs=r_specs + [const((1, RWKV_PROJ)), vec, vec, vec, vec, vec, vec, vec, mat, mat, mat]
                 + h_specs + [const((DEPTH, gw)), vec],
        out_specs=[pl.BlockSpec((lr, nr * gw), lambda g: (g, 0)), const((nr * gw, gw)), const((nr, 1, RWKV_PROJ)),
                   pl.BlockSpec((lh, nh * gw), lambda g: (g % hgrn_steps, g // hgrn_steps)),
                   pl.BlockSpec((nh * gw, gw), lambda g: (g // hgrn_steps, 0))],
        out_shape=[jax.ShapeDtypeStruct((t_len, bsz * gw), BF16), jax.ShapeDtypeStruct((bsz * gw, gw), F32),
                   jax.ShapeDtypeStruct((bsz, 1, RWKV_PROJ), F32),
                   jax.ShapeDtypeStruct((t_len, bsz * gw), BF16), jax.ShapeDtypeStruct((bsz * gw, gw), F32)],
        scratch_shapes=[pltpu.VMEM((nr, gw, gw), F32), pltpu.VMEM((nr, 1, RWKV_PROJ), F32),
                        pltpu.VMEM((nh, gw, gw), F32), tile, tile, tile, pltpu.VMEM((nh, n * n, gw), F32), tile],
        compiler_params=_params(("arbitrary",)),
        name="hgrn_rwkv_chunk_mixer",
    )(*([proj2] * (4 * nr)), mu, w0, a0, k_k, k_a, r_k, ln_g, ln_b, w2p, a2p, g2p,
      *([proj2] * (4 * nh)), lb_logits, norm_g)
    return yh, st_h, yr, st_r, sh_r


S5_CHUNK = 64


def _s5_seq_kernel(*refs, nseq):
    u_refs = refs[:nseq]
    lr_ref, li_ref, ldt_ref, bre_ref, bim_ref, ccat_ref, d_ref, gw_ref, gb_ref = refs[nseq:nseq + 9]
    y_ref, hr_out, hi_out, h_scr, bu_scr, bb_scr, ab_scr, perm_scr = refs[nseq + 9:]
    tc = S5_CHUNK
    n = nseq * tc
    c = pl.program_id(0)

    @pl.when(c == 0)
    def _():
        h_scr[...] = jnp.zeros_like(h_scr)
        lr, li = lr_ref[...], li_ref[...]
        dt = jnp.exp(ldt_ref[...])
        mag = jnp.exp(lr * dt)
        ab_re, ab_im = mag * jnp.cos(li * dt), mag * jnp.sin(li * dt)
        den = lr * lr + li * li
        zr, zi = ab_re - 1.0, ab_im
        cr = (zr * lr + zi * li) / den
        ci = (zi * lr - zr * li) / den
        bre, bim = bre_ref[...], bim_ref[...]
        bb_scr[:, 0:SSM_FLAT] = (cr * bre - ci * bim).astype(BF16)
        bb_scr[:, SSM_FLAT:2 * SSM_FLAT] = (cr * bim + ci * bre).astype(BF16)
        ab_scr[0] = jnp.broadcast_to(ab_re, (nseq, SSM_FLAT))
        ab_scr[1] = jnp.broadcast_to(ab_im, (nseq, SSM_FLAT))
        ri, cj = _iota((n, n), 0), _iota((n, n), 1)
        lseq, lt = nseq.bit_length() - 1, tc.bit_length() - 1
        perm_scr[0] = jnp.where(cj == (ri & (nseq - 1)) * tc + lax.shift_right_logical(ri, lseq), 1.0, 0.0).astype(BF16)
        perm_scr[1] = jnp.where(cj == (ri & (tc - 1)) * nseq + lax.shift_right_logical(ri, lt), 1.0, 0.0).astype(BF16)

    u = jnp.concatenate([r[...] for r in u_refs], axis=0)
    to_seq = perm_scr[1]
    u_t = _dot(perm_scr[0], u.astype(BF16)).astype(BF16)
    bu_scr[...] = _dot(u_t, bb_scr[...])

    ar, ai = ab_scr[0], ab_scr[1]

    def step(t, carry):
        hr, hi = carry
        rows = pl.ds(pl.multiple_of(t * nseq, nseq), nseq)
        nhr = ar * hr - ai * hi + bu_scr[rows, 0:SSM_FLAT]
        nhi = ar * hi + ai * hr + bu_scr[rows, SSM_FLAT:2 * SSM_FLAT]
        bu_scr[rows, 0:SSM_FLAT] = nhr
        bu_scr[rows, SSM_FLAT:2 * SSM_FLAT] = nhi
        return nhr, nhi

    hr, hi = lax.fori_loop(0, tc, step, (h_scr[0], h_scr[1]))
    h_scr[0] = hr
    h_scr[1] = hi

    y_t = _dot(bu_scr[...].astype(BF16), ccat_ref[...])
    hi_p = y_t.astype(BF16)
    rest = y_t - hi_p.astype(F32)
    mid_p = rest.astype(BF16)
    lo_p = (rest - mid_p.astype(F32)).astype(BF16)
    y = _dot(to_seq, hi_p) + _dot(to_seq, mid_p) + _dot(to_seq, lo_p) + d_ref[...] * u
    z = jax.nn.gelu(y)
    out = z * jax.nn.sigmoid(_dot(z.astype(BF16), gw_ref[...]) + gb_ref[...])
    for b in range(nseq):
        y_ref[:, b * GROUP_WIDTH:(b + 1) * GROUP_WIDTH] = out[b * tc:(b + 1) * tc].astype(y_ref.dtype)

    @pl.when(c == pl.num_programs(0) - 1)
    def _():
        hr_out[...] = hr
        hi_out[...] = hi


def s5_seq_mixer(proj2, lam_re, lam_im, log_dt, b_re_bd, b_im_bd, c_cat, d_skip, glu_w, glu_b, *, bsz, t_len):
    tc = S5_CHUNK
    ncols = PROJ_WIDTH // GROUP_WIDTH
    full = lambda shape: pl.BlockSpec(shape, lambda c: (0,) * len(shape))
    st = jax.ShapeDtypeStruct((bsz, SSM_FLAT), F32)
    u_specs = [pl.BlockSpec((tc, GROUP_WIDTH), functools.partial(lambda c, b: (c, b * ncols + COL_SSM), b=b))
               for b in range(bsz)]
    return pl.pallas_call(
        functools.partial(_s5_seq_kernel, nseq=bsz),
        grid=(t_len // tc,),
        in_specs=u_specs + [full((1, SSM_FLAT)), full((1, SSM_FLAT)), full((1, SSM_FLAT)),
                            full((GROUP_WIDTH, SSM_FLAT)), full((GROUP_WIDTH, SSM_FLAT)),
                            full((2 * SSM_FLAT, GROUP_WIDTH)), full((1, GROUP_WIDTH)),
                            full((GROUP_WIDTH, GROUP_WIDTH)), full((1, GROUP_WIDTH))],
        out_specs=[pl.BlockSpec((tc, bsz * GROUP_WIDTH), lambda c: (c, 0)), full((bsz, SSM_FLAT)),
                   full((bsz, SSM_FLAT))],
        out_shape=[jax.ShapeDtypeStruct((t_len, bsz * GROUP_WIDTH), BF16), st, st],
        scratch_shapes=[pltpu.VMEM((2, bsz, SSM_FLAT), F32), pltpu.VMEM((bsz * tc, 2 * SSM_FLAT), F32),
                        pltpu.VMEM((GROUP_WIDTH, 2 * SSM_FLAT), BF16), pltpu.VMEM((2, bsz, SSM_FLAT), F32),
                        pltpu.VMEM((2, bsz * tc, bsz * tc), BF16)],
        compiler_params=_params(("arbitrary",)),
        name="s5_seq_mixer",
    )(*([proj2] * bsz), lam_re, lam_im, log_dt, b_re_bd, b_im_bd, c_cat, d_skip, glu_w, glu_b)


POOL_CHUNK = 1024
POOL_HIST = 16


def _pool_seq_kernel(u_ref, w_ref, sc_ref, y_ref, nbuf_ref, ext_scr):
    L, hist = u_ref.shape[0], POOL_HIST
    c = pl.program_id(1)

    @pl.when(c == 0)
    def _():
        ext_scr[0:hist] = jnp.zeros((hist, GROUP_WIDTH), F32)

    u = u_ref[...]
    ext_scr[hist:hist + L] = u
    e = ext_scr[...]
    a2 = e + pltpu.roll(e, 1, axis=0)
    a4 = a2 + pltpu.roll(a2, 2, axis=0)
    a8 = a4 + pltpu.roll(a4, 4, axis=0)
    a16 = a8 + pltpu.roll(a8, 8, axis=0)
    sums = (a2[hist:], a4[hist:], a8[hist:], a16[hist:])

    shape = (L, GROUP_WIDTH)
    pos = _iota(shape, 0) + c * L
    lane = _iota(shape, 1)
    pooled = None
    for gi in reversed(range(len(POOL_WINDOWS))):
        win = POOL_WINDOWS[gi]
        mean = sums[gi] / jnp.minimum(pos + 1, win).astype(F32)
        pooled = mean if pooled is None else jnp.where(lane < (gi + 1) * HEAD, mean, pooled)
    y_ref[...] = (_dot((pooled - u).astype(BF16), w_ref[...]) * sc_ref[...]).astype(y_ref.dtype)

    nb = ext_scr[L:L + hist]
    ext_scr[0:hist] = nb

    @pl.when(c == pl.num_programs(1) - 1)
    def _():
        nbuf_ref[...] = nb


def pool_seq_mixer(proj2, w_bd, scale, *, bsz, t_len):
    L, hist = min(POOL_CHUNK, t_len), POOL_HIST
    ncols = PROJ_WIDTH // GROUP_WIDTH
    return pl.pallas_call(
        _pool_seq_kernel,
        grid=(bsz, t_len // L),
        in_specs=[_seq_col_spec(L, COL_POOL, ncols),
                  pl.BlockSpec((GROUP_WIDTH, GROUP_WIDTH), lambda b, c: (0, 0)),
                  pl.BlockSpec((1, GROUP_WIDTH), lambda b, c: (0, 0))],
        out_specs=[_seq_col_spec(L, 0, 1), pl.BlockSpec((hist, GROUP_WIDTH), lambda b, c: (b, 0))],
        out_shape=[jax.ShapeDtypeStruct((t_len, bsz * GROUP_WIDTH), BF16),
                   jax.ShapeDtypeStruct((bsz * hist, GROUP_WIDTH), F32)],
        scratch_shapes=[pltpu.VMEM((hist + L, GROUP_WIDTH), F32)],
        compiler_params=_params(("parallel", "arbitrary")),
        name="pool_seq_mixer",
    )(proj2, w_bd, scale)


def _diag_heads(st, bsz):
    nh = GROUP_WIDTH // HEAD
    s = st.reshape(bsz, nh, HEAD, nh, HEAD)
    return jnp.stack([s[:, h, :, h, :] for h in range(nh)], axis=1)


def _mix_mlp_kernel(h_ref, ya_ref, yb_ref, yc_ref, yd_ref, wo_ref, g2_ref, wu_ref, wd_ref, gf_ref, o_ref,
                    h1_scr, xn_scr, acc_scr, *, final_norm):
    j = pl.program_id(1)
    gw = GROUP_WIDTH

    @pl.when(j == 0)
    def _():
        mix = (_dot(ya_ref[...], wo_ref[0:gw]) + _dot(yb_ref[...], wo_ref[gw:2 * gw])
               + _dot(yc_ref[...], wo_ref[2 * gw:3 * gw]) + _dot(yd_ref[...], wo_ref[3 * gw:4 * gw]))
        h1 = h_ref[...] + mix
        h1_scr[...] = h1
        xn_scr[...] = _rms(h1, g2_ref[...]).astype(BF16)
        acc_scr[...] = jnp.zeros_like(acc_scr)

    up = _dot(xn_scr[...], wu_ref[...])
    act = jnp.square(jnp.maximum(up, 0.0)).astype(BF16)
    acc_scr[...] += _dot(act, wd_ref[...])

    @pl.when(j == pl.num_programs(1) - 1)
    def _():
        out = h1_scr[...] + acc_scr[...]
        if final_norm:
            out = _rms(out, gf_ref[...])
        o_ref[...] = out


def mix_mlp(h, ys, w_out, g2, w_up, w_down, g_final, *, final_norm, nseq=0):
    n = h.shape[0]
    tm = min(512, n // max(nseq, 1))
    tf = 1024
    nt = n // tm // max(nseq, 1)
    row = lambda w: pl.BlockSpec((tm, w), lambda i, j: (i, 0))
    mix = pl.BlockSpec((tm, GROUP_WIDTH), _tile_map(nseq, nt))
    return pl.pallas_call(
        functools.partial(_mix_mlp_kernel, final_norm=final_norm),
        grid=(n // tm, D_FF // tf),
        in_specs=[row(D_MODEL), mix, mix, mix, mix,
                  pl.BlockSpec((D_MODEL, D_MODEL), lambda i, j: (0, 0)),
                  pl.BlockSpec((1, D_MODEL), lambda i, j: (0, 0)),
                  pl.BlockSpec((D_MODEL, tf), lambda i, j: (0, j)),
                  pl.BlockSpec((tf, D_MODEL), lambda i, j: (j, 0)),
                  pl.BlockSpec((1, D_MODEL), lambda i, j: (0, 0))],
        out_specs=row(D_MODEL),
        out_shape=jax.ShapeDtypeStruct((n, D_MODEL), F32),
        scratch_shapes=[pltpu.VMEM((tm, D_MODEL), F32), pltpu.VMEM((tm, D_MODEL), BF16),
                        pltpu.VMEM((tm, D_MODEL), F32)],
        compiler_params=_params(("parallel", "arbitrary")),
        name="mix_mlp",
    )(h, *ys, w_out, g2, w_up, w_down, g_final)


def _hgrn_state_in(s):
    b = s.shape[0]
    return s.reshape(b, 2, 2, HEAD, HEAD).transpose(0, 1, 4, 2, 3).reshape(b, 2, HEAD, LANES)


def _hgrn_state_out(s):
    b = s.shape[0]
    return s.reshape(b, 2, HEAD, 2, HEAD).transpose(0, 1, 3, 4, 2).reshape(b, 4, HEAD, HEAD)


def _wkv_state_in(s):
    b = s.shape[0]
    return s.reshape(b, 2, 2, HEAD, HEAD).transpose(0, 1, 3, 2, 4).reshape(b, 2, HEAD, LANES)


def _wkv_state_out(s):
    b = s.shape[0]
    return s.reshape(b, 2, HEAD, 2, HEAD).transpose(0, 1, 3, 2, 4).reshape(b, 4, HEAD, HEAD)


def _block_diag(blocks):
    g, r, c = blocks.shape
    tiled = jnp.tile(blocks.reshape(g * r, c), (1, g))
    row_blk = lax.broadcasted_iota(jnp.int32, tiled.shape, 0) // r
    col_blk = lax.broadcasted_iota(jnp.int32, tiled.shape, 1) // c
    return jnp.where(row_blk == col_blk, tiled, 0.0)


def _pad_rows(w, start):
    rows = GROUP_WIDTH - start - w.shape[0]
    return jnp.concatenate([jnp.zeros((start, GROUP_WIDTH), w.dtype), w, jnp.zeros((rows, GROUP_WIDTH), w.dtype)])


def _layer_params(l, P):
    row = lambda a: a.reshape(1, -1)
    q = {}
    q["norm1_g"] = row(P["norm1_g"][l])
    q["w_in"] = P["w_in"][l].astype(BF16)
    q["lam_re"] = row(P["ssm_lambda_re"][l])
    q["lam_im"] = row(P["ssm_lambda_im"][l])
    q["log_dt"] = row(jnp.repeat(P["ssm_log_dt"][l], SSM_STATE))
    q["b_re"] = _block_diag(P["ssm_b_re"][l].transpose(0, 2, 1))
    q["b_im"] = _block_diag(P["ssm_b_im"][l].transpose(0, 2, 1))
    q["c_cat"] = jnp.concatenate([_block_diag(P["ssm_c_re"][l].transpose(0, 2, 1)),
                                  -_block_diag(P["ssm_c_im"][l].transpose(0, 2, 1))], axis=0).astype(BF16)
    q["ssm_d"] = row(P["ssm_d"][l])
    q["glu_w"] = P["ssm_glu_w"][l].astype(BF16)
    q["glu_b"] = row(P["ssm_glu_b"][l])
    q["hgrn_norm_g"] = row(P["hgrn_norm_g"][l])
    q["mu"] = row(P["rwkv_mu"][l])
    for name in ("w0", "a0", "k_k", "k_a", "r_k", "ln_g", "ln_b"):
        q[name] = row(P["rwkv_" + name][l])
    q["w2p"] = _pad_rows(P["rwkv_w2"][l], 0).astype(BF16)
    q["a2p"] = _pad_rows(P["rwkv_a2"][l], DECAY_LORA).astype(BF16)
    q["g2p"] = _pad_rows(P["rwkv_g2"][l], DECAY_LORA + AAA_LORA).astype(BF16)
    q["pool_w"] = _block_diag(P["pool_w"][l]).astype(BF16)
    q["pool_scale"] = row(P["pool_scale"][l])
    q["w_out"] = P["w_out"][l].astype(BF16)
    q["norm2_g"] = row(P["norm2_g"][l])
    q["mlp_up"] = P["mlp_up"][l].astype(BF16)
    q["mlp_down"] = P["mlp_down"][l].astype(BF16)
    return q


def _trunk_fresh(x_rows, bsz, t_len, layer_params, P):
    h = x_rows
    new = [[] for _ in range(6)]
    g_final = P["norm_f_g"].reshape(1, -1)
    for l in range(DEPTH):
        q = layer_params[l]
        proj2 = rms_proj(h, q["norm1_g"], q["w_in"], nseq=bsz)
        y_a, s_re, s_im = s5_seq_mixer(proj2, q["lam_re"], q["lam_im"], q["log_dt"], q["b_re"], q["b_im"],
                                       q["c_cat"], q["ssm_d"], q["glu_w"], q["glu_b"], bsz=bsz, t_len=t_len)
        y_b, s_hg, y_c, s_wkv, s_sh = hgrn_rwkv_chunk_mixer(
            proj2, P["hgrn_lb_logits"], q["hgrn_norm_g"], q["mu"], q["w0"], q["a0"], q["k_k"], q["k_a"], q["r_k"],
            q["ln_g"], q["ln_b"], q["w2p"], q["a2p"], q["g2p"], bsz=bsz, t_len=t_len, layer=l)
        y_d, s_pool = pool_seq_mixer(proj2, q["pool_w"], q["pool_scale"], bsz=bsz, t_len=t_len)
        h = mix_mlp(h, (y_a, y_b, y_c, y_d), q["w_out"], q["norm2_g"], q["mlp_up"], q["mlp_down"], g_final,
                    final_norm=(l == DEPTH - 1), nseq=bsz)
        s_hg = _diag_heads(s_hg, bsz).swapaxes(-1, -2)
        s_wkv = _diag_heads(s_wkv, bsz)
        s_pool = s_pool.reshape(bsz, POOL_HIST, GROUP_WIDTH)[:, POOL_HIST - POOL_BUF:]
        for lst, s in zip(new, (s_re, s_im, s_hg, s_wkv, s_sh, s_pool)):
            lst.append(s)
    return h, new


def _trunk_carry(x_rows, states, pos0, t_len, layer_params, P):
    ssm_re0, ssm_im0, hgrn0, wkv0, shift0, pool0 = states
    h = x_rows
    new = [[] for _ in range(6)]
    g_final = P["norm_f_g"].reshape(1, -1)
    tc = t_len
    for l in range(DEPTH):
        q = layer_params[l]
        proj = rms_proj(h, q["norm1_g"], q["w_in"])
        y_a, s_re, s_im = s5_mixer(proj, ssm_re0[l], ssm_im0[l], q["lam_re"], q["lam_im"], q["log_dt"], q["b_re"],
                                   q["b_im"], q["c_cat"], q["ssm_d"], q["glu_w"], q["glu_b"], t_len=t_len, tc=tc)
        y_b, s_hg = hgrn_mixer(proj, hgrn0[l], P["hgrn_lb_logits"], q["hgrn_norm_g"], t_len=t_len, tc=tc, layer=l)
        y_c, s_wkv, s_sh = rwkv_mixer(proj, shift0[l], wkv0[l], q["mu"], q["w0"], q["a0"], q["k_k"], q["k_a"],
                                      q["r_k"], q["ln_g"], q["ln_b"], q["w2p"], q["a2p"], q["g2p"],
                                      t_len=t_len, tc=tc)
        y_d, s_pool = pool_mixer(proj, pool0[l], q["pool_w"], q["pool_scale"], t_len=t_len, tc=tc, pos0=pos0)
        h = mix_mlp(h, (y_a, y_b, y_c, y_d), q["w_out"], q["norm2_g"], q["mlp_up"], q["mlp_down"], g_final,
                    final_norm=(l == DEPTH - 1))
        for lst, s in zip(new, (s_re, s_im, _hgrn_state_out(s_hg), _wkv_state_out(s_wkv), s_sh,
                                s_pool.transpose(1, 0, 2))):
            lst.append(s)
    return h, new


def _states_out(new, bsz):
    s_re, s_im, s_hg, s_wkv, s_sh, s_pool = new
    return (jnp.stack([s.reshape(bsz, SSM_GROUPS, SSM_STATE) for s in s_re]),
            jnp.stack([s.reshape(bsz, SSM_GROUPS, SSM_STATE) for s in s_im]),
            jnp.stack(s_hg),
            jnp.stack(s_wkv),
            jnp.stack([s.reshape(bsz, 1, RWKV_PROJ) for s in s_sh]),
            jnp.stack(s_pool))


def kernel(x_prompt, x_sample, state_ssm_re, state_ssm_im, state_hgrn, state_wkv, state_shift, state_pool, norm1_g, w_in, ssm_lambda_re, ssm_lambda_im, ssm_log_dt, ssm_b_re, ssm_b_im, ssm_c_re, ssm_c_im, ssm_d, ssm_glu_w, ssm_glu_b, hgrn_lb_logits, hgrn_norm_g, rwkv_mu, rwkv_w0, rwkv_w2, rwkv_a0, rwkv_a2, rwkv_g2, rwkv_k_k, rwkv_k_a, rwkv_r_k, rwkv_ln_g, rwkv_ln_b, pool_w, pool_scale, w_out, norm2_g, mlp_up, mlp_down, norm_f_g):
    P = dict(norm1_g=norm1_g, w_in=w_in, ssm_lambda_re=ssm_lambda_re, ssm_lambda_im=ssm_lambda_im,
             ssm_log_dt=ssm_log_dt, ssm_b_re=ssm_b_re, ssm_b_im=ssm_b_im, ssm_c_re=ssm_c_re, ssm_c_im=ssm_c_im,
             ssm_d=ssm_d, ssm_glu_w=ssm_glu_w, ssm_glu_b=ssm_glu_b, hgrn_lb_logits=hgrn_lb_logits,
             hgrn_norm_g=hgrn_norm_g, rwkv_mu=rwkv_mu, rwkv_w0=rwkv_w0, rwkv_w2=rwkv_w2, rwkv_a0=rwkv_a0,
             rwkv_a2=rwkv_a2, rwkv_g2=rwkv_g2, rwkv_k_k=rwkv_k_k, rwkv_k_a=rwkv_k_a, rwkv_r_k=rwkv_r_k,
             rwkv_ln_g=rwkv_ln_g, rwkv_ln_b=rwkv_ln_b, pool_w=pool_w, pool_scale=pool_scale, w_out=w_out,
             norm2_g=norm2_g, mlp_up=mlp_up, mlp_down=mlp_down, norm_f_g=norm_f_g)
    layer_params = [_layer_params(l, P) for l in range(DEPTH)]

    bp, t_p, _ = x_prompt.shape
    yp, new_p = _trunk_fresh(x_prompt.reshape(bp * t_p, D_MODEL), bp, t_p, layer_params, P)
    y_prompt = yp.reshape(bp, t_p, D_MODEL)

    bs, t_s, _ = x_sample.shape
    nblk = bs // SEQ_BLK
    xs = x_sample.reshape(nblk, SEQ_BLK, t_s, D_MODEL).transpose(0, 2, 1, 3).reshape(bs * t_s, D_MODEL)
    st_s = ([state_ssm_re[l].reshape(bs, SSM_FLAT) for l in range(DEPTH)],
            [state_ssm_im[l].reshape(bs, SSM_FLAT) for l in range(DEPTH)],
            [_hgrn_state_in(state_hgrn[l]) for l in range(DEPTH)],
            [_wkv_state_in(state_wkv[l]) for l in range(DEPTH)],
            [state_shift[l].reshape(bs, RWKV_PROJ) for l in range(DEPTH)],
            [state_pool[l].transpose(1, 0, 2) for l in range(DEPTH)])
    ys, new_s = _trunk_carry(xs, st_s, PAST_LEN, t_s, layer_params, P)
    y_sample = ys.reshape(nblk, t_s, SEQ_BLK, D_MODEL).transpose(0, 2, 1, 3).reshape(bs, t_s, D_MODEL)

    return (y_prompt, y_sample) + _states_out(new_p, bp) + _states_out(new_s, bs)
```

```python
import functools
import itertools

import jax
import jax.numpy as jnp
from jax import lax
from jax.experimental import pallas as pl
from jax.experimental.pallas import tpu as pltpu

F32 = jnp.float32
BF16 = jnp.bfloat16

D_MODEL = 1024
DEPTH = 2
PAST_LEN = 16384
GROUP_WIDTH = 256
HEAD = 64
SSM_GROUPS = 16
SSM_CH = 16
SSM_STATE = 64
SSM_FLAT = SSM_GROUPS * SSM_STATE
POOL_WINDOWS = (2, 4, 8, 16)
POOL_BUF = 15
DECAY_LORA = 64
AAA_LORA = 64
GATE_LORA = 128
RWKV_PROJ = 1024
PROJ_WIDTH = 2560
D_FF = 4096
NORM_EPS = 1e-6
HGRN_NORM_EPS = 1e-5
RWKV_GN_EPS = 64e-5

SEQ_BLK = 8
LANES = 128
VMEM_LIMIT = 48 * 1024 * 1024

COL_SSM, COL_Q, COL_F, COL_I, COL_G, COL_R, COL_K, COL_V, COL_LORA, COL_POOL = range(10)


def _params(sem):
    return pltpu.CompilerParams(dimension_semantics=sem, vmem_limit_bytes=VMEM_LIMIT)


def _dot(a, b):
    return jnp.dot(a, b, preferred_element_type=F32)


def _rms(x, g):
    return x * lax.rsqrt(jnp.mean(x * x, axis=-1, keepdims=True) + NORM_EPS) * g


def _rms_proj_kernel(x_ref, g_ref, w_ref, o_ref):
    o_ref[...] = _dot(_rms(x_ref[...], g_ref[...]).astype(BF16), w_ref[...])


def _tile_map(nseq, nt):
    if nseq == 0:
        return lambda r, *_: (r, 0)
    return lambda r, *_: (r % nt, r // nt)


def rms_proj(x, g, w, *, nseq=0):
    n = x.shape[0]
    tm = min(512, n // max(nseq, 1))
    nt = n // tm // max(nseq, 1)
    out_shape = (n, PROJ_WIDTH) if nseq == 0 else (n // nseq, nseq * PROJ_WIDTH)
    return pl.pallas_call(
        _rms_proj_kernel,
        grid=(n // tm,),
        in_specs=[pl.BlockSpec((tm, D_MODEL), lambda i: (i, 0)),
                  pl.BlockSpec((1, D_MODEL), lambda i: (0, 0)),
                  pl.BlockSpec((D_MODEL, PROJ_WIDTH), lambda i: (0, 0))],
        out_specs=pl.BlockSpec((tm, PROJ_WIDTH), _tile_map(nseq, nt)),
        out_shape=jax.ShapeDtypeStruct(out_shape, F32),
        compiler_params=_params(("parallel",)),
        name="rms_proj",
    )(x, g, w)


def _row_spec(tc, col, nchunks):
    return pl.BlockSpec((tc * SEQ_BLK, GROUP_WIDTH), lambda s, c: (s * nchunks + c, col))


def _full_spec(shape):
    nd = len(shape)
    return pl.BlockSpec(shape, lambda s, c: (0,) * nd)


def _seq_spec(shape):
    nd = len(shape)
    return pl.BlockSpec((SEQ_BLK,) + shape[1:], lambda s, c: (s,) + (0,) * (nd - 1))


def _head_sums(x):
    lane = lax.broadcasted_iota(jnp.int32, x.shape, 1)
    out = jnp.zeros_like(x)
    for h in range(GROUP_WIDTH // HEAD):
        m = (lane >= h * HEAD) & (lane < (h + 1) * HEAD)
        s = jnp.sum(jnp.where(m, x, 0.0), axis=1, keepdims=True)
        out = jnp.where(m, s, out)
    return out


UNITS = SEQ_BLK * 2


def _unit_masks():
    r, c = _iota((2 * LANES, LANES), 0), _iota((2 * LANES, LANES), 1)
    ones2 = jnp.where(_head_of(r & (LANES - 1)) == _head_of(c), 1.0, 0.0).astype(BF16)
    shape = (UNITS * HEAD, LANES)
    eye = (_iota(shape, 1) & (HEAD - 1)) == (_iota(shape, 0) & (HEAD - 1))
    return ones2, eye


def _seg_sum_mxu(p, ones2):
    hi = p.astype(BF16)
    lo = (p - hi.astype(F32)).astype(BF16)
    return _dot(jnp.concatenate([hi, lo], axis=1), ones2)


def _unit_rows(ref, t):
    return jnp.concatenate([jnp.broadcast_to(ref[t, b:b + 1, p * LANES:(p + 1) * LANES], (HEAD, LANES))
                            for b in range(SEQ_BLK) for p in range(2)], axis=0)


def _store_unit_rows(ref, t, cols, eye):
    picked = jnp.where(eye, cols, 0.0)
    for b in range(SEQ_BLK):
        for p in range(2):
            u = b * 2 + p
            ref[t, b:b + 1, p * LANES:(p + 1) * LANES] = jnp.sum(picked[u * HEAD:(u + 1) * HEAD], axis=0, keepdims=True)


def _s5_kernel(u_ref, h0r_ref, h0i_ref, lr_ref, li_ref, ldt_ref, bre_ref, bim_ref, ccat_ref, d_ref,
               gw_ref, gb_ref, y_ref, hr_out, hi_out, h_scr, bu_scr, *, tc):
    c = pl.program_id(1)

    @pl.when(c == 0)
    def _():
        h_scr[0] = h0r_ref[...]
        h_scr[1] = h0i_ref[...]

    lr, li = lr_ref[...], li_ref[...]
    dt = jnp.exp(ldt_ref[...])
    mag = jnp.exp(lr * dt)
    ab_re, ab_im = mag * jnp.cos(li * dt), mag * jnp.sin(li * dt)
    den = lr * lr + li * li
    zr, zi = ab_re - 1.0, ab_im
    cr = (zr * lr + zi * li) / den
    ci = (zi * lr - zr * li) / den
    bre, bim = bre_ref[...], bim_ref[...]
    bb_re = (cr * bre - ci * bim).astype(BF16)
    bb_im = (cr * bim + ci * bre).astype(BF16)

    u = u_ref[...]
    ub = u.astype(BF16)
    bu_scr[:, 0:SSM_FLAT] = _dot(ub, bb_re)
    bu_scr[:, SSM_FLAT:2 * SSM_FLAT] = _dot(ub, bb_im)

    ar = jnp.broadcast_to(ab_re, (SEQ_BLK, SSM_FLAT))
    ai = jnp.broadcast_to(ab_im, (SEQ_BLK, SSM_FLAT))

    def step(t, carry):
        hr, hi = carry
        rows = pl.ds(pl.multiple_of(t * SEQ_BLK, SEQ_BLK), SEQ_BLK)
        nhr = ar * hr - ai * hi + bu_scr[rows, 0:SSM_FLAT]
        nhi = ar * hi + ai * hr + bu_scr[rows, SSM_FLAT:2 * SSM_FLAT]
        bu_scr[rows, 0:SSM_FLAT] = nhr
        bu_scr[rows, SSM_FLAT:2 * SSM_FLAT] = nhi
        return nhr, nhi

    hr, hi = lax.fori_loop(0, tc, step, (h_scr[0], h_scr[1]))
    h_scr[0] = hr
    h_scr[1] = hi

    y = _dot(bu_scr[...].astype(BF16), ccat_ref[...]) + d_ref[...] * u
    z = jax.nn.gelu(y)
    out = z * jax.nn.sigmoid(_dot(z.astype(BF16), gw_ref[...]) + gb_ref[...])
    y_ref[...] = out.astype(y_ref.dtype)

    @pl.when(c == pl.num_programs(1) - 1)
    def _():
        hr_out[...] = hr
        hi_out[...] = hi


def s5_mixer(proj, h0_re, h0_im, lam_re, lam_im, log_dt, b_re_bd, b_im_bd, c_cat, d_skip, glu_w, glu_b, *, t_len, tc):
    nseq = h0_re.shape[0] // SEQ_BLK
    nchunks = t_len // tc
    n = proj.shape[0]
    st = jax.ShapeDtypeStruct(h0_re.shape, F32)
    return pl.pallas_call(
        functools.partial(_s5_kernel, tc=tc),
        grid=(nseq, nchunks),
        in_specs=[_row_spec(tc, COL_SSM, nchunks),
                  _seq_spec(h0_re.shape), _seq_spec(h0_im.shape),
                  _full_spec((1, SSM_FLAT)), _full_spec((1, SSM_FLAT)), _full_spec((1, SSM_FLAT)),
                  _full_spec((GROUP_WIDTH, SSM_FLAT)), _full_spec((GROUP_WIDTH, SSM_FLAT)),
                  _full_spec((2 * SSM_FLAT, GROUP_WIDTH)), _full_spec((1, GROUP_WIDTH)),
                  _full_spec((GROUP_WIDTH, GROUP_WIDTH)), _full_spec((1, GROUP_WIDTH))],
        out_specs=[_row_spec(tc, 0, nchunks), _seq_spec(h0_re.shape), _seq_spec(h0_im.shape)],
        out_shape=[jax.ShapeDtypeStruct((n, GROUP_WIDTH), BF16), st, st],
        scratch_shapes=[pltpu.VMEM((2, SEQ_BLK, SSM_FLAT), F32),
                        pltpu.VMEM((tc * SEQ_BLK, 2 * SSM_FLAT), F32)],
        compiler_params=_params(("parallel", "arbitrary")),
        name="s5_mixer",
    )(proj, h0_re, h0_im, lam_re, lam_im, log_dt, b_re_bd, b_im_bd, c_cat, d_skip, glu_w, glu_b)


def _pool_kernel(u_ref, buf_ref, w_ref, sc_ref, y_ref, nbuf_ref, ext_scr, *, tc, pos0):
    c = pl.program_id(1)

    @pl.when(c == 0)
    def _():
        ext_scr[0:POOL_BUF] = buf_ref[...]

    u = u_ref[...].reshape(tc, SEQ_BLK, GROUP_WIDTH)
    ext_scr[POOL_BUF:POOL_BUF + tc] = u
    a1 = ext_scr[...]
    a2 = a1[1:] + a1[:-1]
    a4 = a2[2:] + a2[:-2]
    a8 = a4[4:] + a4[:-4]
    a16 = a8[8:] + a8[:-8]
    sums = (a2[14:], a4[12:], a8[8:], a16)

    shape = (tc, SEQ_BLK, GROUP_WIDTH)
    pos = lax.broadcasted_iota(jnp.int32, shape, 0) + (c * tc + pos0)
    lane = lax.broadcasted_iota(jnp.int32, shape, 2)
    pooled = None
    for gi in reversed(range(len(POOL_WINDOWS))):
        win = POOL_WINDOWS[gi]
        mean = sums[gi] / jnp.minimum(pos + 1, win).astype(F32)
        pooled = mean if pooled is None else jnp.where(lane < (gi + 1) * HEAD, mean, pooled)
    pooled = (pooled - u).reshape(tc * SEQ_BLK, GROUP_WIDTH)
    y_ref[...] = (_dot(pooled.astype(BF16), w_ref[...]) * sc_ref[...]).astype(y_ref.dtype)

    nb = ext_scr[tc:tc + POOL_BUF]
    ext_scr[0:POOL_BUF] = nb

    @pl.when(c == pl.num_programs(1) - 1)
    def _():
        nbuf_ref[...] = nb


def pool_mixer(proj, buf, w_bd, scale, *, t_len, tc, pos0):
    nseq = buf.shape[1] // SEQ_BLK
    nchunks = t_len // tc
    n = proj.shape[0]
    buf_spec = pl.BlockSpec((POOL_BUF, SEQ_BLK, GROUP_WIDTH), lambda s, c: (0, s, 0))
    return pl.pallas_call(
        functools.partial(_pool_kernel, tc=tc, pos0=pos0),
        grid=(nseq, nchunks),
        in_specs=[_row_spec(tc, COL_POOL, nchunks), buf_spec,
                  _full_spec((GROUP_WIDTH, GROUP_WIDTH)), _full_spec((1, GROUP_WIDTH))],
        out_specs=[_row_spec(tc, 0, nchunks), buf_spec],
        out_shape=[jax.ShapeDtypeStruct((n, GROUP_WIDTH), BF16), jax.ShapeDtypeStruct(buf.shape, F32)],
        scratch_shapes=[pltpu.VMEM((tc + POOL_BUF, SEQ_BLK, GROUP_WIDTH), F32)],
        compiler_params=_params(("parallel", "arbitrary")),
        name="pool_mixer",
    )(proj, buf, w_bd, scale)


def _hgrn_lower_bound(logits_ref, layer):
    rows = [logits_ref[l:l + 1, :] for l in range(DEPTH)]
    m = functools.reduce(jnp.maximum, rows)
    es = [jnp.exp(r - m) for r in rows]
    tot = functools.reduce(lambda a, b: a + b, es)
    lb = jnp.zeros_like(m)
    for l in range(1, layer + 1):
        lb = lb + es[l] / tot
    return lb


def _hgrn_kernel(pq_ref, pf_ref, pi_ref, pg_ref, s0_ref, lbl_ref, ng_ref, y_ref, st_ref,
                 s_scr, q_scr, f_scr, k_scr, v_scr, o_scr, *, tc, layer):
    c = pl.program_id(1)
    shape3 = (tc, SEQ_BLK, GROUP_WIDTH)

    @pl.when(c == 0)
    def _():
        s_scr[...] = s0_ref[...]

    lb = _hgrn_lower_bound(lbl_ref, layer)
    zf = pf_ref[...]
    f_scr[...] = (lb + (1.0 - lb) * jax.nn.sigmoid(zf)).reshape(shape3)
    k_scr[...] = ((1.0 - lb) * jax.nn.sigmoid(-zf)).reshape(shape3)
    q_scr[...] = jax.nn.silu(pq_ref[...]).reshape(shape3)
    v_scr[...] = pi_ref[...].reshape(shape3)

    ones2, eye = _unit_masks()
    s = s_scr[...].reshape(UNITS * HEAD, LANES)
    for t in range(tc):
        vcol = _seg_sum_mxu(jnp.where(eye, _unit_rows(v_scr, t), 0.0), ones2)
        s = s * _unit_rows(f_scr, t) + vcol * _unit_rows(k_scr, t)
        _store_unit_rows(o_scr, t, _seg_sum_mxu(s * _unit_rows(q_scr, t), ones2), eye)
    s_scr[...] = s.reshape(s_scr.shape)

    o = o_scr[...].reshape(tc * SEQ_BLK, GROUP_WIDTH)
    ms = _head_sums(o * o) * (1.0 / HEAD)
    out = o * lax.rsqrt(ms + HGRN_NORM_EPS) * ng_ref[...] * jax.nn.silu(pg_ref[...])
    y_ref[...] = out.astype(y_ref.dtype)

    @pl.when(c == pl.num_programs(1) - 1)
    def _():
        st_ref[...] = s_scr[...]


def hgrn_mixer(proj, s0, lb_logits, norm_g, *, t_len, tc, layer):
    nseq = s0.shape[0] // SEQ_BLK
    nchunks = t_len // tc
    n = proj.shape[0]
    tile = pltpu.VMEM((tc, SEQ_BLK, GROUP_WIDTH), F32)
    return pl.pallas_call(
        functools.partial(_hgrn_kernel, tc=tc, layer=layer),
        grid=(nseq, nchunks),
        in_specs=[_row_spec(tc, COL_Q, nchunks), _row_spec(tc, COL_F, nchunks),
                  _row_spec(tc, COL_I, nchunks), _row_spec(tc, COL_G, nchunks),
                  _seq_spec(s0.shape), _full_spec((DEPTH, GROUP_WIDTH)), _full_spec((1, GROUP_WIDTH))],
        out_specs=[_row_spec(tc, 0, nchunks), _seq_spec(s0.shape)],
        out_shape=[jax.ShapeDtypeStruct((n, GROUP_WIDTH), BF16), jax.ShapeDtypeStruct(s0.shape, F32)],
        scratch_shapes=[pltpu.VMEM((SEQ_BLK, 2, HEAD, LANES), F32), tile, tile, tile, tile, tile],
        compiler_params=_params(("parallel", "arbitrary")),
        name="hgrn_mixer",
    )(proj, proj, proj, proj, s0, lb_logits, norm_g)


def _rwkv_kernel(pr_ref, pk_ref, pv_ref, pl_ref, sh0_ref, s0_ref, mu_ref, w0_ref, a0_ref, kk_ref, ka_ref,
                 rk_ref, lng_ref, lnb_ref, w2_ref, a2_ref, g2_ref, y_ref, st_ref, sh_ref,
                 s_scr, prev_scr, r_scr, w_scr, k_scr, v_scr, nkk_scr, kka_scr, o_scr, *, tc):
    c = pl.program_id(1)
    shape3 = (tc, SEQ_BLK, GROUP_WIDTH)
    gw = GROUP_WIDTH

    @pl.when(c == 0)
    def _():
        s_scr[...] = s0_ref[...]
        prev_scr[...] = sh0_ref[...]

    def shifted(ref, j):
        x = ref[...].reshape(shape3)
        first = prev_scr[:, j * gw:(j + 1) * gw].reshape(1, SEQ_BLK, gw)
        prev = first if tc == 1 else jnp.concatenate([first, x[:-1]], axis=0)
        prev_scr[:, j * gw:(j + 1) * gw] = x[tc - 1]
        return (x + (prev - x) * mu_ref[:, j * gw:(j + 1) * gw]).reshape(tc * SEQ_BLK, gw)

    xr, xk, xv, xl = shifted(pr_ref, 0), shifted(pk_ref, 1), shifted(pv_ref, 2), shifted(pl_ref, 3)
    w = -jax.nn.softplus(-(w0_ref[...] + _dot(jnp.tanh(xl).astype(BF16), w2_ref[...]))) - 0.5
    decay = jnp.exp(-jnp.exp(w))
    a = jax.nn.sigmoid(a0_ref[...] + _dot(xl.astype(BF16), a2_ref[...]))
    g = _dot(jax.nn.sigmoid(xl).astype(BF16), g2_ref[...])
    kk = xk * kk_ref[...]
    kk = kk / jnp.maximum(jnp.sqrt(_head_sums(kk * kk)), 1e-12)
    k = xk * (1.0 + (a - 1.0) * ka_ref[...])

    r_scr[...] = xr.reshape(shape3)
    w_scr[...] = decay.reshape(shape3)
    k_scr[...] = k.reshape(shape3)
    v_scr[...] = xv.reshape(shape3)
    nkk_scr[...] = (-kk).reshape(shape3)
    kka_scr[...] = (kk * a).reshape(shape3)

    ones2, eye = _unit_masks()
    s = s_scr[...].reshape(UNITS * HEAD, LANES)
    for t in range(tc):
        sa = _seg_sum_mxu(s * _unit_rows(nkk_scr, t), ones2)
        vcol = _seg_sum_mxu(jnp.where(eye, _unit_rows(v_scr, t), 0.0), ones2)
        s = s * _unit_rows(w_scr, t) + sa * _unit_rows(kka_scr, t) + vcol * _unit_rows(k_scr, t)
        _store_unit_rows(o_scr, t, _seg_sum_mxu(s * _unit_rows(r_scr, t), ones2), eye)
    s_scr[...] = s.reshape(s_scr.shape)

    y = o_scr[...].reshape(tc * SEQ_BLK, gw)
    mean = _head_sums(y) * (1.0 / HEAD)
    d = y - mean
    var = _head_sums(d * d) * (1.0 / HEAD)
    yn = d * lax.rsqrt(var + RWKV_GN_EPS) * lng_ref[...] + lnb_ref[...]
    bonus = _head_sums(xr * k * rk_ref[...]) * xv
    y_ref[...] = ((yn + bonus) * g).astype(y_ref.dtype)

    @pl.when(c == pl.num_programs(1) - 1)
    def _():
        st_ref[...] = s_scr[...]
        sh_ref[...] = prev_scr[...]


def rwkv_mixer(proj, shift0, s0, mu, w0, a0, k_k, k_a, r_k, ln_g, ln_b, w2p, a2p, g2p, *, t_len, tc):
    nseq = s0.shape[0] // SEQ_BLK
    nchunks = t_len // tc
    n = proj.shape[0]
    tile = pltpu.VMEM((tc, SEQ_BLK, GROUP_WIDTH), F32)
    vec = _full_spec((1, GROUP_WIDTH))
    mat = _full_spec((GROUP_WIDTH, GROUP_WIDTH))
    return pl.pallas_call(
        functools.partial(_rwkv_kernel, tc=tc),
        grid=(nseq, nchunks),
        in_specs=[_row_spec(tc, COL_R, nchunks), _row_spec(tc, COL_K, nchunks),
                  _row_spec(tc, COL_V, nchunks), _row_spec(tc, COL_LORA, nchunks),
                  _seq_spec(shift0.shape), _seq_spec(s0.shape), _full_spec((1, RWKV_PROJ)),
                  vec, vec, vec, vec, vec, vec, vec, mat, mat, mat],
        out_specs=[_row_spec(tc, 0, nchunks), _seq_spec(s0.shape), _seq_spec(shift0.shape)],
        out_shape=[jax.ShapeDtypeStruct((n, GROUP_WIDTH), BF16), jax.ShapeDtypeStruct(s0.shape, F32),
                   jax.ShapeDtypeStruct(shift0.shape, F32)],
        scratch_shapes=[pltpu.VMEM((SEQ_BLK, 2, HEAD, LANES), F32), pltpu.VMEM((SEQ_BLK, RWKV_PROJ), F32),
                        tile, tile, tile, tile, tile, tile, tile],
        compiler_params=_params(("parallel", "arbitrary")),
        name="rwkv_mixer",
    )(proj, proj, proj, proj, shift0, s0, mu, w0, a0, k_k, k_a, r_k, ln_g, ln_b, w2p, a2p, g2p)


def _dot_nt(a, b):
    return lax.dot_general(a, b, (((1,), (1,)), ((), ())), preferred_element_type=F32)


def _dot_tn(a, b):
    return lax.dot_general(a, b, (((0,), (0,)), ((), ())), preferred_element_type=F32)


def _iota(shape, dim):
    return lax.broadcasted_iota(jnp.int32, shape, dim)


def _head_of(idx):
    return lax.shift_right_logical(idx, HEAD.bit_length() - 1)


def _cumsum_rows(x):
    n = x.shape[0]
    tri = jnp.where(_iota((n, n), 0) >= _iota((n, n), 1), 1.0, 0.0).astype(BF16)
    hi = x.astype(BF16)
    rest = x - hi.astype(F32)
    mid = rest.astype(BF16)
    lo = (rest - mid.astype(F32)).astype(BF16)
    return _dot(tri, hi) + _dot(tri, mid) + _dot(tri, lo)


def _own_head(shape, rows_per_head):
    row_h = lax.shift_right_logical(_iota(shape, 0), rows_per_head.bit_length() - 1)
    return row_h == _head_of(_iota(shape, 1))


def _head_expand(x):
    xx = jnp.concatenate([x] * (GROUP_WIDTH // HEAD), axis=0)
    return jnp.where(_own_head(xx.shape, x.shape[0]), xx, 0.0)


def _head_collapse(xx):
    n = xx.shape[0] // (GROUP_WIDTH // HEAD)
    return xx[0:n] + xx[n:2 * n] + xx[2 * n:3 * n] + xx[3 * n:4 * n]


def _block_diag_mask():
    shape = (GROUP_WIDTH, GROUP_WIDTH)
    return _head_of(_iota(shape, 0)) == _head_of(_iota(shape, 1))


def _seq_col_spec(rows, col, ncols):
    return pl.BlockSpec((rows, GROUP_WIDTH), lambda b, c: (c, b * ncols + col))


HGRN_CHUNK = 128
HGRN_SUB = 16


HGRN_SEQS_PER_STEP = 4


def _run_staged(stages):
    for _ in itertools.zip_longest(*stages):
        pass


def _hgrn_chunk_kernel(*refs, nseq, layer):
    lbl_ref, ng_ref, y_ref, st_ref, w_scr, k_scr, b_scr, v_scr, p_scr, o_scr = refs[4 * nseq:]
    gw = GROUP_WIDTH
    c = pl.program_id(1)

    @pl.when(c == 0)
    def _():
        w_scr[...] = jnp.zeros_like(w_scr)
        p_scr[...] = jnp.zeros_like(p_scr)

    _run_staged([_hgrn_chunk_one(*refs[4 * s:4 * s + 4], lbl_ref, ng_ref, y_ref.at[:, s * gw:(s + 1) * gw],
                                 w_scr.at[s], k_scr.at[s], b_scr.at[s], v_scr.at[s], p_scr.at[s], o_scr.at[s],
                                 layer=layer) for s in range(nseq)])

    @pl.when(c == pl.num_programs(1) - 1)
    def _():
        for s in range(nseq):
            st_ref[s * gw:(s + 1) * gw, :] = w_scr[s]


def _hgrn_chunk_one(pq_ref, pf_ref, pi_ref, pg_ref, lbl_ref, ng_ref, y_ref, w_scr, k_scr, b_scr, v_scr, p_scr,
                    o_scr, *, layer):
    L, n = HGRN_CHUNK, HGRN_SUB
    half = n // 2
    lb = _hgrn_lower_bound(lbl_ref, layer)
    z = pf_ref[...]
    g = jnp.logaddexp(jnp.log1p(-lb) + jax.nn.log_sigmoid(z), jnp.log(lb))
    kg = (1.0 - lb) * jax.nn.sigmoid(-z)
    q = jax.nn.silu(pq_ref[...])
    v = pi_ref[...]
    bc = _cumsum_rows(g)
    k_scr[...] = kg
    b_scr[...] = bc
    v_scr[...] = v
    yield

    bd = _block_diag_mask()
    ones_bd = jnp.where(bd, 1.0, 0.0).astype(BF16)
    rid = _iota((n, GROUP_WIDTH), 0)
    rid_lo = _iota((half, GROUP_WIDTH), 0) + half
    for sb in range(L // n):
        base = sb * n
        qs, bs = q[base:base + n], bc[base:base + n]
        q_lo, b_lo = q[base + half:base + n], bc[base + half:base + n]
        for s in range(n):
            ks, bsrow = k_scr[base + s:base + s + 1, :], b_scr[base + s:base + s + 1, :]
            if s < half:
                p_scr[s * n:(s + 1) * n, :] = qs * ks * jnp.where(rid >= s, jnp.exp(bs - bsrow), 0.0)
            else:
                p_scr[s * n + half:(s + 1) * n, :] = q_lo * ks * jnp.where(rid_lo >= s, jnp.exp(b_lo - bsrow), 0.0)
        r = _dot(p_scr[...].astype(BF16), ones_bd)
        acc = jnp.zeros((n, GROUP_WIDTH), F32)
        for s in range(n):
            acc = acc + r[s * n:(s + 1) * n] * v_scr[base + s:base + s + 1, :]
        o_scr[base:base + n, :] = acc
        yield

    vb = v.astype(BF16)
    for i in range(1, L // n):
        r0 = i * n
        ref = b_scr[r0 - 1:r0, :]
        qt = q[r0:r0 + n] * jnp.exp(bc[r0:r0 + n] - ref)
        kt = jnp.concatenate([kg[:r0] * jnp.exp(ref - bc[:r0]), jnp.zeros((L - r0, GROUP_WIDTH), F32)], axis=0)
        att = _dot_nt(_head_expand(qt).astype(BF16), kt.astype(BF16))
        yield
        ox = _dot(att.astype(BF16), vb)
        o_scr[r0:r0 + n, :] += _head_collapse(jnp.where(_own_head(ox.shape, n), ox, 0.0))
        yield

    w = w_scr[...]
    o = o_scr[...] + _dot_nt((q * jnp.exp(bc)).astype(BF16), w.astype(BF16))
    b_end = b_scr[L - 1:L, :]
    upd = _dot_tn(vb, (kg * jnp.exp(b_end - bc)).astype(BF16))
    w_scr[...] = w * jnp.exp(b_end) + jnp.where(bd, upd, 0.0)
    yield

    ms = _head_sums(o * o) * (1.0 / HEAD)
    out = o * lax.rsqrt(ms + HGRN_NORM_EPS) * ng_ref[...] * jax.nn.silu(pg_ref[...])
    y_ref[...] = out.astype(y_ref.dtype)


def _seq_group_col_spec(rows, col, ncols, s, nseq):
    return pl.BlockSpec((rows, GROUP_WIDTH), lambda i, c: (c, (i * nseq + s) * ncols + col))


def hgrn_chunk_mixer(proj2, lb_logits, norm_g, *, bsz, t_len, layer):
    L, n, nseq = HGRN_CHUNK, HGRN_SUB, HGRN_SEQS_PER_STEP
    gw = GROUP_WIDTH
    ncols = PROJ_WIDTH // gw
    tile = pltpu.VMEM((nseq, L, gw), F32)
    seq_specs = [_seq_group_col_spec(L, col, ncols, s, nseq)
                 for s in range(nseq) for col in (COL_Q, COL_F, COL_I, COL_G)]
    y, st = pl.pallas_call(
        functools.partial(_hgrn_chunk_kernel, nseq=nseq, layer=layer),
        grid=(bsz // nseq, t_len // L),
        in_specs=seq_specs + [pl.BlockSpec((DEPTH, gw), lambda i, c: (0, 0)),
                              pl.BlockSpec((1, gw), lambda i, c: (0, 0))],
        out_specs=[pl.BlockSpec((L, nseq * gw), lambda i, c: (c, i)),
                   pl.BlockSpec((nseq * gw, gw), lambda i, c: (i, 0))],
        out_shape=[jax.ShapeDtypeStruct((t_len, bsz * gw), BF16),
                   jax.ShapeDtypeStruct((bsz * gw, gw), F32)],
        scratch_shapes=[pltpu.VMEM((nseq, gw, gw), F32), tile, tile, tile,
                        pltpu.VMEM((nseq, n * n, gw), F32), tile],
        compiler_params=_params(("parallel", "arbitrary")),
        name="hgrn_chunk_mixer",
    )(*([proj2] * (4 * nseq)), lb_logits, norm_g)
    return y, st


RWKV_CHUNK = 64


def _rwkv_chunk_kernel(*refs, nseq):
    params = refs[4 * nseq:4 * nseq + 11]
    y_ref, st_ref, sh_ref, w_scr, prev_scr = refs[4 * nseq + 11:]
    gw = GROUP_WIDTH
    c = pl.program_id(1)

    @pl.when(c == 0)
    def _():
        w_scr[...] = jnp.zeros_like(w_scr)
        prev_scr[...] = jnp.zeros_like(prev_scr)

    _run_staged([_rwkv_chunk_one(*refs[4 * s:4 * s + 4], *params, y_ref.at[:, s * gw:(s + 1) * gw], w_scr.at[s],
                                 prev_scr.at[s]) for s in range(nseq)])

    @pl.when(c == pl.num_programs(1) - 1)
    def _():
        for s in range(nseq):
            st_ref[s * gw:(s + 1) * gw, :] = w_scr[s]
        sh_ref[...] = prev_scr[...]


def _rwkv_chunk_one(pr_ref, pk_ref, pv_ref, pl_ref, mu_ref, w0_ref, a0_ref, kk_ref, ka_ref, rk_ref, lng_ref,
                    lnb_ref, w2_ref, a2_ref, g2_ref, y_ref, w_scr, prev_scr):
    L = RWKV_CHUNK
    gw = GROUP_WIDTH
    rid = _iota((L, gw), 0)

    def shifted(ref, j):
        x = ref[...]
        prev = jnp.where(rid == 0, prev_scr[:, j * gw:(j + 1) * gw], pltpu.roll(x, 1, axis=0))
        prev_scr[:, j * gw:(j + 1) * gw] = x[L - 1:L]
        return x + (prev - x) * mu_ref[:, j * gw:(j + 1) * gw]

    xr, xk, xv, xl = shifted(pr_ref, 0), shifted(pk_ref, 1), shifted(pv_ref, 2), shifted(pl_ref, 3)
    w = -jax.nn.softplus(-(w0_ref[...] + _dot(jnp.tanh(xl).astype(BF16), w2_ref[...]))) - 0.5
    lw = -jnp.exp(w)
    a = jax.nn.sigmoid(a0_ref[...] + _dot(xl.astype(BF16), a2_ref[...]))
    g = _dot(jax.nn.sigmoid(xl).astype(BF16), g2_ref[...])
    kk = xk * kk_ref[...]
    kk = kk / jnp.maximum(jnp.sqrt(_head_sums(kk * kk)), 1e-12)
    k = xk * (1.0 + (a - 1.0) * ka_ref[...])
    beta = kk * a
    yield

    cs = _cumsum_rows(lw)
    c_end = cs[L - 1:L]
    e_neg = jnp.exp(-cs)
    e_end = jnp.exp(c_end - cs)
    ar = jnp.concatenate([_head_expand(-kk * jnp.exp(cs - lw)), _head_expand(xr * jnp.exp(cs))], axis=0).astype(BF16)
    bk = jnp.concatenate([_head_expand(beta * e_neg), _head_expand(k * e_neg)], axis=0).astype(BF16)
    vx = _head_expand(xv).astype(BF16)
    yield

    nh = 4 * L
    gmat = _dot_nt(ar, bk)
    tt = _iota((nh, nh), 0) & (L - 1)
    ss = _iota((nh, nh), 1) & (L - 1)
    strict, incl = ss < tt, ss <= tt
    nab = jnp.where(strict, gmat[0:nh, 0:nh], 0.0)
    nak = jnp.where(strict, gmat[0:nh, nh:2 * nh], 0.0).astype(BF16)
    nrb = jnp.where(incl, gmat[nh:2 * nh, 0:nh], 0.0).astype(BF16)
    nrk = jnp.where(incl, gmat[nh:2 * nh, nh:2 * nh], 0.0).astype(BF16)
    yield

    ri, ci = _iota((nh, nh), 0), _iota((nh, nh), 1)

    def same_block(size):
        sh = size.bit_length() - 1
        return lax.shift_right_logical(ri, sh) == lax.shift_right_logical(ci, sh)

    base = 8
    m = jnp.where(same_block(base), nab, 0.0)
    t_inv = jnp.where(ri == ci, 1.0, 0.0) + m
    m = m.astype(BF16)
    for _ in range(base.bit_length() - 2):
        m = _dot(m, m).astype(BF16)
        yield
        t_inv = t_inv + _dot(t_inv.astype(BF16), m)
        yield
    size = base
    while size < L:
        off = jnp.where(same_block(2 * size), jnp.where(same_block(size), 0.0, nab), 0.0).astype(BF16)
        tb = t_inv.astype(BF16)
        half = _dot(tb, off).astype(BF16)
        yield
        t_inv = t_inv + _dot(half, tb)
        yield
        size *= 2

    wst = w_scr[...]
    sw = _dot_nt(ar, wst.astype(BF16))
    rhs = (sw[0:nh] + _dot(nak, vx)).astype(BF16)
    yield
    x = _dot(t_inv.astype(BF16), rhs)
    yield
    ux = x.astype(BF16)
    yx = sw[nh:2 * nh] + _dot(nrb, ux) + _dot(nrk, vx)
    y = _head_collapse(yx)
    u = _head_collapse(x)
    yield

    upd = _dot_tn(jnp.concatenate([u, xv], axis=0).astype(BF16),
                  jnp.concatenate([beta * e_end, k * e_end], axis=0).astype(BF16))
    w_scr[...] = wst * jnp.exp(c_end) + jnp.where(_block_diag_mask(), upd, 0.0)
    yield

    mean = _head_sums(y) * (1.0 / HEAD)
    d = y - mean
    var = _head_sums(d * d) * (1.0 / HEAD)
    yn = d * lax.rsqrt(var + RWKV_GN_EPS) * lng_ref[...] + lnb_ref[...]
    bonus = _head_sums(xr * k * rk_ref[...]) * xv
    y_ref[...] = ((yn + bonus) * g).astype(y_ref.dtype)


RWKV_SEQS_PER_STEP = 8


def rwkv_chunk_mixer(proj2, mu, w0, a0, k_k, k_a, r_k, ln_g, ln_b, w2p, a2p, g2p, *, bsz, t_len):
    L, nseq = RWKV_CHUNK, RWKV_SEQS_PER_STEP
    gw = GROUP_WIDTH
    ncols = PROJ_WIDTH // gw
    vec = pl.BlockSpec((1, gw), lambda i, c: (0, 0))
    mat = pl.BlockSpec((gw, gw), lambda i, c: (0, 0))
    seq_specs = [_seq_group_col_spec(L, col, ncols, s, nseq)
                 for s in range(nseq) for col in (COL_R, COL_K, COL_V, COL_LORA)]
    y, st, sh = pl.pallas_call(
        functools.partial(_rwkv_chunk_kernel, nseq=nseq),
        grid=(bsz // nseq, t_len // L),
        in_specs=seq_specs + [pl.BlockSpec((1, RWKV_PROJ), lambda i, c: (0, 0)),
                              vec, vec, vec, vec, vec, vec, vec, mat, mat, mat],
        out_specs=[pl.BlockSpec((L, nseq * gw), lambda i, c: (c, i)),
                   pl.BlockSpec((nseq * gw, gw), lambda i, c: (i, 0)),
                   pl.BlockSpec((nseq, 1, RWKV_PROJ), lambda i, c: (i, 0, 0))],
        out_shape=[jax.ShapeDtypeStruct((t_len, bsz * gw), BF16),
                   jax.ShapeDtypeStruct((bsz * gw, gw), F32),
                   jax.ShapeDtypeStruct((bsz, 1, RWKV_PROJ), F32)],
        scratch_shapes=[pltpu.VMEM((nseq, gw, gw), F32), pltpu.VMEM((nseq, 1, RWKV_PROJ), F32)],
        compiler_params=_params(("parallel", "arbitrary")),
        name="rwkv_chunk_mixer",
    )(*([proj2] * (4 * nseq)), mu, w0, a0, k_k, k_a, r_k, ln_g, ln_b, w2p, a2p, g2p)
    return y, st, sh


def _hgrn_rwkv_chunk_kernel(*refs, layer, rwkv_steps, hgrn_steps):
    nr, nh = RWKV_SEQS_PER_STEP, HGRN_SEQS_PER_STEP
    it = iter(refs)
    take = lambda n: [next(it) for _ in range(n)]
    r_seq, r_par, h_seq = take(4 * nr), take(11), take(4 * nh)
    lbl_ref, ng_ref = take(2)
    yr_ref, str_ref, shr_ref, yh_ref, sth_ref = take(5)
    wr_scr, prev_scr, wh_scr, k_scr, b_scr, v_scr, p_scr, o_scr = take(8)
    gw = GROUP_WIDTH
    g = pl.program_id(0)
    cr, ch = g % rwkv_steps, g % hgrn_steps

    @pl.when(cr == 0)
    def _():
        wr_scr[...] = jnp.zeros_like(wr_scr)
        prev_scr[...] = jnp.zeros_like(prev_scr)

    @pl.when(ch == 0)
    def _():
        wh_scr[...] = jnp.zeros_like(wh_scr)
        p_scr[...] = jnp.zeros_like(p_scr)

    rwkv = [_rwkv_chunk_one(*r_seq[4 * s:4 * s + 4], *r_par, yr_ref.at[:, s * gw:(s + 1) * gw], wr_scr.at[s],
                            prev_scr.at[s]) for s in range(nr)]
    hgrn = [_hgrn_chunk_one(*h_seq[4 * s:4 * s + 4], lbl_ref, ng_ref, yh_ref.at[:, s * gw:(s + 1) * gw],
                            wh_scr.at[s], k_scr.at[s], b_scr.at[s], v_scr.at[s], p_scr.at[s], o_scr.at[s],
                            layer=layer) for s in range(nh)]
    _run_staged(rwkv + hgrn)

    @pl.when(cr == rwkv_steps - 1)
    def _():
        for s in range(nr):
            str_ref[s * gw:(s + 1) * gw, :] = wr_scr[s]
        shr_ref[...] = prev_scr[...]

    @pl.when(ch == hgrn_steps - 1)
    def _():
        for s in range(nh):
            sth_ref[s * gw:(s + 1) * gw, :] = wh_scr[s]


def hgrn_rwkv_chunk_mixer(proj2, lb_logits, norm_g, mu, w0, a0, k_k, k_a, r_k, ln_g, ln_b, w2p, a2p, g2p, *,
                          bsz, t_len, layer):
    nr, nh, lr, lh, n = RWKV_SEQS_PER_STEP, HGRN_SEQS_PER_STEP, RWKV_CHUNK, HGRN_CHUNK, HGRN_SUB
    gw = GROUP_WIDTH
    ncols = PROJ_WIDTH // gw
    rwkv_steps, hgrn_steps = t_len // lr, t_len // lh
    steps = (bsz // nr) * rwkv_steps
    assert (bsz // nh) * hgrn_steps == steps
    const = lambda shape: pl.BlockSpec(shape, lambda g: (0,) * len(shape))
    r_specs = [pl.BlockSpec((lr, gw), functools.partial(
                   lambda g, s, col: (g % rwkv_steps, ((g // rwkv_steps) * nr + s) * ncols + col), s=s, col=col))
               for s in range(nr) for col in (COL_R, COL_K, COL_V, COL_LORA)]
    h_specs = [pl.BlockSpec((lh, gw), functools.partial(
                   lambda g, s, col: (g % hgrn_steps, ((g // hgrn_steps) * nh + s) * ncols + col), s=s, col=col))
               for s in range(nh) for col in (COL_Q, COL_F, COL_I, COL_G)]
    vec, mat = const((1, gw)), const((gw, gw))
    tile = pltpu.VMEM((nh, lh, gw), F32)
    yr, st_r, sh_r, yh, st_h = pl.pallas_call(
        functools.partial(_hgrn_rwkv_chunk_kernel, layer=layer, rwkv_steps=rwkv_steps, hgrn_steps=hgrn_steps),
        grid=(steps,),
        in_specs=r_specs + [const((1, RWKV_PROJ)), vec, vec, vec, vec, vec, vec, vec, mat, mat, mat]
                 + h_specs + [const((DEPTH, gw)), vec],
        out_specs=[pl.BlockSpec((lr, nr * gw), lambda g: (g % rwkv_steps, g // rwkv_steps)),
                   pl.BlockSpec((nr * gw, gw), lambda g: (g // rwkv_steps, 0)),
                   pl.BlockSpec((nr, 1, RWKV_PROJ), lambda g: (g // rwkv_steps, 0, 0)),
                   pl.BlockSpec((lh, nh * gw), lambda g: (g % hgrn_steps, g // hgrn_steps)),
                   pl.BlockSpec((nh * gw, gw), lambda g: (g // hgrn_steps, 0))],
        out_shape=[jax.ShapeDtypeStruct((t_len, bsz * gw), BF16), jax.ShapeDtypeStruct((bsz * gw, gw), F32),
                   jax.ShapeDtypeStruct((bsz, 1, RWKV_PROJ), F32),
                   jax.ShapeDtypeStruct((t_len, bsz * gw), BF16), jax.ShapeDtypeStruct((bsz * gw, gw), F32)],
        scratch_shapes=[pltpu.VMEM((nr, gw, gw), F32), pltpu.VMEM((nr, 1, RWKV_PROJ), F32),
                        pltpu.VMEM((nh, gw, gw), F32), tile, tile, tile, pltpu.VMEM((nh, n * n, gw), F32), tile],
        compiler_params=_params(("arbitrary",)),
        name="hgrn_rwkv_chunk_mixer",
    )(*([proj2] * (4 * nr)), mu, w0, a0, k_k, k_a, r_k, ln_g, ln_b, w2p, a2p, g2p,
      *([proj2] * (4 * nh)), lb_logits, norm_g)
    return yh, st_h, yr, st_r, sh_r


S5_CHUNK = 64


def _s5_seq_kernel(*refs, nseq):
    u_refs = refs[:nseq]
    lr_ref, li_ref, ldt_ref, bre_ref, bim_ref, ccat_ref, d_ref, gw_ref, gb_ref = refs[nseq:nseq + 9]
    y_ref, hr_out, hi_out, h_scr, bu_scr, bb_scr, ab_scr, perm_scr, u_scr = refs[nseq + 9:]
    tc = S5_CHUNK
    n = nseq * tc
    c = pl.program_id(0)

    @pl.when(c == 0)
    def _():
        h_scr[...] = jnp.zeros_like(h_scr)
        lr, li = lr_ref[...], li_ref[...]
        dt = jnp.exp(ldt_ref[...])
        mag = jnp.exp(lr * dt)
        ab_re, ab_im = mag * jnp.cos(li * dt), mag * jnp.sin(li * dt)
        den = lr * lr + li * li
        zr, zi = ab_re - 1.0, ab_im
        cr = (zr * lr + zi * li) / den
        ci = (zi * lr - zr * li) / den
        bre, bim = bre_ref[...], bim_ref[...]
        bb_scr[:, 0:SSM_FLAT] = (cr * bre - ci * bim).astype(BF16)
        bb_scr[:, SSM_FLAT:2 * SSM_FLAT] = (cr * bim + ci * bre).astype(BF16)
        ab_scr[0] = jnp.broadcast_to(ab_re, (nseq, SSM_FLAT))
        ab_scr[1] = jnp.broadcast_to(ab_im, (nseq, SSM_FLAT))
        ri, cj = _iota((n, n), 0), _iota((n, n), 1)
        lseq, lt = nseq.bit_length() - 1, tc.bit_length() - 1
        perm_scr[0] = jnp.where(cj == (ri & (nseq - 1)) * tc + lax.shift_right_logical(ri, lseq), 1.0, 0.0).astype(BF16)
        perm_scr[1] = jnp.where(cj == (ri & (tc - 1)) * nseq + lax.shift_right_logical(ri, lt), 1.0, 0.0).astype(BF16)

        bu_scr[1] = jnp.zeros(bu_scr.shape[1:], F32)
        u_scr[1] = jnp.zeros(u_scr.shape[1:], F32)

    last = pl.num_programs(0) - 1

    def pipeline(slot):

        def scan_chunk():
            u = jnp.concatenate([r[...] for r in u_refs], axis=0)
            u_scr[slot] = u
            u_t = _dot(perm_scr[0], u.astype(BF16)).astype(BF16)
            bu_scr[slot] = _dot(u_t, bb_scr[...])
            yield
            ar, ai = ab_scr[0], ab_scr[1]
            hr0, hi0 = h_scr[0], h_scr[1]
            hr, hi = hr0, hi0
            for t in range(tc):
                rows = slice(t * nseq, (t + 1) * nseq)
                hr, hi = (ar * hr - ai * hi + bu_scr[slot, rows, 0:SSM_FLAT],
                          ar * hi + ai * hr + bu_scr[slot, rows, SSM_FLAT:2 * SSM_FLAT])
                bu_scr[slot, rows, 0:SSM_FLAT] = hr
                bu_scr[slot, rows, SSM_FLAT:2 * SSM_FLAT] = hi
                if t % 8 == 7:
                    yield
            h_scr[0] = jnp.where(c < last, hr, hr0)
            h_scr[1] = jnp.where(c < last, hi, hi0)

        def emit_chunk():
            prev = 1 - slot
            y_t = _dot(bu_scr[prev].astype(BF16), ccat_ref[...])
            yield
            hi_p = y_t.astype(BF16)
            rest = y_t - hi_p.astype(F32)
            mid_p = rest.astype(BF16)
            lo_p = (rest - mid_p.astype(F32)).astype(BF16)
            to_seq = perm_scr[1]
            y = _dot(to_seq, hi_p) + _dot(to_seq, mid_p) + _dot(to_seq, lo_p) + d_ref[...] * u_scr[prev]
            yield
            z = jax.nn.gelu(y)
            gate = _dot(z.astype(BF16), gw_ref[...])
            yield
            out = z * jax.nn.sigmoid(gate + gb_ref[...])
            for b in range(nseq):
                y_ref[:, b * GROUP_WIDTH:(b + 1) * GROUP_WIDTH] = out[b * tc:(b + 1) * tc].astype(y_ref.dtype)

        _run_staged([scan_chunk(), emit_chunk()])

    for parity in range(2):
        pl.when(c % 2 == parity)(functools.partial(pipeline, parity))

    @pl.when(c == last)
    def _():
        hr_out[...] = h_scr[0]
        hi_out[...] = h_scr[1]


def s5_seq_mixer(proj2, lam_re, lam_im, log_dt, b_re_bd, b_im_bd, c_cat, d_skip, glu_w, glu_b, *, bsz, t_len):
    tc = S5_CHUNK
    ncols = PROJ_WIDTH // GROUP_WIDTH
    full = lambda shape: pl.BlockSpec(shape, lambda c: (0,) * len(shape))
    st = jax.ShapeDtypeStruct((bsz, SSM_FLAT), F32)
    nchunks = t_len // tc
    u_specs = [pl.BlockSpec((tc, GROUP_WIDTH), functools.partial(
                   lambda c, b: (jnp.minimum(c, nchunks - 1), b * ncols + COL_SSM), b=b)) for b in range(bsz)]
    return pl.pallas_call(
        functools.partial(_s5_seq_kernel, nseq=bsz),
        grid=(nchunks + 1,),
        in_specs=u_specs + [full((1, SSM_FLAT)), full((1, SSM_FLAT)), full((1, SSM_FLAT)),
                            full((GROUP_WIDTH, SSM_FLAT)), full((GROUP_WIDTH, SSM_FLAT)),
                            full((2 * SSM_FLAT, GROUP_WIDTH)), full((1, GROUP_WIDTH)),
                            full((GROUP_WIDTH, GROUP_WIDTH)), full((1, GROUP_WIDTH))],
        out_specs=[pl.BlockSpec((tc, bsz * GROUP_WIDTH), lambda c: (jnp.maximum(c - 1, 0), 0)),
                   full((bsz, SSM_FLAT)), full((bsz, SSM_FLAT))],
        out_shape=[jax.ShapeDtypeStruct((t_len, bsz * GROUP_WIDTH), BF16), st, st],
        scratch_shapes=[pltpu.VMEM((2, bsz, SSM_FLAT), F32), pltpu.VMEM((2, bsz * tc, 2 * SSM_FLAT), F32),
                        pltpu.VMEM((GROUP_WIDTH, 2 * SSM_FLAT), BF16), pltpu.VMEM((2, bsz, SSM_FLAT), F32),
                        pltpu.VMEM((2, bsz * tc, bsz * tc), BF16), pltpu.VMEM((2, bsz * tc, GROUP_WIDTH), F32)],
        compiler_params=_params(("arbitrary",)),
        name="s5_seq_mixer",
    )(*([proj2] * bsz), lam_re, lam_im, log_dt, b_re_bd, b_im_bd, c_cat, d_skip, glu_w, glu_b)


POOL_CHUNK = 1024
POOL_HIST = 16


def _pool_seq_kernel(u_ref, w_ref, sc_ref, y_ref, nbuf_ref, ext_scr):
    L, hist = u_ref.shape[0], POOL_HIST
    c = pl.program_id(1)

    @pl.when(c == 0)
    def _():
        ext_scr[0:hist] = jnp.zeros((hist, GROUP_WIDTH), F32)

    u = u_ref[...]
    ext_scr[hist:hist + L] = u
    e = ext_scr[...]
    a2 = e + pltpu.roll(e, 1, axis=0)
    a4 = a2 + pltpu.roll(a2, 2, axis=0)
    a8 = a4 + pltpu.roll(a4, 4, axis=0)
    a16 = a8 + pltpu.roll(a8, 8, axis=0)
    sums = (a2[hist:], a4[hist:], a8[hist:], a16[hist:])

    shape = (L, GROUP_WIDTH)
    pos = _iota(shape, 0) + c * L
    lane = _iota(shape, 1)
    pooled = None
    for gi in reversed(range(len(POOL_WINDOWS))):
        win = POOL_WINDOWS[gi]
        mean = sums[gi] / jnp.minimum(pos + 1, win).astype(F32)
        pooled = mean if pooled is None else jnp.where(lane < (gi + 1) * HEAD, mean, pooled)
    y_ref[...] = (_dot((pooled - u).astype(BF16), w_ref[...]) * sc_ref[...]).astype(y_ref.dtype)

    nb = ext_scr[L:L + hist]
    ext_scr[0:hist] = nb

    @pl.when(c == pl.num_programs(1) - 1)
    def _():
        nbuf_ref[...] = nb


def pool_seq_mixer(proj2, w_bd, scale, *, bsz, t_len):
    L, hist = min(POOL_CHUNK, t_len), POOL_HIST
    ncols = PROJ_WIDTH // GROUP_WIDTH
    return pl.pallas_call(
        _pool_seq_kernel,
        grid=(bsz, t_len // L),
        in_specs=[_seq_col_spec(L, COL_POOL, ncols),
                  pl.BlockSpec((GROUP_WIDTH, GROUP_WIDTH), lambda b, c: (0, 0)),
                  pl.BlockSpec((1, GROUP_WIDTH), lambda b, c: (0, 0))],
        out_specs=[_seq_col_spec(L, 0, 1), pl.BlockSpec((hist, GROUP_WIDTH), lambda b, c: (b, 0))],
        out_shape=[jax.ShapeDtypeStruct((t_len, bsz * GROUP_WIDTH), BF16),
                   jax.ShapeDtypeStruct((bsz * hist, GROUP_WIDTH), F32)],
        scratch_shapes=[pltpu.VMEM((hist + L, GROUP_WIDTH), F32)],
        compiler_params=_params(("parallel", "arbitrary")),
        name="pool_seq_mixer",
    )(proj2, w_bd, scale)


def _diag_heads(st, bsz):
    nh = GROUP_WIDTH // HEAD
    s = st.reshape(bsz, nh, HEAD, nh, HEAD)
    return jnp.stack([s[:, h, :, h, :] for h in range(nh)], axis=1)


def _mix_mlp_kernel(h_ref, ya_ref, yb_ref, yc_ref, yd_ref, wo_ref, g2_ref, wu_ref, wd_ref, gf_ref, o_ref,
                    h1_scr, xn_scr, acc_scr, *, final_norm):
    j = pl.program_id(1)
    gw = GROUP_WIDTH

    @pl.when(j == 0)
    def _():
        mix = (_dot(ya_ref[...], wo_ref[0:gw]) + _dot(yb_ref[...], wo_ref[gw:2 * gw])
               + _dot(yc_ref[...], wo_ref[2 * gw:3 * gw]) + _dot(yd_ref[...], wo_ref[3 * gw:4 * gw]))
        h1 = h_ref[...] + mix
        h1_scr[...] = h1
        xn_scr[...] = _rms(h1, g2_ref[...]).astype(BF16)
        acc_scr[...] = jnp.zeros_like(acc_scr)

    up = _dot(xn_scr[...], wu_ref[...])
    act = jnp.square(jnp.maximum(up, 0.0)).astype(BF16)
    acc_scr[...] += _dot(act, wd_ref[...])

    @pl.when(j == pl.num_programs(1) - 1)
    def _():
        out = h1_scr[...] + acc_scr[...]
        if final_norm:
            out = _rms(out, gf_ref[...])
        o_ref[...] = out


def mix_mlp(h, ys, w_out, g2, w_up, w_down, g_final, *, final_norm, nseq=0):
    n = h.shape[0]
    tm = min(512, n // max(nseq, 1))
    tf = 2048
    nt = n // tm // max(nseq, 1)
    row = lambda w: pl.BlockSpec((tm, w), lambda i, j: (i, 0))
    mix = pl.BlockSpec((tm, GROUP_WIDTH), _tile_map(nseq, nt))
    return pl.pallas_call(
        functools.partial(_mix_mlp_kernel, final_norm=final_norm),
        grid=(n // tm, D_FF // tf),
        in_specs=[row(D_MODEL), mix, mix, mix, mix,
                  pl.BlockSpec((D_MODEL, D_MODEL), lambda i, j: (0, 0)),
                  pl.BlockSpec((1, D_MODEL), lambda i, j: (0, 0)),
                  pl.BlockSpec((D_MODEL, tf), lambda i, j: (0, j)),
                  pl.BlockSpec((tf, D_MODEL), lambda i, j: (j, 0)),
                  pl.BlockSpec((1, D_MODEL), lambda i, j: (0, 0))],
        out_specs=row(D_MODEL),
        out_shape=jax.ShapeDtypeStruct((n, D_MODEL), F32),
        scratch_shapes=[pltpu.VMEM((tm, D_MODEL), F32), pltpu.VMEM((tm, D_MODEL), BF16),
                        pltpu.VMEM((tm, D_MODEL), F32)],
        compiler_params=_params(("parallel", "arbitrary")),
        name="mix_mlp",
    )(h, *ys, w_out, g2, w_up, w_down, g_final)


def _hgrn_state_in(s):
    b = s.shape[0]
    return s.reshape(b, 2, 2, HEAD, HEAD).transpose(0, 1, 4, 2, 3).reshape(b, 2, HEAD, LANES)


def _hgrn_state_out(s):
    b = s.shape[0]
    return s.reshape(b, 2, HEAD, 2, HEAD).transpose(0, 1, 3, 4, 2).reshape(b, 4, HEAD, HEAD)


def _wkv_state_in(s):
    b = s.shape[0]
    return s.reshape(b, 2, 2, HEAD, HEAD).transpose(0, 1, 3, 2, 4).reshape(b, 2, HEAD, LANES)


def _wkv_state_out(s):
    b = s.shape[0]
    return s.reshape(b, 2, HEAD, 2, HEAD).transpose(0, 1, 3, 2, 4).reshape(b, 4, HEAD, HEAD)


def _block_diag(blocks):
    g, r, c = blocks.shape
    tiled = jnp.tile(blocks.reshape(g * r, c), (1, g))
    row_blk = lax.broadcasted_iota(jnp.int32, tiled.shape, 0) // r
    col_blk = lax.broadcasted_iota(jnp.int32, tiled.shape, 1) // c
    return jnp.where(row_blk == col_blk, tiled, 0.0)


def _pad_rows(w, start):
    rows = GROUP_WIDTH - start - w.shape[0]
    return jnp.concatenate([jnp.zeros((start, GROUP_WIDTH), w.dtype), w, jnp.zeros((rows, GROUP_WIDTH), w.dtype)])


def _layer_params(l, P):
    row = lambda a: a.reshape(1, -1)
    q = {}
    q["norm1_g"] = row(P["norm1_g"][l])
    q["w_in"] = P["w_in"][l].astype(BF16)
    q["lam_re"] = row(P["ssm_lambda_re"][l])
    q["lam_im"] = row(P["ssm_lambda_im"][l])
    q["log_dt"] = row(jnp.repeat(P["ssm_log_dt"][l], SSM_STATE))
    q["b_re"] = _block_diag(P["ssm_b_re"][l].transpose(0, 2, 1))
    q["b_im"] = _block_diag(P["ssm_b_im"][l].transpose(0, 2, 1))
    q["c_cat"] = jnp.concatenate([_block_diag(P["ssm_c_re"][l]), -_block_diag(P["ssm_c_im"][l])],
                                 axis=1).T.astype(BF16)
    q["ssm_d"] = row(P["ssm_d"][l])
    q["glu_w"] = P["ssm_glu_w"][l].astype(BF16)
    q["glu_b"] = row(P["ssm_glu_b"][l])
    q["hgrn_norm_g"] = row(P["hgrn_norm_g"][l])
    q["mu"] = row(P["rwkv_mu"][l])
    for name in ("w0", "a0", "k_k", "k_a", "r_k", "ln_g", "ln_b"):
        q[name] = row(P["rwkv_" + name][l])
    q["w2p"] = _pad_rows(P["rwkv_w2"][l], 0).astype(BF16)
    q["a2p"] = _pad_rows(P["rwkv_a2"][l], DECAY_LORA).astype(BF16)
    q["g2p"] = _pad_rows(P["rwkv_g2"][l], DECAY_LORA + AAA_LORA).astype(BF16)
    q["pool_w"] = _block_diag(P["pool_w"][l]).astype(BF16)
    q["pool_scale"] = row(P["pool_scale"][l])
    q["w_out"] = P["w_out"][l].astype(BF16)
    q["norm2_g"] = row(P["norm2_g"][l])
    q["mlp_up"] = P["mlp_up"][l].astype(BF16)
    q["mlp_down"] = P["mlp_down"][l].astype(BF16)
    return q


def _trunk_fresh(x_rows, bsz, t_len, layer_params, P):
    h = x_rows
    new = [[] for _ in range(6)]
    g_final = P["norm_f_g"].reshape(1, -1)
    for l in range(DEPTH):
        q = layer_params[l]
        proj2 = rms_proj(h, q["norm1_g"], q["w_in"], nseq=bsz)
        y_a, s_re, s_im = s5_seq_mixer(proj2, q["lam_re"], q["lam_im"], q["log_dt"], q["b_re"], q["b_im"],
                                       q["c_cat"], q["ssm_d"], q["glu_w"], q["glu_b"], bsz=bsz, t_len=t_len)
        y_b, s_hg, y_c, s_wkv, s_sh = hgrn_rwkv_chunk_mixer(
            proj2, P["hgrn_lb_logits"], q["hgrn_norm_g"], q["mu"], q["w0"], q["a0"], q["k_k"], q["k_a"], q["r_k"],
            q["ln_g"], q["ln_b"], q["w2p"], q["a2p"], q["g2p"], bsz=bsz, t_len=t_len, layer=l)
        y_d, s_pool = pool_seq_mixer(proj2, q["pool_w"], q["pool_scale"], bsz=bsz, t_len=t_len)
        h = mix_mlp(h, (y_a, y_b, y_c, y_d), q["w_out"], q["norm2_g"], q["mlp_up"], q["mlp_down"], g_final,
                    final_norm=(l == DEPTH - 1), nseq=bsz)
        s_hg = _diag_heads(s_hg, bsz).swapaxes(-1, -2)
        s_wkv = _diag_heads(s_wkv, bsz)
        s_pool = s_pool.reshape(bsz, POOL_HIST, GROUP_WIDTH)[:, POOL_HIST - POOL_BUF:]
        for lst, s in zip(new, (s_re, s_im, s_hg, s_wkv, s_sh, s_pool)):
            lst.append(s)
    return h, new


def _trunk_carry(x_rows, states, pos0, t_len, layer_params, P):
    ssm_re0, ssm_im0, hgrn0, wkv0, shift0, pool0 = states
    h = x_rows
    new = [[] for _ in range(6)]
    g_final = P["norm_f_g"].reshape(1, -1)
    tc = t_len
    for l in range(DEPTH):
        q = layer_params[l]
        proj = rms_proj(h, q["norm1_g"], q["w_in"])
        y_a, s_re, s_im = s5_mixer(proj, ssm_re0[l], ssm_im0[l], q["lam_re"], q["lam_im"], q["log_dt"], q["b_re"],
                                   q["b_im"], q["c_cat"], q["ssm_d"], q["glu_w"], q["glu_b"], t_len=t_len, tc=tc)
        y_b, s_hg = hgrn_mixer(proj, hgrn0[l], P["hgrn_lb_logits"], q["hgrn_norm_g"], t_len=t_len, tc=tc, layer=l)
        y_c, s_wkv, s_sh = rwkv_mixer(proj, shift0[l], wkv0[l], q["mu"], q["w0"], q["a0"], q["k_k"], q["k_a"],
                                      q["r_k"], q["ln_g"], q["ln_b"], q["w2p"], q["a2p"], q["g2p"],
                                      t_len=t_len, tc=tc)
        y_d, s_pool = pool_mixer(proj, pool0[l], q["pool_w"], q["pool_scale"], t_len=t_len, tc=tc, pos0=pos0)
        h = mix_mlp(h, (y_a, y_b, y_c, y_d), q["w_out"], q["norm2_g"], q["mlp_up"], q["mlp_down"], g_final,
                    final_norm=(l == DEPTH - 1))
        for lst, s in zip(new, (s_re, s_im, _hgrn_state_out(s_hg), _wkv_state_out(s_wkv), s_sh,
                                s_pool.transpose(1, 0, 2))):
            lst.append(s)
    return h, new


def _states_out(new, bsz):
    s_re, s_im, s_hg, s_wkv, s_sh, s_pool = new
    return (jnp.stack([s.reshape(bsz, SSM_GROUPS, SSM_STATE) for s in s_re]),
            jnp.stack([s.reshape(bsz, SSM_GROUPS, SSM_STATE) for s in s_im]),
            jnp.stack(s_hg),
            jnp.stack(s_wkv),
            jnp.stack([s.reshape(bsz, 1, RWKV_PROJ) for s in s_sh]),
            jnp.stack(s_pool))


def kernel(x_prompt, x_sample, state_ssm_re, state_ssm_im, state_hgrn, state_wkv, state_shift, state_pool, norm1_g, w_in, ssm_lambda_re, ssm_lambda_im, ssm_log_dt, ssm_b_re, ssm_b_im, ssm_c_re, ssm_c_im, ssm_d, ssm_glu_w, ssm_glu_b, hgrn_lb_logits, hgrn_norm_g, rwkv_mu, rwkv_w0, rwkv_w2, rwkv_a0, rwkv_a2, rwkv_g2, rwkv_k_k, rwkv_k_a, rwkv_r_k, rwkv_ln_g, rwkv_ln_b, pool_w, pool_scale, w_out, norm2_g, mlp_up, mlp_down, norm_f_g):
    P = dict(norm1_g=norm1_g, w_in=w_in, ssm_lambda_re=ssm_lambda_re, ssm_lambda_im=ssm_lambda_im,
             ssm_log_dt=ssm_log_dt, ssm_b_re=ssm_b_re, ssm_b_im=ssm_b_im, ssm_c_re=ssm_c_re, ssm_c_im=ssm_c_im,
             ssm_d=ssm_d, ssm_glu_w=ssm_glu_w, ssm_glu_b=ssm_glu_b, hgrn_lb_logits=hgrn_lb_logits,
             hgrn_norm_g=hgrn_norm_g, rwkv_mu=rwkv_mu, rwkv_w0=rwkv_w0, rwkv_w2=rwkv_w2, rwkv_a0=rwkv_a0,
             rwkv_a2=rwkv_a2, rwkv_g2=rwkv_g2, rwkv_k_k=rwkv_k_k, rwkv_k_a=rwkv_k_a, rwkv_r_k=rwkv_r_k,
             rwkv_ln_g=rwkv_ln_g, rwkv_ln_b=rwkv_ln_b, pool_w=pool_w, pool_scale=pool_scale, w_out=w_out,
             norm2_g=norm2_g, mlp_up=mlp_up, mlp_down=mlp_down, norm_f_g=norm_f_g)
    layer_params = [_layer_params(l, P) for l in range(DEPTH)]

    bp, t_p, _ = x_prompt.shape
    yp, new_p = _trunk_fresh(x_prompt.reshape(bp * t_p, D_MODEL), bp, t_p, layer_params, P)
    y_prompt = yp.reshape(bp, t_p, D_MODEL)

    bs, t_s, _ = x_sample.shape
    nblk = bs // SEQ_BLK
    xs = x_sample.reshape(nblk, SEQ_BLK, t_s, D_MODEL).transpose(0, 2, 1, 3).reshape(bs * t_s, D_MODEL)
    st_s = ([state_ssm_re[l].reshape(bs, SSM_FLAT) for l in range(DEPTH)],
            [state_ssm_im[l].reshape(bs, SSM_FLAT) for l in range(DEPTH)],
            [_hgrn_state_in(state_hgrn[l]) for l in range(DEPTH)],
            [_wkv_state_in(state_wkv[l]) for l in range(DEPTH)],
            [state_shift[l].reshape(bs, RWKV_PROJ) for l in range(DEPTH)],
            [state_pool[l].transpose(1, 0, 2) for l in range(DEPTH)])
    ys, new_s = _trunk_carry(xs, st_s, PAST_LEN, t_s, layer_params, P)
    y_sample = ys.reshape(nblk, t_s, SEQ_BLK, D_MODEL).transpose(0, 2, 1, 3).reshape(bs, t_s, D_MODEL)

    return (y_prompt, y_sample) + _states_out(new_p, bp) + _states_out(new_s, bs)
```

```python
import functools
import itertools

import jax
import jax.numpy as jnp
from jax import lax
from jax.experimental import pallas as pl
from jax.experimental.pallas import tpu as pltpu

F32 = jnp.float32
BF16 = jnp.bfloat16

D_MODEL = 1024
DEPTH = 2
PAST_LEN = 16384
GROUP_WIDTH = 256
HEAD = 64
SSM_GROUPS = 16
SSM_CH = 16
SSM_STATE = 64
SSM_FLAT = SSM_GROUPS * SSM_STATE
POOL_WINDOWS = (2, 4, 8, 16)
POOL_BUF = 15
DECAY_LORA = 64
AAA_LORA = 64
GATE_LORA = 128
RWKV_PROJ = 1024
PROJ_WIDTH = 2560
D_FF = 4096
NORM_EPS = 1e-6
HGRN_NORM_EPS = 1e-5
RWKV_GN_EPS = 64e-5

SEQ_BLK = 8
LANES = 128
VMEM_LIMIT = 48 * 1024 * 1024

COL_SSM, COL_Q, COL_F, COL_I, COL_G, COL_R, COL_K, COL_V, COL_LORA, COL_POOL = range(10)


def _params(sem):
    return pltpu.CompilerParams(dimension_semantics=sem, vmem_limit_bytes=VMEM_LIMIT)


def _dot(a, b):
    return jnp.dot(a, b, preferred_element_type=F32)


def _rms(x, g):
    return x * lax.rsqrt(jnp.mean(x * x, axis=-1, keepdims=True) + NORM_EPS) * g


def _rms_proj_kernel(x_ref, g_ref, w_ref, o_ref):
    o_ref[...] = _dot(_rms(x_ref[...], g_ref[...]).astype(BF16), w_ref[...])


def _tile_map(nseq, nt):
    if nseq == 0:
        return lambda r, *_: (r, 0)
    return lambda r, *_: (r % nt, r // nt)


def rms_proj(x, g, w, *, nseq=0):
    n = x.shape[0]
    tm = min(1024, n // max(nseq, 1))
    nt = n // tm // max(nseq, 1)
    out_shape = (n, PROJ_WIDTH) if nseq == 0 else (n // nseq, nseq * PROJ_WIDTH)
    return pl.pallas_call(
        _rms_proj_kernel,
        grid=(n // tm,),
        in_specs=[pl.BlockSpec((tm, D_MODEL), lambda i: (i, 0)),
                  pl.BlockSpec((1, D_MODEL), lambda i: (0, 0)),
                  pl.BlockSpec((D_MODEL, PROJ_WIDTH), lambda i: (0, 0))],
        out_specs=pl.BlockSpec((tm, PROJ_WIDTH), _tile_map(nseq, nt)),
        out_shape=jax.ShapeDtypeStruct(out_shape, F32),
        compiler_params=_params(("parallel",)),
        name="rms_proj",
    )(x, g, w)


def _row_spec(tc, col, nchunks):
    return pl.BlockSpec((tc * SEQ_BLK, GROUP_WIDTH), lambda s, c: (s * nchunks + c, col))


def _full_spec(shape):
    nd = len(shape)
    return pl.BlockSpec(shape, lambda s, c: (0,) * nd)


def _seq_spec(shape):
    nd = len(shape)
    return pl.BlockSpec((SEQ_BLK,) + shape[1:], lambda s, c: (s,) + (0,) * (nd - 1))


def _head_sums(x):
    lane = lax.broadcasted_iota(jnp.int32, x.shape, 1)
    out = jnp.zeros_like(x)
    for h in range(GROUP_WIDTH // HEAD):
        m = (lane >= h * HEAD) & (lane < (h + 1) * HEAD)
        s = jnp.sum(jnp.where(m, x, 0.0), axis=1, keepdims=True)
        out = jnp.where(m, s, out)
    return out


UNITS = SEQ_BLK * 2


def _unit_masks():
    r, c = _iota((2 * LANES, LANES), 0), _iota((2 * LANES, LANES), 1)
    ones2 = jnp.where(_head_of(r & (LANES - 1)) == _head_of(c), 1.0, 0.0).astype(BF16)
    shape = (UNITS * HEAD, LANES)
    eye = (_iota(shape, 1) & (HEAD - 1)) == (_iota(shape, 0) & (HEAD - 1))
    return ones2, eye


def _seg_sum_mxu(p, ones2):
    hi = p.astype(BF16)
    lo = (p - hi.astype(F32)).astype(BF16)
    return _dot(jnp.concatenate([hi, lo], axis=1), ones2)


def _unit_rows(ref, t):
    return jnp.concatenate([jnp.broadcast_to(ref[t, b:b + 1, p * LANES:(p + 1) * LANES], (HEAD, LANES))
                            for b in range(SEQ_BLK) for p in range(2)], axis=0)


def _load_unit_tiles(s0_ref, s_scr, *, transpose):
    for b in range(SEQ_BLK):
        for p in range(2):
            heads = [s0_ref[b, 2 * p + j] for j in range(2)]
            s_scr[b, p] = jnp.concatenate([h.T if transpose else h for h in heads], axis=1)


def _store_unit_tiles(st_ref, s_scr, *, transpose):
    for b in range(SEQ_BLK):
        for p in range(2):
            tile = s_scr[b, p]
            for j in range(2):
                h = tile[:, j * HEAD:(j + 1) * HEAD]
                st_ref[b, 2 * p + j] = h.T if transpose else h


def _store_unit_rows(ref, t, cols, eye):
    picked = jnp.where(eye, cols, 0.0)
    for b in range(SEQ_BLK):
        for p in range(2):
            u = b * 2 + p
            ref[t, b:b + 1, p * LANES:(p + 1) * LANES] = jnp.sum(picked[u * HEAD:(u + 1) * HEAD], axis=0, keepdims=True)


def _s5_kernel(u_ref, h0r_ref, h0i_ref, lr_ref, li_ref, ldt_ref, bre_ref, bim_ref, ccat_ref, d_ref,
               gw_ref, gb_ref, y_ref, hr_out, hi_out, h_scr, bu_scr, *, tc):
    c = pl.program_id(1)

    @pl.when(c == 0)
    def _():
        h_scr[0] = h0r_ref[...]
        h_scr[1] = h0i_ref[...]

    lr, li = lr_ref[...], li_ref[...]
    dt = jnp.exp(ldt_ref[...])
    mag = jnp.exp(lr * dt)
    ab_re, ab_im = mag * jnp.cos(li * dt), mag * jnp.sin(li * dt)
    den = lr * lr + li * li
    zr, zi = ab_re - 1.0, ab_im
    cr = (zr * lr + zi * li) / den
    ci = (zi * lr - zr * li) / den
    bre, bim = bre_ref[...], bim_ref[...]
    bb_re = (cr * bre - ci * bim).astype(BF16)
    bb_im = (cr * bim + ci * bre).astype(BF16)

    u = u_ref[...]
    ub = u.astype(BF16)
    bu_scr[:, 0:SSM_FLAT] = _dot(ub, bb_re)
    bu_scr[:, SSM_FLAT:2 * SSM_FLAT] = _dot(ub, bb_im)

    ar = jnp.broadcast_to(ab_re, (SEQ_BLK, SSM_FLAT))
    ai = jnp.broadcast_to(ab_im, (SEQ_BLK, SSM_FLAT))

    def step(t, carry):
        hr, hi = carry
        rows = pl.ds(pl.multiple_of(t * SEQ_BLK, SEQ_BLK), SEQ_BLK)
        nhr = ar * hr - ai * hi + bu_scr[rows, 0:SSM_FLAT]
        nhi = ar * hi + ai * hr + bu_scr[rows, SSM_FLAT:2 * SSM_FLAT]
        bu_scr[rows, 0:SSM_FLAT] = nhr
        bu_scr[rows, SSM_FLAT:2 * SSM_FLAT] = nhi
        return nhr, nhi

    hr, hi = lax.fori_loop(0, tc, step, (h_scr[0], h_scr[1]))
    h_scr[0] = hr
    h_scr[1] = hi

    y = _dot(bu_scr[...].astype(BF16), ccat_ref[...]) + d_ref[...] * u
    z = jax.nn.gelu(y)
    out = z * jax.nn.sigmoid(_dot(z.astype(BF16), gw_ref[...]) + gb_ref[...])
    y_ref[...] = out.astype(y_ref.dtype)

    @pl.when(c == pl.num_programs(1) - 1)
    def _():
        hr_out[...] = hr
        hi_out[...] = hi


def s5_mixer(proj, h0_re, h0_im, lam_re, lam_im, log_dt, b_re_bd, b_im_bd, c_cat, d_skip, glu_w, glu_b, *, t_len, tc):
    nseq = h0_re.shape[0] // SEQ_BLK
    nchunks = t_len // tc
    n = proj.shape[0]
    st = jax.ShapeDtypeStruct(h0_re.shape, F32)
    return pl.pallas_call(
        functools.partial(_s5_kernel, tc=tc),
        grid=(nseq, nchunks),
        in_specs=[_row_spec(tc, COL_SSM, nchunks),
                  _seq_spec(h0_re.shape), _seq_spec(h0_im.shape),
                  _full_spec((1, SSM_FLAT)), _full_spec((1, SSM_FLAT)), _full_spec((1, SSM_FLAT)),
                  _full_spec((GROUP_WIDTH, SSM_FLAT)), _full_spec((GROUP_WIDTH, SSM_FLAT)),
                  _full_spec((2 * SSM_FLAT, GROUP_WIDTH)), _full_spec((1, GROUP_WIDTH)),
                  _full_spec((GROUP_WIDTH, GROUP_WIDTH)), _full_spec((1, GROUP_WIDTH))],
        out_specs=[_row_spec(tc, 0, nchunks), _seq_spec(h0_re.shape), _seq_spec(h0_im.shape)],
        out_shape=[jax.ShapeDtypeStruct((n, GROUP_WIDTH), BF16), st, st],
        scratch_shapes=[pltpu.VMEM((2, SEQ_BLK, SSM_FLAT), F32),
                        pltpu.VMEM((tc * SEQ_BLK, 2 * SSM_FLAT), F32)],
        compiler_params=_params(("parallel", "arbitrary")),
        name="s5_mixer",
    )(proj, h0_re, h0_im, lam_re, lam_im, log_dt, b_re_bd, b_im_bd, c_cat, d_skip, glu_w, glu_b)


def _pool_kernel(u_ref, buf_ref, w_ref, sc_ref, y_ref, nbuf_ref, ext_scr, *, tc, pos0):
    c = pl.program_id(1)

    @pl.when(c == 0)
    def _():
        ext_scr[0:POOL_BUF] = buf_ref[...]

    u = u_ref[...].reshape(tc, SEQ_BLK, GROUP_WIDTH)
    ext_scr[POOL_BUF:POOL_BUF + tc] = u
    a1 = ext_scr[...]
    a2 = a1[1:] + a1[:-1]
    a4 = a2[2:] + a2[:-2]
    a8 = a4[4:] + a4[:-4]
    a16 = a8[8:] + a8[:-8]
    sums = (a2[14:], a4[12:], a8[8:], a16)

    shape = (tc, SEQ_BLK, GROUP_WIDTH)
    pos = lax.broadcasted_iota(jnp.int32, shape, 0) + (c * tc + pos0)
    lane = lax.broadcasted_iota(jnp.int32, shape, 2)
    pooled = None
    for gi in reversed(range(len(POOL_WINDOWS))):
        win = POOL_WINDOWS[gi]
        mean = sums[gi] / jnp.minimum(pos + 1, win).astype(F32)
        pooled = mean if pooled is None else jnp.where(lane < (gi + 1) * HEAD, mean, pooled)
    pooled = (pooled - u).reshape(tc * SEQ_BLK, GROUP_WIDTH)
    y_ref[...] = (_dot(pooled.astype(BF16), w_ref[...]) * sc_ref[...]).astype(y_ref.dtype)

    nb = ext_scr[tc:tc + POOL_BUF]
    ext_scr[0:POOL_BUF] = nb

    @pl.when(c == pl.num_programs(1) - 1)
    def _():
        nbuf_ref[...] = nb


def pool_mixer(proj, buf, w_bd, scale, *, t_len, tc, pos0):
    nseq = buf.shape[1] // SEQ_BLK
    nchunks = t_len // tc
    n = proj.shape[0]
    buf_spec = pl.BlockSpec((POOL_BUF, SEQ_BLK, GROUP_WIDTH), lambda s, c: (0, s, 0))
    return pl.pallas_call(
        functools.partial(_pool_kernel, tc=tc, pos0=pos0),
        grid=(nseq, nchunks),
        in_specs=[_row_spec(tc, COL_POOL, nchunks), buf_spec,
                  _full_spec((GROUP_WIDTH, GROUP_WIDTH)), _full_spec((1, GROUP_WIDTH))],
        out_specs=[_row_spec(tc, 0, nchunks), buf_spec],
        out_shape=[jax.ShapeDtypeStruct((n, GROUP_WIDTH), BF16), jax.ShapeDtypeStruct(buf.shape, F32)],
        scratch_shapes=[pltpu.VMEM((tc + POOL_BUF, SEQ_BLK, GROUP_WIDTH), F32)],
        compiler_params=_params(("parallel", "arbitrary")),
        name="pool_mixer",
    )(proj, buf, w_bd, scale)


def _hgrn_lower_bound(logits_ref, layer):
    rows = [logits_ref[l:l + 1, :] for l in range(DEPTH)]
    m = functools.reduce(jnp.maximum, rows)
    es = [jnp.exp(r - m) for r in rows]
    tot = functools.reduce(lambda a, b: a + b, es)
    lb = jnp.zeros_like(m)
    for l in range(1, layer + 1):
        lb = lb + es[l] / tot
    return lb


def _hgrn_kernel(pq_ref, pf_ref, pi_ref, pg_ref, s0_ref, lbl_ref, ng_ref, y_ref, st_ref,
                 s_scr, q_scr, f_scr, k_scr, v_scr, o_scr, *, tc, layer):
    c = pl.program_id(1)
    shape3 = (tc, SEQ_BLK, GROUP_WIDTH)

    @pl.when(c == 0)
    def _():
        _load_unit_tiles(s0_ref, s_scr, transpose=True)

    lb = _hgrn_lower_bound(lbl_ref, layer)
    zf = pf_ref[...]
    f_scr[...] = (lb + (1.0 - lb) * jax.nn.sigmoid(zf)).reshape(shape3)
    k_scr[...] = ((1.0 - lb) * jax.nn.sigmoid(-zf)).reshape(shape3)
    q_scr[...] = jax.nn.silu(pq_ref[...]).reshape(shape3)
    v_scr[...] = pi_ref[...].reshape(shape3)

    ones2, eye = _unit_masks()
    s = s_scr[...].reshape(UNITS * HEAD, LANES)
    for t in range(tc):
        vcol = _seg_sum_mxu(jnp.where(eye, _unit_rows(v_scr, t), 0.0), ones2)
        s = s * _unit_rows(f_scr, t) + vcol * _unit_rows(k_scr, t)
        _store_unit_rows(o_scr, t, _seg_sum_mxu(s * _unit_rows(q_scr, t), ones2), eye)
    s_scr[...] = s.reshape(s_scr.shape)

    o = o_scr[...].reshape(tc * SEQ_BLK, GROUP_WIDTH)
    ms = _head_sums(o * o) * (1.0 / HEAD)
    out = o * lax.rsqrt(ms + HGRN_NORM_EPS) * ng_ref[...] * jax.nn.silu(pg_ref[...])
    y_ref[...] = out.astype(y_ref.dtype)

    @pl.when(c == pl.num_programs(1) - 1)
    def _():
        _store_unit_tiles(st_ref, s_scr, transpose=True)


def hgrn_mixer(proj, s0, lb_logits, norm_g, *, t_len, tc, layer):
    nseq = s0.shape[0] // SEQ_BLK
    nchunks = t_len // tc
    n = proj.shape[0]
    tile = pltpu.VMEM((tc, SEQ_BLK, GROUP_WIDTH), F32)
    return pl.pallas_call(
        functools.partial(_hgrn_kernel, tc=tc, layer=layer),
        grid=(nseq, nchunks),
        in_specs=[_row_spec(tc, COL_Q, nchunks), _row_spec(tc, COL_F, nchunks),
                  _row_spec(tc, COL_I, nchunks), _row_spec(tc, COL_G, nchunks),
                  _seq_spec(s0.shape), _full_spec((DEPTH, GROUP_WIDTH)), _full_spec((1, GROUP_WIDTH))],
        out_specs=[_row_spec(tc, 0, nchunks), _seq_spec(s0.shape)],
        out_shape=[jax.ShapeDtypeStruct((n, GROUP_WIDTH), BF16), jax.ShapeDtypeStruct(s0.shape, F32)],
        scratch_shapes=[pltpu.VMEM((SEQ_BLK, 2, HEAD, LANES), F32), tile, tile, tile, tile, tile],
        compiler_params=_params(("parallel", "arbitrary")),
        name="hgrn_mixer",
    )(proj, proj, proj, proj, s0, lb_logits, norm_g)


def _rwkv_kernel(pr_ref, pk_ref, pv_ref, pl_ref, sh0_ref, s0_ref, mu_ref, w0_ref, a0_ref, kk_ref, ka_ref,
                 rk_ref, lng_ref, lnb_ref, w2_ref, a2_ref, g2_ref, y_ref, st_ref, sh_ref,
                 s_scr, prev_scr, r_scr, w_scr, k_scr, v_scr, nkk_scr, kka_scr, o_scr, *, tc):
    c = pl.program_id(1)
    shape3 = (tc, SEQ_BLK, GROUP_WIDTH)
    gw = GROUP_WIDTH

    @pl.when(c == 0)
    def _():
        _load_unit_tiles(s0_ref, s_scr, transpose=False)
        prev_scr[...] = sh0_ref[...]

    def shifted(ref, j):
        x = ref[...].reshape(shape3)
        first = prev_scr[:, j * gw:(j + 1) * gw].reshape(1, SEQ_BLK, gw)
        prev = first if tc == 1 else jnp.concatenate([first, x[:-1]], axis=0)
        prev_scr[:, j * gw:(j + 1) * gw] = x[tc - 1]
        return (x + (prev - x) * mu_ref[:, j * gw:(j + 1) * gw]).reshape(tc * SEQ_BLK, gw)

    xr, xk, xv, xl = shifted(pr_ref, 0), shifted(pk_ref, 1), shifted(pv_ref, 2), shifted(pl_ref, 3)
    w = -jax.nn.softplus(-(w0_ref[...] + _dot(jnp.tanh(xl).astype(BF16), w2_ref[...]))) - 0.5
    decay = jnp.exp(-jnp.exp(w))
    a = jax.nn.sigmoid(a0_ref[...] + _dot(xl.astype(BF16), a2_ref[...]))
    g = _dot(jax.nn.sigmoid(xl).astype(BF16), g2_ref[...])
    kk = xk * kk_ref[...]
    kk = kk / jnp.maximum(jnp.sqrt(_head_sums(kk * kk)), 1e-12)
    k = xk * (1.0 + (a - 1.0) * ka_ref[...])

    r_scr[...] = xr.reshape(shape3)
    w_scr[...] = decay.reshape(shape3)
    k_scr[...] = k.reshape(shape3)
    v_scr[...] = xv.reshape(shape3)
    nkk_scr[...] = (-kk).reshape(shape3)
    kka_scr[...] = (kk * a).reshape(shape3)

    ones2, eye = _unit_masks()
    s = s_scr[...].reshape(UNITS * HEAD, LANES)
    for t in range(tc):
        sa = _seg_sum_mxu(s * _unit_rows(nkk_scr, t), ones2)
        vcol = _seg_sum_mxu(jnp.where(eye, _unit_rows(v_scr, t), 0.0), ones2)
        s = s * _unit_rows(w_scr, t) + sa * _unit_rows(kka_scr, t) + vcol * _unit_rows(k_scr, t)
        _store_unit_rows(o_scr, t, _seg_sum_mxu(s * _unit_rows(r_scr, t), ones2), eye)
    s_scr[...] = s.reshape(s_scr.shape)

    y = o_scr[...].reshape(tc * SEQ_BLK, gw)
    mean = _head_sums(y) * (1.0 / HEAD)
    d = y - mean
    var = _head_sums(d * d) * (1.0 / HEAD)
    yn = d * lax.rsqrt(var + RWKV_GN_EPS) * lng_ref[...] + lnb_ref[...]
    bonus = _head_sums(xr * k * rk_ref[...]) * xv
    y_ref[...] = ((yn + bonus) * g).astype(y_ref.dtype)

    @pl.when(c == pl.num_programs(1) - 1)
    def _():
        _store_unit_tiles(st_ref, s_scr, transpose=False)
        sh_ref[...] = prev_scr[...]


def rwkv_mixer(proj, shift0, s0, mu, w0, a0, k_k, k_a, r_k, ln_g, ln_b, w2p, a2p, g2p, *, t_len, tc):
    nseq = s0.shape[0] // SEQ_BLK
    nchunks = t_len // tc
    n = proj.shape[0]
    tile = pltpu.VMEM((tc, SEQ_BLK, GROUP_WIDTH), F32)
    vec = _full_spec((1, GROUP_WIDTH))
    mat = _full_spec((GROUP_WIDTH, GROUP_WIDTH))
    return pl.pallas_call(
        functools.partial(_rwkv_kernel, tc=tc),
        grid=(nseq, nchunks),
        in_specs=[_row_spec(tc, COL_R, nchunks), _row_spec(tc, COL_K, nchunks),
                  _row_spec(tc, COL_V, nchunks), _row_spec(tc, COL_LORA, nchunks),
                  _seq_spec(shift0.shape), _seq_spec(s0.shape), _full_spec((1, RWKV_PROJ)),
                  vec, vec, vec, vec, vec, vec, vec, mat, mat, mat],
        out_specs=[_row_spec(tc, 0, nchunks), _seq_spec(s0.shape), _seq_spec(shift0.shape)],
        out_shape=[jax.ShapeDtypeStruct((n, GROUP_WIDTH), BF16), jax.ShapeDtypeStruct(s0.shape, F32),
                   jax.ShapeDtypeStruct(shift0.shape, F32)],
        scratch_shapes=[pltpu.VMEM((SEQ_BLK, 2, HEAD, LANES), F32), pltpu.VMEM((SEQ_BLK, RWKV_PROJ), F32),
                        tile, tile, tile, tile, tile, tile, tile],
        compiler_params=_params(("parallel", "arbitrary")),
        name="rwkv_mixer",
    )(proj, proj, proj, proj, shift0, s0, mu, w0, a0, k_k, k_a, r_k, ln_g, ln_b, w2p, a2p, g2p)


def _dot_nt(a, b):
    return lax.dot_general(a, b, (((1,), (1,)), ((), ())), preferred_element_type=F32)


def _dot_tn(a, b):
    return lax.dot_general(a, b, (((0,), (0,)), ((), ())), preferred_element_type=F32)


def _iota(shape, dim):
    return lax.broadcasted_iota(jnp.int32, shape, dim)


def _head_of(idx):
    return lax.shift_right_logical(idx, HEAD.bit_length() - 1)


def _cumsum_rows(x):
    n = x.shape[0]
    tri = jnp.where(_iota((n, n), 0) >= _iota((n, n), 1), 1.0, 0.0).astype(BF16)
    hi = x.astype(BF16)
    rest = x - hi.astype(F32)
    mid = rest.astype(BF16)
    lo = (rest - mid.astype(F32)).astype(BF16)
    return _dot(tri, hi) + _dot(tri, mid) + _dot(tri, lo)


def _own_head(shape, rows_per_head):
    row_h = lax.shift_right_logical(_iota(shape, 0), rows_per_head.bit_length() - 1)
    return row_h == _head_of(_iota(shape, 1))


def _head_expand(x):
    xx = jnp.concatenate([x] * (GROUP_WIDTH // HEAD), axis=0)
    return jnp.where(_own_head(xx.shape, x.shape[0]), xx, 0.0)


def _head_collapse(xx):
    n = xx.shape[0] // (GROUP_WIDTH // HEAD)
    return xx[0:n] + xx[n:2 * n] + xx[2 * n:3 * n] + xx[3 * n:4 * n]


def _block_diag_mask():
    shape = (GROUP_WIDTH, GROUP_WIDTH)
    return _head_of(_iota(shape, 0)) == _head_of(_iota(shape, 1))


def _seq_col_spec(rows, col, ncols):
    return pl.BlockSpec((rows, GROUP_WIDTH), lambda b, c: (c, b * ncols + col))


HGRN_CHUNK = 128
HGRN_SUB = 16


HGRN_SEQS_PER_STEP = 4


def _run_staged(stages):
    for _ in itertools.zip_longest(*stages):
        pass


def _hgrn_chunk_kernel(*refs, nseq, layer):
    lbl_ref, ng_ref, y_ref, st_ref, w_scr, k_scr, b_scr, v_scr, p_scr, o_scr = refs[4 * nseq:]
    gw = GROUP_WIDTH
    c = pl.program_id(1)

    @pl.when(c == 0)
    def _():
        w_scr[...] = jnp.zeros_like(w_scr)
        p_scr[...] = jnp.zeros_like(p_scr)

    _run_staged([_hgrn_chunk_one(*refs[4 * s:4 * s + 4], lbl_ref, ng_ref, y_ref.at[:, s * gw:(s + 1) * gw],
                                 w_scr.at[s], k_scr.at[s], b_scr.at[s], v_scr.at[s], p_scr.at[s], o_scr.at[s],
                                 layer=layer) for s in range(nseq)])

    @pl.when(c == pl.num_programs(1) - 1)
    def _():
        for s in range(nseq):
            st_ref[s * gw:(s + 1) * gw, :] = w_scr[s]


def _hgrn_chunk_one(pq_ref, pf_ref, pi_ref, pg_ref, lbl_ref, ng_ref, y_ref, w_scr, k_scr, b_scr, v_scr, p_scr,
                    o_scr, *, layer):
    L, n = HGRN_CHUNK, HGRN_SUB
    half = n // 2
    lb = _hgrn_lower_bound(lbl_ref, layer)
    z = pf_ref[...]
    g = jnp.logaddexp(jnp.log1p(-lb) + jax.nn.log_sigmoid(z), jnp.log(lb))
    kg = (1.0 - lb) * jax.nn.sigmoid(-z)
    q = jax.nn.silu(pq_ref[...])
    v = pi_ref[...]
    bc = _cumsum_rows(g)
    k_scr[...] = kg
    b_scr[...] = bc
    v_scr[...] = v
    yield

    bd = _block_diag_mask()
    ones_bd = jnp.where(bd, 1.0, 0.0).astype(BF16)
    rid = _iota((n, GROUP_WIDTH), 0)
    rid_lo = _iota((half, GROUP_WIDTH), 0) + half
    for sb in range(L // n):
        base = sb * n
        qs, bs = q[base:base + n], bc[base:base + n]
        q_lo, b_lo = q[base + half:base + n], bc[base + half:base + n]
        for s in range(n):
            ks, bsrow = k_scr[base + s:base + s + 1, :], b_scr[base + s:base + s + 1, :]
            if s < half:
                p_scr[s * n:(s + 1) * n, :] = qs * ks * jnp.where(rid >= s, jnp.exp(bs - bsrow), 0.0)
            else:
                p_scr[s * n + half:(s + 1) * n, :] = q_lo * ks * jnp.where(rid_lo >= s, jnp.exp(b_lo - bsrow), 0.0)
        r = _dot(p_scr[...].astype(BF16), ones_bd)
        acc = jnp.zeros((n, GROUP_WIDTH), F32)
        for s in range(n):
            acc = acc + r[s * n:(s + 1) * n] * v_scr[base + s:base + s + 1, :]
        o_scr[base:base + n, :] = acc
        yield

    vb = v.astype(BF16)
    for i in range(1, L // n):
        r0 = i * n
        ref = b_scr[r0 - 1:r0, :]
        qt = q[r0:r0 + n] * jnp.exp(bc[r0:r0 + n] - ref)
        kt = jnp.concatenate([kg[:r0] * jnp.exp(ref - bc[:r0]), jnp.zeros((L - r0, GROUP_WIDTH), F32)], axis=0)
        att = _dot_nt(_head_expand(qt).astype(BF16), kt.astype(BF16))
        yield
        ox = _dot(att.astype(BF16), vb)
        o_scr[r0:r0 + n, :] += _head_collapse(jnp.where(_own_head(ox.shape, n), ox, 0.0))
        yield

    w = w_scr[...]
    o = o_scr[...] + _dot_nt((q * jnp.exp(bc)).astype(BF16), w.astype(BF16))
    b_end = b_scr[L - 1:L, :]
    upd = _dot_tn(vb, (kg * jnp.exp(b_end - bc)).astype(BF16))
    w_scr[...] = w * jnp.exp(b_end) + jnp.where(bd, upd, 0.0)
    yield

    ms = _head_sums(o * o) * (1.0 / HEAD)
    out = o * lax.rsqrt(ms + HGRN_NORM_EPS) * ng_ref[...] * jax.nn.silu(pg_ref[...])
    y_ref[...] = out.astype(y_ref.dtype)


def _seq_group_col_spec(rows, col, ncols, s, nseq):
    return pl.BlockSpec((rows, GROUP_WIDTH), lambda i, c: (c, (i * nseq + s) * ncols + col))


def hgrn_chunk_mixer(proj2, lb_logits, norm_g, *, bsz, t_len, layer):
    L, n, nseq = HGRN_CHUNK, HGRN_SUB, HGRN_SEQS_PER_STEP
    gw = GROUP_WIDTH
    ncols = PROJ_WIDTH // gw
    tile = pltpu.VMEM((nseq, L, gw), F32)
    seq_specs = [_seq_group_col_spec(L, col, ncols, s, nseq)
                 for s in range(nseq) for col in (COL_Q, COL_F, COL_I, COL_G)]
    y, st = pl.pallas_call(
        functools.partial(_hgrn_chunk_kernel, nseq=nseq, layer=layer),
        grid=(bsz // nseq, t_len // L),
        in_specs=seq_specs + [pl.BlockSpec((DEPTH, gw), lambda i, c: (0, 0)),
                              pl.BlockSpec((1, gw), lambda i, c: (0, 0))],
        out_specs=[pl.BlockSpec((L, nseq * gw), lambda i, c: (c, i)),
                   pl.BlockSpec((nseq * gw, gw), lambda i, c: (i, 0))],
        out_shape=[jax.ShapeDtypeStruct((t_len, bsz * gw), BF16),
                   jax.ShapeDtypeStruct((bsz * gw, gw), F32)],
        scratch_shapes=[pltpu.VMEM((nseq, gw, gw), F32), tile, tile, tile,
                        pltpu.VMEM((nseq, n * n, gw), F32), tile],
        compiler_params=_params(("parallel", "arbitrary")),
        name="hgrn_chunk_mixer",
    )(*([proj2] * (4 * nseq)), lb_logits, norm_g)
    return y, st


RWKV_CHUNK = 64


def _rwkv_chunk_kernel(*refs, nseq):
    params = refs[4 * nseq:4 * nseq + 11]
    y_ref, st_ref, sh_ref, w_scr, prev_scr = refs[4 * nseq + 11:]
    gw = GROUP_WIDTH
    c = pl.program_id(1)

    @pl.when(c == 0)
    def _():
        w_scr[...] = jnp.zeros_like(w_scr)
        prev_scr[...] = jnp.zeros_like(prev_scr)

    _run_staged([_rwkv_chunk_one(*refs[4 * s:4 * s + 4], *params, y_ref.at[:, s * gw:(s + 1) * gw], w_scr.at[s],
                                 prev_scr.at[s]) for s in range(nseq)])

    @pl.when(c == pl.num_programs(1) - 1)
    def _():
        for s in range(nseq):
            st_ref[s * gw:(s + 1) * gw, :] = w_scr[s]
        sh_ref[...] = prev_scr[...]


def _rwkv_chunk_one(pr_ref, pk_ref, pv_ref, pl_ref, mu_ref, w0_ref, a0_ref, kk_ref, ka_ref, rk_ref, lng_ref,
                    lnb_ref, w2_ref, a2_ref, g2_ref, y_ref, w_scr, prev_scr):
    L = RWKV_CHUNK
    gw = GROUP_WIDTH
    rid = _iota((L, gw), 0)

    def shifted(ref, j):
        x = ref[...]
        prev = jnp.where(rid == 0, prev_scr[:, j * gw:(j + 1) * gw], pltpu.roll(x, 1, axis=0))
        prev_scr[:, j * gw:(j + 1) * gw] = x[L - 1:L]
        return x + (prev - x) * mu_ref[:, j * gw:(j + 1) * gw]

    xr, xk, xv, xl = shifted(pr_ref, 0), shifted(pk_ref, 1), shifted(pv_ref, 2), shifted(pl_ref, 3)
    w = -jax.nn.softplus(-(w0_ref[...] + _dot(jnp.tanh(xl).astype(BF16), w2_ref[...]))) - 0.5
    lw = -jnp.exp(w)
    a = jax.nn.sigmoid(a0_ref[...] + _dot(xl.astype(BF16), a2_ref[...]))
    g = _dot(jax.nn.sigmoid(xl).astype(BF16), g2_ref[...])
    kk = xk * kk_ref[...]
    kk = kk / jnp.maximum(jnp.sqrt(_head_sums(kk * kk)), 1e-12)
    k = xk * (1.0 + (a - 1.0) * ka_ref[...])
    beta = kk * a
    yield

    cs = _cumsum_rows(lw)
    c_end = cs[L - 1:L]
    e_neg = jnp.exp(-cs)
    e_end = jnp.exp(c_end - cs)
    ar = jnp.concatenate([_head_expand(-kk * jnp.exp(cs - lw)), _head_expand(xr * jnp.exp(cs))], axis=0).astype(BF16)
    bk = jnp.concatenate([_head_expand(beta * e_neg), _head_expand(k * e_neg)], axis=0).astype(BF16)
    vx = _head_expand(xv).astype(BF16)
    yield

    nh = 4 * L
    gmat = _dot_nt(ar, bk)
    tt = _iota((nh, nh), 0) & (L - 1)
    ss = _iota((nh, nh), 1) & (L - 1)
    strict, incl = ss < tt, ss <= tt
    nab = jnp.where(strict, gmat[0:nh, 0:nh], 0.0)
    nak = jnp.where(strict, gmat[0:nh, nh:2 * nh], 0.0).astype(BF16)
    nrb = jnp.where(incl, gmat[nh:2 * nh, 0:nh], 0.0).astype(BF16)
    nrk = jnp.where(incl, gmat[nh:2 * nh, nh:2 * nh], 0.0).astype(BF16)
    yield

    ri, ci = _iota((nh, nh), 0), _iota((nh, nh), 1)

    def same_block(size):
        sh = size.bit_length() - 1
        return lax.shift_right_logical(ri, sh) == lax.shift_right_logical(ci, sh)

    base = 8
    m = jnp.where(same_block(base), nab, 0.0)
    t_inv = jnp.where(ri == ci, 1.0, 0.0) + m
    m = m.astype(BF16)
    for _ in range(base.bit_length() - 2):
        m = _dot(m, m).astype(BF16)
        yield
        t_inv = t_inv + _dot(t_inv.astype(BF16), m)
        yield
    size = base
    while size < L:
        off = jnp.where(same_block(2 * size), jnp.where(same_block(size), 0.0, nab), 0.0).astype(BF16)
        tb = t_inv.astype(BF16)
        half = _dot(tb, off).astype(BF16)
        yield
        t_inv = t_inv + _dot(half, tb)
        yield
        size *= 2

    wst = w_scr[...]
    sw = _dot_nt(ar, wst.astype(BF16))
    rhs = (sw[0:nh] + _dot(nak, vx)).astype(BF16)
    yield
    x = _dot(t_inv.astype(BF16), rhs)
    yield
    ux = x.astype(BF16)
    yx = sw[nh:2 * nh] + _dot(nrb, ux) + _dot(nrk, vx)
    y = _head_collapse(yx)
    u = _head_collapse(x)
    yield

    upd = _dot_tn(jnp.concatenate([u, xv], axis=0).astype(BF16),
                  jnp.concatenate([beta * e_end, k * e_end], axis=0).astype(BF16))
    w_scr[...] = wst * jnp.exp(c_end) + jnp.where(_block_diag_mask(), upd, 0.0)
    yield

    mean = _head_sums(y) * (1.0 / HEAD)
    d = y - mean
    var = _head_sums(d * d) * (1.0 / HEAD)
    yn = d * lax.rsqrt(var + RWKV_GN_EPS) * lng_ref[...] + lnb_ref[...]
    bonus = _head_sums(xr * k * rk_ref[...]) * xv
    y_ref[...] = ((yn + bonus) * g).astype(y_ref.dtype)


RWKV_SEQS_PER_STEP = 8


def rwkv_chunk_mixer(proj2, mu, w0, a0, k_k, k_a, r_k, ln_g, ln_b, w2p, a2p, g2p, *, bsz, t_len):
    L, nseq = RWKV_CHUNK, RWKV_SEQS_PER_STEP
    gw = GROUP_WIDTH
    ncols = PROJ_WIDTH // gw
    vec = pl.BlockSpec((1, gw), lambda i, c: (0, 0))
    mat = pl.BlockSpec((gw, gw), lambda i, c: (0, 0))
    seq_specs = [_seq_group_col_spec(L, col, ncols, s, nseq)
                 for s in range(nseq) for col in (COL_R, COL_K, COL_V, COL_LORA)]
    y, st, sh = pl.pallas_call(
        functools.partial(_rwkv_chunk_kernel, nseq=nseq),
        grid=(bsz // nseq, t_len // L),
        in_specs=seq_specs + [pl.BlockSpec((1, RWKV_PROJ), lambda i, c: (0, 0)),
                              vec, vec, vec, vec, vec, vec, vec, mat, mat, mat],
        out_specs=[pl.BlockSpec((L, nseq * gw), lambda i, c: (c, i)),
                   pl.BlockSpec((nseq * gw, gw), lambda i, c: (i, 0)),
                   pl.BlockSpec((nseq, 1, RWKV_PROJ), lambda i, c: (i, 0, 0))],
        out_shape=[jax.ShapeDtypeStruct((t_len, bsz * gw), BF16),
                   jax.ShapeDtypeStruct((bsz * gw, gw), F32),
                   jax.ShapeDtypeStruct((bsz, 1, RWKV_PROJ), F32)],
        scratch_shapes=[pltpu.VMEM((nseq, gw, gw), F32), pltpu.VMEM((nseq, 1, RWKV_PROJ), F32)],
        compiler_params=_params(("parallel", "arbitrary")),
        name="rwkv_chunk_mixer",
    )(*([proj2] * (4 * nseq)), mu, w0, a0, k_k, k_a, r_k, ln_g, ln_b, w2p, a2p, g2p)
    return y, st, sh


def _hgrn_rwkv_chunk_kernel(*refs, layer, rwkv_steps, hgrn_steps):
    nr, nh = RWKV_SEQS_PER_STEP, HGRN_SEQS_PER_STEP
    it = iter(refs)
    take = lambda n: [next(it) for _ in range(n)]
    r_seq, r_par, h_seq = take(4 * nr), take(11), take(4 * nh)
    lbl_ref, ng_ref = take(2)
    yr_ref, str_ref, shr_ref, yh_ref, sth_ref = take(5)
    wr_scr, prev_scr, wh_scr, k_scr, b_scr, v_scr, p_scr, o_scr = take(8)
    gw = GROUP_WIDTH
    g = pl.program_id(0)
    cr, ch = g % rwkv_steps, g % hgrn_steps

    @pl.when(cr == 0)
    def _():
        wr_scr[...] = jnp.zeros_like(wr_scr)
        prev_scr[...] = jnp.zeros_like(prev_scr)

    @pl.when(ch == 0)
    def _():
        wh_scr[...] = jnp.zeros_like(wh_scr)
        p_scr[...] = jnp.zeros_like(p_scr)

    rwkv = [_rwkv_chunk_one(*r_seq[4 * s:4 * s + 4], *r_par, yr_ref.at[:, s * gw:(s + 1) * gw], wr_scr.at[s],
                            prev_scr.at[s]) for s in range(nr)]
    hgrn = [_hgrn_chunk_one(*h_seq[4 * s:4 * s + 4], lbl_ref, ng_ref, yh_ref.at[:, s * gw:(s + 1) * gw],
                            wh_scr.at[s], k_scr.at[s], b_scr.at[s], v_scr.at[s], p_scr.at[s], o_scr.at[s],
                            layer=layer) for s in range(nh)]
    order = []
    for s in range(max(nr, nh)):
        order += rwkv[s:s + 1] + hgrn[s:s + 1]
    _run_staged(order)

    @pl.when(cr == rwkv_steps - 1)
    def _():
        for s in range(nr):
            str_ref[s * gw:(s + 1) * gw, :] = wr_scr[s]
        shr_ref[...] = prev_scr[...]

    @pl.when(ch == hgrn_steps - 1)
    def _():
        for s in range(nh):
            sth_ref[s * gw:(s + 1) * gw, :] = wh_scr[s]


def hgrn_rwkv_chunk_mixer(proj2, lb_logits, norm_g, mu, w0, a0, k_k, k_a, r_k, ln_g, ln_b, w2p, a2p, g2p, *,
                          bsz, t_len, layer):
    nr, nh, lr, lh, n = RWKV_SEQS_PER_STEP, HGRN_SEQS_PER_STEP, RWKV_CHUNK, HGRN_CHUNK, HGRN_SUB
    gw = GROUP_WIDTH
    ncols = PROJ_WIDTH // gw
    rwkv_steps, hgrn_steps = t_len // lr, t_len // lh
    steps = (bsz // nr) * rwkv_steps
    assert (bsz // nh) * hgrn_steps == steps
    const = lambda shape: pl.BlockSpec(shape, lambda g: (0,) * len(shape))
    r_specs = [pl.BlockSpec((lr, gw), functools.partial(
                   lambda g, s, col: (g % rwkv_steps, ((g // rwkv_steps) * nr + s) * ncols + col), s=s, col=col))
               for s in range(nr) for col in (COL_R, COL_K, COL_V, COL_LORA)]
    h_specs = [pl.BlockSpec((lh, gw), functools.partial(
                   lambda g, s, col: (g % hgrn_steps, ((g // hgrn_steps) * nh + s) * ncols + col), s=s, col=col))
               for s in range(nh) for col in (COL_Q, COL_F, COL_I, COL_G)]
    vec, mat = const((1, gw)), const((gw, gw))
    tile = pltpu.VMEM((nh, lh, gw), F32)
    yr, st_r, sh_r, yh, st_h = pl.pallas_call(
        functools.partial(_hgrn_rwkv_chunk_kernel, layer=layer, rwkv_steps=rwkv_steps, hgrn_steps=hgrn_steps),
        grid=(steps,),
        in_specs=r_specs + [const((1, RWKV_PROJ)), vec, vec, vec, vec, vec, vec, vec, mat, mat, mat]
                 + h_specs + [const((DEPTH, gw)), vec],
        out_specs=[pl.BlockSpec((lr, nr * gw), lambda g: (g % rwkv_steps, g // rwkv_steps)),
                   pl.BlockSpec((nr * gw, gw), lambda g: (g // rwkv_steps, 0)),
                   pl.BlockSpec((nr, 1, RWKV_PROJ), lambda g: (g // rwkv_steps, 0, 0)),
                   pl.BlockSpec((lh, nh * gw), lambda g: (g % hgrn_steps, g // hgrn_steps)),
                   pl.BlockSpec((nh * gw, gw), lambda g: (g // hgrn_steps, 0))],
        out_shape=[jax.ShapeDtypeStruct((t_len, bsz * gw), BF16), jax.ShapeDtypeStruct((bsz * gw, gw), F32),
                   jax.ShapeDtypeStruct((bsz, 1, RWKV_PROJ), F32),
                   jax.ShapeDtypeStruct((t_len, bsz * gw), BF16), jax.ShapeDtypeStruct((bsz * gw, gw), F32)],
        scratch_shapes=[pltpu.VMEM((nr, gw, gw), F32), pltpu.VMEM((nr, 1, RWKV_PROJ), F32),
                        pltpu.VMEM((nh, gw, gw), F32), tile, tile, tile, pltpu.VMEM((nh, n * n, gw), F32), tile],
        compiler_params=_params(("arbitrary",)),
        name="hgrn_rwkv_chunk_mixer",
    )(*([proj2] * (4 * nr)), mu, w0, a0, k_k, k_a, r_k, ln_g, ln_b, w2p, a2p, g2p,
      *([proj2] * (4 * nh)), lb_logits, norm_g)
    return yh, st_h, yr, st_r, sh_r


S5_CHUNK = 64


def _s5_seq_kernel(*refs, nseq):
    u_refs = refs[:nseq]
    lr_ref, li_ref, ldt_ref, bre_ref, bim_ref, ccat_ref, d_ref, gw_ref, gb_ref = refs[nseq:nseq + 9]
    y_ref, hr_out, hi_out, h_scr, bu_scr, bb_scr, ab_scr, perm_scr, u_scr = refs[nseq + 9:]
    tc = S5_CHUNK
    n = nseq * tc
    c = pl.program_id(0)

    @pl.when(c == 0)
    def _():
        h_scr[...] = jnp.zeros_like(h_scr)
        lr, li = lr_ref[...], li_ref[...]
        dt = jnp.exp(ldt_ref[...])
        mag = jnp.exp(lr * dt)
        ab_re, ab_im = mag * jnp.cos(li * dt), mag * jnp.sin(li * dt)
        den = lr * lr + li * li
        zr, zi = ab_re - 1.0, ab_im
        cr = (zr * lr + zi * li) / den
        ci = (zi * lr - zr * li) / den
        bre, bim = bre_ref[...], bim_ref[...]
        bb_scr[:, 0:SSM_FLAT] = (cr * bre - ci * bim).astype(BF16)
        bb_scr[:, SSM_FLAT:2 * SSM_FLAT] = (cr * bim + ci * bre).astype(BF16)
        ab_scr[0] = jnp.broadcast_to(ab_re, (nseq, SSM_FLAT))
        ab_scr[1] = jnp.broadcast_to(ab_im, (nseq, SSM_FLAT))
        ri, cj = _iota((n, n), 0), _iota((n, n), 1)
        lseq, lt = nseq.bit_length() - 1, tc.bit_length() - 1
        perm_scr[0] = jnp.where(cj == (ri & (nseq - 1)) * tc + lax.shift_right_logical(ri, lseq), 1.0, 0.0).astype(BF16)
        perm_scr[1] = jnp.where(cj == (ri & (tc - 1)) * nseq + lax.shift_right_logical(ri, lt), 1.0, 0.0).astype(BF16)

        bu_scr[1] = jnp.zeros(bu_scr.shape[1:], F32)
        u_scr[1] = jnp.zeros(u_scr.shape[1:], F32)

    last = pl.num_programs(0) - 1

    def pipeline(slot):

        def scan_chunk():
            u = jnp.concatenate([r[...] for r in u_refs], axis=0)
            u_scr[slot] = u
            u_t = _dot(perm_scr[0], u.astype(BF16)).astype(BF16)
            bu_scr[slot] = _dot(u_t, bb_scr[...])
            yield
            ar, ai = ab_scr[0], ab_scr[1]
            hr0, hi0 = h_scr[0], h_scr[1]
            hr, hi = hr0, hi0
            for t in range(tc):
                rows = slice(t * nseq, (t + 1) * nseq)
                hr, hi = (ar * hr - ai * hi + bu_scr[slot, rows, 0:SSM_FLAT],
                          ar * hi + ai * hr + bu_scr[slot, rows, SSM_FLAT:2 * SSM_FLAT])
                bu_scr[slot, rows, 0:SSM_FLAT] = hr
                bu_scr[slot, rows, SSM_FLAT:2 * SSM_FLAT] = hi
                if t % 8 == 7:
                    yield
            h_scr[0] = jnp.where(c < last, hr, hr0)
            h_scr[1] = jnp.where(c < last, hi, hi0)

        def emit_chunk():
            prev = 1 - slot
            y_t = _dot(bu_scr[prev].astype(BF16), ccat_ref[...])
            yield
            hi_p = y_t.astype(BF16)
            rest = y_t - hi_p.astype(F32)
            mid_p = rest.astype(BF16)
            lo_p = (rest - mid_p.astype(F32)).astype(BF16)
            to_seq = perm_scr[1]
            y = _dot(to_seq, hi_p) + _dot(to_seq, mid_p) + _dot(to_seq, lo_p) + d_ref[...] * u_scr[prev]
            yield
            z = jax.nn.gelu(y)
            gate = _dot(z.astype(BF16), gw_ref[...])
            yield
            out = z * jax.nn.sigmoid(gate + gb_ref[...])
            for b in range(nseq):
                y_ref[:, b * GROUP_WIDTH:(b + 1) * GROUP_WIDTH] = out[b * tc:(b + 1) * tc].astype(y_ref.dtype)

        _run_staged([scan_chunk(), emit_chunk()])

    for parity in range(2):
        pl.when(c % 2 == parity)(functools.partial(pipeline, parity))

    @pl.when(c == last)
    def _():
        hr_out[...] = h_scr[0]
        hi_out[...] = h_scr[1]


def s5_seq_mixer(proj2, lam_re, lam_im, log_dt, b_re_bd, b_im_bd, c_cat, d_skip, glu_w, glu_b, *, bsz, t_len):
    tc = S5_CHUNK
    ncols = PROJ_WIDTH // GROUP_WIDTH
    full = lambda shape: pl.BlockSpec(shape, lambda c: (0,) * len(shape))
    st = jax.ShapeDtypeStruct((bsz, SSM_FLAT), F32)
    nchunks = t_len // tc
    u_specs = [pl.BlockSpec((tc, GROUP_WIDTH), functools.partial(
                   lambda c, b: (jnp.minimum(c, nchunks - 1), b * ncols + COL_SSM), b=b)) for b in range(bsz)]
    return pl.pallas_call(
        functools.partial(_s5_seq_kernel, nseq=bsz),
        grid=(nchunks + 1,),
        in_specs=u_specs + [full((1, SSM_FLAT)), full((1, SSM_FLAT)), full((1, SSM_FLAT)),
                            full((GROUP_WIDTH, SSM_FLAT)), full((GROUP_WIDTH, SSM_FLAT)),
                            full((2 * SSM_FLAT, GROUP_WIDTH)), full((1, GROUP_WIDTH)),
                            full((GROUP_WIDTH, GROUP_WIDTH)), full((1, GROUP_WIDTH))],
        out_specs=[pl.BlockSpec((tc, bsz * GROUP_WIDTH), lambda c: (jnp.maximum(c - 1, 0), 0)),
                   full((bsz, SSM_FLAT)), full((bsz, SSM_FLAT))],
        out_shape=[jax.ShapeDtypeStruct((t_len, bsz * GROUP_WIDTH), BF16), st, st],
        scratch_shapes=[pltpu.VMEM((2, bsz, SSM_FLAT), F32), pltpu.VMEM((2, bsz * tc, 2 * SSM_FLAT), F32),
                        pltpu.VMEM((GROUP_WIDTH, 2 * SSM_FLAT), BF16), pltpu.VMEM((2, bsz, SSM_FLAT), F32),
                        pltpu.VMEM((2, bsz * tc, bsz * tc), BF16), pltpu.VMEM((2, bsz * tc, GROUP_WIDTH), F32)],
        compiler_params=_params(("arbitrary",)),
        name="s5_seq_mixer",
    )(*([proj2] * bsz), lam_re, lam_im, log_dt, b_re_bd, b_im_bd, c_cat, d_skip, glu_w, glu_b)


POOL_CHUNK = 1024
POOL_HIST = 16


def _pool_seq_kernel(u_ref, w_ref, sc_ref, y_ref, nbuf_ref, ext_scr):
    L, hist = u_ref.shape[0], POOL_HIST
    c = pl.program_id(1)

    @pl.when(c == 0)
    def _():
        ext_scr[0:hist] = jnp.zeros((hist, GROUP_WIDTH), F32)

    u = u_ref[...]
    ext_scr[hist:hist + L] = u
    e = ext_scr[...]
    a2 = e + pltpu.roll(e, 1, axis=0)
    a4 = a2 + pltpu.roll(a2, 2, axis=0)
    a8 = a4 + pltpu.roll(a4, 4, axis=0)
    a16 = a8 + pltpu.roll(a8, 8, axis=0)
    sums = (a2[hist:], a4[hist:], a8[hist:], a16[hist:])

    shape = (L, GROUP_WIDTH)
    pos = _iota(shape, 0) + c * L
    lane = _iota(shape, 1)
    pooled = None
    for gi in reversed(range(len(POOL_WINDOWS))):
        win = POOL_WINDOWS[gi]
        mean = sums[gi] / jnp.minimum(pos + 1, win).astype(F32)
        pooled = mean if pooled is None else jnp.where(lane < (gi + 1) * HEAD, mean, pooled)
    y_ref[...] = (_dot((pooled - u).astype(BF16), w_ref[...]) * sc_ref[...]).astype(y_ref.dtype)

    nb = ext_scr[L:L + hist]
    ext_scr[0:hist] = nb

    @pl.when(c == pl.num_programs(1) - 1)
    def _():
        nbuf_ref[...] = nb


def pool_seq_mixer(proj2, w_bd, scale, *, bsz, t_len):
    L, hist = min(POOL_CHUNK, t_len), POOL_HIST
    ncols = PROJ_WIDTH // GROUP_WIDTH
    return pl.pallas_call(
        _pool_seq_kernel,
        grid=(bsz, t_len // L),
        in_specs=[_seq_col_spec(L, COL_POOL, ncols),
                  pl.BlockSpec((GROUP_WIDTH, GROUP_WIDTH), lambda b, c: (0, 0)),
                  pl.BlockSpec((1, GROUP_WIDTH), lambda b, c: (0, 0))],
        out_specs=[_seq_col_spec(L, 0, 1), pl.BlockSpec((hist, GROUP_WIDTH), lambda b, c: (b, 0))],
        out_shape=[jax.ShapeDtypeStruct((t_len, bsz * GROUP_WIDTH), BF16),
                   jax.ShapeDtypeStruct((bsz * hist, GROUP_WIDTH), F32)],
        scratch_shapes=[pltpu.VMEM((hist + L, GROUP_WIDTH), F32)],
        compiler_params=_params(("parallel", "arbitrary")),
        name="pool_seq_mixer",
    )(proj2, w_bd, scale)


def _diag_heads(st, bsz):
    nh = GROUP_WIDTH // HEAD
    s = st.reshape(bsz, nh, HEAD, nh, HEAD)
    return jnp.stack([s[:, h, :, h, :] for h in range(nh)], axis=1)


def _mix_mlp_kernel(h_ref, ya_ref, yb_ref, yc_ref, yd_ref, wo_ref, g2_ref, wu_ref, wd_ref, gf_ref, o_ref,
                    h1_scr, xn_scr, acc_scr, *, final_norm):
    j = pl.program_id(1)
    gw = GROUP_WIDTH

    @pl.when(j == 0)
    def _():
        mix = (_dot(ya_ref[...], wo_ref[0:gw]) + _dot(yb_ref[...], wo_ref[gw:2 * gw])
               + _dot(yc_ref[...], wo_ref[2 * gw:3 * gw]) + _dot(yd_ref[...], wo_ref[3 * gw:4 * gw]))
        h1 = h_ref[...] + mix
        h1_scr[...] = h1
        xn_scr[...] = _rms(h1, g2_ref[...]).astype(BF16)
        acc_scr[...] = jnp.zeros_like(acc_scr)

    up = _dot(xn_scr[...], wu_ref[...])
    act = jnp.square(jnp.maximum(up, 0.0)).astype(BF16)
    acc_scr[...] += _dot(act, wd_ref[...])

    @pl.when(j == pl.num_programs(1) - 1)
    def _():
        out = h1_scr[...] + acc_scr[...]
        if final_norm:
            out = _rms(out, gf_ref[...])
        o_ref[...] = out


def mix_mlp(h, ys, w_out, g2, w_up, w_down, g_final, *, final_norm, nseq=0):
    n = h.shape[0]
    tm = min(512, n // max(nseq, 1))
    tf = 2048
    nt = n // tm // max(nseq, 1)
    row = lambda w: pl.BlockSpec((tm, w), lambda i, j: (i, 0))
    mix = pl.BlockSpec((tm, GROUP_WIDTH), _tile_map(nseq, nt))
    return pl.pallas_call(
        functools.partial(_mix_mlp_kernel, final_norm=final_norm),
        grid=(n // tm, D_FF // tf),
        in_specs=[row(D_MODEL), mix, mix, mix, mix,
                  pl.BlockSpec((D_MODEL, D_MODEL), lambda i, j: (0, 0)),
                  pl.BlockSpec((1, D_MODEL), lambda i, j: (0, 0)),
                  pl.BlockSpec((D_MODEL, tf), lambda i, j: (0, j)),
                  pl.BlockSpec((tf, D_MODEL), lambda i, j: (j, 0)),
                  pl.BlockSpec((1, D_MODEL), lambda i, j: (0, 0))],
        out_specs=row(D_MODEL),
        out_shape=jax.ShapeDtypeStruct((n, D_MODEL), F32),
        scratch_shapes=[pltpu.VMEM((tm, D_MODEL), F32), pltpu.VMEM((tm, D_MODEL), BF16),
                        pltpu.VMEM((tm, D_MODEL), F32)],
        compiler_params=_params(("parallel", "arbitrary")),
        name="mix_mlp",
    )(h, *ys, w_out, g2, w_up, w_down, g_final)


def _block_diag(blocks):
    g, r, c = blocks.shape
    tiled = jnp.tile(blocks.reshape(g * r, c), (1, g))
    row_blk = lax.broadcasted_iota(jnp.int32, tiled.shape, 0) // r
    col_blk = lax.broadcasted_iota(jnp.int32, tiled.shape, 1) // c
    return jnp.where(row_blk == col_blk, tiled, 0.0)


def _pad_rows(w, start):
    rows = GROUP_WIDTH - start - w.shape[0]
    return jnp.concatenate([jnp.zeros((start, GROUP_WIDTH), w.dtype), w, jnp.zeros((rows, GROUP_WIDTH), w.dtype)])


def _layer_params(l, P):
    row = lambda a: a.reshape(1, -1)
    q = {}
    q["norm1_g"] = row(P["norm1_g"][l])
    q["w_in"] = P["w_in"][l].astype(BF16)
    q["lam_re"] = row(P["ssm_lambda_re"][l])
    q["lam_im"] = row(P["ssm_lambda_im"][l])
    q["log_dt"] = row(jnp.repeat(P["ssm_log_dt"][l], SSM_STATE))
    q["b_re"] = _block_diag(P["ssm_b_re"][l].transpose(0, 2, 1))
    q["b_im"] = _block_diag(P["ssm_b_im"][l].transpose(0, 2, 1))
    q["c_cat"] = jnp.concatenate([_block_diag(P["ssm_c_re"][l]), -_block_diag(P["ssm_c_im"][l])],
                                 axis=1).T.astype(BF16)
    q["ssm_d"] = row(P["ssm_d"][l])
    q["glu_w"] = P["ssm_glu_w"][l].astype(BF16)
    q["glu_b"] = row(P["ssm_glu_b"][l])
    q["hgrn_norm_g"] = row(P["hgrn_norm_g"][l])
    q["mu"] = row(P["rwkv_mu"][l])
    for name in ("w0", "a0", "k_k", "k_a", "r_k", "ln_g", "ln_b"):
        q[name] = row(P["rwkv_" + name][l])
    q["w2p"] = _pad_rows(P["rwkv_w2"][l], 0).astype(BF16)
    q["a2p"] = _pad_rows(P["rwkv_a2"][l], DECAY_LORA).astype(BF16)
    q["g2p"] = _pad_rows(P["rwkv_g2"][l], DECAY_LORA + AAA_LORA).astype(BF16)
    q["pool_w"] = _block_diag(P["pool_w"][l]).astype(BF16)
    q["pool_scale"] = row(P["pool_scale"][l])
    q["w_out"] = P["w_out"][l].astype(BF16)
    q["norm2_g"] = row(P["norm2_g"][l])
    q["mlp_up"] = P["mlp_up"][l].astype(BF16)
    q["mlp_down"] = P["mlp_down"][l].astype(BF16)
    return q


def _trunk_fresh(x_rows, bsz, t_len, layer_params, P):
    h = x_rows
    new = [[] for _ in range(6)]
    g_final = P["norm_f_g"].reshape(1, -1)
    for l in range(DEPTH):
        q = layer_params[l]
        proj2 = rms_proj(h, q["norm1_g"], q["w_in"], nseq=bsz)
        y_a, s_re, s_im = s5_seq_mixer(proj2, q["lam_re"], q["lam_im"], q["log_dt"], q["b_re"], q["b_im"],
                                       q["c_cat"], q["ssm_d"], q["glu_w"], q["glu_b"], bsz=bsz, t_len=t_len)
        y_b, s_hg, y_c, s_wkv, s_sh = hgrn_rwkv_chunk_mixer(
            proj2, P["hgrn_lb_logits"], q["hgrn_norm_g"], q["mu"], q["w0"], q["a0"], q["k_k"], q["k_a"], q["r_k"],
            q["ln_g"], q["ln_b"], q["w2p"], q["a2p"], q["g2p"], bsz=bsz, t_len=t_len, layer=l)
        y_d, s_pool = pool_seq_mixer(proj2, q["pool_w"], q["pool_scale"], bsz=bsz, t_len=t_len)
        h = mix_mlp(h, (y_a, y_b, y_c, y_d), q["w_out"], q["norm2_g"], q["mlp_up"], q["mlp_down"], g_final,
                    final_norm=(l == DEPTH - 1), nseq=bsz)
        s_hg = _diag_heads(s_hg, bsz).swapaxes(-1, -2)
        s_wkv = _diag_heads(s_wkv, bsz)
        s_pool = s_pool.reshape(bsz, POOL_HIST, GROUP_WIDTH)[:, POOL_HIST - POOL_BUF:]
        for lst, s in zip(new, (s_re, s_im, s_hg, s_wkv, s_sh, s_pool)):
            lst.append(s)
    return h, new


def _trunk_carry(x_rows, states, pos0, t_len, layer_params, P):
    ssm_re0, ssm_im0, hgrn0, wkv0, shift0, pool0 = states
    h = x_rows
    new = [[] for _ in range(6)]
    g_final = P["norm_f_g"].reshape(1, -1)
    tc = t_len
    for l in range(DEPTH):
        q = layer_params[l]
        proj = rms_proj(h, q["norm1_g"], q["w_in"])
        y_a, s_re, s_im = s5_mixer(proj, ssm_re0[l], ssm_im0[l], q["lam_re"], q["lam_im"], q["log_dt"], q["b_re"],
                                   q["b_im"], q["c_cat"], q["ssm_d"], q["glu_w"], q["glu_b"], t_len=t_len, tc=tc)
        y_b, s_hg = hgrn_mixer(proj, hgrn0[l], P["hgrn_lb_logits"], q["hgrn_norm_g"], t_len=t_len, tc=tc, layer=l)
        y_c, s_wkv, s_sh = rwkv_mixer(proj, shift0[l], wkv0[l], q["mu"], q["w0"], q["a0"], q["k_k"], q["k_a"],
                                      q["r_k"], q["ln_g"], q["ln_b"], q["w2p"], q["a2p"], q["g2p"],
                                      t_len=t_len, tc=tc)
        y_d, s_pool = pool_mixer(proj, pool0[l], q["pool_w"], q["pool_scale"], t_len=t_len, tc=tc, pos0=pos0)
        h = mix_mlp(h, (y_a, y_b, y_c, y_d), q["w_out"], q["norm2_g"], q["mlp_up"], q["mlp_down"], g_final,
                    final_norm=(l == DEPTH - 1))
        for lst, s in zip(new, (s_re, s_im, s_hg, s_wkv, s_sh,
                                s_pool.transpose(1, 0, 2))):
            lst.append(s)
    return h, new


def _states_out(new, bsz):
    s_re, s_im, s_hg, s_wkv, s_sh, s_pool = new
    return (jnp.stack([s.reshape(bsz, SSM_GROUPS, SSM_STATE) for s in s_re]),
            jnp.stack([s.reshape(bsz, SSM_GROUPS, SSM_STATE) for s in s_im]),
            jnp.stack(s_hg),
            jnp.stack(s_wkv),
            jnp.stack([s.reshape(bsz, 1, RWKV_PROJ) for s in s_sh]),
            jnp.stack(s_pool))


def kernel(x_prompt, x_sample, state_ssm_re, state_ssm_im, state_hgrn, state_wkv, state_shift, state_pool, norm1_g, w_in, ssm_lambda_re, ssm_lambda_im, ssm_log_dt, ssm_b_re, ssm_b_im, ssm_c_re, ssm_c_im, ssm_d, ssm_glu_w, ssm_glu_b, hgrn_lb_logits, hgrn_norm_g, rwkv_mu, rwkv_w0, rwkv_w2, rwkv_a0, rwkv_a2, rwkv_g2, rwkv_k_k, rwkv_k_a, rwkv_r_k, rwkv_ln_g, rwkv_ln_b, pool_w, pool_scale, w_out, norm2_g, mlp_up, mlp_down, norm_f_g):
    P = dict(norm1_g=norm1_g, w_in=w_in, ssm_lambda_re=ssm_lambda_re, ssm_lambda_im=ssm_lambda_im,
             ssm_log_dt=ssm_log_dt, ssm_b_re=ssm_b_re, ssm_b_im=ssm_b_im, ssm_c_re=ssm_c_re, ssm_c_im=ssm_c_im,
             ssm_d=ssm_d, ssm_glu_w=ssm_glu_w, ssm_glu_b=ssm_glu_b, hgrn_lb_logits=hgrn_lb_logits,
             hgrn_norm_g=hgrn_norm_g, rwkv_mu=rwkv_mu, rwkv_w0=rwkv_w0, rwkv_w2=rwkv_w2, rwkv_a0=rwkv_a0,
             rwkv_a2=rwkv_a2, rwkv_g2=rwkv_g2, rwkv_k_k=rwkv_k_k, rwkv_k_a=rwkv_k_a, rwkv_r_k=rwkv_r_k,
             rwkv_ln_g=rwkv_ln_g, rwkv_ln_b=rwkv_ln_b, pool_w=pool_w, pool_scale=pool_scale, w_out=w_out,
             norm2_g=norm2_g, mlp_up=mlp_up, mlp_down=mlp_down, norm_f_g=norm_f_g)
    layer_params = [_layer_params(l, P) for l in range(DEPTH)]

    bp, t_p, _ = x_prompt.shape
    yp, new_p = _trunk_fresh(x_prompt.reshape(bp * t_p, D_MODEL), bp, t_p, layer_params, P)
    y_prompt = yp.reshape(bp, t_p, D_MODEL)

    bs, t_s, _ = x_sample.shape
    nblk = bs // SEQ_BLK
    xs = x_sample.reshape(nblk, SEQ_BLK, t_s, D_MODEL).transpose(0, 2, 1, 3).reshape(bs * t_s, D_MODEL)
    st_s = ([state_ssm_re[l].reshape(bs, SSM_FLAT) for l in range(DEPTH)],
            [state_ssm_im[l].reshape(bs, SSM_FLAT) for l in range(DEPTH)],
            [state_hgrn[l] for l in range(DEPTH)],
            [state_wkv[l] for l in range(DEPTH)],
            [state_shift[l].reshape(bs, RWKV_PROJ) for l in range(DEPTH)],
            [state_pool[l].transpose(1, 0, 2) for l in range(DEPTH)])
    ys, new_s = _trunk_carry(xs, st_s, PAST_LEN, t_s, layer_params, P)
    y_sample = ys.reshape(nblk, t_s, SEQ_BLK, D_MODEL).transpose(0, 2, 1, 3).reshape(bs, t_s, D_MODEL)

    return (y_prompt, y_sample) + _states_out(new_p, bp) + _states_out(new_s, bs)
```

```python
import functools
import itertools

import jax
import jax.numpy as jnp
from jax import lax
from jax.experimental import pallas as pl
from jax.experimental.pallas import tpu as pltpu

F32 = jnp.float32
BF16 = jnp.bfloat16

D_MODEL = 1024
DEPTH = 2
PAST_LEN = 16384
GROUP_WIDTH = 256
HEAD = 64
SSM_GROUPS = 16
SSM_CH = 16
SSM_STATE = 64
SSM_FLAT = SSM_GROUPS * SSM_STATE
POOL_WINDOWS = (2, 4, 8, 16)
POOL_BUF = 15
DECAY_LORA = 64
AAA_LORA = 64
GATE_LORA = 128
RWKV_PROJ = 1024
PROJ_WIDTH = 2560
D_FF = 4096
NORM_EPS = 1e-6
HGRN_NORM_EPS = 1e-5
RWKV_GN_EPS = 64e-5

SEQ_BLK = 8
LANES = 128
VMEM_LIMIT = 48 * 1024 * 1024

COL_SSM, COL_Q, COL_F, COL_I, COL_G, COL_R, COL_K, COL_V, COL_LORA, COL_POOL = range(10)


def _params(sem):
    return pltpu.CompilerParams(dimension_semantics=sem, vmem_limit_bytes=VMEM_LIMIT)


def _dot(a, b):
    return jnp.dot(a, b, preferred_element_type=F32)


def _rms(x, g):
    return x * lax.rsqrt(jnp.mean(x * x, axis=-1, keepdims=True) + NORM_EPS) * g


def _rms_proj_kernel(x_ref, g_ref, w_ref, o_ref):
    o_ref[...] = _dot(_rms(x_ref[...], g_ref[...]).astype(BF16), w_ref[...])


def _tile_map(nseq, nt):
    if nseq == 0:
        return lambda r, *_: (r, 0)
    return lambda r, *_: (r % nt, r // nt)


def rms_proj(x, g, w, *, nseq=0):
    n = x.shape[0]
    tm = min(1024, n // max(nseq, 1))
    nt = n // tm // max(nseq, 1)
    out_shape = (n, PROJ_WIDTH) if nseq == 0 else (n // nseq, nseq * PROJ_WIDTH)
    return pl.pallas_call(
        _rms_proj_kernel,
        grid=(n // tm,),
        in_specs=[pl.BlockSpec((tm, D_MODEL), lambda i: (i, 0)),
                  pl.BlockSpec((1, D_MODEL), lambda i: (0, 0)),
                  pl.BlockSpec((D_MODEL, PROJ_WIDTH), lambda i: (0, 0))],
        out_specs=pl.BlockSpec((tm, PROJ_WIDTH), _tile_map(nseq, nt)),
        out_shape=jax.ShapeDtypeStruct(out_shape, F32),
        compiler_params=_params(("parallel",)),
        name="rms_proj",
    )(x, g, w)


def _row_spec(tc, col, nchunks):
    return pl.BlockSpec((tc * SEQ_BLK, GROUP_WIDTH), lambda s, c: (s * nchunks + c, col))


def _full_spec(shape):
    nd = len(shape)
    return pl.BlockSpec(shape, lambda s, c: (0,) * nd)


def _seq_spec(shape):
    nd = len(shape)
    return pl.BlockSpec((SEQ_BLK,) + shape[1:], lambda s, c: (s,) + (0,) * (nd - 1))


def _head_sums(x):
    lane = lax.broadcasted_iota(jnp.int32, x.shape, 1)
    out = jnp.zeros_like(x)
    for h in range(GROUP_WIDTH // HEAD):
        m = (lane >= h * HEAD) & (lane < (h + 1) * HEAD)
        s = jnp.sum(jnp.where(m, x, 0.0), axis=1, keepdims=True)
        out = jnp.where(m, s, out)
    return out


UNITS = SEQ_BLK * 2


def _unit_masks():
    r, c = _iota((2 * LANES, LANES), 0), _iota((2 * LANES, LANES), 1)
    ones2 = jnp.where(_head_of(r & (LANES - 1)) == _head_of(c), 1.0, 0.0).astype(BF16)
    shape = (UNITS * HEAD, LANES)
    eye = (_iota(shape, 1) & (HEAD - 1)) == (_iota(shape, 0) & (HEAD - 1))
    return ones2, eye


def _seg_sum_mxu(p, ones2):
    hi = p.astype(BF16)
    lo = (p - hi.astype(F32)).astype(BF16)
    return _dot(jnp.concatenate([hi, lo], axis=1), ones2)


def _unit_rows(ref, t):
    return jnp.concatenate([jnp.broadcast_to(ref[t, b:b + 1, p * LANES:(p + 1) * LANES], (HEAD, LANES))
                            for b in range(SEQ_BLK) for p in range(2)], axis=0)


def _load_unit_tiles(s0_ref, s_scr, *, transpose):
    for b in range(SEQ_BLK):
        for p in range(2):
            heads = [s0_ref[b, 2 * p + j] for j in range(2)]
            s_scr[b, p] = jnp.concatenate([h.T if transpose else h for h in heads], axis=1)


def _store_unit_tiles(st_ref, s_scr, *, transpose):
    for b in range(SEQ_BLK):
        for p in range(2):
            tile = s_scr[b, p]
            for j in range(2):
                h = tile[:, j * HEAD:(j + 1) * HEAD]
                st_ref[b, 2 * p + j] = h.T if transpose else h


def _store_unit_rows(ref, t, cols, eye):
    picked = jnp.where(eye, cols, 0.0)
    for b in range(SEQ_BLK):
        for p in range(2):
            u = b * 2 + p
            ref[t, b:b + 1, p * LANES:(p + 1) * LANES] = jnp.sum(picked[u * HEAD:(u + 1) * HEAD], axis=0, keepdims=True)


def _s5_kernel(u_ref, h0r_ref, h0i_ref, lr_ref, li_ref, ldt_ref, bre_ref, bim_ref, ccat_ref, d_ref,
               gw_ref, gb_ref, y_ref, hr_out, hi_out, h_scr, bu_scr, *, tc):
    c = pl.program_id(1)

    @pl.when(c == 0)
    def _():
        h_scr[0] = h0r_ref[...]
        h_scr[1] = h0i_ref[...]

    lr, li = lr_ref[...], li_ref[...]
    dt = jnp.exp(ldt_ref[...])
    mag = jnp.exp(lr * dt)
    ab_re, ab_im = mag * jnp.cos(li * dt), mag * jnp.sin(li * dt)
    den = lr * lr + li * li
    zr, zi = ab_re - 1.0, ab_im
    cr = (zr * lr + zi * li) / den
    ci = (zi * lr - zr * li) / den
    bre, bim = bre_ref[...], bim_ref[...]
    bb_re = (cr * bre - ci * bim).astype(BF16)
    bb_im = (cr * bim + ci * bre).astype(BF16)

    u = u_ref[...]
    ub = u.astype(BF16)
    bu_scr[:, 0:SSM_FLAT] = _dot(ub, bb_re)
    bu_scr[:, SSM_FLAT:2 * SSM_FLAT] = _dot(ub, bb_im)

    ar = jnp.broadcast_to(ab_re, (SEQ_BLK, SSM_FLAT))
    ai = jnp.broadcast_to(ab_im, (SEQ_BLK, SSM_FLAT))

    def step(t, carry):
        hr, hi = carry
        rows = pl.ds(pl.multiple_of(t * SEQ_BLK, SEQ_BLK), SEQ_BLK)
        nhr = ar * hr - ai * hi + bu_scr[rows, 0:SSM_FLAT]
        nhi = ar * hi + ai * hr + bu_scr[rows, SSM_FLAT:2 * SSM_FLAT]
        bu_scr[rows, 0:SSM_FLAT] = nhr
        bu_scr[rows, SSM_FLAT:2 * SSM_FLAT] = nhi
        return nhr, nhi

    hr, hi = lax.fori_loop(0, tc, step, (h_scr[0], h_scr[1]))
    h_scr[0] = hr
    h_scr[1] = hi

    y = _dot_nt(bu_scr[...].astype(BF16), ccat_ref[...]) + d_ref[...] * u
    z = jax.nn.gelu(y)
    out = z * jax.nn.sigmoid(_dot(z.astype(BF16), gw_ref[...]) + gb_ref[...])
    y_ref[...] = out.astype(y_ref.dtype)

    @pl.when(c == pl.num_programs(1) - 1)
    def _():
        hr_out[...] = hr
        hi_out[...] = hi


def s5_mixer(proj, h0_re, h0_im, lam_re, lam_im, log_dt, b_re_bd, b_im_bd, c_cat, d_skip, glu_w, glu_b, *, t_len, tc):
    nseq = h0_re.shape[0] // SEQ_BLK
    nchunks = t_len // tc
    n = proj.shape[0]
    st = jax.ShapeDtypeStruct(h0_re.shape, F32)
    return pl.pallas_call(
        functools.partial(_s5_kernel, tc=tc),
        grid=(nseq, nchunks),
        in_specs=[_row_spec(tc, COL_SSM, nchunks),
                  _seq_spec(h0_re.shape), _seq_spec(h0_im.shape),
                  _full_spec((1, SSM_FLAT)), _full_spec((1, SSM_FLAT)), _full_spec((1, SSM_FLAT)),
                  _full_spec((GROUP_WIDTH, SSM_FLAT)), _full_spec((GROUP_WIDTH, SSM_FLAT)),
                  _full_spec((GROUP_WIDTH, 2 * SSM_FLAT)), _full_spec((1, GROUP_WIDTH)),
                  _full_spec((GROUP_WIDTH, GROUP_WIDTH)), _full_spec((1, GROUP_WIDTH))],
        out_specs=[_row_spec(tc, 0, nchunks), _seq_spec(h0_re.shape), _seq_spec(h0_im.shape)],
        out_shape=[jax.ShapeDtypeStruct((n, GROUP_WIDTH), BF16), st, st],
        scratch_shapes=[pltpu.VMEM((2, SEQ_BLK, SSM_FLAT), F32),
                        pltpu.VMEM((tc * SEQ_BLK, 2 * SSM_FLAT), F32)],
        compiler_params=_params(("parallel", "arbitrary")),
        name="s5_mixer",
    )(proj, h0_re, h0_im, lam_re, lam_im, log_dt, b_re_bd, b_im_bd, c_cat, d_skip, glu_w, glu_b)


def _pool_kernel(u_ref, buf_ref, w_ref, sc_ref, y_ref, nbuf_ref, ext_scr, *, tc, pos0):
    c = pl.program_id(1)

    @pl.when(c == 0)
    def _():
        ext_scr[0:POOL_BUF] = buf_ref[...]

    u = u_ref[...].reshape(tc, SEQ_BLK, GROUP_WIDTH)
    ext_scr[POOL_BUF:POOL_BUF + tc] = u
    a1 = ext_scr[...]
    a2 = a1[1:] + a1[:-1]
    a4 = a2[2:] + a2[:-2]
    a8 = a4[4:] + a4[:-4]
    a16 = a8[8:] + a8[:-8]
    sums = (a2[14:], a4[12:], a8[8:], a16)

    shape = (tc, SEQ_BLK, GROUP_WIDTH)
    pos = lax.broadcasted_iota(jnp.int32, shape, 0) + (c * tc + pos0)
    lane = lax.broadcasted_iota(jnp.int32, shape, 2)
    pooled = None
    for gi in reversed(range(len(POOL_WINDOWS))):
        win = POOL_WINDOWS[gi]
        mean = sums[gi] / jnp.minimum(pos + 1, win).astype(F32)
        pooled = mean if pooled is None else jnp.where(lane < (gi + 1) * HEAD, mean, pooled)
    pooled = (pooled - u).reshape(tc * SEQ_BLK, GROUP_WIDTH)
    y_ref[...] = (_dot(pooled.astype(BF16), w_ref[...]) * sc_ref[...]).astype(y_ref.dtype)

    nb = ext_scr[tc:tc + POOL_BUF]
    ext_scr[0:POOL_BUF] = nb

    @pl.when(c == pl.num_programs(1) - 1)
    def _():
        nbuf_ref[...] = nb


def pool_mixer(proj, buf, w_bd, scale, *, t_len, tc, pos0):
    nseq = buf.shape[1] // SEQ_BLK
    nchunks = t_len // tc
    n = proj.shape[0]
    buf_spec = pl.BlockSpec((POOL_BUF, SEQ_BLK, GROUP_WIDTH), lambda s, c: (0, s, 0))
    return pl.pallas_call(
        functools.partial(_pool_kernel, tc=tc, pos0=pos0),
        grid=(nseq, nchunks),
        in_specs=[_row_spec(tc, COL_POOL, nchunks), buf_spec,
                  _full_spec((GROUP_WIDTH, GROUP_WIDTH)), _full_spec((1, GROUP_WIDTH))],
        out_specs=[_row_spec(tc, 0, nchunks), buf_spec],
        out_shape=[jax.ShapeDtypeStruct((n, GROUP_WIDTH), BF16), jax.ShapeDtypeStruct(buf.shape, F32)],
        scratch_shapes=[pltpu.VMEM((tc + POOL_BUF, SEQ_BLK, GROUP_WIDTH), F32)],
        compiler_params=_params(("parallel", "arbitrary")),
        name="pool_mixer",
    )(proj, buf, w_bd, scale)


def _hgrn_lower_bound(logits_ref, layer):
    rows = [logits_ref[l:l + 1, :] for l in range(DEPTH)]
    m = functools.reduce(jnp.maximum, rows)
    es = [jnp.exp(r - m) for r in rows]
    tot = functools.reduce(lambda a, b: a + b, es)
    lb = jnp.zeros_like(m)
    for l in range(1, layer + 1):
        lb = lb + es[l] / tot
    return lb


def _hgrn_kernel(pq_ref, pf_ref, pi_ref, pg_ref, s0_ref, lbl_ref, ng_ref, y_ref, st_ref,
                 s_scr, q_scr, f_scr, k_scr, v_scr, o_scr, *, tc, layer):
    c = pl.program_id(1)
    shape3 = (tc, SEQ_BLK, GROUP_WIDTH)

    @pl.when(c == 0)
    def _():
        _load_unit_tiles(s0_ref, s_scr, transpose=True)

    lb = _hgrn_lower_bound(lbl_ref, layer)
    zf = pf_ref[...]
    f_scr[...] = (lb + (1.0 - lb) * jax.nn.sigmoid(zf)).reshape(shape3)
    k_scr[...] = ((1.0 - lb) * jax.nn.sigmoid(-zf)).reshape(shape3)
    q_scr[...] = jax.nn.silu(pq_ref[...]).reshape(shape3)
    v_scr[...] = pi_ref[...].reshape(shape3)

    ones2, eye = _unit_masks()
    s = s_scr[...].reshape(UNITS * HEAD, LANES)
    for t in range(tc):
        vcol = _seg_sum_mxu(jnp.where(eye, _unit_rows(v_scr, t), 0.0), ones2)
        s = s * _unit_rows(f_scr, t) + vcol * _unit_rows(k_scr, t)
        _store_unit_rows(o_scr, t, _seg_sum_mxu(s * _unit_rows(q_scr, t), ones2), eye)
    s_scr[...] = s.reshape(s_scr.shape)

    o = o_scr[...].reshape(tc * SEQ_BLK, GROUP_WIDTH)
    ms = _head_sums(o * o) * (1.0 / HEAD)
    out = o * lax.rsqrt(ms + HGRN_NORM_EPS) * ng_ref[...] * jax.nn.silu(pg_ref[...])
    y_ref[...] = out.astype(y_ref.dtype)

    @pl.when(c == pl.num_programs(1) - 1)
    def _():
        _store_unit_tiles(st_ref, s_scr, transpose=True)


def hgrn_mixer(proj, s0, lb_logits, norm_g, *, t_len, tc, layer):
    nseq = s0.shape[0] // SEQ_BLK
    nchunks = t_len // tc
    n = proj.shape[0]
    tile = pltpu.VMEM((tc, SEQ_BLK, GROUP_WIDTH), F32)
    return pl.pallas_call(
        functools.partial(_hgrn_kernel, tc=tc, layer=layer),
        grid=(nseq, nchunks),
        in_specs=[_row_spec(tc, COL_Q, nchunks), _row_spec(tc, COL_F, nchunks),
                  _row_spec(tc, COL_I, nchunks), _row_spec(tc, COL_G, nchunks),
                  _seq_spec(s0.shape), _full_spec((DEPTH, GROUP_WIDTH)), _full_spec((1, GROUP_WIDTH))],
        out_specs=[_row_spec(tc, 0, nchunks), _seq_spec(s0.shape)],
        out_shape=[jax.ShapeDtypeStruct((n, GROUP_WIDTH), BF16), jax.ShapeDtypeStruct(s0.shape, F32)],
        scratch_shapes=[pltpu.VMEM((SEQ_BLK, 2, HEAD, LANES), F32), tile, tile, tile, tile, tile],
        compiler_params=_params(("parallel", "arbitrary")),
        name="hgrn_mixer",
    )(proj, proj, proj, proj, s0, lb_logits, norm_g)


def _rwkv_kernel(pr_ref, pk_ref, pv_ref, pl_ref, sh0_ref, s0_ref, mu_ref, w0_ref, a0_ref, kk_ref, ka_ref,
                 rk_ref, lng_ref, lnb_ref, w2_ref, a2_ref, g2_ref, y_ref, st_ref, sh_ref,
                 s_scr, prev_scr, r_scr, w_scr, k_scr, v_scr, nkk_scr, kka_scr, o_scr, *, tc):
    c = pl.program_id(1)
    shape3 = (tc, SEQ_BLK, GROUP_WIDTH)
    gw = GROUP_WIDTH

    @pl.when(c == 0)
    def _():
        _load_unit_tiles(s0_ref, s_scr, transpose=False)
        prev_scr[...] = sh0_ref[...]

    def shifted(ref, j):
        x = ref[...].reshape(shape3)
        first = prev_scr[:, j * gw:(j + 1) * gw].reshape(1, SEQ_BLK, gw)
        prev = first if tc == 1 else jnp.concatenate([first, x[:-1]], axis=0)
        prev_scr[:, j * gw:(j + 1) * gw] = x[tc - 1]
        return (x + (prev - x) * mu_ref[:, j * gw:(j + 1) * gw]).reshape(tc * SEQ_BLK, gw)

    xr, xk, xv, xl = shifted(pr_ref, 0), shifted(pk_ref, 1), shifted(pv_ref, 2), shifted(pl_ref, 3)
    w = -jax.nn.softplus(-(w0_ref[...] + _dot(jnp.tanh(xl).astype(BF16), w2_ref[...]))) - 0.5
    decay = jnp.exp(-jnp.exp(w))
    a = jax.nn.sigmoid(a0_ref[...] + _dot(xl.astype(BF16), a2_ref[...]))
    g = _dot(jax.nn.sigmoid(xl).astype(BF16), g2_ref[...])
    kk = xk * kk_ref[...]
    kk = kk / jnp.maximum(jnp.sqrt(_head_sums(kk * kk)), 1e-12)
    k = xk * (1.0 + (a - 1.0) * ka_ref[...])

    r_scr[...] = xr.reshape(shape3)
    w_scr[...] = decay.reshape(shape3)
    k_scr[...] = k.reshape(shape3)
    v_scr[...] = xv.reshape(shape3)
    nkk_scr[...] = (-kk).reshape(shape3)
    kka_scr[...] = (kk * a).reshape(shape3)

    ones2, eye = _unit_masks()
    s = s_scr[...].reshape(UNITS * HEAD, LANES)
    for t in range(tc):
        sa = _seg_sum_mxu(s * _unit_rows(nkk_scr, t), ones2)
        vcol = _seg_sum_mxu(jnp.where(eye, _unit_rows(v_scr, t), 0.0), ones2)
        s = s * _unit_rows(w_scr, t) + sa * _unit_rows(kka_scr, t) + vcol * _unit_rows(k_scr, t)
        _store_unit_rows(o_scr, t, _seg_sum_mxu(s * _unit_rows(r_scr, t), ones2), eye)
    s_scr[...] = s.reshape(s_scr.shape)

    y = o_scr[...].reshape(tc * SEQ_BLK, gw)
    mean = _head_sums(y) * (1.0 / HEAD)
    d = y - mean
    var = _head_sums(d * d) * (1.0 / HEAD)
    yn = d * lax.rsqrt(var + RWKV_GN_EPS) * lng_ref[...] + lnb_ref[...]
    bonus = _head_sums(xr * k * rk_ref[...]) * xv
    y_ref[...] = ((yn + bonus) * g).astype(y_ref.dtype)

    @pl.when(c == pl.num_programs(1) - 1)
    def _():
        _store_unit_tiles(st_ref, s_scr, transpose=False)
        sh_ref[...] = prev_scr[...]


def rwkv_mixer(proj, shift0, s0, mu, w0, a0, k_k, k_a, r_k, ln_g, ln_b, w2p, a2p, g2p, *, t_len, tc):
    nseq = s0.shape[0] // SEQ_BLK
    nchunks = t_len // tc
    n = proj.shape[0]
    tile = pltpu.VMEM((tc, SEQ_BLK, GROUP_WIDTH), F32)
    vec = _full_spec((1, GROUP_WIDTH))
    mat = _full_spec((GROUP_WIDTH, GROUP_WIDTH))
    return pl.pallas_call(
        functools.partial(_rwkv_kernel, tc=tc),
        grid=(nseq, nchunks),
        in_specs=[_row_spec(tc, COL_R, nchunks), _row_spec(tc, COL_K, nchunks),
                  _row_spec(tc, COL_V, nchunks), _row_spec(tc, COL_LORA, nchunks),
                  _seq_spec(shift0.shape), _seq_spec(s0.shape), _full_spec((1, RWKV_PROJ)),
                  vec, vec, vec, vec, vec, vec, vec, mat, mat, mat],
        out_specs=[_row_spec(tc, 0, nchunks), _seq_spec(s0.shape), _seq_spec(shift0.shape)],
        out_shape=[jax.ShapeDtypeStruct((n, GROUP_WIDTH), BF16), jax.ShapeDtypeStruct(s0.shape, F32),
                   jax.ShapeDtypeStruct(shift0.shape, F32)],
        scratch_shapes=[pltpu.VMEM((SEQ_BLK, 2, HEAD, LANES), F32), pltpu.VMEM((SEQ_BLK, RWKV_PROJ), F32),
                        tile, tile, tile, tile, tile, tile, tile],
        compiler_params=_params(("parallel", "arbitrary")),
        name="rwkv_mixer",
    )(proj, proj, proj, proj, shift0, s0, mu, w0, a0, k_k, k_a, r_k, ln_g, ln_b, w2p, a2p, g2p)


def _dot_nt(a, b):
    return lax.dot_general(a, b, (((1,), (1,)), ((), ())), preferred_element_type=F32)


def _dot_tn(a, b):
    return lax.dot_general(a, b, (((0,), (0,)), ((), ())), preferred_element_type=F32)


def _iota(shape, dim):
    return lax.broadcasted_iota(jnp.int32, shape, dim)


def _head_of(idx):
    return lax.shift_right_logical(idx, HEAD.bit_length() - 1)


def _cumsum_rows(x):
    n = x.shape[0]
    tri = jnp.where(_iota((n, n), 0) >= _iota((n, n), 1), 1.0, 0.0).astype(BF16)
    hi = x.astype(BF16)
    rest = x - hi.astype(F32)
    mid = rest.astype(BF16)
    lo = (rest - mid.astype(F32)).astype(BF16)
    return _dot(tri, hi) + _dot(tri, mid) + _dot(tri, lo)


def _own_head(shape, rows_per_head):
    row_h = lax.shift_right_logical(_iota(shape, 0), rows_per_head.bit_length() - 1)
    return row_h == _head_of(_iota(shape, 1))


def _head_expand(x):
    xx = jnp.concatenate([x] * (GROUP_WIDTH // HEAD), axis=0)
    return jnp.where(_own_head(xx.shape, x.shape[0]), xx, 0.0)


def _head_collapse(xx):
    n = xx.shape[0] // (GROUP_WIDTH // HEAD)
    return xx[0:n] + xx[n:2 * n] + xx[2 * n:3 * n] + xx[3 * n:4 * n]


def _block_diag_mask():
    shape = (GROUP_WIDTH, GROUP_WIDTH)
    return _head_of(_iota(shape, 0)) == _head_of(_iota(shape, 1))


def _seq_col_spec(rows, col, ncols):
    return pl.BlockSpec((rows, GROUP_WIDTH), lambda b, c: (c, b * ncols + col))


HGRN_CHUNK = 128
HGRN_SUB = 16


HGRN_SEQS_PER_STEP = 4


def _run_staged(stages):
    for _ in itertools.zip_longest(*stages):
        pass


def _hgrn_chunk_one(pq_ref, pf_ref, pi_ref, pg_ref, lbl_ref, ng_ref, y_ref, w_scr, k_scr, b_scr, v_scr, p_scr,
                    o_scr, *, layer):
    L, n = HGRN_CHUNK, HGRN_SUB
    half = n // 2
    lb = _hgrn_lower_bound(lbl_ref, layer)
    z = pf_ref[...]
    g = jnp.logaddexp(jnp.log1p(-lb) + jax.nn.log_sigmoid(z), jnp.log(lb))
    kg = (1.0 - lb) * jax.nn.sigmoid(-z)
    q = jax.nn.silu(pq_ref[...])
    v = pi_ref[...]
    bc = _cumsum_rows(g)
    k_scr[...] = kg
    b_scr[...] = bc
    v_scr[...] = v
    yield

    bd = _block_diag_mask()
    ones_bd = jnp.where(bd, 1.0, 0.0).astype(BF16)
    rid = _iota((n, GROUP_WIDTH), 0)
    rid_lo = _iota((half, GROUP_WIDTH), 0) + half
    for sb in range(L // n):
        base = sb * n
        qs, bs = q[base:base + n], bc[base:base + n]
        q_lo, b_lo = q[base + half:base + n], bc[base + half:base + n]
        for s in range(n):
            ks, bsrow = k_scr[base + s:base + s + 1, :], b_scr[base + s:base + s + 1, :]
            if s < half:
                p_scr[s * n:(s + 1) * n, :] = qs * ks * jnp.where(rid >= s, jnp.exp(bs - bsrow), 0.0)
            else:
                p_scr[s * n + half:(s + 1) * n, :] = q_lo * ks * jnp.where(rid_lo >= s, jnp.exp(b_lo - bsrow), 0.0)
        r = _dot(p_scr[...].astype(BF16), ones_bd)
        acc = jnp.zeros((n, GROUP_WIDTH), F32)
        for s in range(n):
            acc = acc + r[s * n:(s + 1) * n] * v_scr[base + s:base + s + 1, :]
        o_scr[base:base + n, :] = acc
        yield

    vb = v.astype(BF16)
    for i in range(1, L // n):
        r0 = i * n
        ref = b_scr[r0 - 1:r0, :]
        qt = q[r0:r0 + n] * jnp.exp(bc[r0:r0 + n] - ref)
        kt = jnp.concatenate([kg[:r0] * jnp.exp(ref - bc[:r0]), jnp.zeros((L - r0, GROUP_WIDTH), F32)], axis=0)
        att = _dot_nt(_head_expand(qt).astype(BF16), kt.astype(BF16))
        yield
        ox = _dot(att.astype(BF16), vb)
        o_scr[r0:r0 + n, :] += _head_collapse(jnp.where(_own_head(ox.shape, n), ox, 0.0))
        yield

    w = w_scr[...]
    o = o_scr[...] + _dot_nt((q * jnp.exp(bc)).astype(BF16), w.astype(BF16))
    b_end = b_scr[L - 1:L, :]
    upd = _dot_tn(vb, (kg * jnp.exp(b_end - bc)).astype(BF16))
    w_scr[...] = w * jnp.exp(b_end) + jnp.where(bd, upd, 0.0)
    yield

    ms = _head_sums(o * o) * (1.0 / HEAD)
    out = o * lax.rsqrt(ms + HGRN_NORM_EPS) * ng_ref[...] * jax.nn.silu(pg_ref[...])
    y_ref[...] = out.astype(y_ref.dtype)


RWKV_CHUNK = 64


def _rwkv_chunk_one(pr_ref, pk_ref, pv_ref, pl_ref, mu_ref, w0_ref, a0_ref, kk_ref, ka_ref, rk_ref, lng_ref,
                    lnb_ref, w2_ref, a2_ref, g2_ref, y_ref, w_scr, prev_scr):
    L = RWKV_CHUNK
    gw = GROUP_WIDTH
    rid = _iota((L, gw), 0)

    def shifted(ref, j):
        x = ref[...]
        prev = jnp.where(rid == 0, prev_scr[:, j * gw:(j + 1) * gw], pltpu.roll(x, 1, axis=0))
        prev_scr[:, j * gw:(j + 1) * gw] = x[L - 1:L]
        return x + (prev - x) * mu_ref[:, j * gw:(j + 1) * gw]

    xr, xk, xv, xl = shifted(pr_ref, 0), shifted(pk_ref, 1), shifted(pv_ref, 2), shifted(pl_ref, 3)
    w = -jax.nn.softplus(-(w0_ref[...] + _dot(jnp.tanh(xl).astype(BF16), w2_ref[...]))) - 0.5
    lw = -jnp.exp(w)
    a = jax.nn.sigmoid(a0_ref[...] + _dot(xl.astype(BF16), a2_ref[...]))
    g = _dot(jax.nn.sigmoid(xl).astype(BF16), g2_ref[...])
    kk = xk * kk_ref[...]
    kk = kk / jnp.maximum(jnp.sqrt(_head_sums(kk * kk)), 1e-12)
    k = xk * (1.0 + (a - 1.0) * ka_ref[...])
    beta = kk * a
    yield

    cs = _cumsum_rows(lw)
    c_end = cs[L - 1:L]
    e_neg = jnp.exp(-cs)
    e_end = jnp.exp(c_end - cs)
    ar = jnp.concatenate([_head_expand(-kk * jnp.exp(cs - lw)), _head_expand(xr * jnp.exp(cs))], axis=0).astype(BF16)
    bk = jnp.concatenate([_head_expand(beta * e_neg), _head_expand(k * e_neg)], axis=0).astype(BF16)
    vx = _head_expand(xv).astype(BF16)
    yield

    nh = 4 * L
    gmat = _dot_nt(ar, bk)
    tt = _iota((nh, nh), 0) & (L - 1)
    ss = _iota((nh, nh), 1) & (L - 1)
    strict, incl = ss < tt, ss <= tt
    nab = jnp.where(strict, gmat[0:nh, 0:nh], 0.0)
    nak = jnp.where(strict, gmat[0:nh, nh:2 * nh], 0.0).astype(BF16)
    nrb = jnp.where(incl, gmat[nh:2 * nh, 0:nh], 0.0).astype(BF16)
    nrk = jnp.where(incl, gmat[nh:2 * nh, nh:2 * nh], 0.0).astype(BF16)
    yield

    ri, ci = _iota((nh, nh), 0), _iota((nh, nh), 1)

    def same_block(size):
        sh = size.bit_length() - 1
        return lax.shift_right_logical(ri, sh) == lax.shift_right_logical(ci, sh)

    base = 8
    m = jnp.where(same_block(base), nab, 0.0)
    t_inv = jnp.where(ri == ci, 1.0, 0.0) + m
    m = m.astype(BF16)
    for _ in range(base.bit_length() - 2):
        m = _dot(m, m).astype(BF16)
        yield
        t_inv = t_inv + _dot(t_inv.astype(BF16), m)
        yield
    size = base
    while size < L:
        off = jnp.where(same_block(2 * size), jnp.where(same_block(size), 0.0, nab), 0.0).astype(BF16)
        tb = t_inv.astype(BF16)
        half = _dot(tb, off).astype(BF16)
        yield
        t_inv = t_inv + _dot(half, tb)
        yield
        size *= 2

    wst = w_scr[...]
    sw = _dot_nt(ar, wst.astype(BF16))
    rhs = (sw[0:nh] + _dot(nak, vx)).astype(BF16)
    yield
    x = _dot(t_inv.astype(BF16), rhs)
    yield
    ux = x.astype(BF16)
    yx = sw[nh:2 * nh] + _dot(nrb, ux) + _dot(nrk, vx)
    y = _head_collapse(yx)
    u = _head_collapse(x)
    yield

    upd = _dot_tn(jnp.concatenate([u, xv], axis=0).astype(BF16),
                  jnp.concatenate([beta * e_end, k * e_end], axis=0).astype(BF16))
    w_scr[...] = wst * jnp.exp(c_end) + jnp.where(_block_diag_mask(), upd, 0.0)
    yield

    mean = _head_sums(y) * (1.0 / HEAD)
    d = y - mean
    var = _head_sums(d * d) * (1.0 / HEAD)
    yn = d * lax.rsqrt(var + RWKV_GN_EPS) * lng_ref[...] + lnb_ref[...]
    bonus = _head_sums(xr * k * rk_ref[...]) * xv
    y_ref[...] = ((yn + bonus) * g).astype(y_ref.dtype)


RWKV_SEQS_PER_STEP = 8


def _hgrn_rwkv_chunk_kernel(*refs, layer, rwkv_steps, hgrn_steps):
    nr, nh = RWKV_SEQS_PER_STEP, HGRN_SEQS_PER_STEP
    it = iter(refs)
    take = lambda n: [next(it) for _ in range(n)]
    r_seq, r_par, h_seq = take(4 * nr), take(11), take(4 * nh)
    lbl_ref, ng_ref = take(2)
    yr_ref, str_ref, shr_ref, yh_ref, sth_ref = take(5)
    wr_scr, prev_scr, wh_scr, k_scr, b_scr, v_scr, p_scr, o_scr = take(8)
    gw = GROUP_WIDTH
    g = pl.program_id(0)
    cr, ch = g % rwkv_steps, g % hgrn_steps

    @pl.when(cr == 0)
    def _():
        wr_scr[...] = jnp.zeros_like(wr_scr)
        prev_scr[...] = jnp.zeros_like(prev_scr)

    @pl.when(ch == 0)
    def _():
        wh_scr[...] = jnp.zeros_like(wh_scr)
        p_scr[...] = jnp.zeros_like(p_scr)

    rwkv = [_rwkv_chunk_one(*r_seq[4 * s:4 * s + 4], *r_par, yr_ref.at[:, s * gw:(s + 1) * gw], wr_scr.at[s],
                            prev_scr.at[s]) for s in range(nr)]
    hgrn = [_hgrn_chunk_one(*h_seq[4 * s:4 * s + 4], lbl_ref, ng_ref, yh_ref.at[:, s * gw:(s + 1) * gw],
                            wh_scr.at[s], k_scr.at[s], b_scr.at[s], v_scr.at[s], p_scr.at[s], o_scr.at[s],
                            layer=layer) for s in range(nh)]
    order = []
    for s in range(max(nr, nh)):
        order += rwkv[s:s + 1] + hgrn[s:s + 1]
    _run_staged(order)

    def head_blocks(w):
        return [w[h * HEAD:(h + 1) * HEAD, h * HEAD:(h + 1) * HEAD] for h in range(gw // HEAD)]

    @pl.when(cr == rwkv_steps - 1)
    def _():
        for s in range(nr):
            for h, blk in enumerate(head_blocks(wr_scr[s])):
                str_ref[s, h] = blk
        shr_ref[...] = prev_scr[...]

    @pl.when(ch == hgrn_steps - 1)
    def _():
        for s in range(nh):
            for h, blk in enumerate(head_blocks(wh_scr[s])):
                sth_ref[s, h] = blk.T


def hgrn_rwkv_chunk_mixer(proj2, lb_logits, norm_g, mu, w0, a0, k_k, k_a, r_k, ln_g, ln_b, w2p, a2p, g2p, *,
                          bsz, t_len, layer):
    nr, nh, lr, lh, n = RWKV_SEQS_PER_STEP, HGRN_SEQS_PER_STEP, RWKV_CHUNK, HGRN_CHUNK, HGRN_SUB
    gw = GROUP_WIDTH
    ncols = PROJ_WIDTH // gw
    heads = (gw // HEAD, HEAD, HEAD)
    rwkv_steps, hgrn_steps = t_len // lr, t_len // lh
    steps = (bsz // nr) * rwkv_steps
    assert (bsz // nh) * hgrn_steps == steps
    const = lambda shape: pl.BlockSpec(shape, lambda g: (0,) * len(shape))
    r_specs = [pl.BlockSpec((lr, gw), functools.partial(
                   lambda g, s, col: (g % rwkv_steps, ((g // rwkv_steps) * nr + s) * ncols + col), s=s, col=col))
               for s in range(nr) for col in (COL_R, COL_K, COL_V, COL_LORA)]
    h_specs = [pl.BlockSpec((lh, gw), functools.partial(
                   lambda g, s, col: (g % hgrn_steps, ((g // hgrn_steps) * nh + s) * ncols + col), s=s, col=col))
               for s in range(nh) for col in (COL_Q, COL_F, COL_I, COL_G)]
    vec, mat = const((1, gw)), const((gw, gw))
    tile = pltpu.VMEM((nh, lh, gw), F32)
    yr, st_r, sh_r, yh, st_h = pl.pallas_call(
        functools.partial(_hgrn_rwkv_chunk_kernel, layer=layer, rwkv_steps=rwkv_steps, hgrn_steps=hgrn_steps),
        grid=(steps,),
        in_specs=r_specs + [const((1, RWKV_PROJ)), vec, vec, vec, vec, vec, vec, vec, mat, mat, mat]
                 + h_specs + [const((DEPTH, gw)), vec],
        out_specs=[pl.BlockSpec((lr, nr * gw), lambda g: (g % rwkv_steps, g // rwkv_steps)),
                   pl.BlockSpec((nr,) + heads, lambda g: (g // rwkv_steps, 0, 0, 0)),
                   pl.BlockSpec((nr, 1, RWKV_PROJ), lambda g: (g // rwkv_steps, 0, 0)),
                   pl.BlockSpec((lh, nh * gw), lambda g: (g % hgrn_steps, g // hgrn_steps)),
                   pl.BlockSpec((nh,) + heads, lambda g: (g // hgrn_steps, 0, 0, 0))],
        out_shape=[jax.ShapeDtypeStruct((t_len, bsz * gw), BF16), jax.ShapeDtypeStruct((bsz,) + heads, F32),
                   jax.ShapeDtypeStruct((bsz, 1, RWKV_PROJ), F32),
                   jax.ShapeDtypeStruct((t_len, bsz * gw), BF16), jax.ShapeDtypeStruct((bsz,) + heads, F32)],
        scratch_shapes=[pltpu.VMEM((nr, gw, gw), F32), pltpu.VMEM((nr, 1, RWKV_PROJ), F32),
                        pltpu.VMEM((nh, gw, gw), F32), tile, tile, tile, pltpu.VMEM((nh, n * n, gw), F32), tile],
        compiler_params=_params(("arbitrary",)),
        name="hgrn_rwkv_chunk_mixer",
    )(*([proj2] * (4 * nr)), mu, w0, a0, k_k, k_a, r_k, ln_g, ln_b, w2p, a2p, g2p,
      *([proj2] * (4 * nh)), lb_logits, norm_g)
    return yh, st_h, yr, st_r, sh_r


S5_CHUNK = 64


def _s5_seq_kernel(*refs, nseq):
    u_refs = refs[:nseq]
    lr_ref, li_ref, ldt_ref, bre_ref, bim_ref, ccat_ref, d_ref, gw_ref, gb_ref = refs[nseq:nseq + 9]
    y_ref, hr_out, hi_out, h_scr, bu_scr, bb_scr, ab_scr, perm_scr, u_scr = refs[nseq + 9:]
    tc = S5_CHUNK
    n = nseq * tc
    c = pl.program_id(0)

    @pl.when(c == 0)
    def _():
        h_scr[...] = jnp.zeros_like(h_scr)
        lr, li = lr_ref[...], li_ref[...]
        dt = jnp.exp(ldt_ref[...])
        mag = jnp.exp(lr * dt)
        ab_re, ab_im = mag * jnp.cos(li * dt), mag * jnp.sin(li * dt)
        den = lr * lr + li * li
        zr, zi = ab_re - 1.0, ab_im
        cr = (zr * lr + zi * li) / den
        ci = (zi * lr - zr * li) / den
        bre, bim = bre_ref[...], bim_ref[...]
        bb_scr[:, 0:SSM_FLAT] = (cr * bre - ci * bim).astype(BF16)
        bb_scr[:, SSM_FLAT:2 * SSM_FLAT] = (cr * bim + ci * bre).astype(BF16)
        ab_scr[0] = jnp.broadcast_to(ab_re, (nseq, SSM_FLAT))
        ab_scr[1] = jnp.broadcast_to(ab_im, (nseq, SSM_FLAT))
        ri, cj = _iota((n, n), 0), _iota((n, n), 1)
        lseq, lt = nseq.bit_length() - 1, tc.bit_length() - 1
        perm_scr[0] = jnp.where(cj == (ri & (nseq - 1)) * tc + lax.shift_right_logical(ri, lseq), 1.0, 0.0).astype(BF16)
        perm_scr[1] = jnp.where(cj == (ri & (tc - 1)) * nseq + lax.shift_right_logical(ri, lt), 1.0, 0.0).astype(BF16)

        bu_scr[1] = jnp.zeros(bu_scr.shape[1:], F32)
        u_scr[1] = jnp.zeros(u_scr.shape[1:], F32)

    last = pl.num_programs(0) - 1

    def pipeline(slot):

        def scan_chunk():
            u = jnp.concatenate([r[...] for r in u_refs], axis=0)
            u_scr[slot] = u
            u_t = _dot(perm_scr[0], u.astype(BF16)).astype(BF16)
            bu_scr[slot] = _dot(u_t, bb_scr[...])
            yield
            ar, ai = ab_scr[0], ab_scr[1]
            hr0, hi0 = h_scr[0], h_scr[1]
            hr, hi = hr0, hi0
            for t in range(tc):
                rows = slice(t * nseq, (t + 1) * nseq)
                hr, hi = (ar * hr - ai * hi + bu_scr[slot, rows, 0:SSM_FLAT],
                          ar * hi + ai * hr + bu_scr[slot, rows, SSM_FLAT:2 * SSM_FLAT])
                bu_scr[slot, rows, 0:SSM_FLAT] = hr
                bu_scr[slot, rows, SSM_FLAT:2 * SSM_FLAT] = hi
                if t % 8 == 7:
                    yield
            h_scr[0] = jnp.where(c < last, hr, hr0)
            h_scr[1] = jnp.where(c < last, hi, hi0)

        def emit_chunk():
            prev = 1 - slot
            y_t = _dot_nt(bu_scr[prev].astype(BF16), ccat_ref[...])
            yield
            hi_p = y_t.astype(BF16)
            rest = y_t - hi_p.astype(F32)
            mid_p = rest.astype(BF16)
            lo_p = (rest - mid_p.astype(F32)).astype(BF16)
            to_seq = perm_scr[1]
            y = _dot(to_seq, hi_p) + _dot(to_seq, mid_p) + _dot(to_seq, lo_p) + d_ref[...] * u_scr[prev]
            yield
            z = jax.nn.gelu(y)
            gate = _dot(z.astype(BF16), gw_ref[...])
            yield
            out = z * jax.nn.sigmoid(gate + gb_ref[...])
            for b in range(nseq):
                y_ref[:, b * GROUP_WIDTH:(b + 1) * GROUP_WIDTH] = out[b * tc:(b + 1) * tc].astype(y_ref.dtype)

        _run_staged([scan_chunk(), emit_chunk()])

    for parity in range(2):
        pl.when(c % 2 == parity)(functools.partial(pipeline, parity))

    @pl.when(c == last)
    def _():
        hr_out[...] = h_scr[0]
        hi_out[...] = h_scr[1]


def s5_seq_mixer(proj2, lam_re, lam_im, log_dt, b_re_bd, b_im_bd, c_cat, d_skip, glu_w, glu_b, *, bsz, t_len):
    tc = S5_CHUNK
    ncols = PROJ_WIDTH // GROUP_WIDTH
    full = lambda shape: pl.BlockSpec(shape, lambda c: (0,) * len(shape))
    st = jax.ShapeDtypeStruct((bsz, SSM_FLAT), F32)
    nchunks = t_len // tc
    u_specs = [pl.BlockSpec((tc, GROUP_WIDTH), functools.partial(
                   lambda c, b: (jnp.minimum(c, nchunks - 1), b * ncols + COL_SSM), b=b)) for b in range(bsz)]
    return pl.pallas_call(
        functools.partial(_s5_seq_kernel, nseq=bsz),
        grid=(nchunks + 1,),
        in_specs=u_specs + [full((1, SSM_FLAT)), full((1, SSM_FLAT)), full((1, SSM_FLAT)),
                            full((GROUP_WIDTH, SSM_FLAT)), full((GROUP_WIDTH, SSM_FLAT)),
                            full((GROUP_WIDTH, 2 * SSM_FLAT)), full((1, GROUP_WIDTH)),
                            full((GROUP_WIDTH, GROUP_WIDTH)), full((1, GROUP_WIDTH))],
        out_specs=[pl.BlockSpec((tc, bsz * GROUP_WIDTH), lambda c: (jnp.maximum(c - 1, 0), 0)),
                   full((bsz, SSM_FLAT)), full((bsz, SSM_FLAT))],
        out_shape=[jax.ShapeDtypeStruct((t_len, bsz * GROUP_WIDTH), BF16), st, st],
        scratch_shapes=[pltpu.VMEM((2, bsz, SSM_FLAT), F32), pltpu.VMEM((2, bsz * tc, 2 * SSM_FLAT), F32),
                        pltpu.VMEM((GROUP_WIDTH, 2 * SSM_FLAT), BF16), pltpu.VMEM((2, bsz, SSM_FLAT), F32),
                        pltpu.VMEM((2, bsz * tc, bsz * tc), BF16), pltpu.VMEM((2, bsz * tc, GROUP_WIDTH), F32)],
        compiler_params=_params(("arbitrary",)),
        name="s5_seq_mixer",
    )(*([proj2] * bsz), lam_re, lam_im, log_dt, b_re_bd, b_im_bd, c_cat, d_skip, glu_w, glu_b)


POOL_CHUNK = 1024
POOL_HIST = 16


def _pool_seq_kernel(u_ref, w_ref, sc_ref, y_ref, nbuf_ref, ext_scr):
    L, hist = u_ref.shape[0], POOL_HIST
    c = pl.program_id(1)

    @pl.when(c == 0)
    def _():
        ext_scr[0:hist] = jnp.zeros((hist, GROUP_WIDTH), F32)

    u = u_ref[...]
    ext_scr[hist:hist + L] = u
    e = ext_scr[...]
    a2 = e + pltpu.roll(e, 1, axis=0)
    a4 = a2 + pltpu.roll(a2, 2, axis=0)
    a8 = a4 + pltpu.roll(a4, 4, axis=0)
    a16 = a8 + pltpu.roll(a8, 8, axis=0)
    sums = (a2[hist:], a4[hist:], a8[hist:], a16[hist:])

    shape = (L, GROUP_WIDTH)
    pos = _iota(shape, 0) + c * L
    lane = _iota(shape, 1)
    pooled = None
    for gi in reversed(range(len(POOL_WINDOWS))):
        win = POOL_WINDOWS[gi]
        mean = sums[gi] / jnp.minimum(pos + 1, win).astype(F32)
        pooled = mean if pooled is None else jnp.where(lane < (gi + 1) * HEAD, mean, pooled)
    y_ref[...] = (_dot((pooled - u).astype(BF16), w_ref[...]) * sc_ref[...]).astype(y_ref.dtype)

    nb = ext_scr[L:L + hist]
    ext_scr[0:hist] = nb

    @pl.when(c == pl.num_programs(1) - 1)
    def _():
        nbuf_ref[...] = nb


def pool_seq_mixer(proj2, w_bd, scale, *, bsz, t_len):
    L, hist = min(POOL_CHUNK, t_len), POOL_HIST
    ncols = PROJ_WIDTH // GROUP_WIDTH
    return pl.pallas_call(
        _pool_seq_kernel,
        grid=(bsz, t_len // L),
        in_specs=[_seq_col_spec(L, COL_POOL, ncols),
                  pl.BlockSpec((GROUP_WIDTH, GROUP_WIDTH), lambda b, c: (0, 0)),
                  pl.BlockSpec((1, GROUP_WIDTH), lambda b, c: (0, 0))],
        out_specs=[_seq_col_spec(L, 0, 1), pl.BlockSpec((hist, GROUP_WIDTH), lambda b, c: (b, 0))],
        out_shape=[jax.ShapeDtypeStruct((t_len, bsz * GROUP_WIDTH), BF16),
                   jax.ShapeDtypeStruct((bsz * hist, GROUP_WIDTH), F32)],
        scratch_shapes=[pltpu.VMEM((hist + L, GROUP_WIDTH), F32)],
        compiler_params=_params(("parallel", "arbitrary")),
        name="pool_seq_mixer",
    )(proj2, w_bd, scale)


def _mix_mlp_kernel(h_ref, ya_ref, yb_ref, yc_ref, yd_ref, wo_ref, g2_ref, wu_ref, wd_ref, gf_ref, o_ref,
                    h1_scr, xn_scr, acc_scr, *, final_norm):
    j = pl.program_id(1)
    gw = GROUP_WIDTH

    @pl.when(j == 0)
    def _():
        mix = (_dot(ya_ref[...], wo_ref[0:gw]) + _dot(yb_ref[...], wo_ref[gw:2 * gw])
               + _dot(yc_ref[...], wo_ref[2 * gw:3 * gw]) + _dot(yd_ref[...], wo_ref[3 * gw:4 * gw]))
        h1 = h_ref[...] + mix
        h1_scr[...] = h1
        xn_scr[...] = _rms(h1, g2_ref[...]).astype(BF16)
        acc_scr[...] = jnp.zeros_like(acc_scr)

    up = _dot(xn_scr[...], wu_ref[...])
    act = jnp.square(jnp.maximum(up, 0.0)).astype(BF16)
    acc_scr[...] += _dot(act, wd_ref[...])

    @pl.when(j == pl.num_programs(1) - 1)
    def _():
        out = h1_scr[...] + acc_scr[...]
        if final_norm:
            out = _rms(out, gf_ref[...])
        o_ref[...] = out


def mix_mlp(h, ys, w_out, g2, w_up, w_down, g_final, *, final_norm, nseq=0):
    n = h.shape[0]
    tm = min(512, n // max(nseq, 1))
    tf = 2048
    nt = n // tm // max(nseq, 1)
    row = lambda w: pl.BlockSpec((tm, w), lambda i, j: (i, 0))
    mix = pl.BlockSpec((tm, GROUP_WIDTH), _tile_map(nseq, nt))
    return pl.pallas_call(
        functools.partial(_mix_mlp_kernel, final_norm=final_norm),
        grid=(n // tm, D_FF // tf),
        in_specs=[row(D_MODEL), mix, mix, mix, mix,
                  pl.BlockSpec((D_MODEL, D_MODEL), lambda i, j: (0, 0)),
                  pl.BlockSpec((1, D_MODEL), lambda i, j: (0, 0)),
                  pl.BlockSpec((D_MODEL, tf), lambda i, j: (0, j)),
                  pl.BlockSpec((tf, D_MODEL), lambda i, j: (j, 0)),
                  pl.BlockSpec((1, D_MODEL), lambda i, j: (0, 0))],
        out_specs=row(D_MODEL),
        out_shape=jax.ShapeDtypeStruct((n, D_MODEL), F32),
        scratch_shapes=[pltpu.VMEM((tm, D_MODEL), F32), pltpu.VMEM((tm, D_MODEL), BF16),
                        pltpu.VMEM((tm, D_MODEL), F32)],
        compiler_params=_params(("parallel", "arbitrary")),
        name="mix_mlp",
    )(h, *ys, w_out, g2, w_up, w_down, g_final)


def _block_diag(blocks):
    g, r, c = blocks.shape
    tiled = jnp.tile(blocks.reshape(g * r, c), (1, g))
    row_blk = lax.broadcasted_iota(jnp.int32, tiled.shape, 0) // r
    col_blk = lax.broadcasted_iota(jnp.int32, tiled.shape, 1) // c
    return jnp.where(row_blk == col_blk, tiled, 0.0)


def _pad_rows(w, start):
    rows = GROUP_WIDTH - start - w.shape[0]
    return jnp.concatenate([jnp.zeros((start, GROUP_WIDTH), w.dtype), w, jnp.zeros((rows, GROUP_WIDTH), w.dtype)])


def _layer_params(l, P):
    row = lambda a: a.reshape(1, -1)
    q = {}
    q["norm1_g"] = row(P["norm1_g"][l])
    q["w_in"] = P["w_in"][l].astype(BF16)
    q["lam_re"] = row(P["ssm_lambda_re"][l])
    q["lam_im"] = row(P["ssm_lambda_im"][l])
    q["log_dt"] = row(jnp.repeat(P["ssm_log_dt"][l], SSM_STATE))
    q["b_re"] = _block_diag(P["ssm_b_re"][l].transpose(0, 2, 1))
    q["b_im"] = _block_diag(P["ssm_b_im"][l].transpose(0, 2, 1))
    q["c_cat"] = jnp.concatenate([_block_diag(P["ssm_c_re"][l]), -_block_diag(P["ssm_c_im"][l])],
                                 axis=1).astype(BF16)
    q["ssm_d"] = row(P["ssm_d"][l])
    q["glu_w"] = P["ssm_glu_w"][l].astype(BF16)
    q["glu_b"] = row(P["ssm_glu_b"][l])
    q["hgrn_norm_g"] = row(P["hgrn_norm_g"][l])
    q["mu"] = row(P["rwkv_mu"][l])
    for name in ("w0", "a0", "k_k", "k_a", "r_k", "ln_g", "ln_b"):
        q[name] = row(P["rwkv_" + name][l])
    q["w2p"] = _pad_rows(P["rwkv_w2"][l], 0).astype(BF16)
    q["a2p"] = _pad_rows(P["rwkv_a2"][l], DECAY_LORA).astype(BF16)
    q["g2p"] = _pad_rows(P["rwkv_g2"][l], DECAY_LORA + AAA_LORA).astype(BF16)
    q["pool_w"] = _block_diag(P["pool_w"][l]).astype(BF16)
    q["pool_scale"] = row(P["pool_scale"][l])
    q["w_out"] = P["w_out"][l].astype(BF16)
    q["norm2_g"] = row(P["norm2_g"][l])
    q["mlp_up"] = P["mlp_up"][l].astype(BF16)
    q["mlp_down"] = P["mlp_down"][l].astype(BF16)
    return q


def _trunk_fresh(x_rows, bsz, t_len, layer_params, P):
    h = x_rows
    new = [[] for _ in range(6)]
    g_final = P["norm_f_g"].reshape(1, -1)
    for l in range(DEPTH):
        q = layer_params[l]
        proj2 = rms_proj(h, q["norm1_g"], q["w_in"], nseq=bsz)
        y_a, s_re, s_im = s5_seq_mixer(proj2, q["lam_re"], q["lam_im"], q["log_dt"], q["b_re"], q["b_im"],
                                       q["c_cat"], q["ssm_d"], q["glu_w"], q["glu_b"], bsz=bsz, t_len=t_len)
        y_b, s_hg, y_c, s_wkv, s_sh = hgrn_rwkv_chunk_mixer(
            proj2, P["hgrn_lb_logits"], q["hgrn_norm_g"], q["mu"], q["w0"], q["a0"], q["k_k"], q["k_a"], q["r_k"],
            q["ln_g"], q["ln_b"], q["w2p"], q["a2p"], q["g2p"], bsz=bsz, t_len=t_len, layer=l)
        y_d, s_pool = pool_seq_mixer(proj2, q["pool_w"], q["pool_scale"], bsz=bsz, t_len=t_len)
        h = mix_mlp(h, (y_a, y_b, y_c, y_d), q["w_out"], q["norm2_g"], q["mlp_up"], q["mlp_down"], g_final,
                    final_norm=(l == DEPTH - 1), nseq=bsz)
        s_pool = s_pool.reshape(bsz, POOL_HIST, GROUP_WIDTH)[:, POOL_HIST - POOL_BUF:]
        for lst, s in zip(new, (s_re, s_im, s_hg, s_wkv, s_sh, s_pool)):
            lst.append(s)
    return h, new


def _trunk_carry(x_rows, states, pos0, t_len, layer_params, P):
    ssm_re0, ssm_im0, hgrn0, wkv0, shift0, pool0 = states
    h = x_rows
    new = [[] for _ in range(6)]
    g_final = P["norm_f_g"].reshape(1, -1)
    tc = t_len
    for l in range(DEPTH):
        q = layer_params[l]
        proj = rms_proj(h, q["norm1_g"], q["w_in"])
        y_a, s_re, s_im = s5_mixer(proj, ssm_re0[l], ssm_im0[l], q["lam_re"], q["lam_im"], q["log_dt"], q["b_re"],
                                   q["b_im"], q["c_cat"], q["ssm_d"], q["glu_w"], q["glu_b"], t_len=t_len, tc=tc)
        y_b, s_hg = hgrn_mixer(proj, hgrn0[l], P["hgrn_lb_logits"], q["hgrn_norm_g"], t_len=t_len, tc=tc, layer=l)
        y_c, s_wkv, s_sh = rwkv_mixer(proj, shift0[l], wkv0[l], q["mu"], q["w0"], q["a0"], q["k_k"], q["k_a"],
                                      q["r_k"], q["ln_g"], q["ln_b"], q["w2p"], q["a2p"], q["g2p"],
                                      t_len=t_len, tc=tc)
        y_d, s_pool = pool_mixer(proj, pool0[l], q["pool_w"], q["pool_scale"], t_len=t_len, tc=tc, pos0=pos0)
        h = mix_mlp(h, (y_a, y_b, y_c, y_d), q["w_out"], q["norm2_g"], q["mlp_up"], q["mlp_down"], g_final,
                    final_norm=(l == DEPTH - 1))
        for lst, s in zip(new, (s_re, s_im, s_hg, s_wkv, s_sh,
                                s_pool.transpose(1, 0, 2))):
            lst.append(s)
    return h, new


def _states_out(new, bsz):
    s_re, s_im, s_hg, s_wkv, s_sh, s_pool = new
    return (jnp.stack([s.reshape(bsz, SSM_GROUPS, SSM_STATE) for s in s_re]),
            jnp.stack([s.reshape(bsz, SSM_GROUPS, SSM_STATE) for s in s_im]),
            jnp.stack(s_hg),
            jnp.stack(s_wkv),
            jnp.stack([s.reshape(bsz, 1, RWKV_PROJ) for s in s_sh]),
            jnp.stack(s_pool))


def kernel(x_prompt, x_sample, state_ssm_re, state_ssm_im, state_hgrn, state_wkv, state_shift, state_pool, norm1_g, w_in, ssm_lambda_re, ssm_lambda_im, ssm_log_dt, ssm_b_re, ssm_b_im, ssm_c_re, ssm_c_im, ssm_d, ssm_glu_w, ssm_glu_b, hgrn_lb_logits, hgrn_norm_g, rwkv_mu, rwkv_w0, rwkv_w2, rwkv_a0, rwkv_a2, rwkv_g2, rwkv_k_k, rwkv_k_a, rwkv_r_k, rwkv_ln_g, rwkv_ln_b, pool_w, pool_scale, w_out, norm2_g, mlp_up, mlp_down, norm_f_g):
    P = dict(norm1_g=norm1_g, w_in=w_in, ssm_lambda_re=ssm_lambda_re, ssm_lambda_im=ssm_lambda_im,
             ssm_log_dt=ssm_log_dt, ssm_b_re=ssm_b_re, ssm_b_im=ssm_b_im, ssm_c_re=ssm_c_re, ssm_c_im=ssm_c_im,
             ssm_d=ssm_d, ssm_glu_w=ssm_glu_w, ssm_glu_b=ssm_glu_b, hgrn_lb_logits=hgrn_lb_logits,
             hgrn_norm_g=hgrn_norm_g, rwkv_mu=rwkv_mu, rwkv_w0=rwkv_w0, rwkv_w2=rwkv_w2, rwkv_a0=rwkv_a0,
             rwkv_a2=rwkv_a2, rwkv_g2=rwkv_g2, rwkv_k_k=rwkv_k_k, rwkv_k_a=rwkv_k_a, rwkv_r_k=rwkv_r_k,
             rwkv_ln_g=rwkv_ln_g, rwkv_ln_b=rwkv_ln_b, pool_w=pool_w, pool_scale=pool_scale, w_out=w_out,
             norm2_g=norm2_g, mlp_up=mlp_up, mlp_down=mlp_down, norm_f_g=norm_f_g)
    layer_params = [_layer_params(l, P) for l in range(DEPTH)]

    bp, t_p, _ = x_prompt.shape
    yp, new_p = _trunk_fresh(x_prompt.reshape(bp * t_p, D_MODEL), bp, t_p, layer_params, P)
    y_prompt = yp.reshape(bp, t_p, D_MODEL)

    bs, t_s, _ = x_sample.shape
    nblk = bs // SEQ_BLK
    xs = x_sample.reshape(nblk, SEQ_BLK, t_s, D_MODEL).transpose(0, 2, 1, 3).reshape(bs * t_s, D_MODEL)
    st_s = ([state_ssm_re[l].reshape(bs, SSM_FLAT) for l in range(DEPTH)],
            [state_ssm_im[l].reshape(bs, SSM_FLAT) for l in range(DEPTH)],
            [state_hgrn[l] for l in range(DEPTH)],
            [state_wkv[l] for l in range(DEPTH)],
            [state_shift[l].reshape(bs, RWKV_PROJ) for l in range(DEPTH)],
            [state_pool[l].transpose(1, 0, 2) for l in range(DEPTH)])
    ys, new_s = _trunk_carry(xs, st_s, PAST_LEN, t_s, layer_params, P)
    y_sample = ys.reshape(nblk, t_s, SEQ_BLK, D_MODEL).transpose(0, 2, 1, 3).reshape(bs, t_s, D_MODEL)

    return (y_prompt, y_sample) + _states_out(new_p, bp) + _states_out(new_s, bs)
```

```python
import functools
import itertools

import jax
import jax.numpy as jnp
from jax import lax
from jax.experimental import pallas as pl
from jax.experimental.pallas import tpu as pltpu

F32 = jnp.float32
BF16 = jnp.bfloat16

D_MODEL = 1024
DEPTH = 2
PAST_LEN = 16384
GROUP_WIDTH = 256
HEAD = 64
SSM_GROUPS = 16
SSM_CH = 16
SSM_STATE = 64
SSM_FLAT = SSM_GROUPS * SSM_STATE
POOL_WINDOWS = (2, 4, 8, 16)
POOL_BUF = 15
DECAY_LORA = 64
AAA_LORA = 64
GATE_LORA = 128
RWKV_PROJ = 1024
PROJ_WIDTH = 2560
D_FF = 4096
NORM_EPS = 1e-6
HGRN_NORM_EPS = 1e-5
RWKV_GN_EPS = 64e-5

SEQ_BLK = 8
LANES = 128
VMEM_LIMIT = 48 * 1024 * 1024

COL_SSM, COL_Q, COL_F, COL_I, COL_G, COL_R, COL_K, COL_V, COL_LORA, COL_POOL = range(10)


def _params(sem):
    return pltpu.CompilerParams(dimension_semantics=sem, vmem_limit_bytes=VMEM_LIMIT)


def _dot(a, b):
    return jnp.dot(a, b, preferred_element_type=F32)


def _rms(x, g):
    return x * lax.rsqrt(jnp.mean(x * x, axis=-1, keepdims=True) + NORM_EPS) * g


def _rms_proj_kernel(x_ref, g_ref, w_ref, o_ref):
    o_ref[...] = _dot(_rms(x_ref[...], g_ref[...]).astype(BF16), w_ref[...])


def _tile_map(nseq, nt):
    if nseq == 0:
        return lambda r, *_: (r, 0)
    return lambda r, *_: (r % nt, r // nt)


def rms_proj(x, g, w, *, layer, nseq=0):
    n = x.shape[0]
    tm = min(1024, n // max(nseq, 1))
    nt = n // tm // max(nseq, 1)
    out_shape = (n, PROJ_WIDTH) if nseq == 0 else (n // nseq, nseq * PROJ_WIDTH)
    return pl.pallas_call(
        _rms_proj_kernel,
        grid=(n // tm,),
        in_specs=[pl.BlockSpec((tm, D_MODEL), lambda i: (i, 0)),
                  pl.BlockSpec((1, D_MODEL), lambda i: (0, 0)),
                  pl.BlockSpec((None, D_MODEL, PROJ_WIDTH), lambda i: (layer, 0, 0))],
        out_specs=pl.BlockSpec((tm, PROJ_WIDTH), _tile_map(nseq, nt)),
        out_shape=jax.ShapeDtypeStruct(out_shape, F32),
        compiler_params=_params(("parallel",)),
        name="rms_proj",
    )(x, g, w)


def _row_spec(tc, col, nchunks):
    return pl.BlockSpec((tc * SEQ_BLK, GROUP_WIDTH), lambda s, c: (s * nchunks + c, col))


def _full_spec(shape):
    nd = len(shape)
    return pl.BlockSpec(shape, lambda s, c: (0,) * nd)


def _seq_spec(shape):
    nd = len(shape)
    return pl.BlockSpec((SEQ_BLK,) + shape[1:], lambda s, c: (s,) + (0,) * (nd - 1))


def _head_sums(x):
    lane = lax.broadcasted_iota(jnp.int32, x.shape, 1)
    out = jnp.zeros_like(x)
    for h in range(GROUP_WIDTH // HEAD):
        m = (lane >= h * HEAD) & (lane < (h + 1) * HEAD)
        s = jnp.sum(jnp.where(m, x, 0.0), axis=1, keepdims=True)
        out = jnp.where(m, s, out)
    return out


UNITS = SEQ_BLK * 2


def _unit_masks():
    r, c = _iota((2 * LANES, LANES), 0), _iota((2 * LANES, LANES), 1)
    ones2 = jnp.where(_head_of(r & (LANES - 1)) == _head_of(c), 1.0, 0.0).astype(BF16)
    shape = (UNITS * HEAD, LANES)
    eye = (_iota(shape, 1) & (HEAD - 1)) == (_iota(shape, 0) & (HEAD - 1))
    return ones2, eye


def _seg_sum_mxu(p, ones2):
    hi = p.astype(BF16)
    lo = (p - hi.astype(F32)).astype(BF16)
    return _dot(jnp.concatenate([hi, lo], axis=1), ones2)


def _unit_rows(ref, t):
    return jnp.concatenate([jnp.broadcast_to(ref[t, b:b + 1, p * LANES:(p + 1) * LANES], (HEAD, LANES))
                            for b in range(SEQ_BLK) for p in range(2)], axis=0)


def _load_unit_tiles(s0_ref, s_scr, *, transpose):
    for b in range(SEQ_BLK):
        for p in range(2):
            heads = [s0_ref[b, 2 * p + j] for j in range(2)]
            s_scr[b, p] = jnp.concatenate([h.T if transpose else h for h in heads], axis=1)


def _store_unit_tiles(st_ref, s_scr, *, transpose):
    for b in range(SEQ_BLK):
        for p in range(2):
            tile = s_scr[b, p]
            for j in range(2):
                h = tile[:, j * HEAD:(j + 1) * HEAD]
                st_ref[b, 2 * p + j] = h.T if transpose else h


def _store_unit_rows(ref, t, cols, eye):
    picked = jnp.where(eye, cols, 0.0)
    for b in range(SEQ_BLK):
        for p in range(2):
            u = b * 2 + p
            ref[t, b:b + 1, p * LANES:(p + 1) * LANES] = jnp.sum(picked[u * HEAD:(u + 1) * HEAD], axis=0, keepdims=True)


def _s5_kernel(u_ref, h0r_ref, h0i_ref, lr_ref, li_ref, ldt_ref, bre_ref, bim_ref, ccat_ref, d_ref,
               gw_ref, gb_ref, y_ref, hr_out, hi_out, h_scr, bu_scr, *, tc):
    c = pl.program_id(1)

    @pl.when(c == 0)
    def _():
        h_scr[0] = h0r_ref[...]
        h_scr[1] = h0i_ref[...]

    lr, li = lr_ref[...], li_ref[...]
    dt = jnp.exp(ldt_ref[...])
    mag = jnp.exp(lr * dt)
    ab_re, ab_im = mag * jnp.cos(li * dt), mag * jnp.sin(li * dt)
    den = lr * lr + li * li
    zr, zi = ab_re - 1.0, ab_im
    cr = (zr * lr + zi * li) / den
    ci = (zi * lr - zr * li) / den
    bre, bim = bre_ref[...], bim_ref[...]
    bb_re = (cr * bre - ci * bim).astype(BF16)
    bb_im = (cr * bim + ci * bre).astype(BF16)

    u = u_ref[...]
    ub = u.astype(BF16)
    bu_scr[:, 0:SSM_FLAT] = _dot(ub, bb_re)
    bu_scr[:, SSM_FLAT:2 * SSM_FLAT] = _dot(ub, bb_im)

    ar = jnp.broadcast_to(ab_re, (SEQ_BLK, SSM_FLAT))
    ai = jnp.broadcast_to(ab_im, (SEQ_BLK, SSM_FLAT))

    def step(t, carry):
        hr, hi = carry
        rows = pl.ds(pl.multiple_of(t * SEQ_BLK, SEQ_BLK), SEQ_BLK)
        nhr = ar * hr - ai * hi + bu_scr[rows, 0:SSM_FLAT]
        nhi = ar * hi + ai * hr + bu_scr[rows, SSM_FLAT:2 * SSM_FLAT]
        bu_scr[rows, 0:SSM_FLAT] = nhr
        bu_scr[rows, SSM_FLAT:2 * SSM_FLAT] = nhi
        return nhr, nhi

    hr, hi = lax.fori_loop(0, tc, step, (h_scr[0], h_scr[1]))
    h_scr[0] = hr
    h_scr[1] = hi

    y = _dot_nt(bu_scr[...].astype(BF16), ccat_ref[...]) + d_ref[...] * u
    z = jax.nn.gelu(y)
    out = z * jax.nn.sigmoid(_dot(z.astype(BF16), gw_ref[...]) + gb_ref[...])
    y_ref[...] = out.astype(y_ref.dtype)

    @pl.when(c == pl.num_programs(1) - 1)
    def _():
        hr_out[...] = hr
        hi_out[...] = hi


def s5_mixer(proj, h0_re, h0_im, lam_re, lam_im, log_dt, b_re_bd, b_im_bd, c_cat, d_skip, glu_w, glu_b, *, t_len, tc):
    nseq = h0_re.shape[0] // SEQ_BLK
    nchunks = t_len // tc
    n = proj.shape[0]
    st = jax.ShapeDtypeStruct(h0_re.shape, F32)
    return pl.pallas_call(
        functools.partial(_s5_kernel, tc=tc),
        grid=(nseq, nchunks),
        in_specs=[_row_spec(tc, COL_SSM, nchunks),
                  _seq_spec(h0_re.shape), _seq_spec(h0_im.shape),
                  _full_spec((1, SSM_FLAT)), _full_spec((1, SSM_FLAT)), _full_spec((1, SSM_FLAT)),
                  _full_spec((GROUP_WIDTH, SSM_FLAT)), _full_spec((GROUP_WIDTH, SSM_FLAT)),
                  _full_spec((GROUP_WIDTH, 2 * SSM_FLAT)), _full_spec((1, GROUP_WIDTH)),
                  _full_spec((GROUP_WIDTH, GROUP_WIDTH)), _full_spec((1, GROUP_WIDTH))],
        out_specs=[_row_spec(tc, 0, nchunks), _seq_spec(h0_re.shape), _seq_spec(h0_im.shape)],
        out_shape=[jax.ShapeDtypeStruct((n, GROUP_WIDTH), BF16), st, st],
        scratch_shapes=[pltpu.VMEM((2, SEQ_BLK, SSM_FLAT), F32),
                        pltpu.VMEM((tc * SEQ_BLK, 2 * SSM_FLAT), F32)],
        compiler_params=_params(("parallel", "arbitrary")),
        name="s5_mixer",
    )(proj, h0_re, h0_im, lam_re, lam_im, log_dt, b_re_bd, b_im_bd, c_cat, d_skip, glu_w, glu_b)


def _pool_kernel(u_ref, buf_ref, w_ref, sc_ref, y_ref, nbuf_ref, ext_scr, *, tc, pos0):
    c = pl.program_id(1)

    @pl.when(c == 0)
    def _():
        ext_scr[0:POOL_BUF] = buf_ref[...]

    u = u_ref[...].reshape(tc, SEQ_BLK, GROUP_WIDTH)
    ext_scr[POOL_BUF:POOL_BUF + tc] = u
    a1 = ext_scr[...]
    a2 = a1[1:] + a1[:-1]
    a4 = a2[2:] + a2[:-2]
    a8 = a4[4:] + a4[:-4]
    a16 = a8[8:] + a8[:-8]
    sums = (a2[14:], a4[12:], a8[8:], a16)

    shape = (tc, SEQ_BLK, GROUP_WIDTH)
    pos = lax.broadcasted_iota(jnp.int32, shape, 0) + (c * tc + pos0)
    lane = lax.broadcasted_iota(jnp.int32, shape, 2)
    pooled = None
    for gi in reversed(range(len(POOL_WINDOWS))):
        win = POOL_WINDOWS[gi]
        mean = sums[gi] / jnp.minimum(pos + 1, win).astype(F32)
        pooled = mean if pooled is None else jnp.where(lane < (gi + 1) * HEAD, mean, pooled)
    pooled = (pooled - u).reshape(tc * SEQ_BLK, GROUP_WIDTH)
    y_ref[...] = (_dot(pooled.astype(BF16), w_ref[...]) * sc_ref[...]).astype(y_ref.dtype)

    nb = ext_scr[tc:tc + POOL_BUF]
    ext_scr[0:POOL_BUF] = nb

    @pl.when(c == pl.num_programs(1) - 1)
    def _():
        nbuf_ref[...] = nb


def pool_mixer(proj, buf, w_bd, scale, *, t_len, tc, pos0):
    nseq = buf.shape[1] // SEQ_BLK
    nchunks = t_len // tc
    n = proj.shape[0]
    buf_spec = pl.BlockSpec((POOL_BUF, SEQ_BLK, GROUP_WIDTH), lambda s, c: (0, s, 0))
    return pl.pallas_call(
        functools.partial(_pool_kernel, tc=tc, pos0=pos0),
        grid=(nseq, nchunks),
        in_specs=[_row_spec(tc, COL_POOL, nchunks), buf_spec,
                  _full_spec((GROUP_WIDTH, GROUP_WIDTH)), _full_spec((1, GROUP_WIDTH))],
        out_specs=[_row_spec(tc, 0, nchunks), buf_spec],
        out_shape=[jax.ShapeDtypeStruct((n, GROUP_WIDTH), BF16), jax.ShapeDtypeStruct(buf.shape, F32)],
        scratch_shapes=[pltpu.VMEM((tc + POOL_BUF, SEQ_BLK, GROUP_WIDTH), F32)],
        compiler_params=_params(("parallel", "arbitrary")),
        name="pool_mixer",
    )(proj, buf, w_bd, scale)


def _hgrn_lower_bound(logits_ref, layer):
    rows = [logits_ref[l:l + 1, :] for l in range(DEPTH)]
    m = functools.reduce(jnp.maximum, rows)
    es = [jnp.exp(r - m) for r in rows]
    tot = functools.reduce(lambda a, b: a + b, es)
    lb = jnp.zeros_like(m)
    for l in range(1, layer + 1):
        lb = lb + es[l] / tot
    return lb


def _hgrn_kernel(pq_ref, pf_ref, pi_ref, pg_ref, s0_ref, acc_ref, lbl_ref, ng_ref, y_ref, st_ref,
                 s_scr, q_scr, f_scr, k_scr, v_scr, o_scr, *, tc, layer):
    c = pl.program_id(1)
    shape3 = (tc, SEQ_BLK, GROUP_WIDTH)

    @pl.when(c == 0)
    def _():
        _load_unit_tiles(s0_ref, s_scr, transpose=True)

    lb = _hgrn_lower_bound(lbl_ref, layer)
    zf = pf_ref[...]
    f_scr[...] = (lb + (1.0 - lb) * jax.nn.sigmoid(zf)).reshape(shape3)
    k_scr[...] = ((1.0 - lb) * jax.nn.sigmoid(-zf)).reshape(shape3)
    q_scr[...] = jax.nn.silu(pq_ref[...]).reshape(shape3)
    v_scr[...] = pi_ref[...].reshape(shape3)

    ones2, eye = _unit_masks()
    s = s_scr[...].reshape(UNITS * HEAD, LANES)
    for t in range(tc):
        vcol = _seg_sum_mxu(jnp.where(eye, _unit_rows(v_scr, t), 0.0), ones2)
        s = s * _unit_rows(f_scr, t) + vcol * _unit_rows(k_scr, t)
        _store_unit_rows(o_scr, t, _seg_sum_mxu(s * _unit_rows(q_scr, t), ones2), eye)
    s_scr[...] = s.reshape(s_scr.shape)

    o = o_scr[...].reshape(tc * SEQ_BLK, GROUP_WIDTH)
    ms = _head_sums(o * o) * (1.0 / HEAD)
    out = o * lax.rsqrt(ms + HGRN_NORM_EPS) * ng_ref[...] * jax.nn.silu(pg_ref[...])
    y_ref[...] = out.astype(y_ref.dtype)

    @pl.when(c == pl.num_programs(1) - 1)
    def _():
        _store_unit_tiles(st_ref, s_scr, transpose=True)


def _layer_seq_spec(shape, layer):
    rest = len(shape) - 2
    return pl.BlockSpec((None, SEQ_BLK) + shape[2:], lambda s, c: (layer, s) + (0,) * rest)


def _state_alias(acc, in_index, out_index):
    return {} if acc is None else {in_index: out_index}


def hgrn_mixer(proj, s_all, acc, lb_logits, norm_g, *, t_len, tc, layer):
    nseq = s_all.shape[1] // SEQ_BLK
    nchunks = t_len // tc
    n = proj.shape[0]
    tile = pltpu.VMEM((tc, SEQ_BLK, GROUP_WIDTH), F32)
    return pl.pallas_call(
        functools.partial(_hgrn_kernel, tc=tc, layer=layer),
        grid=(nseq, nchunks),
        in_specs=[_row_spec(tc, COL_Q, nchunks), _row_spec(tc, COL_F, nchunks),
                  _row_spec(tc, COL_I, nchunks), _row_spec(tc, COL_G, nchunks),
                  _layer_seq_spec(s_all.shape, layer), pl.BlockSpec(memory_space=pl.ANY),
                  _full_spec((DEPTH, GROUP_WIDTH)), _full_spec((1, GROUP_WIDTH))],
        out_specs=[_row_spec(tc, 0, nchunks), _layer_seq_spec(s_all.shape, layer)],
        out_shape=[jax.ShapeDtypeStruct((n, GROUP_WIDTH), BF16), jax.ShapeDtypeStruct(s_all.shape, F32)],
        scratch_shapes=[pltpu.VMEM((SEQ_BLK, 2, HEAD, LANES), F32), tile, tile, tile, tile, tile],
        input_output_aliases=_state_alias(acc, 5, 1),
        compiler_params=_params(("parallel", "arbitrary")),
        name="hgrn_mixer",
    )(proj, proj, proj, proj, s_all, s_all if acc is None else acc, lb_logits, norm_g)


def _rwkv_kernel(pr_ref, pk_ref, pv_ref, pl_ref, sh0_ref, s0_ref, acc_ref, mu_ref, w0_ref, a0_ref, kk_ref, ka_ref,
                 rk_ref, lng_ref, lnb_ref, w2_ref, a2_ref, g2_ref, y_ref, st_ref, sh_ref,
                 s_scr, prev_scr, r_scr, w_scr, k_scr, v_scr, nkk_scr, kka_scr, o_scr, *, tc):
    c = pl.program_id(1)
    shape3 = (tc, SEQ_BLK, GROUP_WIDTH)
    gw = GROUP_WIDTH

    @pl.when(c == 0)
    def _():
        _load_unit_tiles(s0_ref, s_scr, transpose=False)
        prev_scr[...] = sh0_ref[...]

    def shifted(ref, j):
        x = ref[...].reshape(shape3)
        first = prev_scr[:, j * gw:(j + 1) * gw].reshape(1, SEQ_BLK, gw)
        prev = first if tc == 1 else jnp.concatenate([first, x[:-1]], axis=0)
        prev_scr[:, j * gw:(j + 1) * gw] = x[tc - 1]
        return (x + (prev - x) * mu_ref[:, j * gw:(j + 1) * gw]).reshape(tc * SEQ_BLK, gw)

    xr, xk, xv, xl = shifted(pr_ref, 0), shifted(pk_ref, 1), shifted(pv_ref, 2), shifted(pl_ref, 3)
    w = -jax.nn.softplus(-(w0_ref[...] + _dot(jnp.tanh(xl).astype(BF16), w2_ref[...]))) - 0.5
    decay = jnp.exp(-jnp.exp(w))
    a = jax.nn.sigmoid(a0_ref[...] + _dot(xl.astype(BF16), a2_ref[...]))
    g = _dot(jax.nn.sigmoid(xl).astype(BF16), g2_ref[...])
    kk = xk * kk_ref[...]
    kk = kk / jnp.maximum(jnp.sqrt(_head_sums(kk * kk)), 1e-12)
    k = xk * (1.0 + (a - 1.0) * ka_ref[...])

    r_scr[...] = xr.reshape(shape3)
    w_scr[...] = decay.reshape(shape3)
    k_scr[...] = k.reshape(shape3)
    v_scr[...] = xv.reshape(shape3)
    nkk_scr[...] = (-kk).reshape(shape3)
    kka_scr[...] = (kk * a).reshape(shape3)

    ones2, eye = _unit_masks()
    s = s_scr[...].reshape(UNITS * HEAD, LANES)
    for t in range(tc):
        sa = _seg_sum_mxu(s * _unit_rows(nkk_scr, t), ones2)
        vcol = _seg_sum_mxu(jnp.where(eye, _unit_rows(v_scr, t), 0.0), ones2)
        s = s * _unit_rows(w_scr, t) + sa * _unit_rows(kka_scr, t) + vcol * _unit_rows(k_scr, t)
        _store_unit_rows(o_scr, t, _seg_sum_mxu(s * _unit_rows(r_scr, t), ones2), eye)
    s_scr[...] = s.reshape(s_scr.shape)

    y = o_scr[...].reshape(tc * SEQ_BLK, gw)
    mean = _head_sums(y) * (1.0 / HEAD)
    d = y - mean
    var = _head_sums(d * d) * (1.0 / HEAD)
    yn = d * lax.rsqrt(var + RWKV_GN_EPS) * lng_ref[...] + lnb_ref[...]
    bonus = _head_sums(xr * k * rk_ref[...]) * xv
    y_ref[...] = ((yn + bonus) * g).astype(y_ref.dtype)

    @pl.when(c == pl.num_programs(1) - 1)
    def _():
        _store_unit_tiles(st_ref, s_scr, transpose=False)
        sh_ref[...] = prev_scr[...]


def rwkv_mixer(proj, shift0, s_all, acc, mu, w0, a0, k_k, k_a, r_k, ln_g, ln_b, w2p, a2p, g2p, *, t_len, tc, layer):
    nseq = s_all.shape[1] // SEQ_BLK
    nchunks = t_len // tc
    n = proj.shape[0]
    tile = pltpu.VMEM((tc, SEQ_BLK, GROUP_WIDTH), F32)
    vec = _full_spec((1, GROUP_WIDTH))
    mat = _full_spec((GROUP_WIDTH, GROUP_WIDTH))
    return pl.pallas_call(
        functools.partial(_rwkv_kernel, tc=tc),
        grid=(nseq, nchunks),
        in_specs=[_row_spec(tc, COL_R, nchunks), _row_spec(tc, COL_K, nchunks),
                  _row_spec(tc, COL_V, nchunks), _row_spec(tc, COL_LORA, nchunks),
                  _seq_spec(shift0.shape), _layer_seq_spec(s_all.shape, layer), pl.BlockSpec(memory_space=pl.ANY),
                  _full_spec((1, RWKV_PROJ)), vec, vec, vec, vec, vec, vec, vec, mat, mat, mat],
        out_specs=[_row_spec(tc, 0, nchunks), _layer_seq_spec(s_all.shape, layer), _seq_spec(shift0.shape)],
        out_shape=[jax.ShapeDtypeStruct((n, GROUP_WIDTH), BF16), jax.ShapeDtypeStruct(s_all.shape, F32),
                   jax.ShapeDtypeStruct(shift0.shape, F32)],
        scratch_shapes=[pltpu.VMEM((SEQ_BLK, 2, HEAD, LANES), F32), pltpu.VMEM((SEQ_BLK, RWKV_PROJ), F32),
                        tile, tile, tile, tile, tile, tile, tile],
        input_output_aliases=_state_alias(acc, 6, 1),
        compiler_params=_params(("parallel", "arbitrary")),
        name="rwkv_mixer",
    )(proj, proj, proj, proj, shift0, s_all, s_all if acc is None else acc, mu, w0, a0, k_k, k_a, r_k, ln_g, ln_b,
      w2p, a2p, g2p)


def _dot_nt(a, b):
    return lax.dot_general(a, b, (((1,), (1,)), ((), ())), preferred_element_type=F32)


def _dot_tn(a, b):
    return lax.dot_general(a, b, (((0,), (0,)), ((), ())), preferred_element_type=F32)


def _iota(shape, dim):
    return lax.broadcasted_iota(jnp.int32, shape, dim)


def _head_of(idx):
    return lax.shift_right_logical(idx, HEAD.bit_length() - 1)


def _cumsum_rows(x):
    n = x.shape[0]
    tri = jnp.where(_iota((n, n), 0) >= _iota((n, n), 1), 1.0, 0.0).astype(BF16)
    hi = x.astype(BF16)
    rest = x - hi.astype(F32)
    mid = rest.astype(BF16)
    lo = (rest - mid.astype(F32)).astype(BF16)
    return _dot(tri, hi) + _dot(tri, mid) + _dot(tri, lo)


def _own_head(shape, rows_per_head):
    row_h = lax.shift_right_logical(_iota(shape, 0), rows_per_head.bit_length() - 1)
    return row_h == _head_of(_iota(shape, 1))


def _head_expand(x):
    xx = jnp.concatenate([x] * (GROUP_WIDTH // HEAD), axis=0)
    return jnp.where(_own_head(xx.shape, x.shape[0]), xx, 0.0)


def _head_collapse(xx):
    n = xx.shape[0] // (GROUP_WIDTH // HEAD)
    return xx[0:n] + xx[n:2 * n] + xx[2 * n:3 * n] + xx[3 * n:4 * n]


def _block_diag_mask():
    shape = (GROUP_WIDTH, GROUP_WIDTH)
    return _head_of(_iota(shape, 0)) == _head_of(_iota(shape, 1))


def _seq_col_spec(rows, col, ncols):
    return pl.BlockSpec((rows, GROUP_WIDTH), lambda b, c: (c, b * ncols + col))


HGRN_CHUNK = 128
HGRN_SUB = 16


HGRN_SEQS_PER_STEP = 4


def _run_staged(stages):
    for _ in itertools.zip_longest(*stages):
        pass


def _hgrn_chunk_one(pq_ref, pf_ref, pi_ref, pg_ref, lbl_ref, ng_ref, y_ref, w_scr, k_scr, b_scr, v_scr, p_scr,
                    o_scr, *, layer):
    L, n = HGRN_CHUNK, HGRN_SUB
    half = n // 2
    lb = _hgrn_lower_bound(lbl_ref, layer)
    z = pf_ref[...]
    g = jnp.logaddexp(jnp.log1p(-lb) + jax.nn.log_sigmoid(z), jnp.log(lb))
    kg = (1.0 - lb) * jax.nn.sigmoid(-z)
    q = jax.nn.silu(pq_ref[...])
    v = pi_ref[...]
    bc = _cumsum_rows(g)
    k_scr[...] = kg
    b_scr[...] = bc
    v_scr[...] = v
    yield

    bd = _block_diag_mask()
    ones_bd = jnp.where(bd, 1.0, 0.0).astype(BF16)
    rid = _iota((n, GROUP_WIDTH), 0)
    rid_lo = _iota((half, GROUP_WIDTH), 0) + half
    for sb in range(L // n):
        base = sb * n
        qs, bs = q[base:base + n], bc[base:base + n]
        q_lo, b_lo = q[base + half:base + n], bc[base + half:base + n]
        for s in range(n):
            ks, bsrow = k_scr[base + s:base + s + 1, :], b_scr[base + s:base + s + 1, :]
            if s < half:
                p_scr[s * n:(s + 1) * n, :] = qs * ks * jnp.where(rid >= s, jnp.exp(bs - bsrow), 0.0)
            else:
                p_scr[s * n + half:(s + 1) * n, :] = q_lo * ks * jnp.where(rid_lo >= s, jnp.exp(b_lo - bsrow), 0.0)
        r = _dot(p_scr[...].astype(BF16), ones_bd)
        acc = jnp.zeros((n, GROUP_WIDTH), F32)
        for s in range(n):
            acc = acc + r[s * n:(s + 1) * n] * v_scr[base + s:base + s + 1, :]
        o_scr[base:base + n, :] = acc
        yield

    vb = v.astype(BF16)
    for i in range(1, L // n):
        r0 = i * n
        ref = b_scr[r0 - 1:r0, :]
        qt = q[r0:r0 + n] * jnp.exp(bc[r0:r0 + n] - ref)
        kt = jnp.concatenate([kg[:r0] * jnp.exp(ref - bc[:r0]), jnp.zeros((L - r0, GROUP_WIDTH), F32)], axis=0)
        att = _dot_nt(_head_expand(qt).astype(BF16), kt.astype(BF16))
        yield
        ox = _dot(att.astype(BF16), vb)
        o_scr[r0:r0 + n, :] += _head_collapse(jnp.where(_own_head(ox.shape, n), ox, 0.0))
        yield

    w = w_scr[...]
    o = o_scr[...] + _dot_nt((q * jnp.exp(bc)).astype(BF16), w.astype(BF16))
    b_end = b_scr[L - 1:L, :]
    upd = _dot_tn(vb, (kg * jnp.exp(b_end - bc)).astype(BF16))
    w_scr[...] = w * jnp.exp(b_end) + jnp.where(bd, upd, 0.0)
    yield

    ms = _head_sums(o * o) * (1.0 / HEAD)
    out = o * lax.rsqrt(ms + HGRN_NORM_EPS) * ng_ref[...] * jax.nn.silu(pg_ref[...])
    y_ref[...] = out.astype(y_ref.dtype)


RWKV_CHUNK = 64


def _rwkv_chunk_one(pr_ref, pk_ref, pv_ref, pl_ref, mu_ref, w0_ref, a0_ref, kk_ref, ka_ref, rk_ref, lng_ref,
                    lnb_ref, w2_ref, a2_ref, g2_ref, y_ref, w_scr, prev_scr):
    L = RWKV_CHUNK
    gw = GROUP_WIDTH
    rid = _iota((L, gw), 0)

    def shifted(ref, j):
        x = ref[...]
        prev = jnp.where(rid == 0, prev_scr[:, j * gw:(j + 1) * gw], pltpu.roll(x, 1, axis=0))
        prev_scr[:, j * gw:(j + 1) * gw] = x[L - 1:L]
        return x + (prev - x) * mu_ref[:, j * gw:(j + 1) * gw]

    xr, xk, xv, xl = shifted(pr_ref, 0), shifted(pk_ref, 1), shifted(pv_ref, 2), shifted(pl_ref, 3)
    w = -jax.nn.softplus(-(w0_ref[...] + _dot(jnp.tanh(xl).astype(BF16), w2_ref[...]))) - 0.5
    lw = -jnp.exp(w)
    a = jax.nn.sigmoid(a0_ref[...] + _dot(xl.astype(BF16), a2_ref[...]))
    g = _dot(jax.nn.sigmoid(xl).astype(BF16), g2_ref[...])
    kk = xk * kk_ref[...]
    kk = kk / jnp.maximum(jnp.sqrt(_head_sums(kk * kk)), 1e-12)
    k = xk * (1.0 + (a - 1.0) * ka_ref[...])
    beta = kk * a
    yield

    cs = _cumsum_rows(lw)
    c_end = cs[L - 1:L]
    e_neg = jnp.exp(-cs)
    e_end = jnp.exp(c_end - cs)
    ar = jnp.concatenate([_head_expand(-kk * jnp.exp(cs - lw)), _head_expand(xr * jnp.exp(cs))], axis=0).astype(BF16)
    bk = jnp.concatenate([_head_expand(beta * e_neg), _head_expand(k * e_neg)], axis=0).astype(BF16)
    vx = _head_expand(xv).astype(BF16)
    yield

    nh = 4 * L
    gmat = _dot_nt(ar, bk)
    tt = _iota((nh, nh), 0) & (L - 1)
    ss = _iota((nh, nh), 1) & (L - 1)
    strict, incl = ss < tt, ss <= tt
    nab = jnp.where(strict, gmat[0:nh, 0:nh], 0.0)
    nak = jnp.where(strict, gmat[0:nh, nh:2 * nh], 0.0).astype(BF16)
    nrb = jnp.where(incl, gmat[nh:2 * nh, 0:nh], 0.0).astype(BF16)
    nrk = jnp.where(incl, gmat[nh:2 * nh, nh:2 * nh], 0.0).astype(BF16)
    yield

    ri, ci = _iota((nh, nh), 0), _iota((nh, nh), 1)

    def same_block(size):
        sh = size.bit_length() - 1
        return lax.shift_right_logical(ri, sh) == lax.shift_right_logical(ci, sh)

    base = 8
    m = jnp.where(same_block(base), nab, 0.0)
    t_inv = jnp.where(ri == ci, 1.0, 0.0) + m
    m = m.astype(BF16)
    for _ in range(base.bit_length() - 2):
        m = _dot(m, m).astype(BF16)
        yield
        t_inv = t_inv + _dot(t_inv.astype(BF16), m)
        yield
    size = base
    while size < L:
        off = jnp.where(same_block(2 * size), jnp.where(same_block(size), 0.0, nab), 0.0).astype(BF16)
        tb = t_inv.astype(BF16)
        half = _dot(tb, off).astype(BF16)
        yield
        t_inv = t_inv + _dot(half, tb)
        yield
        size *= 2

    wst = w_scr[...]
    sw = _dot_nt(ar, wst.astype(BF16))
    rhs = (sw[0:nh] + _dot(nak, vx)).astype(BF16)
    yield
    x = _dot(t_inv.astype(BF16), rhs)
    yield
    ux = x.astype(BF16)
    yx = sw[nh:2 * nh] + _dot(nrb, ux) + _dot(nrk, vx)
    y = _head_collapse(yx)
    u = _head_collapse(x)
    yield

    upd = _dot_tn(jnp.concatenate([u, xv], axis=0).astype(BF16),
                  jnp.concatenate([beta * e_end, k * e_end], axis=0).astype(BF16))
    w_scr[...] = wst * jnp.exp(c_end) + jnp.where(_block_diag_mask(), upd, 0.0)
    yield

    mean = _head_sums(y) * (1.0 / HEAD)
    d = y - mean
    var = _head_sums(d * d) * (1.0 / HEAD)
    yn = d * lax.rsqrt(var + RWKV_GN_EPS) * lng_ref[...] + lnb_ref[...]
    bonus = _head_sums(xr * k * rk_ref[...]) * xv
    y_ref[...] = ((yn + bonus) * g).astype(y_ref.dtype)


RWKV_SEQS_PER_STEP = 8


def _hgrn_rwkv_chunk_kernel(*refs, layer, rwkv_steps, hgrn_steps):
    nr, nh = RWKV_SEQS_PER_STEP, HGRN_SEQS_PER_STEP
    it = iter(refs)
    take = lambda n: [next(it) for _ in range(n)]
    r_seq, r_par, h_seq = take(4 * nr), take(11), take(4 * nh)
    lbl_ref, ng_ref = take(2)
    yr_ref, str_ref, shr_ref, yh_ref, sth_ref = take(5)
    wr_scr, prev_scr, wh_scr, k_scr, b_scr, v_scr, p_scr, o_scr = take(8)
    gw = GROUP_WIDTH
    g = pl.program_id(0)
    cr, ch = g % rwkv_steps, g % hgrn_steps

    @pl.when(cr == 0)
    def _():
        wr_scr[...] = jnp.zeros_like(wr_scr)
        prev_scr[...] = jnp.zeros_like(prev_scr)

    @pl.when(ch == 0)
    def _():
        wh_scr[...] = jnp.zeros_like(wh_scr)
        p_scr[...] = jnp.zeros_like(p_scr)

    rwkv = [_rwkv_chunk_one(*r_seq[4 * s:4 * s + 4], *r_par, yr_ref.at[:, s * gw:(s + 1) * gw], wr_scr.at[s],
                            prev_scr.at[s]) for s in range(nr)]
    hgrn = [_hgrn_chunk_one(*h_seq[4 * s:4 * s + 4], lbl_ref, ng_ref, yh_ref.at[:, s * gw:(s + 1) * gw],
                            wh_scr.at[s], k_scr.at[s], b_scr.at[s], v_scr.at[s], p_scr.at[s], o_scr.at[s],
                            layer=layer) for s in range(nh)]
    order = []
    for s in range(max(nr, nh)):
        order += rwkv[s:s + 1] + hgrn[s:s + 1]
    _run_staged(order)

    def head_blocks(w):
        return [w[h * HEAD:(h + 1) * HEAD, h * HEAD:(h + 1) * HEAD] for h in range(gw // HEAD)]

    @pl.when(cr == rwkv_steps - 1)
    def _():
        for s in range(nr):
            for h, blk in enumerate(head_blocks(wr_scr[s])):
                str_ref[s, h] = blk
        shr_ref[...] = prev_scr[...]

    @pl.when(ch == hgrn_steps - 1)
    def _():
        for s in range(nh):
            for h, blk in enumerate(head_blocks(wh_scr[s])):
                sth_ref[s, h] = blk.T


def hgrn_rwkv_chunk_mixer(proj2, lb_logits, norm_g, mu, w0, a0, k_k, k_a, r_k, ln_g, ln_b, w2p, a2p, g2p, *,
                          bsz, t_len, layer):
    nr, nh, lr, lh, n = RWKV_SEQS_PER_STEP, HGRN_SEQS_PER_STEP, RWKV_CHUNK, HGRN_CHUNK, HGRN_SUB
    gw = GROUP_WIDTH
    ncols = PROJ_WIDTH // gw
    heads = (gw // HEAD, HEAD, HEAD)
    rwkv_steps, hgrn_steps = t_len // lr, t_len // lh
    steps = (bsz // nr) * rwkv_steps
    assert (bsz // nh) * hgrn_steps == steps
    const = lambda shape: pl.BlockSpec(shape, lambda g: (0,) * len(shape))
    r_specs = [pl.BlockSpec((lr, gw), functools.partial(
                   lambda g, s, col: (g % rwkv_steps, ((g // rwkv_steps) * nr + s) * ncols + col), s=s, col=col))
               for s in range(nr) for col in (COL_R, COL_K, COL_V, COL_LORA)]
    h_specs = [pl.BlockSpec((lh, gw), functools.partial(
                   lambda g, s, col: (g % hgrn_steps, ((g // hgrn_steps) * nh + s) * ncols + col), s=s, col=col))
               for s in range(nh) for col in (COL_Q, COL_F, COL_I, COL_G)]
    vec, mat = const((1, gw)), const((gw, gw))
    tile = pltpu.VMEM((nh, lh, gw), F32)
    yr, st_r, sh_r, yh, st_h = pl.pallas_call(
        functools.partial(_hgrn_rwkv_chunk_kernel, layer=layer, rwkv_steps=rwkv_steps, hgrn_steps=hgrn_steps),
        grid=(steps,),
        in_specs=r_specs + [const((1, RWKV_PROJ)), vec, vec, vec, vec, vec, vec, vec, mat, mat, mat]
                 + h_specs + [const((DEPTH, gw)), vec],
        out_specs=[pl.BlockSpec((lr, nr * gw), lambda g: (g % rwkv_steps, g // rwkv_steps)),
                   pl.BlockSpec((nr,) + heads, lambda g: (g // rwkv_steps, 0, 0, 0)),
                   pl.BlockSpec((nr, 1, RWKV_PROJ), lambda g: (g // rwkv_steps, 0, 0)),
                   pl.BlockSpec((lh, nh * gw), lambda g: (g % hgrn_steps, g // hgrn_steps)),
                   pl.BlockSpec((nh,) + heads, lambda g: (g // hgrn_steps, 0, 0, 0))],
        out_shape=[jax.ShapeDtypeStruct((t_len, bsz * gw), BF16), jax.ShapeDtypeStruct((bsz,) + heads, F32),
                   jax.ShapeDtypeStruct((bsz, 1, RWKV_PROJ), F32),
                   jax.ShapeDtypeStruct((t_len, bsz * gw), BF16), jax.ShapeDtypeStruct((bsz,) + heads, F32)],
        scratch_shapes=[pltpu.VMEM((nr, gw, gw), F32), pltpu.VMEM((nr, 1, RWKV_PROJ), F32),
                        pltpu.VMEM((nh, gw, gw), F32), tile, tile, tile, pltpu.VMEM((nh, n * n, gw), F32), tile],
        compiler_params=_params(("arbitrary",)),
        name="hgrn_rwkv_chunk_mixer",
    )(*([proj2] * (4 * nr)), mu, w0, a0, k_k, k_a, r_k, ln_g, ln_b, w2p, a2p, g2p,
      *([proj2] * (4 * nh)), lb_logits, norm_g)
    return yh, st_h, yr, st_r, sh_r


S5_CHUNK = 64


def _s5_seq_kernel(*refs, nseq):
    u_refs = refs[:nseq]
    lr_ref, li_ref, ldt_ref, bre_ref, bim_ref, ccat_ref, d_ref, gw_ref, gb_ref = refs[nseq:nseq + 9]
    y_ref, hr_out, hi_out, h_scr, bu_scr, bb_scr, ab_scr, perm_scr, u_scr = refs[nseq + 9:]
    tc = S5_CHUNK
    n = nseq * tc
    c = pl.program_id(0)

    @pl.when(c == 0)
    def _():
        h_scr[...] = jnp.zeros_like(h_scr)
        lr, li = lr_ref[...], li_ref[...]
        dt = jnp.exp(ldt_ref[...])
        mag = jnp.exp(lr * dt)
        ab_re, ab_im = mag * jnp.cos(li * dt), mag * jnp.sin(li * dt)
        den = lr * lr + li * li
        zr, zi = ab_re - 1.0, ab_im
        cr = (zr * lr + zi * li) / den
        ci = (zi * lr - zr * li) / den
        bre, bim = bre_ref[...], bim_ref[...]
        bb_scr[:, 0:SSM_FLAT] = (cr * bre - ci * bim).astype(BF16)
        bb_scr[:, SSM_FLAT:2 * SSM_FLAT] = (cr * bim + ci * bre).astype(BF16)
        ab_scr[0] = jnp.broadcast_to(ab_re, (nseq, SSM_FLAT))
        ab_scr[1] = jnp.broadcast_to(ab_im, (nseq, SSM_FLAT))
        ri, cj = _iota((n, n), 0), _iota((n, n), 1)
        lseq, lt = nseq.bit_length() - 1, tc.bit_length() - 1
        perm_scr[0] = jnp.where(cj == (ri & (nseq - 1)) * tc + lax.shift_right_logical(ri, lseq), 1.0, 0.0).astype(BF16)
        perm_scr[1] = jnp.where(cj == (ri & (tc - 1)) * nseq + lax.shift_right_logical(ri, lt), 1.0, 0.0).astype(BF16)

        bu_scr[1] = jnp.zeros(bu_scr.shape[1:], F32)
        u_scr[1] = jnp.zeros(u_scr.shape[1:], F32)

    last = pl.num_programs(0) - 1

    def pipeline(slot):

        def scan_chunk():
            u = jnp.concatenate([r[...] for r in u_refs], axis=0)
            u_scr[slot] = u
            u_t = _dot(perm_scr[0], u.astype(BF16)).astype(BF16)
            bu_scr[slot] = _dot(u_t, bb_scr[...])
            yield
            ar, ai = ab_scr[0], ab_scr[1]
            hr0, hi0 = h_scr[0], h_scr[1]
            hr, hi = hr0, hi0
            for t in range(tc):
                rows = slice(t * nseq, (t + 1) * nseq)
                hr, hi = (ar * hr - ai * hi + bu_scr[slot, rows, 0:SSM_FLAT],
                          ar * hi + ai * hr + bu_scr[slot, rows, SSM_FLAT:2 * SSM_FLAT])
                bu_scr[slot, rows, 0:SSM_FLAT] = hr
                bu_scr[slot, rows, SSM_FLAT:2 * SSM_FLAT] = hi
                if t % 8 == 7:
                    yield
            h_scr[0] = jnp.where(c < last, hr, hr0)
            h_scr[1] = jnp.where(c < last, hi, hi0)

        def emit_chunk():
            prev = 1 - slot
            y_t = _dot_nt(bu_scr[prev].astype(BF16), ccat_ref[...])
            yield
            hi_p = y_t.astype(BF16)
            rest = y_t - hi_p.astype(F32)
            mid_p = rest.astype(BF16)
            lo_p = (rest - mid_p.astype(F32)).astype(BF16)
            to_seq = perm_scr[1]
            y = _dot(to_seq, hi_p) + _dot(to_seq, mid_p) + _dot(to_seq, lo_p) + d_ref[...] * u_scr[prev]
            yield
            z = jax.nn.gelu(y)
            gate = _dot(z.astype(BF16), gw_ref[...])
            yield
            out = z * jax.nn.sigmoid(gate + gb_ref[...])
            for b in range(nseq):
                y_ref[:, b * GROUP_WIDTH:(b + 1) * GROUP_WIDTH] = out[b * tc:(b + 1) * tc].astype(y_ref.dtype)

        _run_staged([scan_chunk(), emit_chunk()])

    for parity in range(2):
        pl.when(c % 2 == parity)(functools.partial(pipeline, parity))

    @pl.when(c == last)
    def _():
        hr_out[...] = h_scr[0]
        hi_out[...] = h_scr[1]


def s5_seq_mixer(proj2, lam_re, lam_im, log_dt, b_re_bd, b_im_bd, c_cat, d_skip, glu_w, glu_b, *, bsz, t_len):
    tc = S5_CHUNK
    ncols = PROJ_WIDTH // GROUP_WIDTH
    full = lambda shape: pl.BlockSpec(shape, lambda c: (0,) * len(shape))
    st = jax.ShapeDtypeStruct((bsz, SSM_FLAT), F32)
    nchunks = t_len // tc
    u_specs = [pl.BlockSpec((tc, GROUP_WIDTH), functools.partial(
                   lambda c, b: (jnp.minimum(c, nchunks - 1), b * ncols + COL_SSM), b=b)) for b in range(bsz)]
    return pl.pallas_call(
        functools.partial(_s5_seq_kernel, nseq=bsz),
        grid=(nchunks + 1,),
        in_specs=u_specs + [full((1, SSM_FLAT)), full((1, SSM_FLAT)), full((1, SSM_FLAT)),
                            full((GROUP_WIDTH, SSM_FLAT)), full((GROUP_WIDTH, SSM_FLAT)),
                            full((GROUP_WIDTH, 2 * SSM_FLAT)), full((1, GROUP_WIDTH)),
                            full((GROUP_WIDTH, GROUP_WIDTH)), full((1, GROUP_WIDTH))],
        out_specs=[pl.BlockSpec((tc, bsz * GROUP_WIDTH), lambda c: (jnp.maximum(c - 1, 0), 0)),
                   full((bsz, SSM_FLAT)), full((bsz, SSM_FLAT))],
        out_shape=[jax.ShapeDtypeStruct((t_len, bsz * GROUP_WIDTH), BF16), st, st],
        scratch_shapes=[pltpu.VMEM((2, bsz, SSM_FLAT), F32), pltpu.VMEM((2, bsz * tc, 2 * SSM_FLAT), F32),
                        pltpu.VMEM((GROUP_WIDTH, 2 * SSM_FLAT), BF16), pltpu.VMEM((2, bsz, SSM_FLAT), F32),
                        pltpu.VMEM((2, bsz * tc, bsz * tc), BF16), pltpu.VMEM((2, bsz * tc, GROUP_WIDTH), F32)],
        compiler_params=_params(("arbitrary",)),
        name="s5_seq_mixer",
    )(*([proj2] * bsz), lam_re, lam_im, log_dt, b_re_bd, b_im_bd, c_cat, d_skip, glu_w, glu_b)


POOL_CHUNK = 1024
POOL_HIST = 16


def _pool_seq_kernel(u_ref, w_ref, sc_ref, y_ref, nbuf_ref, ext_scr):
    L, hist = u_ref.shape[0], POOL_HIST
    c = pl.program_id(1)

    @pl.when(c == 0)
    def _():
        ext_scr[0:hist] = jnp.zeros((hist, GROUP_WIDTH), F32)

    u = u_ref[...]
    ext_scr[hist:hist + L] = u
    e = ext_scr[...]
    a2 = e + pltpu.roll(e, 1, axis=0)
    a4 = a2 + pltpu.roll(a2, 2, axis=0)
    a8 = a4 + pltpu.roll(a4, 4, axis=0)
    a16 = a8 + pltpu.roll(a8, 8, axis=0)
    sums = (a2[hist:], a4[hist:], a8[hist:], a16[hist:])

    shape = (L, GROUP_WIDTH)
    pos = _iota(shape, 0) + c * L
    lane = _iota(shape, 1)
    pooled = None
    for gi in reversed(range(len(POOL_WINDOWS))):
        win = POOL_WINDOWS[gi]
        mean = sums[gi] / jnp.minimum(pos + 1, win).astype(F32)
        pooled = mean if pooled is None else jnp.where(lane < (gi + 1) * HEAD, mean, pooled)
    y_ref[...] = (_dot((pooled - u).astype(BF16), w_ref[...]) * sc_ref[...]).astype(y_ref.dtype)

    nb = ext_scr[L:L + hist]
    ext_scr[0:hist] = nb

    @pl.when(c == pl.num_programs(1) - 1)
    def _():
        nbuf_ref[...] = nb


def pool_seq_mixer(proj2, w_bd, scale, *, bsz, t_len):
    L, hist = min(POOL_CHUNK, t_len), POOL_HIST
    ncols = PROJ_WIDTH // GROUP_WIDTH
    return pl.pallas_call(
        _pool_seq_kernel,
        grid=(bsz, t_len // L),
        in_specs=[_seq_col_spec(L, COL_POOL, ncols),
                  pl.BlockSpec((GROUP_WIDTH, GROUP_WIDTH), lambda b, c: (0, 0)),
                  pl.BlockSpec((1, GROUP_WIDTH), lambda b, c: (0, 0))],
        out_specs=[_seq_col_spec(L, 0, 1), pl.BlockSpec((hist, GROUP_WIDTH), lambda b, c: (b, 0))],
        out_shape=[jax.ShapeDtypeStruct((t_len, bsz * GROUP_WIDTH), BF16),
                   jax.ShapeDtypeStruct((bsz * hist, GROUP_WIDTH), F32)],
        scratch_shapes=[pltpu.VMEM((hist + L, GROUP_WIDTH), F32)],
        compiler_params=_params(("parallel", "arbitrary")),
        name="pool_seq_mixer",
    )(proj2, w_bd, scale)


def _mix_mlp_kernel(h_ref, ya_ref, yb_ref, yc_ref, yd_ref, wo_ref, g2_ref, wu_ref, wd_ref, gf_ref, o_ref,
                    h1_scr, xn_scr, acc_scr, *, final_norm):
    j = pl.program_id(1)
    gw = GROUP_WIDTH

    @pl.when(j == 0)
    def _():
        mix = (_dot(ya_ref[...], wo_ref[0:gw]) + _dot(yb_ref[...], wo_ref[gw:2 * gw])
               + _dot(yc_ref[...], wo_ref[2 * gw:3 * gw]) + _dot(yd_ref[...], wo_ref[3 * gw:4 * gw]))
        h1 = h_ref[...] + mix
        h1_scr[...] = h1
        xn_scr[...] = _rms(h1, g2_ref[...]).astype(BF16)
        acc_scr[...] = jnp.zeros_like(acc_scr)

    up = _dot(xn_scr[...], wu_ref[...])
    act = jnp.square(jnp.maximum(up, 0.0)).astype(BF16)
    acc_scr[...] += _dot(act, wd_ref[...])

    @pl.when(j == pl.num_programs(1) - 1)
    def _():
        out = h1_scr[...] + acc_scr[...]
        if final_norm:
            out = _rms(out, gf_ref[...])
        o_ref[...] = out


def mix_mlp(h, ys, w_out, g2, w_up, w_down, g_final, *, layer, final_norm, nseq=0):
    n = h.shape[0]
    tm = min(512, n // max(nseq, 1))
    tf = 2048
    nt = n // tm // max(nseq, 1)
    row = lambda w: pl.BlockSpec((tm, w), lambda i, j: (i, 0))
    mix = pl.BlockSpec((tm, GROUP_WIDTH), _tile_map(nseq, nt))
    return pl.pallas_call(
        functools.partial(_mix_mlp_kernel, final_norm=final_norm),
        grid=(n // tm, D_FF // tf),
        in_specs=[row(D_MODEL), mix, mix, mix, mix,
                  pl.BlockSpec((None, D_MODEL, D_MODEL), lambda i, j: (layer, 0, 0)),
                  pl.BlockSpec((1, D_MODEL), lambda i, j: (0, 0)),
                  pl.BlockSpec((None, D_MODEL, tf), lambda i, j: (layer, 0, j)),
                  pl.BlockSpec((None, tf, D_MODEL), lambda i, j: (layer, j, 0)),
                  pl.BlockSpec((1, D_MODEL), lambda i, j: (0, 0))],
        out_specs=row(D_MODEL),
        out_shape=jax.ShapeDtypeStruct((n, D_MODEL), F32),
        scratch_shapes=[pltpu.VMEM((tm, D_MODEL), F32), pltpu.VMEM((tm, D_MODEL), BF16),
                        pltpu.VMEM((tm, D_MODEL), F32)],
        compiler_params=_params(("parallel", "arbitrary")),
        name="mix_mlp",
    )(h, *ys, w_out, g2, w_up, w_down, g_final)


def _block_diag(blocks):
    g, r, c = blocks.shape
    tiled = jnp.tile(blocks.reshape(g * r, c), (1, g))
    row_blk = lax.broadcasted_iota(jnp.int32, tiled.shape, 0) // r
    col_blk = lax.broadcasted_iota(jnp.int32, tiled.shape, 1) // c
    return jnp.where(row_blk == col_blk, tiled, 0.0)


def _pad_rows(w, start):
    rows = GROUP_WIDTH - start - w.shape[0]
    return jnp.concatenate([jnp.zeros((start, GROUP_WIDTH), w.dtype), w, jnp.zeros((rows, GROUP_WIDTH), w.dtype)])


def _layer_params(l, P):
    row = lambda a: a.reshape(1, -1)
    q = {}
    q["norm1_g"] = row(P["norm1_g"][l])
    q["lam_re"] = row(P["ssm_lambda_re"][l])
    q["lam_im"] = row(P["ssm_lambda_im"][l])
    q["log_dt"] = row(jnp.repeat(P["ssm_log_dt"][l], SSM_STATE))
    q["b_re"] = _block_diag(P["ssm_b_re"][l].transpose(0, 2, 1))
    q["b_im"] = _block_diag(P["ssm_b_im"][l].transpose(0, 2, 1))
    q["c_cat"] = jnp.concatenate([_block_diag(P["ssm_c_re"][l]), -_block_diag(P["ssm_c_im"][l])],
                                 axis=1).astype(BF16)
    q["ssm_d"] = row(P["ssm_d"][l])
    q["glu_w"] = P["ssm_glu_w"][l].astype(BF16)
    q["glu_b"] = row(P["ssm_glu_b"][l])
    q["hgrn_norm_g"] = row(P["hgrn_norm_g"][l])
    q["mu"] = row(P["rwkv_mu"][l])
    for name in ("w0", "a0", "k_k", "k_a", "r_k", "ln_g", "ln_b"):
        q[name] = row(P["rwkv_" + name][l])
    q["w2p"] = _pad_rows(P["rwkv_w2"][l], 0).astype(BF16)
    q["a2p"] = _pad_rows(P["rwkv_a2"][l], DECAY_LORA).astype(BF16)
    q["g2p"] = _pad_rows(P["rwkv_g2"][l], DECAY_LORA + AAA_LORA).astype(BF16)
    q["pool_w"] = _block_diag(P["pool_w"][l]).astype(BF16)
    q["pool_scale"] = row(P["pool_scale"][l])
    q["norm2_g"] = row(P["norm2_g"][l])
    return q


def _trunk_fresh(x_rows, bsz, t_len, layer_params, P):
    h = x_rows
    new = [[] for _ in range(6)]
    g_final = P["norm_f_g"].reshape(1, -1)
    for l in range(DEPTH):
        q = layer_params[l]
        proj2 = rms_proj(h, q["norm1_g"], P["w_in"], layer=l, nseq=bsz)
        y_a, s_re, s_im = s5_seq_mixer(proj2, q["lam_re"], q["lam_im"], q["log_dt"], q["b_re"], q["b_im"],
                                       q["c_cat"], q["ssm_d"], q["glu_w"], q["glu_b"], bsz=bsz, t_len=t_len)
        y_b, s_hg, y_c, s_wkv, s_sh = hgrn_rwkv_chunk_mixer(
            proj2, P["hgrn_lb_logits"], q["hgrn_norm_g"], q["mu"], q["w0"], q["a0"], q["k_k"], q["k_a"], q["r_k"],
            q["ln_g"], q["ln_b"], q["w2p"], q["a2p"], q["g2p"], bsz=bsz, t_len=t_len, layer=l)
        y_d, s_pool = pool_seq_mixer(proj2, q["pool_w"], q["pool_scale"], bsz=bsz, t_len=t_len)
        h = mix_mlp(h, (y_a, y_b, y_c, y_d), P["w_out"], q["norm2_g"], P["mlp_up"], P["mlp_down"], g_final,
                    layer=l, final_norm=(l == DEPTH - 1), nseq=bsz)
        s_pool = s_pool.reshape(bsz, POOL_HIST, GROUP_WIDTH)[:, POOL_HIST - POOL_BUF:]
        for lst, s in zip(new, (s_re, s_im, s_hg, s_wkv, s_sh, s_pool)):
            lst.append(s)
    return h, new


def _trunk_carry(x_rows, states, pos0, t_len, layer_params, P):
    ssm_re0, ssm_im0, hgrn0, wkv0, shift0, pool0 = states
    h = x_rows
    new = [[] for _ in range(6)]
    g_final = P["norm_f_g"].reshape(1, -1)
    tc = t_len
    s_hg = s_wkv = None
    for l in range(DEPTH):
        q = layer_params[l]
        proj = rms_proj(h, q["norm1_g"], P["w_in"], layer=l)
        y_a, s_re, s_im = s5_mixer(proj, ssm_re0[l], ssm_im0[l], q["lam_re"], q["lam_im"], q["log_dt"], q["b_re"],
                                   q["b_im"], q["c_cat"], q["ssm_d"], q["glu_w"], q["glu_b"], t_len=t_len, tc=tc)
        y_b, s_hg = hgrn_mixer(proj, hgrn0, s_hg, P["hgrn_lb_logits"], q["hgrn_norm_g"], t_len=t_len, tc=tc,
                               layer=l)
        y_c, s_wkv, s_sh = rwkv_mixer(proj, shift0[l], wkv0, s_wkv, q["mu"], q["w0"], q["a0"], q["k_k"], q["k_a"],
                                      q["r_k"], q["ln_g"], q["ln_b"], q["w2p"], q["a2p"], q["g2p"],
                                      t_len=t_len, tc=tc, layer=l)
        y_d, s_pool = pool_mixer(proj, pool0[l], q["pool_w"], q["pool_scale"], t_len=t_len, tc=tc, pos0=pos0)
        h = mix_mlp(h, (y_a, y_b, y_c, y_d), P["w_out"], q["norm2_g"], P["mlp_up"], P["mlp_down"], g_final,
                    layer=l, final_norm=(l == DEPTH - 1))
        for i, s in ((0, s_re), (1, s_im), (4, s_sh), (5, s_pool.transpose(1, 0, 2))):
            new[i].append(s)
    new[2], new[3] = s_hg, s_wkv
    return h, new


def _states_out(new, bsz):
    s_re, s_im, s_hg, s_wkv, s_sh, s_pool = new
    return (jnp.stack([s.reshape(bsz, SSM_GROUPS, SSM_STATE) for s in s_re]),
            jnp.stack([s.reshape(bsz, SSM_GROUPS, SSM_STATE) for s in s_im]),
            jnp.stack(s_hg) if isinstance(s_hg, list) else s_hg,
            jnp.stack(s_wkv) if isinstance(s_wkv, list) else s_wkv,
            jnp.stack([s.reshape(bsz, 1, RWKV_PROJ) for s in s_sh]),
            jnp.stack(s_pool))


def kernel(x_prompt, x_sample, state_ssm_re, state_ssm_im, state_hgrn, state_wkv, state_shift, state_pool, norm1_g, w_in, ssm_lambda_re, ssm_lambda_im, ssm_log_dt, ssm_b_re, ssm_b_im, ssm_c_re, ssm_c_im, ssm_d, ssm_glu_w, ssm_glu_b, hgrn_lb_logits, hgrn_norm_g, rwkv_mu, rwkv_w0, rwkv_w2, rwkv_a0, rwkv_a2, rwkv_g2, rwkv_k_k, rwkv_k_a, rwkv_r_k, rwkv_ln_g, rwkv_ln_b, pool_w, pool_scale, w_out, norm2_g, mlp_up, mlp_down, norm_f_g):
    P = dict(norm1_g=norm1_g, w_in=w_in, ssm_lambda_re=ssm_lambda_re, ssm_lambda_im=ssm_lambda_im,
             ssm_log_dt=ssm_log_dt, ssm_b_re=ssm_b_re, ssm_b_im=ssm_b_im, ssm_c_re=ssm_c_re, ssm_c_im=ssm_c_im,
             ssm_d=ssm_d, ssm_glu_w=ssm_glu_w, ssm_glu_b=ssm_glu_b, hgrn_lb_logits=hgrn_lb_logits,
             hgrn_norm_g=hgrn_norm_g, rwkv_mu=rwkv_mu, rwkv_w0=rwkv_w0, rwkv_w2=rwkv_w2, rwkv_a0=rwkv_a0,
             rwkv_a2=rwkv_a2, rwkv_g2=rwkv_g2, rwkv_k_k=rwkv_k_k, rwkv_k_a=rwkv_k_a, rwkv_r_k=rwkv_r_k,
             rwkv_ln_g=rwkv_ln_g, rwkv_ln_b=rwkv_ln_b, pool_w=pool_w, pool_scale=pool_scale, w_out=w_out,
             norm2_g=norm2_g, mlp_up=mlp_up, mlp_down=mlp_down, norm_f_g=norm_f_g)
    layer_params = [_layer_params(l, P) for l in range(DEPTH)]
    for name in ("w_in", "w_out", "mlp_up", "mlp_down"):
        P[name] = P[name].astype(BF16)

    bp, t_p, _ = x_prompt.shape
    yp, new_p = _trunk_fresh(x_prompt.reshape(bp * t_p, D_MODEL), bp, t_p, layer_params, P)
    y_prompt = yp.reshape(bp, t_p, D_MODEL)

    bs, t_s, _ = x_sample.shape
    nblk = bs // SEQ_BLK
    xs = x_sample.reshape(nblk, SEQ_BLK, t_s, D_MODEL).transpose(0, 2, 1, 3).reshape(bs * t_s, D_MODEL)
    st_s = ([state_ssm_re[l].reshape(bs, SSM_FLAT) for l in range(DEPTH)],
            [state_ssm_im[l].reshape(bs, SSM_FLAT) for l in range(DEPTH)],
            state_hgrn, state_wkv,
            [state_shift[l].reshape(bs, RWKV_PROJ) for l in range(DEPTH)],
            [state_pool[l].transpose(1, 0, 2) for l in range(DEPTH)])
    ys, new_s = _trunk_carry(xs, st_s, PAST_LEN, t_s, layer_params, P)
    y_sample = ys.reshape(nblk, t_s, SEQ_BLK, D_MODEL).transpose(0, 2, 1, 3).reshape(bs, t_s, D_MODEL)

    return (y_prompt, y_sample) + _states_out(new_p, bp) + _states_out(new_s, bs)
```

```python
import functools
import itertools

import jax
import jax.numpy as jnp
from jax import lax
from jax.experimental import pallas as pl
from jax.experimental.pallas import tpu as pltpu

F32 = jnp.float32
BF16 = jnp.bfloat16

D_MODEL = 1024
DEPTH = 2
PAST_LEN = 16384
GROUP_WIDTH = 256
HEAD = 64
SSM_GROUPS = 16
SSM_STATE = 64
SSM_FLAT = SSM_GROUPS * SSM_STATE
POOL_WINDOWS = (2, 4, 8, 16)
POOL_BUF = 15
DECAY_LORA = 64
AAA_LORA = 64
RWKV_PROJ = 1024
PROJ_WIDTH = 2560
D_FF = 4096
NORM_EPS = 1e-6
HGRN_NORM_EPS = 1e-5
RWKV_GN_EPS = 64e-5

SEQ_BLK = 8
LANES = 128
VMEM_LIMIT = 56 * 1024 * 1024

COL_SSM, COL_Q, COL_F, COL_I, COL_G, COL_R, COL_K, COL_V, COL_LORA, COL_POOL = range(10)


def _params(sem):
    return pltpu.CompilerParams(dimension_semantics=sem, vmem_limit_bytes=VMEM_LIMIT)


def _dot(a, b):
    return jnp.dot(a, b, preferred_element_type=F32)


def _rms(x, g):
    return x * lax.rsqrt(jnp.mean(x * x, axis=-1, keepdims=True) + NORM_EPS) * g


def _rms_proj_kernel(x_ref, g_ref, w_ref, o_ref):
    o_ref[...] = _dot(_rms(x_ref[...], g_ref[...]).astype(BF16), w_ref[...])


def _tile_map(nseq, nt):
    if nseq == 0:
        return lambda r, *_: (r, 0)
    return lambda r, *_: (r % nt, r // nt)


def rms_proj(x, g, w, *, layer, nseq=0):
    n = x.shape[0]
    tm = min(1024, n // max(nseq, 1))
    nt = n // tm // max(nseq, 1)
    out_shape = (n, PROJ_WIDTH) if nseq == 0 else (n // nseq, nseq * PROJ_WIDTH)
    return pl.pallas_call(
        _rms_proj_kernel,
        grid=(n // tm,),
        in_specs=[pl.BlockSpec((tm, D_MODEL), lambda i: (i, 0)),
                  pl.BlockSpec((1, D_MODEL), lambda i: (0, 0)),
                  pl.BlockSpec((None, D_MODEL, PROJ_WIDTH), lambda i: (layer, 0, 0))],
        out_specs=pl.BlockSpec((tm, PROJ_WIDTH), _tile_map(nseq, nt)),
        out_shape=jax.ShapeDtypeStruct(out_shape, F32),
        compiler_params=_params(("parallel",)),
        name="rms_proj",
    )(x, g, w)


def _row_spec(tc, col, nchunks):
    return pl.BlockSpec((tc * SEQ_BLK, GROUP_WIDTH), lambda s, c: (s * nchunks + c, col))


def _full_spec(shape):
    nd = len(shape)
    return pl.BlockSpec(shape, lambda s, c: (0,) * nd)


def _seq_spec(shape):
    nd = len(shape)
    return pl.BlockSpec((SEQ_BLK,) + shape[1:], lambda s, c: (s,) + (0,) * (nd - 1))


def _head_sums(x):
    lane = lax.broadcasted_iota(jnp.int32, x.shape, 1)
    out = jnp.zeros_like(x)
    for h in range(GROUP_WIDTH // HEAD):
        m = (lane >= h * HEAD) & (lane < (h + 1) * HEAD)
        s = jnp.sum(jnp.where(m, x, 0.0), axis=1, keepdims=True)
        out = jnp.where(m, s, out)
    return out


UNITS = SEQ_BLK * 2


def _unit_masks():
    r, c = _iota((2 * LANES, LANES), 0), _iota((2 * LANES, LANES), 1)
    ones2 = jnp.where(_head_of(r & (LANES - 1)) == _head_of(c), 1.0, 0.0).astype(BF16)
    shape = (UNITS * HEAD, LANES)
    eye = (_iota(shape, 1) & (HEAD - 1)) == (_iota(shape, 0) & (HEAD - 1))
    return ones2, eye


def _seg_sum_mxu(p, ones2):
    hi = p.astype(BF16)
    lo = (p - hi.astype(F32)).astype(BF16)
    return _dot(jnp.concatenate([hi, lo], axis=1), ones2)


def _unit_rows(ref, t):
    return jnp.concatenate([jnp.broadcast_to(ref[t, b:b + 1, p * LANES:(p + 1) * LANES], (HEAD, LANES))
                            for b in range(SEQ_BLK) for p in range(2)], axis=0)


def _load_unit_tiles(s0_ref, s_scr, *, transpose):
    for b in range(SEQ_BLK):
        for p in range(2):
            heads = [s0_ref[b, 2 * p + j] for j in range(2)]
            s_scr[b, p] = jnp.concatenate([h.T if transpose else h for h in heads], axis=1)


def _store_unit_tiles(st_ref, s_scr, *, transpose):
    for b in range(SEQ_BLK):
        for p in range(2):
            tile = s_scr[b, p]
            for j in range(2):
                h = tile[:, j * HEAD:(j + 1) * HEAD]
                st_ref[b, 2 * p + j] = h.T if transpose else h


def _store_unit_rows(ref, t, cols, eye):
    picked = jnp.where(eye, cols, 0.0)
    for b in range(SEQ_BLK):
        for p in range(2):
            u = b * 2 + p
            ref[t, b:b + 1, p * LANES:(p + 1) * LANES] = jnp.sum(picked[u * HEAD:(u + 1) * HEAD], axis=0, keepdims=True)


def _s5_kernel(u_ref, h0r_ref, h0i_ref, lr_ref, li_ref, ldt_ref, bre_ref, bim_ref, ccat_ref, d_ref,
               gw_ref, gb_ref, y_ref, hr_out, hi_out, h_scr, bu_scr, *, tc):
    c = pl.program_id(1)

    @pl.when(c == 0)
    def _():
        h_scr[0] = h0r_ref[...]
        h_scr[1] = h0i_ref[...]

    lr, li = lr_ref[...], li_ref[...]
    dt = jnp.exp(ldt_ref[...])
    mag = jnp.exp(lr * dt)
    ab_re, ab_im = mag * jnp.cos(li * dt), mag * jnp.sin(li * dt)
    den = lr * lr + li * li
    zr, zi = ab_re - 1.0, ab_im
    cr = (zr * lr + zi * li) / den
    ci = (zi * lr - zr * li) / den
    bre, bim = bre_ref[...], bim_ref[...]
    bb_re = (cr * bre - ci * bim).astype(BF16)
    bb_im = (cr * bim + ci * bre).astype(BF16)

    u = u_ref[...]
    ub = u.astype(BF16)
    bu_scr[:, 0:SSM_FLAT] = _dot(ub, bb_re)
    bu_scr[:, SSM_FLAT:2 * SSM_FLAT] = _dot(ub, bb_im)

    ar = jnp.broadcast_to(ab_re, (SEQ_BLK, SSM_FLAT))
    ai = jnp.broadcast_to(ab_im, (SEQ_BLK, SSM_FLAT))

    def step(t, carry):
        hr, hi = carry
        rows = pl.ds(pl.multiple_of(t * SEQ_BLK, SEQ_BLK), SEQ_BLK)
        nhr = ar * hr - ai * hi + bu_scr[rows, 0:SSM_FLAT]
        nhi = ar * hi + ai * hr + bu_scr[rows, SSM_FLAT:2 * SSM_FLAT]
        bu_scr[rows, 0:SSM_FLAT] = nhr
        bu_scr[rows, SSM_FLAT:2 * SSM_FLAT] = nhi
        return nhr, nhi

    hr, hi = lax.fori_loop(0, tc, step, (h_scr[0], h_scr[1]))
    h_scr[0] = hr
    h_scr[1] = hi

    y = _dot_nt(bu_scr[...].astype(BF16), ccat_ref[...]) + d_ref[...] * u
    z = jax.nn.gelu(y)
    out = z * jax.nn.sigmoid(_dot(z.astype(BF16), gw_ref[...]) + gb_ref[...])
    y_ref[...] = out.astype(y_ref.dtype)

    @pl.when(c == pl.num_programs(1) - 1)
    def _():
        hr_out[...] = hr
        hi_out[...] = hi


def s5_mixer(proj, h0_re, h0_im, lam_re, lam_im, log_dt, b_re_bd, b_im_bd, c_cat, d_skip, glu_w, glu_b, *, t_len, tc):
    nseq = h0_re.shape[0] // SEQ_BLK
    nchunks = t_len // tc
    n = proj.shape[0]
    st = jax.ShapeDtypeStruct(h0_re.shape, F32)
    return pl.pallas_call(
        functools.partial(_s5_kernel, tc=tc),
        grid=(nseq, nchunks),
        in_specs=[_row_spec(tc, COL_SSM, nchunks),
                  _seq_spec(h0_re.shape), _seq_spec(h0_im.shape),
                  _full_spec((1, SSM_FLAT)), _full_spec((1, SSM_FLAT)), _full_spec((1, SSM_FLAT)),
                  _full_spec((GROUP_WIDTH, SSM_FLAT)), _full_spec((GROUP_WIDTH, SSM_FLAT)),
                  _full_spec((GROUP_WIDTH, 2 * SSM_FLAT)), _full_spec((1, GROUP_WIDTH)),
                  _full_spec((GROUP_WIDTH, GROUP_WIDTH)), _full_spec((1, GROUP_WIDTH))],
        out_specs=[_row_spec(tc, 0, nchunks), _seq_spec(h0_re.shape), _seq_spec(h0_im.shape)],
        out_shape=[jax.ShapeDtypeStruct((n, GROUP_WIDTH), BF16), st, st],
        scratch_shapes=[pltpu.VMEM((2, SEQ_BLK, SSM_FLAT), F32),
                        pltpu.VMEM((tc * SEQ_BLK, 2 * SSM_FLAT), F32)],
        compiler_params=_params(("parallel", "arbitrary")),
        name="s5_mixer",
    )(proj, h0_re, h0_im, lam_re, lam_im, log_dt, b_re_bd, b_im_bd, c_cat, d_skip, glu_w, glu_b)


def _pool_kernel(u_ref, buf_ref, w_ref, sc_ref, y_ref, nbuf_ref, ext_scr, *, tc, pos0):
    c = pl.program_id(1)

    @pl.when(c == 0)
    def _():
        ext_scr[0:POOL_BUF] = buf_ref[...]

    u = u_ref[...].reshape(tc, SEQ_BLK, GROUP_WIDTH)
    ext_scr[POOL_BUF:POOL_BUF + tc] = u
    a1 = ext_scr[...]
    a2 = a1[1:] + a1[:-1]
    a4 = a2[2:] + a2[:-2]
    a8 = a4[4:] + a4[:-4]
    a16 = a8[8:] + a8[:-8]
    sums = (a2[14:], a4[12:], a8[8:], a16)

    shape = (tc, SEQ_BLK, GROUP_WIDTH)
    pos = lax.broadcasted_iota(jnp.int32, shape, 0) + (c * tc + pos0)
    lane = lax.broadcasted_iota(jnp.int32, shape, 2)
    pooled = None
    for gi in reversed(range(len(POOL_WINDOWS))):
        win = POOL_WINDOWS[gi]
        mean = sums[gi] / jnp.minimum(pos + 1, win).astype(F32)
        pooled = mean if pooled is None else jnp.where(lane < (gi + 1) * HEAD, mean, pooled)
    pooled = (pooled - u).reshape(tc * SEQ_BLK, GROUP_WIDTH)
    y_ref[...] = (_dot(pooled.astype(BF16), w_ref[...]) * sc_ref[...]).astype(y_ref.dtype)

    nb = ext_scr[tc:tc + POOL_BUF]
    ext_scr[0:POOL_BUF] = nb

    @pl.when(c == pl.num_programs(1) - 1)
    def _():
        nbuf_ref[...] = nb


def pool_mixer(proj, buf, w_bd, scale, *, t_len, tc, pos0):
    nseq = buf.shape[1] // SEQ_BLK
    nchunks = t_len // tc
    n = proj.shape[0]
    buf_spec = pl.BlockSpec((POOL_BUF, SEQ_BLK, GROUP_WIDTH), lambda s, c: (0, s, 0))
    return pl.pallas_call(
        functools.partial(_pool_kernel, tc=tc, pos0=pos0),
        grid=(nseq, nchunks),
        in_specs=[_row_spec(tc, COL_POOL, nchunks), buf_spec,
                  _full_spec((GROUP_WIDTH, GROUP_WIDTH)), _full_spec((1, GROUP_WIDTH))],
        out_specs=[_row_spec(tc, 0, nchunks), buf_spec],
        out_shape=[jax.ShapeDtypeStruct((n, GROUP_WIDTH), BF16), jax.ShapeDtypeStruct(buf.shape, F32)],
        scratch_shapes=[pltpu.VMEM((tc + POOL_BUF, SEQ_BLK, GROUP_WIDTH), F32)],
        compiler_params=_params(("parallel", "arbitrary")),
        name="pool_mixer",
    )(proj, buf, w_bd, scale)


def _hgrn_lower_bound(logits_ref, layer):
    rows = [logits_ref[l:l + 1, :] for l in range(DEPTH)]
    m = functools.reduce(jnp.maximum, rows)
    es = [jnp.exp(r - m) for r in rows]
    tot = functools.reduce(lambda a, b: a + b, es)
    lb = jnp.zeros_like(m)
    for l in range(1, layer + 1):
        lb = lb + es[l] / tot
    return lb


def _hgrn_kernel(pq_ref, pf_ref, pi_ref, pg_ref, s0_ref, acc_ref, lbl_ref, ng_ref, y_ref, st_ref,
                 s_scr, q_scr, f_scr, k_scr, v_scr, o_scr, *, tc, layer):
    c = pl.program_id(1)
    shape3 = (tc, SEQ_BLK, GROUP_WIDTH)

    @pl.when(c == 0)
    def _():
        _load_unit_tiles(s0_ref, s_scr, transpose=True)

    lb = _hgrn_lower_bound(lbl_ref, layer)
    zf = pf_ref[...]
    f_scr[...] = (lb + (1.0 - lb) * jax.nn.sigmoid(zf)).reshape(shape3)
    k_scr[...] = ((1.0 - lb) * jax.nn.sigmoid(-zf)).reshape(shape3)
    q_scr[...] = jax.nn.silu(pq_ref[...]).reshape(shape3)
    v_scr[...] = pi_ref[...].reshape(shape3)

    ones2, eye = _unit_masks()
    s = s_scr[...].reshape(UNITS * HEAD, LANES)
    for t in range(tc):
        vcol = _seg_sum_mxu(jnp.where(eye, _unit_rows(v_scr, t), 0.0), ones2)
        s = s * _unit_rows(f_scr, t) + vcol * _unit_rows(k_scr, t)
        _store_unit_rows(o_scr, t, _seg_sum_mxu(s * _unit_rows(q_scr, t), ones2), eye)
    s_scr[...] = s.reshape(s_scr.shape)

    o = o_scr[...].reshape(tc * SEQ_BLK, GROUP_WIDTH)
    ms = _head_sums(o * o) * (1.0 / HEAD)
    out = o * lax.rsqrt(ms + HGRN_NORM_EPS) * ng_ref[...] * jax.nn.silu(pg_ref[...])
    y_ref[...] = out.astype(y_ref.dtype)

    @pl.when(c == pl.num_programs(1) - 1)
    def _():
        _store_unit_tiles(st_ref, s_scr, transpose=True)


def _layer_seq_spec(shape, layer):
    rest = len(shape) - 2
    return pl.BlockSpec((None, SEQ_BLK) + shape[2:], lambda s, c: (layer, s) + (0,) * rest)


def _state_alias(acc, in_index, out_index):
    return {} if acc is None else {in_index: out_index}


def hgrn_mixer(proj, s_all, acc, lb_logits, norm_g, *, t_len, tc, layer):
    nseq = s_all.shape[1] // SEQ_BLK
    nchunks = t_len // tc
    n = proj.shape[0]
    tile = pltpu.VMEM((tc, SEQ_BLK, GROUP_WIDTH), F32)
    return pl.pallas_call(
        functools.partial(_hgrn_kernel, tc=tc, layer=layer),
        grid=(nseq, nchunks),
        in_specs=[_row_spec(tc, COL_Q, nchunks), _row_spec(tc, COL_F, nchunks),
                  _row_spec(tc, COL_I, nchunks), _row_spec(tc, COL_G, nchunks),
                  _layer_seq_spec(s_all.shape, layer), pl.BlockSpec(memory_space=pl.ANY),
                  _full_spec((DEPTH, GROUP_WIDTH)), _full_spec((1, GROUP_WIDTH))],
        out_specs=[_row_spec(tc, 0, nchunks), _layer_seq_spec(s_all.shape, layer)],
        out_shape=[jax.ShapeDtypeStruct((n, GROUP_WIDTH), BF16), jax.ShapeDtypeStruct(s_all.shape, F32)],
        scratch_shapes=[pltpu.VMEM((SEQ_BLK, 2, HEAD, LANES), F32), tile, tile, tile, tile, tile],
        input_output_aliases=_state_alias(acc, 5, 1),
        compiler_params=_params(("parallel", "arbitrary")),
        name="hgrn_mixer",
    )(proj, proj, proj, proj, s_all, s_all if acc is None else acc, lb_logits, norm_g)


def _rwkv_kernel(pr_ref, pk_ref, pv_ref, pl_ref, sh0_ref, s0_ref, acc_ref, mu_ref, w0_ref, a0_ref, kk_ref, ka_ref,
                 rk_ref, lng_ref, lnb_ref, w2_ref, a2_ref, g2_ref, y_ref, st_ref, sh_ref,
                 s_scr, prev_scr, r_scr, w_scr, k_scr, v_scr, nkk_scr, kka_scr, o_scr, *, tc):
    c = pl.program_id(1)
    shape3 = (tc, SEQ_BLK, GROUP_WIDTH)
    gw = GROUP_WIDTH

    @pl.when(c == 0)
    def _():
        _load_unit_tiles(s0_ref, s_scr, transpose=False)
        prev_scr[...] = sh0_ref[...]

    def shifted(ref, j):
        x = ref[...].reshape(shape3)
        first = prev_scr[:, j * gw:(j + 1) * gw].reshape(1, SEQ_BLK, gw)
        prev = first if tc == 1 else jnp.concatenate([first, x[:-1]], axis=0)
        prev_scr[:, j * gw:(j + 1) * gw] = x[tc - 1]
        return (x + (prev - x) * mu_ref[:, j * gw:(j + 1) * gw]).reshape(tc * SEQ_BLK, gw)

    xr, xk, xv, xl = shifted(pr_ref, 0), shifted(pk_ref, 1), shifted(pv_ref, 2), shifted(pl_ref, 3)
    w = -jax.nn.softplus(-(w0_ref[...] + _dot(jnp.tanh(xl).astype(BF16), w2_ref[...]))) - 0.5
    decay = jnp.exp(-jnp.exp(w))
    a = jax.nn.sigmoid(a0_ref[...] + _dot(xl.astype(BF16), a2_ref[...]))
    g = _dot(jax.nn.sigmoid(xl).astype(BF16), g2_ref[...])
    kk = xk * kk_ref[...]
    kk = kk / jnp.maximum(jnp.sqrt(_head_sums(kk * kk)), 1e-12)
    k = xk * (1.0 + (a - 1.0) * ka_ref[...])

    r_scr[...] = xr.reshape(shape3)
    w_scr[...] = decay.reshape(shape3)
    k_scr[...] = k.reshape(shape3)
    v_scr[...] = xv.reshape(shape3)
    nkk_scr[...] = (-kk).reshape(shape3)
    kka_scr[...] = (kk * a).reshape(shape3)

    ones2, eye = _unit_masks()
    s = s_scr[...].reshape(UNITS * HEAD, LANES)
    for t in range(tc):
        sa = _seg_sum_mxu(s * _unit_rows(nkk_scr, t), ones2)
        vcol = _seg_sum_mxu(jnp.where(eye, _unit_rows(v_scr, t), 0.0), ones2)
        s = s * _unit_rows(w_scr, t) + sa * _unit_rows(kka_scr, t) + vcol * _unit_rows(k_scr, t)
        _store_unit_rows(o_scr, t, _seg_sum_mxu(s * _unit_rows(r_scr, t), ones2), eye)
    s_scr[...] = s.reshape(s_scr.shape)

    y = o_scr[...].reshape(tc * SEQ_BLK, gw)
    mean = _head_sums(y) * (1.0 / HEAD)
    d = y - mean
    var = _head_sums(d * d) * (1.0 / HEAD)
    yn = d * lax.rsqrt(var + RWKV_GN_EPS) * lng_ref[...] + lnb_ref[...]
    bonus = _head_sums(xr * k * rk_ref[...]) * xv
    y_ref[...] = ((yn + bonus) * g).astype(y_ref.dtype)

    @pl.when(c == pl.num_programs(1) - 1)
    def _():
        _store_unit_tiles(st_ref, s_scr, transpose=False)
        sh_ref[...] = prev_scr[...]


def rwkv_mixer(proj, shift0, s_all, acc, mu, w0, a0, k_k, k_a, r_k, ln_g, ln_b, w2p, a2p, g2p, *, t_len, tc, layer):
    nseq = s_all.shape[1] // SEQ_BLK
    nchunks = t_len // tc
    n = proj.shape[0]
    tile = pltpu.VMEM((tc, SEQ_BLK, GROUP_WIDTH), F32)
    vec = _full_spec((1, GROUP_WIDTH))
    mat = _full_spec((GROUP_WIDTH, GROUP_WIDTH))
    return pl.pallas_call(
        functools.partial(_rwkv_kernel, tc=tc),
        grid=(nseq, nchunks),
        in_specs=[_row_spec(tc, COL_R, nchunks), _row_spec(tc, COL_K, nchunks),
                  _row_spec(tc, COL_V, nchunks), _row_spec(tc, COL_LORA, nchunks),
                  _seq_spec(shift0.shape), _layer_seq_spec(s_all.shape, layer), pl.BlockSpec(memory_space=pl.ANY),
                  _full_spec((1, RWKV_PROJ)), vec, vec, vec, vec, vec, vec, vec, mat, mat, mat],
        out_specs=[_row_spec(tc, 0, nchunks), _layer_seq_spec(s_all.shape, layer), _seq_spec(shift0.shape)],
        out_shape=[jax.ShapeDtypeStruct((n, GROUP_WIDTH), BF16), jax.ShapeDtypeStruct(s_all.shape, F32),
                   jax.ShapeDtypeStruct(shift0.shape, F32)],
        scratch_shapes=[pltpu.VMEM((SEQ_BLK, 2, HEAD, LANES), F32), pltpu.VMEM((SEQ_BLK, RWKV_PROJ), F32),
                        tile, tile, tile, tile, tile, tile, tile],
        input_output_aliases=_state_alias(acc, 6, 1),
        compiler_params=_params(("parallel", "arbitrary")),
        name="rwkv_mixer",
    )(proj, proj, proj, proj, shift0, s_all, s_all if acc is None else acc, mu, w0, a0, k_k, k_a, r_k, ln_g, ln_b,
      w2p, a2p, g2p)


def _dot_nt(a, b):
    return lax.dot_general(a, b, (((1,), (1,)), ((), ())), preferred_element_type=F32)


def _dot_tn(a, b):
    return lax.dot_general(a, b, (((0,), (0,)), ((), ())), preferred_element_type=F32)


def _iota(shape, dim):
    return lax.broadcasted_iota(jnp.int32, shape, dim)


def _head_of(idx):
    return lax.shift_right_logical(idx, HEAD.bit_length() - 1)


def _cumsum_rows(x):
    n = x.shape[0]
    tri = jnp.where(_iota((n, n), 0) >= _iota((n, n), 1), 1.0, 0.0).astype(BF16)
    hi = x.astype(BF16)
    rest = x - hi.astype(F32)
    mid = rest.astype(BF16)
    lo = (rest - mid.astype(F32)).astype(BF16)
    return _dot(tri, hi) + _dot(tri, mid) + _dot(tri, lo)


def _own_head(shape, rows_per_head):
    row_h = lax.shift_right_logical(_iota(shape, 0), rows_per_head.bit_length() - 1)
    return row_h == _head_of(_iota(shape, 1))


def _head_expand(x):
    xx = jnp.concatenate([x] * (GROUP_WIDTH // HEAD), axis=0)
    return jnp.where(_own_head(xx.shape, x.shape[0]), xx, 0.0)


def _head_collapse(xx):
    n = xx.shape[0] // (GROUP_WIDTH // HEAD)
    return xx[0:n] + xx[n:2 * n] + xx[2 * n:3 * n] + xx[3 * n:4 * n]


def _block_diag_mask():
    shape = (GROUP_WIDTH, GROUP_WIDTH)
    return _head_of(_iota(shape, 0)) == _head_of(_iota(shape, 1))


def _seq_col_spec(rows, col, ncols):
    return pl.BlockSpec((rows, GROUP_WIDTH), lambda b, c: (c, b * ncols + col))


HGRN_CHUNK = 128
HGRN_SUB = 16


HGRN_SEQS_PER_STEP = 4


def _run_staged(stages):
    for _ in itertools.zip_longest(*stages):
        pass


def _hgrn_chunk_one(pq_ref, pf_ref, pi_ref, pg_ref, lbl_ref, ng_ref, y_ref, w_scr, k_scr, b_scr, v_scr, p_scr,
                    o_scr, *, layer):
    L, n = HGRN_CHUNK, HGRN_SUB
    half = n // 2
    lb = _hgrn_lower_bound(lbl_ref, layer)
    z = pf_ref[...]
    g = jnp.logaddexp(jnp.log1p(-lb) + jax.nn.log_sigmoid(z), jnp.log(lb))
    kg = (1.0 - lb) * jax.nn.sigmoid(-z)
    q = jax.nn.silu(pq_ref[...])
    v = pi_ref[...]
    bc = _cumsum_rows(g)
    k_scr[...] = kg
    b_scr[...] = bc
    v_scr[...] = v
    yield

    bd = _block_diag_mask()
    ones_bd = jnp.where(bd, 1.0, 0.0).astype(BF16)
    rid = _iota((n, GROUP_WIDTH), 0)
    rid_lo = _iota((half, GROUP_WIDTH), 0) + half
    for sb in range(L // n):
        base = sb * n
        qs, bs = q[base:base + n], bc[base:base + n]
        q_lo, b_lo = q[base + half:base + n], bc[base + half:base + n]
        for s in range(n):
            ks, bsrow = k_scr[base + s:base + s + 1, :], b_scr[base + s:base + s + 1, :]
            if s < half:
                p_scr[s * n:(s + 1) * n, :] = qs * ks * jnp.where(rid >= s, jnp.exp(bs - bsrow), 0.0)
            else:
                p_scr[s * n + half:(s + 1) * n, :] = q_lo * ks * jnp.where(rid_lo >= s, jnp.exp(b_lo - bsrow), 0.0)
        r = _dot(p_scr[...].astype(BF16), ones_bd)
        acc = jnp.zeros((n, GROUP_WIDTH), F32)
        for s in range(n):
            acc = acc + r[s * n:(s + 1) * n] * v_scr[base + s:base + s + 1, :]
        o_scr[base:base + n, :] = acc
        yield

    vb = v.astype(BF16)
    for i in range(1, L // n):
        r0 = i * n
        ref = b_scr[r0 - 1:r0, :]
        qt = q[r0:r0 + n] * jnp.exp(bc[r0:r0 + n] - ref)
        kt = jnp.concatenate([kg[:r0] * jnp.exp(ref - bc[:r0]), jnp.zeros((L - r0, GROUP_WIDTH), F32)], axis=0)
        att = _dot_nt(_head_expand(qt).astype(BF16), kt.astype(BF16))
        yield
        ox = _dot(att.astype(BF16), vb)
        o_scr[r0:r0 + n, :] += _head_collapse(jnp.where(_own_head(ox.shape, n), ox, 0.0))
        yield

    w = w_scr[...]
    o = o_scr[...] + _dot_nt((q * jnp.exp(bc)).astype(BF16), w.astype(BF16))
    b_end = b_scr[L - 1:L, :]
    upd = _dot_tn(vb, (kg * jnp.exp(b_end - bc)).astype(BF16))
    w_scr[...] = w * jnp.exp(b_end) + jnp.where(bd, upd, 0.0)
    yield

    ms = _head_sums(o * o) * (1.0 / HEAD)
    out = o * lax.rsqrt(ms + HGRN_NORM_EPS) * ng_ref[...] * jax.nn.silu(pg_ref[...])
    y_ref[...] = out.astype(y_ref.dtype)


RWKV_CHUNK = 64


def _rwkv_chunk_one(pr_ref, pk_ref, pv_ref, pl_ref, mu_ref, w0_ref, a0_ref, kk_ref, ka_ref, rk_ref, lng_ref,
                    lnb_ref, w2_ref, a2_ref, g2_ref, y_ref, w_scr, prev_scr):
    L = RWKV_CHUNK
    gw = GROUP_WIDTH
    rid = _iota((L, gw), 0)

    def shifted(ref, j):
        x = ref[...]
        prev = jnp.where(rid == 0, prev_scr[:, j * gw:(j + 1) * gw], pltpu.roll(x, 1, axis=0))
        prev_scr[:, j * gw:(j + 1) * gw] = x[L - 1:L]
        return x + (prev - x) * mu_ref[:, j * gw:(j + 1) * gw]

    xr, xk, xv, xl = shifted(pr_ref, 0), shifted(pk_ref, 1), shifted(pv_ref, 2), shifted(pl_ref, 3)
    w = -jax.nn.softplus(-(w0_ref[...] + _dot(jnp.tanh(xl).astype(BF16), w2_ref[...]))) - 0.5
    lw = -jnp.exp(w)
    a = jax.nn.sigmoid(a0_ref[...] + _dot(xl.astype(BF16), a2_ref[...]))
    g = _dot(jax.nn.sigmoid(xl).astype(BF16), g2_ref[...])
    kk = xk * kk_ref[...]
    kk = kk / jnp.maximum(jnp.sqrt(_head_sums(kk * kk)), 1e-12)
    k = xk * (1.0 + (a - 1.0) * ka_ref[...])
    beta = kk * a
    yield

    cs = _cumsum_rows(lw)
    c_end = cs[L - 1:L]
    e_neg = jnp.exp(-cs)
    e_end = jnp.exp(c_end - cs)
    ar = jnp.concatenate([_head_expand(-kk * jnp.exp(cs - lw)), _head_expand(xr * jnp.exp(cs))], axis=0).astype(BF16)
    bk = jnp.concatenate([_head_expand(beta * e_neg), _head_expand(k * e_neg)], axis=0).astype(BF16)
    vx = _head_expand(xv).astype(BF16)
    yield

    nh = 4 * L
    gmat = _dot_nt(ar, bk)
    tt = _iota((nh, nh), 0) & (L - 1)
    ss = _iota((nh, nh), 1) & (L - 1)
    strict, incl = ss < tt, ss <= tt
    nab = jnp.where(strict, gmat[0:nh, 0:nh], 0.0)
    nak = jnp.where(strict, gmat[0:nh, nh:2 * nh], 0.0).astype(BF16)
    nrb = jnp.where(incl, gmat[nh:2 * nh, 0:nh], 0.0).astype(BF16)
    nrk = jnp.where(incl, gmat[nh:2 * nh, nh:2 * nh], 0.0).astype(BF16)
    yield

    ri, ci = _iota((nh, nh), 0), _iota((nh, nh), 1)

    def same_block(size):
        sh = size.bit_length() - 1
        return lax.shift_right_logical(ri, sh) == lax.shift_right_logical(ci, sh)

    base = 8
    m = jnp.where(same_block(base), nab, 0.0)
    t_inv = jnp.where(ri == ci, 1.0, 0.0) + m
    m = m.astype(BF16)
    for _ in range(base.bit_length() - 2):
        m = _dot(m, m).astype(BF16)
        yield
        t_inv = t_inv + _dot(t_inv.astype(BF16), m)
        yield
    size = base
    while size < L:
        off = jnp.where(same_block(2 * size), jnp.where(same_block(size), 0.0, nab), 0.0).astype(BF16)
        tb = t_inv.astype(BF16)
        half = _dot(tb, off).astype(BF16)
        yield
        t_inv = t_inv + _dot(half, tb)
        yield
        size *= 2

    wst = w_scr[...]
    sw = _dot_nt(ar, wst.astype(BF16))
    rhs = (sw[0:nh] + _dot(nak, vx)).astype(BF16)
    yield
    x = _dot(t_inv.astype(BF16), rhs)
    yield
    ux = x.astype(BF16)
    yx = sw[nh:2 * nh] + _dot(nrb, ux) + _dot(nrk, vx)
    y = _head_collapse(yx)
    u = _head_collapse(x)
    yield

    upd = _dot_tn(jnp.concatenate([u, xv], axis=0).astype(BF16),
                  jnp.concatenate([beta * e_end, k * e_end], axis=0).astype(BF16))
    w_scr[...] = wst * jnp.exp(c_end) + jnp.where(_block_diag_mask(), upd, 0.0)
    yield

    mean = _head_sums(y) * (1.0 / HEAD)
    d = y - mean
    var = _head_sums(d * d) * (1.0 / HEAD)
    yn = d * lax.rsqrt(var + RWKV_GN_EPS) * lng_ref[...] + lnb_ref[...]
    bonus = _head_sums(xr * k * rk_ref[...]) * xv
    y_ref[...] = ((yn + bonus) * g).astype(y_ref.dtype)


RWKV_SEQS_PER_STEP = 8


def _hgrn_rwkv_chunk_kernel(*refs, layer, rwkv_steps, hgrn_steps):
    nr, nh = RWKV_SEQS_PER_STEP, HGRN_SEQS_PER_STEP
    it = iter(refs)
    take = lambda n: [next(it) for _ in range(n)]
    r_seq, r_par, h_seq = take(4 * nr), take(11), take(4 * nh)
    lbl_ref, ng_ref = take(2)
    yr_ref, str_ref, shr_ref, yh_ref, sth_ref = take(5)
    wr_scr, prev_scr, wh_scr, k_scr, b_scr, v_scr, p_scr, o_scr = take(8)
    gw = GROUP_WIDTH
    g = pl.program_id(0)
    cr, ch = g % rwkv_steps, g % hgrn_steps

    @pl.when(cr == 0)
    def _():
        wr_scr[...] = jnp.zeros_like(wr_scr)
        prev_scr[...] = jnp.zeros_like(prev_scr)

    @pl.when(ch == 0)
    def _():
        wh_scr[...] = jnp.zeros_like(wh_scr)
        p_scr[...] = jnp.zeros_like(p_scr)

    rwkv = [_rwkv_chunk_one(*r_seq[4 * s:4 * s + 4], *r_par, yr_ref.at[:, s * gw:(s + 1) * gw], wr_scr.at[s],
                            prev_scr.at[s]) for s in range(nr)]
    hgrn = [_hgrn_chunk_one(*h_seq[4 * s:4 * s + 4], lbl_ref, ng_ref, yh_ref.at[:, s * gw:(s + 1) * gw],
                            wh_scr.at[s], k_scr.at[s], b_scr.at[s], v_scr.at[s], p_scr.at[s], o_scr.at[s],
                            layer=layer) for s in range(nh)]
    order = []
    for s in range(max(nr, nh)):
        order += rwkv[s:s + 1] + hgrn[s:s + 1]
    _run_staged(order)

    def head_blocks(w):
        return [w[h * HEAD:(h + 1) * HEAD, h * HEAD:(h + 1) * HEAD] for h in range(gw // HEAD)]

    @pl.when(cr == rwkv_steps - 1)
    def _():
        for s in range(nr):
            for h, blk in enumerate(head_blocks(wr_scr[s])):
                str_ref[s, h] = blk
        shr_ref[...] = prev_scr[...]

    @pl.when(ch == hgrn_steps - 1)
    def _():
        for s in range(nh):
            for h, blk in enumerate(head_blocks(wh_scr[s])):
                sth_ref[s, h] = blk.T


def hgrn_rwkv_chunk_mixer(proj2, lb_logits, norm_g, mu, w0, a0, k_k, k_a, r_k, ln_g, ln_b, w2p, a2p, g2p, *,
                          bsz, t_len, layer):
    nr, nh, lr, lh, n = RWKV_SEQS_PER_STEP, HGRN_SEQS_PER_STEP, RWKV_CHUNK, HGRN_CHUNK, HGRN_SUB
    gw = GROUP_WIDTH
    ncols = PROJ_WIDTH // gw
    heads = (gw // HEAD, HEAD, HEAD)
    rwkv_steps, hgrn_steps = t_len // lr, t_len // lh
    steps = (bsz // nr) * rwkv_steps
    assert (bsz // nh) * hgrn_steps == steps
    const = lambda shape: pl.BlockSpec(shape, lambda g: (0,) * len(shape))
    r_specs = [pl.BlockSpec((lr, gw), functools.partial(
                   lambda g, s, col: (g % rwkv_steps, ((g // rwkv_steps) * nr + s) * ncols + col), s=s, col=col))
               for s in range(nr) for col in (COL_R, COL_K, COL_V, COL_LORA)]
    h_specs = [pl.BlockSpec((lh, gw), functools.partial(
                   lambda g, s, col: (g % hgrn_steps, ((g // hgrn_steps) * nh + s) * ncols + col), s=s, col=col))
               for s in range(nh) for col in (COL_Q, COL_F, COL_I, COL_G)]
    vec, mat = const((1, gw)), const((gw, gw))
    tile = pltpu.VMEM((nh, lh, gw), F32)
    yr, st_r, sh_r, yh, st_h = pl.pallas_call(
        functools.partial(_hgrn_rwkv_chunk_kernel, layer=layer, rwkv_steps=rwkv_steps, hgrn_steps=hgrn_steps),
        grid=(steps,),
        in_specs=r_specs + [const((1, RWKV_PROJ)), vec, vec, vec, vec, vec, vec, vec, mat, mat, mat]
                 + h_specs + [const((DEPTH, gw)), vec],
        out_specs=[pl.BlockSpec((lr, nr * gw), lambda g: (g % rwkv_steps, g // rwkv_steps)),
                   pl.BlockSpec((nr,) + heads, lambda g: (g // rwkv_steps, 0, 0, 0)),
                   pl.BlockSpec((nr, 1, RWKV_PROJ), lambda g: (g // rwkv_steps, 0, 0)),
                   pl.BlockSpec((lh, nh * gw), lambda g: (g % hgrn_steps, g // hgrn_steps)),
                   pl.BlockSpec((nh,) + heads, lambda g: (g // hgrn_steps, 0, 0, 0))],
        out_shape=[jax.ShapeDtypeStruct((t_len, bsz * gw), BF16), jax.ShapeDtypeStruct((bsz,) + heads, F32),
                   jax.ShapeDtypeStruct((bsz, 1, RWKV_PROJ), F32),
                   jax.ShapeDtypeStruct((t_len, bsz * gw), BF16), jax.ShapeDtypeStruct((bsz,) + heads, F32)],
        scratch_shapes=[pltpu.VMEM((nr, gw, gw), F32), pltpu.VMEM((nr, 1, RWKV_PROJ), F32),
                        pltpu.VMEM((nh, gw, gw), F32), tile, tile, tile, pltpu.VMEM((nh, n * n, gw), F32), tile],
        compiler_params=_params(("arbitrary",)),
        name="hgrn_rwkv_chunk_mixer",
    )(*([proj2] * (4 * nr)), mu, w0, a0, k_k, k_a, r_k, ln_g, ln_b, w2p, a2p, g2p,
      *([proj2] * (4 * nh)), lb_logits, norm_g)
    return yh, st_h, yr, st_r, sh_r


S5_CHUNK = 64


def _s5_seq_kernel(*refs, nseq):
    u_refs = refs[:nseq]
    lr_ref, li_ref, ldt_ref, bre_ref, bim_ref, ccat_ref, d_ref, gw_ref, gb_ref = refs[nseq:nseq + 9]
    y_ref, hr_out, hi_out, h_scr, bu_scr, bb_scr, ab_scr, perm_scr, u_scr = refs[nseq + 9:]
    tc = S5_CHUNK
    n = nseq * tc
    c = pl.program_id(0)

    @pl.when(c == 0)
    def _():
        h_scr[...] = jnp.zeros_like(h_scr)
        lr, li = lr_ref[...], li_ref[...]
        dt = jnp.exp(ldt_ref[...])
        mag = jnp.exp(lr * dt)
        ab_re, ab_im = mag * jnp.cos(li * dt), mag * jnp.sin(li * dt)
        den = lr * lr + li * li
        zr, zi = ab_re - 1.0, ab_im
        cr = (zr * lr + zi * li) / den
        ci = (zi * lr - zr * li) / den
        bre, bim = bre_ref[...], bim_ref[...]
        bb_scr[:, 0:SSM_FLAT] = (cr * bre - ci * bim).astype(BF16)
        bb_scr[:, SSM_FLAT:2 * SSM_FLAT] = (cr * bim + ci * bre).astype(BF16)
        ab_scr[0] = jnp.broadcast_to(ab_re, (nseq, SSM_FLAT))
        ab_scr[1] = jnp.broadcast_to(ab_im, (nseq, SSM_FLAT))
        ri, cj = _iota((n, n), 0), _iota((n, n), 1)
        lseq, lt = nseq.bit_length() - 1, tc.bit_length() - 1
        perm_scr[0] = jnp.where(cj == (ri & (nseq - 1)) * tc + lax.shift_right_logical(ri, lseq), 1.0, 0.0).astype(BF16)
        perm_scr[1] = jnp.where(cj == (ri & (tc - 1)) * nseq + lax.shift_right_logical(ri, lt), 1.0, 0.0).astype(BF16)

        bu_scr[1] = jnp.zeros(bu_scr.shape[1:], F32)
        u_scr[1] = jnp.zeros(u_scr.shape[1:], F32)

    last = pl.num_programs(0) - 1

    def pipeline(slot):

        def scan_chunk():
            u = jnp.concatenate([r[...] for r in u_refs], axis=0)
            u_scr[slot] = u
            u_t = _dot(perm_scr[0], u.astype(BF16)).astype(BF16)
            bu_scr[slot] = _dot(u_t, bb_scr[...])
            yield
            ar, ai = ab_scr[0], ab_scr[1]
            hr0, hi0 = h_scr[0], h_scr[1]
            hr, hi = hr0, hi0
            for t in range(tc):
                rows = slice(t * nseq, (t + 1) * nseq)
                hr, hi = (ar * hr - ai * hi + bu_scr[slot, rows, 0:SSM_FLAT],
                          ar * hi + ai * hr + bu_scr[slot, rows, SSM_FLAT:2 * SSM_FLAT])
                bu_scr[slot, rows, 0:SSM_FLAT] = hr
                bu_scr[slot, rows, SSM_FLAT:2 * SSM_FLAT] = hi
                if t % 8 == 7:
                    yield
            h_scr[0] = jnp.where(c < last, hr, hr0)
            h_scr[1] = jnp.where(c < last, hi, hi0)

        def emit_chunk():
            prev = 1 - slot
            y_t = _dot_nt(bu_scr[prev].astype(BF16), ccat_ref[...])
            yield
            hi_p = y_t.astype(BF16)
            rest = y_t - hi_p.astype(F32)
            mid_p = rest.astype(BF16)
            lo_p = (rest - mid_p.astype(F32)).astype(BF16)
            to_seq = perm_scr[1]
            y = _dot(to_seq, hi_p) + _dot(to_seq, mid_p) + _dot(to_seq, lo_p) + d_ref[...] * u_scr[prev]
            yield
            z = jax.nn.gelu(y)
            gate = _dot(z.astype(BF16), gw_ref[...])
            yield
            out = z * jax.nn.sigmoid(gate + gb_ref[...])
            for b in range(nseq):
                y_ref[:, b * GROUP_WIDTH:(b + 1) * GROUP_WIDTH] = out[b * tc:(b + 1) * tc].astype(y_ref.dtype)

        _run_staged([scan_chunk(), emit_chunk()])

    for parity in range(2):
        pl.when(c % 2 == parity)(functools.partial(pipeline, parity))

    @pl.when(c == last)
    def _():
        hr_out[...] = h_scr[0]
        hi_out[...] = h_scr[1]


def s5_seq_mixer(proj2, lam_re, lam_im, log_dt, b_re_bd, b_im_bd, c_cat, d_skip, glu_w, glu_b, *, bsz, t_len):
    tc = S5_CHUNK
    ncols = PROJ_WIDTH // GROUP_WIDTH
    full = lambda shape: pl.BlockSpec(shape, lambda c: (0,) * len(shape))
    st = jax.ShapeDtypeStruct((bsz, SSM_FLAT), F32)
    nchunks = t_len // tc
    u_specs = [pl.BlockSpec((tc, GROUP_WIDTH), functools.partial(
                   lambda c, b: (jnp.minimum(c, nchunks - 1), b * ncols + COL_SSM), b=b)) for b in range(bsz)]
    return pl.pallas_call(
        functools.partial(_s5_seq_kernel, nseq=bsz),
        grid=(nchunks + 1,),
        in_specs=u_specs + [full((1, SSM_FLAT)), full((1, SSM_FLAT)), full((1, SSM_FLAT)),
                            full((GROUP_WIDTH, SSM_FLAT)), full((GROUP_WIDTH, SSM_FLAT)),
                            full((GROUP_WIDTH, 2 * SSM_FLAT)), full((1, GROUP_WIDTH)),
                            full((GROUP_WIDTH, GROUP_WIDTH)), full((1, GROUP_WIDTH))],
        out_specs=[pl.BlockSpec((tc, bsz * GROUP_WIDTH), lambda c: (jnp.maximum(c - 1, 0), 0)),
                   full((bsz, SSM_FLAT)), full((bsz, SSM_FLAT))],
        out_shape=[jax.ShapeDtypeStruct((t_len, bsz * GROUP_WIDTH), BF16), st, st],
        scratch_shapes=[pltpu.VMEM((2, bsz, SSM_FLAT), F32), pltpu.VMEM((2, bsz * tc, 2 * SSM_FLAT), F32),
                        pltpu.VMEM((GROUP_WIDTH, 2 * SSM_FLAT), BF16), pltpu.VMEM((2, bsz, SSM_FLAT), F32),
                        pltpu.VMEM((2, bsz * tc, bsz * tc), BF16), pltpu.VMEM((2, bsz * tc, GROUP_WIDTH), F32)],
        compiler_params=_params(("arbitrary",)),
        name="s5_seq_mixer",
    )(*([proj2] * bsz), lam_re, lam_im, log_dt, b_re_bd, b_im_bd, c_cat, d_skip, glu_w, glu_b)


POOL_CHUNK = 1024
POOL_HIST = 16


def _pool_seq_kernel(u_ref, w_ref, sc_ref, y_ref, nbuf_ref, ext_scr):
    L, hist = u_ref.shape[0], POOL_HIST
    c = pl.program_id(1)

    @pl.when(c == 0)
    def _():
        ext_scr[0:hist] = jnp.zeros((hist, GROUP_WIDTH), F32)

    u = u_ref[...]
    ext_scr[hist:hist + L] = u
    e = ext_scr[...]
    a2 = e + pltpu.roll(e, 1, axis=0)
    a4 = a2 + pltpu.roll(a2, 2, axis=0)
    a8 = a4 + pltpu.roll(a4, 4, axis=0)
    a16 = a8 + pltpu.roll(a8, 8, axis=0)
    sums = (a2[hist:], a4[hist:], a8[hist:], a16[hist:])

    shape = (L, GROUP_WIDTH)
    pos = _iota(shape, 0) + c * L
    lane = _iota(shape, 1)
    pooled = None
    for gi in reversed(range(len(POOL_WINDOWS))):
        win = POOL_WINDOWS[gi]
        mean = sums[gi] / jnp.minimum(pos + 1, win).astype(F32)
        pooled = mean if pooled is None else jnp.where(lane < (gi + 1) * HEAD, mean, pooled)
    y_ref[...] = (_dot((pooled - u).astype(BF16), w_ref[...]) * sc_ref[...]).astype(y_ref.dtype)

    nb = ext_scr[L:L + hist]
    ext_scr[0:hist] = nb

    @pl.when(c == pl.num_programs(1) - 1)
    def _():
        nbuf_ref[...] = nb


def pool_seq_mixer(proj2, w_bd, scale, *, bsz, t_len):
    L, hist = min(POOL_CHUNK, t_len), POOL_HIST
    ncols = PROJ_WIDTH // GROUP_WIDTH
    return pl.pallas_call(
        _pool_seq_kernel,
        grid=(bsz, t_len // L),
        in_specs=[_seq_col_spec(L, COL_POOL, ncols),
                  pl.BlockSpec((GROUP_WIDTH, GROUP_WIDTH), lambda b, c: (0, 0)),
                  pl.BlockSpec((1, GROUP_WIDTH), lambda b, c: (0, 0))],
        out_specs=[_seq_col_spec(L, 0, 1), pl.BlockSpec((hist, GROUP_WIDTH), lambda b, c: (b, 0))],
        out_shape=[jax.ShapeDtypeStruct((t_len, bsz * GROUP_WIDTH), BF16),
                   jax.ShapeDtypeStruct((bsz * hist, GROUP_WIDTH), F32)],
        scratch_shapes=[pltpu.VMEM((hist + L, GROUP_WIDTH), F32)],
        compiler_params=_params(("parallel", "arbitrary")),
        name="pool_seq_mixer",
    )(proj2, w_bd, scale)


def _mix_mlp_kernel(h_ref, ya_ref, yb_ref, yc_ref, yd_ref, wo_ref, g2_ref, wu_ref, wd_ref, gf_ref, o_ref,
                    h1_scr, xn_scr, acc_scr, *, final_norm):
    j = pl.program_id(1)
    gw = GROUP_WIDTH

    @pl.when(j == 0)
    def _():
        mix = (_dot(ya_ref[...], wo_ref[0:gw]) + _dot(yb_ref[...], wo_ref[gw:2 * gw])
               + _dot(yc_ref[...], wo_ref[2 * gw:3 * gw]) + _dot(yd_ref[...], wo_ref[3 * gw:4 * gw]))
        h1 = h_ref[...] + mix
        h1_scr[...] = h1
        xn_scr[...] = _rms(h1, g2_ref[...]).astype(BF16)
        acc_scr[...] = jnp.zeros_like(acc_scr)

    up = _dot(xn_scr[...], wu_ref[...])
    act = jnp.square(jnp.maximum(up, 0.0)).astype(BF16)
    acc_scr[...] += _dot(act, wd_ref[...])

    @pl.when(j == pl.num_programs(1) - 1)
    def _():
        out = h1_scr[...] + acc_scr[...]
        if final_norm:
            out = _rms(out, gf_ref[...])
        o_ref[...] = out


def mix_mlp(h, ys, w_out, g2, w_up, w_down, g_final, *, layer, final_norm, nseq=0):
    n = h.shape[0]
    tm = min(1024, n // max(nseq, 1))
    tf = 1024
    nt = n // tm // max(nseq, 1)
    row = lambda w: pl.BlockSpec((tm, w), lambda i, j: (i, 0))
    mix = pl.BlockSpec((tm, GROUP_WIDTH), _tile_map(nseq, nt))
    return pl.pallas_call(
        functools.partial(_mix_mlp_kernel, final_norm=final_norm),
        grid=(n // tm, D_FF // tf),
        in_specs=[row(D_MODEL), mix, mix, mix, mix,
                  pl.BlockSpec((None, D_MODEL, D_MODEL), lambda i, j: (layer, 0, 0)),
                  pl.BlockSpec((1, D_MODEL), lambda i, j: (0, 0)),
                  pl.BlockSpec((None, D_MODEL, tf), lambda i, j: (layer, 0, j)),
                  pl.BlockSpec((None, tf, D_MODEL), lambda i, j: (layer, j, 0)),
                  pl.BlockSpec((1, D_MODEL), lambda i, j: (0, 0))],
        out_specs=row(D_MODEL),
        out_shape=jax.ShapeDtypeStruct((n, D_MODEL), F32),
        scratch_shapes=[pltpu.VMEM((tm, D_MODEL), F32), pltpu.VMEM((tm, D_MODEL), BF16),
                        pltpu.VMEM((tm, D_MODEL), F32)],
        compiler_params=_params(("parallel", "arbitrary")),
        name="mix_mlp",
    )(h, *ys, w_out, g2, w_up, w_down, g_final)


def _block_diag(blocks):
    g, r, c = blocks.shape
    tiled = jnp.tile(blocks.reshape(g * r, c), (1, g))
    row_blk = lax.broadcasted_iota(jnp.int32, tiled.shape, 0) // r
    col_blk = lax.broadcasted_iota(jnp.int32, tiled.shape, 1) // c
    return jnp.where(row_blk == col_blk, tiled, 0.0)


def _pad_rows(w, start):
    rows = GROUP_WIDTH - start - w.shape[0]
    return jnp.concatenate([jnp.zeros((start, GROUP_WIDTH), w.dtype), w, jnp.zeros((rows, GROUP_WIDTH), w.dtype)])


def _layer_params(l, P):
    row = lambda a: a.reshape(1, -1)
    q = {}
    q["norm1_g"] = row(P["norm1_g"][l])
    q["lam_re"] = row(P["ssm_lambda_re"][l])
    q["lam_im"] = row(P["ssm_lambda_im"][l])
    q["log_dt"] = row(jnp.repeat(P["ssm_log_dt"][l], SSM_STATE))
    q["b_re"] = _block_diag(P["ssm_b_re"][l].transpose(0, 2, 1))
    q["b_im"] = _block_diag(P["ssm_b_im"][l].transpose(0, 2, 1))
    q["c_cat"] = jnp.concatenate([_block_diag(P["ssm_c_re"][l]), -_block_diag(P["ssm_c_im"][l])],
                                 axis=1).astype(BF16)
    q["ssm_d"] = row(P["ssm_d"][l])
    q["glu_w"] = P["ssm_glu_w"][l].astype(BF16)
    q["glu_b"] = row(P["ssm_glu_b"][l])
    q["hgrn_norm_g"] = row(P["hgrn_norm_g"][l])
    q["mu"] = row(P["rwkv_mu"][l])
    for name in ("w0", "a0", "k_k", "k_a", "r_k", "ln_g", "ln_b"):
        q[name] = row(P["rwkv_" + name][l])
    q["w2p"] = _pad_rows(P["rwkv_w2"][l], 0).astype(BF16)
    q["a2p"] = _pad_rows(P["rwkv_a2"][l], DECAY_LORA).astype(BF16)
    q["g2p"] = _pad_rows(P["rwkv_g2"][l], DECAY_LORA + AAA_LORA).astype(BF16)
    q["pool_w"] = _block_diag(P["pool_w"][l]).astype(BF16)
    q["pool_scale"] = row(P["pool_scale"][l])
    q["norm2_g"] = row(P["norm2_g"][l])
    return q


def _trunk_fresh(x_rows, bsz, t_len, layer_params, P):
    h = x_rows
    new = [[] for _ in range(6)]
    g_final = P["norm_f_g"].reshape(1, -1)
    for l in range(DEPTH):
        q = layer_params[l]
        proj2 = rms_proj(h, q["norm1_g"], P["w_in"], layer=l, nseq=bsz)
        y_a, s_re, s_im = s5_seq_mixer(proj2, q["lam_re"], q["lam_im"], q["log_dt"], q["b_re"], q["b_im"],
                                       q["c_cat"], q["ssm_d"], q["glu_w"], q["glu_b"], bsz=bsz, t_len=t_len)
        y_b, s_hg, y_c, s_wkv, s_sh = hgrn_rwkv_chunk_mixer(
            proj2, P["hgrn_lb_logits"], q["hgrn_norm_g"], q["mu"], q["w0"], q["a0"], q["k_k"], q["k_a"], q["r_k"],
            q["ln_g"], q["ln_b"], q["w2p"], q["a2p"], q["g2p"], bsz=bsz, t_len=t_len, layer=l)
        y_d, s_pool = pool_seq_mixer(proj2, q["pool_w"], q["pool_scale"], bsz=bsz, t_len=t_len)
        h = mix_mlp(h, (y_a, y_b, y_c, y_d), P["w_out"], q["norm2_g"], P["mlp_up"], P["mlp_down"], g_final,
                    layer=l, final_norm=(l == DEPTH - 1), nseq=bsz)
        s_pool = s_pool.reshape(bsz, POOL_HIST, GROUP_WIDTH)[:, POOL_HIST - POOL_BUF:]
        for lst, s in zip(new, (s_re, s_im, s_hg, s_wkv, s_sh, s_pool)):
            lst.append(s)
    return h, new


def _trunk_carry(x_rows, states, pos0, t_len, layer_params, P):
    ssm_re0, ssm_im0, hgrn0, wkv0, shift0, pool0 = states
    h = x_rows
    new = [[] for _ in range(6)]
    g_final = P["norm_f_g"].reshape(1, -1)
    tc = t_len
    s_hg = s_wkv = None
    for l in range(DEPTH):
        q = layer_params[l]
        proj = rms_proj(h, q["norm1_g"], P["w_in"], layer=l)
        y_a, s_re, s_im = s5_mixer(proj, ssm_re0[l], ssm_im0[l], q["lam_re"], q["lam_im"], q["log_dt"], q["b_re"],
                                   q["b_im"], q["c_cat"], q["ssm_d"], q["glu_w"], q["glu_b"], t_len=t_len, tc=tc)
        y_b, s_hg = hgrn_mixer(proj, hgrn0, s_hg, P["hgrn_lb_logits"], q["hgrn_norm_g"], t_len=t_len, tc=tc,
                               layer=l)
        y_c, s_wkv, s_sh = rwkv_mixer(proj, shift0[l], wkv0, s_wkv, q["mu"], q["w0"], q["a0"], q["k_k"], q["k_a"],
                                      q["r_k"], q["ln_g"], q["ln_b"], q["w2p"], q["a2p"], q["g2p"],
                                      t_len=t_len, tc=tc, layer=l)
        y_d, s_pool = pool_mixer(proj, pool0[l], q["pool_w"], q["pool_scale"], t_len=t_len, tc=tc, pos0=pos0)
        h = mix_mlp(h, (y_a, y_b, y_c, y_d), P["w_out"], q["norm2_g"], P["mlp_up"], P["mlp_down"], g_final,
                    layer=l, final_norm=(l == DEPTH - 1))
        for i, s in ((0, s_re), (1, s_im), (4, s_sh), (5, s_pool.transpose(1, 0, 2))):
            new[i].append(s)
    new[2], new[3] = s_hg, s_wkv
    return h, new


def _states_out(new, bsz):
    s_re, s_im, s_hg, s_wkv, s_sh, s_pool = new
    return (jnp.stack([s.reshape(bsz, SSM_GROUPS, SSM_STATE) for s in s_re]),
            jnp.stack([s.reshape(bsz, SSM_GROUPS, SSM_STATE) for s in s_im]),
            jnp.stack(s_hg) if isinstance(s_hg, list) else s_hg,
            jnp.stack(s_wkv) if isinstance(s_wkv, list) else s_wkv,
            jnp.stack([s.reshape(bsz, 1, RWKV_PROJ) for s in s_sh]),
            jnp.stack(s_pool))


def kernel(x_prompt, x_sample, state_ssm_re, state_ssm_im, state_hgrn, state_wkv, state_shift, state_pool, norm1_g, w_in, ssm_lambda_re, ssm_lambda_im, ssm_log_dt, ssm_b_re, ssm_b_im, ssm_c_re, ssm_c_im, ssm_d, ssm_glu_w, ssm_glu_b, hgrn_lb_logits, hgrn_norm_g, rwkv_mu, rwkv_w0, rwkv_w2, rwkv_a0, rwkv_a2, rwkv_g2, rwkv_k_k, rwkv_k_a, rwkv_r_k, rwkv_ln_g, rwkv_ln_b, pool_w, pool_scale, w_out, norm2_g, mlp_up, mlp_down, norm_f_g):
    P = dict(norm1_g=norm1_g, w_in=w_in, ssm_lambda_re=ssm_lambda_re, ssm_lambda_im=ssm_lambda_im,
             ssm_log_dt=ssm_log_dt, ssm_b_re=ssm_b_re, ssm_b_im=ssm_b_im, ssm_c_re=ssm_c_re, ssm_c_im=ssm_c_im,
             ssm_d=ssm_d, ssm_glu_w=ssm_glu_w, ssm_glu_b=ssm_glu_b, hgrn_lb_logits=hgrn_lb_logits,
             hgrn_norm_g=hgrn_norm_g, rwkv_mu=rwkv_mu, rwkv_w0=rwkv_w0, rwkv_w2=rwkv_w2, rwkv_a0=rwkv_a0,
             rwkv_a2=rwkv_a2, rwkv_g2=rwkv_g2, rwkv_k_k=rwkv_k_k, rwkv_k_a=rwkv_k_a, rwkv_r_k=rwkv_r_k,
             rwkv_ln_g=rwkv_ln_g, rwkv_ln_b=rwkv_ln_b, pool_w=pool_w, pool_scale=pool_scale, w_out=w_out,
             norm2_g=norm2_g, mlp_up=mlp_up, mlp_down=mlp_down, norm_f_g=norm_f_g)
    layer_params = [_layer_params(l, P) for l in range(DEPTH)]
    for name in ("w_in", "w_out", "mlp_up", "mlp_down"):
        P[name] = P[name].astype(BF16)

    bp, t_p, _ = x_prompt.shape
    yp, new_p = _trunk_fresh(x_prompt.reshape(bp * t_p, D_MODEL), bp, t_p, layer_params, P)
    y_prompt = yp.reshape(bp, t_p, D_MODEL)

    bs, t_s, _ = x_sample.shape
    nblk = bs // SEQ_BLK
    xs = x_sample.reshape(nblk, SEQ_BLK, t_s, D_MODEL).transpose(0, 2, 1, 3).reshape(bs * t_s, D_MODEL)
    st_s = ([state_ssm_re[l].reshape(bs, SSM_FLAT) for l in range(DEPTH)],
            [state_ssm_im[l].reshape(bs, SSM_FLAT) for l in range(DEPTH)],
            state_hgrn, state_wkv,
            [state_shift[l].reshape(bs, RWKV_PROJ) for l in range(DEPTH)],
            [state_pool[l].transpose(1, 0, 2) for l in range(DEPTH)])
    ys, new_s = _trunk_carry(xs, st_s, PAST_LEN, t_s, layer_params, P)
    y_sample = ys.reshape(nblk, t_s, SEQ_BLK, D_MODEL).transpose(0, 2, 1, 3).reshape(bs, t_s, D_MODEL)

    return (y_prompt, y_sample) + _states_out(new_p, bp) + _states_out(new_s, bs)
```

```python
import functools
import itertools

import jax
import jax.numpy as jnp
from jax import lax
from jax.experimental import pallas as pl
from jax.experimental.pallas import tpu as pltpu

F32 = jnp.float32
BF16 = jnp.bfloat16

D_MODEL = 1024
DEPTH = 2
PAST_LEN = 16384
GROUP_WIDTH = 256
HEAD = 64
SSM_GROUPS = 16
SSM_STATE = 64
SSM_FLAT = SSM_GROUPS * SSM_STATE
POOL_WINDOWS = (2, 4, 8, 16)
POOL_BUF = 15
DECAY_LORA = 64
AAA_LORA = 64
RWKV_PROJ = 1024
PROJ_WIDTH = 2560
D_FF = 4096
NORM_EPS = 1e-6
HGRN_NORM_EPS = 1e-5
RWKV_GN_EPS = 64e-5

SEQ_BLK = 8
LANES = 128
VMEM_LIMIT = 48 * 1024 * 1024

COL_SSM, COL_Q, COL_F, COL_I, COL_G, COL_R, COL_K, COL_V, COL_LORA, COL_POOL = range(10)


def _params(sem):
    return pltpu.CompilerParams(dimension_semantics=sem, vmem_limit_bytes=VMEM_LIMIT)


def _dot(a, b):
    return jnp.dot(a, b, preferred_element_type=F32)


def _rms(x, g):
    return x * lax.rsqrt(jnp.mean(x * x, axis=-1, keepdims=True) + NORM_EPS) * g


def _rms_proj_kernel(x_ref, g_ref, w_ref, o_ref):
    o_ref[...] = _dot(_rms(x_ref[...], g_ref[...]).astype(BF16), w_ref[...])


def _tile_map(nseq, nt):
    if nseq == 0:
        return lambda r, *_: (r, 0)
    return lambda r, *_: (r % nt, r // nt)


def rms_proj(x, g, w, *, layer, nseq=0):
    n = x.shape[0]
    tm = min(1024, n // max(nseq, 1))
    nt = n // tm // max(nseq, 1)
    out_shape = (n, PROJ_WIDTH) if nseq == 0 else (n // nseq, nseq * PROJ_WIDTH)
    return pl.pallas_call(
        _rms_proj_kernel,
        grid=(n // tm,),
        in_specs=[pl.BlockSpec((tm, D_MODEL), lambda i: (i, 0)),
                  pl.BlockSpec((1, D_MODEL), lambda i: (0, 0)),
                  pl.BlockSpec((None, D_MODEL, PROJ_WIDTH), lambda i: (layer, 0, 0))],
        out_specs=pl.BlockSpec((tm, PROJ_WIDTH), _tile_map(nseq, nt)),
        out_shape=jax.ShapeDtypeStruct(out_shape, F32),
        compiler_params=_params(("parallel",)),
        name="rms_proj",
    )(x, g, w)


def _row_spec(tc, col, nchunks):
    return pl.BlockSpec((tc * SEQ_BLK, GROUP_WIDTH), lambda s, c: (s * nchunks + c, col))


def _full_spec(shape):
    nd = len(shape)
    return pl.BlockSpec(shape, lambda s, c: (0,) * nd)


def _seq_spec(shape):
    nd = len(shape)
    return pl.BlockSpec((SEQ_BLK,) + shape[1:], lambda s, c: (s,) + (0,) * (nd - 1))


def _head_sums(x):
    lane = lax.broadcasted_iota(jnp.int32, x.shape, 1)
    out = jnp.zeros_like(x)
    for h in range(GROUP_WIDTH // HEAD):
        m = (lane >= h * HEAD) & (lane < (h + 1) * HEAD)
        s = jnp.sum(jnp.where(m, x, 0.0), axis=1, keepdims=True)
        out = jnp.where(m, s, out)
    return out


UNITS = SEQ_BLK * 2


def _unit_masks():
    r, c = _iota((2 * LANES, LANES), 0), _iota((2 * LANES, LANES), 1)
    ones2 = jnp.where(_head_of(r & (LANES - 1)) == _head_of(c), 1.0, 0.0).astype(BF16)
    shape = (UNITS * HEAD, LANES)
    eye = (_iota(shape, 1) & (HEAD - 1)) == (_iota(shape, 0) & (HEAD - 1))
    return ones2, eye


def _seg_sum_mxu(p, ones2):
    hi = p.astype(BF16)
    lo = (p - hi.astype(F32)).astype(BF16)
    return _dot(jnp.concatenate([hi, lo], axis=1), ones2)


def _unit_rows(ref, t):
    return jnp.concatenate([jnp.broadcast_to(ref[t, b:b + 1, p * LANES:(p + 1) * LANES], (HEAD, LANES))
                            for b in range(SEQ_BLK) for p in range(2)], axis=0)


def _load_unit_tiles(s0_ref, s_scr, *, transpose):
    for b in range(SEQ_BLK):
        for p in range(2):
            heads = [s0_ref[b, 2 * p + j] for j in range(2)]
            s_scr[b, p] = jnp.concatenate([h.T if transpose else h for h in heads], axis=1)


def _store_unit_tiles(st_ref, s_scr, *, transpose):
    for b in range(SEQ_BLK):
        for p in range(2):
            tile = s_scr[b, p]
            for j in range(2):
                h = tile[:, j * HEAD:(j + 1) * HEAD]
                st_ref[b, 2 * p + j] = h.T if transpose else h


def _store_unit_rows(ref, t, cols, eye):
    picked = jnp.where(eye, cols, 0.0)
    for b in range(SEQ_BLK):
        for p in range(2):
            u = b * 2 + p
            ref[t, b:b + 1, p * LANES:(p + 1) * LANES] = jnp.sum(picked[u * HEAD:(u + 1) * HEAD], axis=0, keepdims=True)


def _s5_kernel(u_ref, h0r_ref, h0i_ref, lr_ref, li_ref, ldt_ref, bre_ref, bim_ref, ccat_ref, d_ref,
               gw_ref, gb_ref, y_ref, hr_out, hi_out, h_scr, bu_scr, *, tc):
    c = pl.program_id(1)

    @pl.when(c == 0)
    def _():
        h_scr[0] = h0r_ref[...]
        h_scr[1] = h0i_ref[...]

    lr, li = lr_ref[...], li_ref[...]
    dt = jnp.exp(ldt_ref[...])
    mag = jnp.exp(lr * dt)
    ab_re, ab_im = mag * jnp.cos(li * dt), mag * jnp.sin(li * dt)
    den = lr * lr + li * li
    zr, zi = ab_re - 1.0, ab_im
    cr = (zr * lr + zi * li) / den
    ci = (zi * lr - zr * li) / den
    bre, bim = bre_ref[...], bim_ref[...]
    bb_re = (cr * bre - ci * bim).astype(BF16)
    bb_im = (cr * bim + ci * bre).astype(BF16)

    u = u_ref[...]
    ub = u.astype(BF16)
    bu_scr[:, 0:SSM_FLAT] = _dot(ub, bb_re)
    bu_scr[:, SSM_FLAT:2 * SSM_FLAT] = _dot(ub, bb_im)

    ar = jnp.broadcast_to(ab_re, (SEQ_BLK, SSM_FLAT))
    ai = jnp.broadcast_to(ab_im, (SEQ_BLK, SSM_FLAT))

    def step(t, carry):
        hr, hi = carry
        rows = pl.ds(pl.multiple_of(t * SEQ_BLK, SEQ_BLK), SEQ_BLK)
        nhr = ar * hr - ai * hi + bu_scr[rows, 0:SSM_FLAT]
        nhi = ar * hi + ai * hr + bu_scr[rows, SSM_FLAT:2 * SSM_FLAT]
        bu_scr[rows, 0:SSM_FLAT] = nhr
        bu_scr[rows, SSM_FLAT:2 * SSM_FLAT] = nhi
        return nhr, nhi

    hr, hi = lax.fori_loop(0, tc, step, (h_scr[0], h_scr[1]))
    h_scr[0] = hr
    h_scr[1] = hi

    y = _dot_nt(bu_scr[...].astype(BF16), ccat_ref[...]) + d_ref[...] * u
    z = jax.nn.gelu(y)
    out = z * jax.nn.sigmoid(_dot(z.astype(BF16), gw_ref[...]) + gb_ref[...])
    y_ref[...] = out.astype(y_ref.dtype)

    @pl.when(c == pl.num_programs(1) - 1)
    def _():
        hr_out[...] = hr
        hi_out[...] = hi


def s5_mixer(proj, h0_re, h0_im, lam_re, lam_im, log_dt, b_re_bd, b_im_bd, c_cat, d_skip, glu_w, glu_b, *, t_len, tc):
    nseq = h0_re.shape[0] // SEQ_BLK
    nchunks = t_len // tc
    n = proj.shape[0]
    st = jax.ShapeDtypeStruct(h0_re.shape, F32)
    return pl.pallas_call(
        functools.partial(_s5_kernel, tc=tc),
        grid=(nseq, nchunks),
        in_specs=[_row_spec(tc, COL_SSM, nchunks),
                  _seq_spec(h0_re.shape), _seq_spec(h0_im.shape),
                  _full_spec((1, SSM_FLAT)), _full_spec((1, SSM_FLAT)), _full_spec((1, SSM_FLAT)),
                  _full_spec((GROUP_WIDTH, SSM_FLAT)), _full_spec((GROUP_WIDTH, SSM_FLAT)),
                  _full_spec((GROUP_WIDTH, 2 * SSM_FLAT)), _full_spec((1, GROUP_WIDTH)),
                  _full_spec((GROUP_WIDTH, GROUP_WIDTH)), _full_spec((1, GROUP_WIDTH))],
        out_specs=[_row_spec(tc, 0, nchunks), _seq_spec(h0_re.shape), _seq_spec(h0_im.shape)],
        out_shape=[jax.ShapeDtypeStruct((n, GROUP_WIDTH), BF16), st, st],
        scratch_shapes=[pltpu.VMEM((2, SEQ_BLK, SSM_FLAT), F32),
                        pltpu.VMEM((tc * SEQ_BLK, 2 * SSM_FLAT), F32)],
        compiler_params=_params(("parallel", "arbitrary")),
        name="s5_mixer",
    )(proj, h0_re, h0_im, lam_re, lam_im, log_dt, b_re_bd, b_im_bd, c_cat, d_skip, glu_w, glu_b)


def _pool_kernel(u_ref, buf_ref, w_ref, sc_ref, y_ref, nbuf_ref, ext_scr, *, tc, pos0):
    c = pl.program_id(1)

    @pl.when(c == 0)
    def _():
        ext_scr[0:POOL_BUF] = buf_ref[...]

    u = u_ref[...].reshape(tc, SEQ_BLK, GROUP_WIDTH)
    ext_scr[POOL_BUF:POOL_BUF + tc] = u
    a1 = ext_scr[...]
    a2 = a1[1:] + a1[:-1]
    a4 = a2[2:] + a2[:-2]
    a8 = a4[4:] + a4[:-4]
    a16 = a8[8:] + a8[:-8]
    sums = (a2[14:], a4[12:], a8[8:], a16)

    shape = (tc, SEQ_BLK, GROUP_WIDTH)
    pos = lax.broadcasted_iota(jnp.int32, shape, 0) + (c * tc + pos0)
    lane = lax.broadcasted_iota(jnp.int32, shape, 2)
    pooled = None
    for gi in reversed(range(len(POOL_WINDOWS))):
        win = POOL_WINDOWS[gi]
        mean = sums[gi] / jnp.minimum(pos + 1, win).astype(F32)
        pooled = mean if pooled is None else jnp.where(lane < (gi + 1) * HEAD, mean, pooled)
    pooled = (pooled - u).reshape(tc * SEQ_BLK, GROUP_WIDTH)
    y_ref[...] = (_dot(pooled.astype(BF16), w_ref[...]) * sc_ref[...]).astype(y_ref.dtype)

    nb = ext_scr[tc:tc + POOL_BUF]
    ext_scr[0:POOL_BUF] = nb

    @pl.when(c == pl.num_programs(1) - 1)
    def _():
        nbuf_ref[...] = nb


def pool_mixer(proj, buf, w_bd, scale, *, t_len, tc, pos0):
    nseq = buf.shape[1] // SEQ_BLK
    nchunks = t_len // tc
    n = proj.shape[0]
    buf_spec = pl.BlockSpec((POOL_BUF, SEQ_BLK, GROUP_WIDTH), lambda s, c: (0, s, 0))
    return pl.pallas_call(
        functools.partial(_pool_kernel, tc=tc, pos0=pos0),
        grid=(nseq, nchunks),
        in_specs=[_row_spec(tc, COL_POOL, nchunks), buf_spec,
                  _full_spec((GROUP_WIDTH, GROUP_WIDTH)), _full_spec((1, GROUP_WIDTH))],
        out_specs=[_row_spec(tc, 0, nchunks), buf_spec],
        out_shape=[jax.ShapeDtypeStruct((n, GROUP_WIDTH), BF16), jax.ShapeDtypeStruct(buf.shape, F32)],
        scratch_shapes=[pltpu.VMEM((tc + POOL_BUF, SEQ_BLK, GROUP_WIDTH), F32)],
        compiler_params=_params(("parallel", "arbitrary")),
        name="pool_mixer",
    )(proj, buf, w_bd, scale)


def _hgrn_lower_bound(logits_ref, layer):
    rows = [logits_ref[l:l + 1, :] for l in range(DEPTH)]
    m = functools.reduce(jnp.maximum, rows)
    es = [jnp.exp(r - m) for r in rows]
    tot = functools.reduce(lambda a, b: a + b, es)
    lb = jnp.zeros_like(m)
    for l in range(1, layer + 1):
        lb = lb + es[l] / tot
    return lb


def _hgrn_kernel(pq_ref, pf_ref, pi_ref, pg_ref, s0_ref, acc_ref, lbl_ref, ng_ref, y_ref, st_ref,
                 s_scr, q_scr, f_scr, k_scr, v_scr, o_scr, *, tc, layer):
    c = pl.program_id(1)
    shape3 = (tc, SEQ_BLK, GROUP_WIDTH)

    @pl.when(c == 0)
    def _():
        _load_unit_tiles(s0_ref, s_scr, transpose=True)

    lb = _hgrn_lower_bound(lbl_ref, layer)
    zf = pf_ref[...]
    f_scr[...] = (lb + (1.0 - lb) * jax.nn.sigmoid(zf)).reshape(shape3)
    k_scr[...] = ((1.0 - lb) * jax.nn.sigmoid(-zf)).reshape(shape3)
    q_scr[...] = jax.nn.silu(pq_ref[...]).reshape(shape3)
    v_scr[...] = pi_ref[...].reshape(shape3)

    ones2, eye = _unit_masks()
    s = s_scr[...].reshape(UNITS * HEAD, LANES)
    for t in range(tc):
        vcol = _seg_sum_mxu(jnp.where(eye, _unit_rows(v_scr, t), 0.0), ones2)
        s = s * _unit_rows(f_scr, t) + vcol * _unit_rows(k_scr, t)
        _store_unit_rows(o_scr, t, _seg_sum_mxu(s * _unit_rows(q_scr, t), ones2), eye)
    s_scr[...] = s.reshape(s_scr.shape)

    o = o_scr[...].reshape(tc * SEQ_BLK, GROUP_WIDTH)
    ms = _head_sums(o * o) * (1.0 / HEAD)
    out = o * lax.rsqrt(ms + HGRN_NORM_EPS) * ng_ref[...] * jax.nn.silu(pg_ref[...])
    y_ref[...] = out.astype(y_ref.dtype)

    @pl.when(c == pl.num_programs(1) - 1)
    def _():
        _store_unit_tiles(st_ref, s_scr, transpose=True)


def _layer_seq_spec(shape, layer):
    rest = len(shape) - 2
    return pl.BlockSpec((None, SEQ_BLK) + shape[2:], lambda s, c: (layer, s) + (0,) * rest)


def _state_alias(acc, in_index, out_index):
    return {} if acc is None else {in_index: out_index}


def hgrn_mixer(proj, s_all, acc, lb_logits, norm_g, *, t_len, tc, layer):
    nseq = s_all.shape[1] // SEQ_BLK
    nchunks = t_len // tc
    n = proj.shape[0]
    tile = pltpu.VMEM((tc, SEQ_BLK, GROUP_WIDTH), F32)
    return pl.pallas_call(
        functools.partial(_hgrn_kernel, tc=tc, layer=layer),
        grid=(nseq, nchunks),
        in_specs=[_row_spec(tc, COL_Q, nchunks), _row_spec(tc, COL_F, nchunks),
                  _row_spec(tc, COL_I, nchunks), _row_spec(tc, COL_G, nchunks),
                  _layer_seq_spec(s_all.shape, layer), pl.BlockSpec(memory_space=pl.ANY),
                  _full_spec((DEPTH, GROUP_WIDTH)), _full_spec((1, GROUP_WIDTH))],
        out_specs=[_row_spec(tc, 0, nchunks), _layer_seq_spec(s_all.shape, layer)],
        out_shape=[jax.ShapeDtypeStruct((n, GROUP_WIDTH), BF16), jax.ShapeDtypeStruct(s_all.shape, F32)],
        scratch_shapes=[pltpu.VMEM((SEQ_BLK, 2, HEAD, LANES), F32), tile, tile, tile, tile, tile],
        input_output_aliases=_state_alias(acc, 5, 1),
        compiler_params=_params(("parallel", "arbitrary")),
        name="hgrn_mixer",
    )(proj, proj, proj, proj, s_all, s_all if acc is None else acc, lb_logits, norm_g)


def _rwkv_kernel(pr_ref, pk_ref, pv_ref, pl_ref, sh0_ref, s0_ref, acc_ref, mu_ref, w0_ref, a0_ref, kk_ref, ka_ref,
                 rk_ref, lng_ref, lnb_ref, w2_ref, a2_ref, g2_ref, y_ref, st_ref, sh_ref,
                 s_scr, prev_scr, r_scr, w_scr, k_scr, v_scr, nkk_scr, kka_scr, o_scr, *, tc):
    c = pl.program_id(1)
    shape3 = (tc, SEQ_BLK, GROUP_WIDTH)
    gw = GROUP_WIDTH

    @pl.when(c == 0)
    def _():
        _load_unit_tiles(s0_ref, s_scr, transpose=False)
        prev_scr[...] = sh0_ref[...]

    def shifted(ref, j):
        x = ref[...].reshape(shape3)
        first = prev_scr[:, j * gw:(j + 1) * gw].reshape(1, SEQ_BLK, gw)
        prev = first if tc == 1 else jnp.concatenate([first, x[:-1]], axis=0)
        prev_scr[:, j * gw:(j + 1) * gw] = x[tc - 1]
        return (x + (prev - x) * mu_ref[:, j * gw:(j + 1) * gw]).reshape(tc * SEQ_BLK, gw)

    xr, xk, xv, xl = shifted(pr_ref, 0), shifted(pk_ref, 1), shifted(pv_ref, 2), shifted(pl_ref, 3)
    w = -jax.nn.softplus(-(w0_ref[...] + _dot(jnp.tanh(xl).astype(BF16), w2_ref[...]))) - 0.5
    decay = jnp.exp(-jnp.exp(w))
    a = jax.nn.sigmoid(a0_ref[...] + _dot(xl.astype(BF16), a2_ref[...]))
    g = _dot(jax.nn.sigmoid(xl).astype(BF16), g2_ref[...])
    kk = xk * kk_ref[...]
    kk = kk / jnp.maximum(jnp.sqrt(_head_sums(kk * kk)), 1e-12)
    k = xk * (1.0 + (a - 1.0) * ka_ref[...])

    r_scr[...] = xr.reshape(shape3)
    w_scr[...] = decay.reshape(shape3)
    k_scr[...] = k.reshape(shape3)
    v_scr[...] = xv.reshape(shape3)
    nkk_scr[...] = (-kk).reshape(shape3)
    kka_scr[...] = (kk * a).reshape(shape3)

    ones2, eye = _unit_masks()
    s = s_scr[...].reshape(UNITS * HEAD, LANES)
    for t in range(tc):
        sa = _seg_sum_mxu(s * _unit_rows(nkk_scr, t), ones2)
        vcol = _seg_sum_mxu(jnp.where(eye, _unit_rows(v_scr, t), 0.0), ones2)
        s = s * _unit_rows(w_scr, t) + sa * _unit_rows(kka_scr, t) + vcol * _unit_rows(k_scr, t)
        _store_unit_rows(o_scr, t, _seg_sum_mxu(s * _unit_rows(r_scr, t), ones2), eye)
    s_scr[...] = s.reshape(s_scr.shape)

    y = o_scr[...].reshape(tc * SEQ_BLK, gw)
    mean = _head_sums(y) * (1.0 / HEAD)
    d = y - mean
    var = _head_sums(d * d) * (1.0 / HEAD)
    yn = d * lax.rsqrt(var + RWKV_GN_EPS) * lng_ref[...] + lnb_ref[...]
    bonus = _head_sums(xr * k * rk_ref[...]) * xv
    y_ref[...] = ((yn + bonus) * g).astype(y_ref.dtype)

    @pl.when(c == pl.num_programs(1) - 1)
    def _():
        _store_unit_tiles(st_ref, s_scr, transpose=False)
        sh_ref[...] = prev_scr[...]


def rwkv_mixer(proj, shift0, s_all, acc, mu, w0, a0, k_k, k_a, r_k, ln_g, ln_b, w2p, a2p, g2p, *, t_len, tc, layer):
    nseq = s_all.shape[1] // SEQ_BLK
    nchunks = t_len // tc
    n = proj.shape[0]
    tile = pltpu.VMEM((tc, SEQ_BLK, GROUP_WIDTH), F32)
    vec = _full_spec((1, GROUP_WIDTH))
    mat = _full_spec((GROUP_WIDTH, GROUP_WIDTH))
    return pl.pallas_call(
        functools.partial(_rwkv_kernel, tc=tc),
        grid=(nseq, nchunks),
        in_specs=[_row_spec(tc, COL_R, nchunks), _row_spec(tc, COL_K, nchunks),
                  _row_spec(tc, COL_V, nchunks), _row_spec(tc, COL_LORA, nchunks),
                  _seq_spec(shift0.shape), _layer_seq_spec(s_all.shape, layer), pl.BlockSpec(memory_space=pl.ANY),
                  _full_spec((1, RWKV_PROJ)), vec, vec, vec, vec, vec, vec, vec, mat, mat, mat],
        out_specs=[_row_spec(tc, 0, nchunks), _layer_seq_spec(s_all.shape, layer), _seq_spec(shift0.shape)],
        out_shape=[jax.ShapeDtypeStruct((n, GROUP_WIDTH), BF16), jax.ShapeDtypeStruct(s_all.shape, F32),
                   jax.ShapeDtypeStruct(shift0.shape, F32)],
        scratch_shapes=[pltpu.VMEM((SEQ_BLK, 2, HEAD, LANES), F32), pltpu.VMEM((SEQ_BLK, RWKV_PROJ), F32),
                        tile, tile, tile, tile, tile, tile, tile],
        input_output_aliases=_state_alias(acc, 6, 1),
        compiler_params=_params(("parallel", "arbitrary")),
        name="rwkv_mixer",
    )(proj, proj, proj, proj, shift0, s_all, s_all if acc is None else acc, mu, w0, a0, k_k, k_a, r_k, ln_g, ln_b,
      w2p, a2p, g2p)


def _dot_nt(a, b):
    return lax.dot_general(a, b, (((1,), (1,)), ((), ())), preferred_element_type=F32)


def _dot_tn(a, b):
    return lax.dot_general(a, b, (((0,), (0,)), ((), ())), preferred_element_type=F32)


def _iota(shape, dim):
    return lax.broadcasted_iota(jnp.int32, shape, dim)


def _head_of(idx):
    return lax.shift_right_logical(idx, HEAD.bit_length() - 1)


def _cumsum_rows(x):
    n = x.shape[0]
    tri = jnp.where(_iota((n, n), 0) >= _iota((n, n), 1), 1.0, 0.0).astype(BF16)
    hi = x.astype(BF16)
    rest = x - hi.astype(F32)
    mid = rest.astype(BF16)
    lo = (rest - mid.astype(F32)).astype(BF16)
    return _dot(tri, hi) + _dot(tri, mid) + _dot(tri, lo)


def _own_head(shape, rows_per_head):
    row_h = lax.shift_right_logical(_iota(shape, 0), rows_per_head.bit_length() - 1)
    return row_h == _head_of(_iota(shape, 1))


def _head_expand(x):
    xx = jnp.concatenate([x] * (GROUP_WIDTH // HEAD), axis=0)
    return jnp.where(_own_head(xx.shape, x.shape[0]), xx, 0.0)


def _head_collapse(xx):
    n = xx.shape[0] // (GROUP_WIDTH // HEAD)
    return xx[0:n] + xx[n:2 * n] + xx[2 * n:3 * n] + xx[3 * n:4 * n]


def _block_diag_mask():
    shape = (GROUP_WIDTH, GROUP_WIDTH)
    return _head_of(_iota(shape, 0)) == _head_of(_iota(shape, 1))


def _seq_col_spec(rows, col, ncols):
    return pl.BlockSpec((rows, GROUP_WIDTH), lambda b, c: (c, b * ncols + col))


HGRN_CHUNK = 128
HGRN_SUB = 16


HGRN_SEQS_PER_STEP = 4


def _run_staged(stages):
    for _ in itertools.zip_longest(*stages):
        pass


def _hgrn_chunk_one(pq_ref, pf_ref, pi_ref, pg_ref, lbl_ref, ng_ref, y_ref, w_scr, k_scr, b_scr, v_scr, p_scr,
                    o_scr, *, layer):
    L, n = HGRN_CHUNK, HGRN_SUB
    half = n // 2
    lb = _hgrn_lower_bound(lbl_ref, layer)
    z = pf_ref[...]
    g = jnp.logaddexp(jnp.log1p(-lb) + jax.nn.log_sigmoid(z), jnp.log(lb))
    kg = (1.0 - lb) * jax.nn.sigmoid(-z)
    q = jax.nn.silu(pq_ref[...])
    v = pi_ref[...]
    bc = _cumsum_rows(g)
    k_scr[...] = kg
    b_scr[...] = bc
    v_scr[...] = v
    yield

    bd = _block_diag_mask()
    ones_bd = jnp.where(bd, 1.0, 0.0).astype(BF16)
    rid = _iota((n, GROUP_WIDTH), 0)
    rid_lo = _iota((half, GROUP_WIDTH), 0) + half
    for sb in range(L // n):
        base = sb * n
        qs, bs = q[base:base + n], bc[base:base + n]
        q_lo, b_lo = q[base + half:base + n], bc[base + half:base + n]
        for s in range(n):
            ks, bsrow = k_scr[base + s:base + s + 1, :], b_scr[base + s:base + s + 1, :]
            if s < half:
                p_scr[s * n:(s + 1) * n, :] = qs * ks * jnp.where(rid >= s, jnp.exp(bs - bsrow), 0.0)
            else:
                p_scr[s * n + half:(s + 1) * n, :] = q_lo * ks * jnp.where(rid_lo >= s, jnp.exp(b_lo - bsrow), 0.0)
        r = _dot(p_scr[...].astype(BF16), ones_bd)
        acc = jnp.zeros((n, GROUP_WIDTH), F32)
        for s in range(n):
            acc = acc + r[s * n:(s + 1) * n] * v_scr[base + s:base + s + 1, :]
        o_scr[base:base + n, :] = acc
        yield

    vb = v.astype(BF16)
    for i in range(1, L // n):
        r0 = i * n
        ref = b_scr[r0 - 1:r0, :]
        qt = q[r0:r0 + n] * jnp.exp(bc[r0:r0 + n] - ref)
        kt = jnp.concatenate([kg[:r0] * jnp.exp(ref - bc[:r0]), jnp.zeros((L - r0, GROUP_WIDTH), F32)], axis=0)
        att = _dot_nt(_head_expand(qt).astype(BF16), kt.astype(BF16))
        yield
        ox = _dot(att.astype(BF16), vb)
        o_scr[r0:r0 + n, :] += _head_collapse(jnp.where(_own_head(ox.shape, n), ox, 0.0))
        yield

    w = w_scr[...]
    o = o_scr[...] + _dot_nt((q * jnp.exp(bc)).astype(BF16), w.astype(BF16))
    b_end = b_scr[L - 1:L, :]
    upd = _dot_tn(vb, (kg * jnp.exp(b_end - bc)).astype(BF16))
    w_scr[...] = w * jnp.exp(b_end) + jnp.where(bd, upd, 0.0)
    yield

    ms = _head_sums(o * o) * (1.0 / HEAD)
    out = o * lax.rsqrt(ms + HGRN_NORM_EPS) * ng_ref[...] * jax.nn.silu(pg_ref[...])
    y_ref[...] = out.astype(y_ref.dtype)


RWKV_CHUNK = 64


def _rwkv_chunk_one(pr_ref, pk_ref, pv_ref, pl_ref, mu_ref, w0_ref, a0_ref, kk_ref, ka_ref, rk_ref, lng_ref,
                    lnb_ref, w2_ref, a2_ref, g2_ref, y_ref, w_scr, prev_scr):
    L = RWKV_CHUNK
    gw = GROUP_WIDTH
    rid = _iota((L, gw), 0)

    def shifted(ref, j):
        x = ref[...]
        prev = jnp.where(rid == 0, prev_scr[:, j * gw:(j + 1) * gw], pltpu.roll(x, 1, axis=0))
        prev_scr[:, j * gw:(j + 1) * gw] = x[L - 1:L]
        return x + (prev - x) * mu_ref[:, j * gw:(j + 1) * gw]

    xr, xk, xv, xl = shifted(pr_ref, 0), shifted(pk_ref, 1), shifted(pv_ref, 2), shifted(pl_ref, 3)
    w = -jax.nn.softplus(-(w0_ref[...] + _dot(jnp.tanh(xl).astype(BF16), w2_ref[...]))) - 0.5
    lw = -jnp.exp(w)
    a = jax.nn.sigmoid(a0_ref[...] + _dot(xl.astype(BF16), a2_ref[...]))
    g = _dot(jax.nn.sigmoid(xl).astype(BF16), g2_ref[...])
    kk = xk * kk_ref[...]
    kk = kk / jnp.maximum(jnp.sqrt(_head_sums(kk * kk)), 1e-12)
    k = xk * (1.0 + (a - 1.0) * ka_ref[...])
    beta = kk * a
    yield

    cs = _cumsum_rows(lw)
    c_end = cs[L - 1:L]
    e_neg = jnp.exp(-cs)
    e_end = jnp.exp(c_end - cs)
    ar = jnp.concatenate([_head_expand(-kk * jnp.exp(cs - lw)), _head_expand(xr * jnp.exp(cs))], axis=0).astype(BF16)
    bk = jnp.concatenate([_head_expand(beta * e_neg), _head_expand(k * e_neg)], axis=0).astype(BF16)
    vx = _head_expand(xv).astype(BF16)
    yield

    nh = 4 * L
    gmat = _dot_nt(ar, bk)
    tt = _iota((nh, nh), 0) & (L - 1)
    ss = _iota((nh, nh), 1) & (L - 1)
    strict, incl = ss < tt, ss <= tt
    nab = jnp.where(strict, gmat[0:nh, 0:nh], 0.0)
    nak = jnp.where(strict, gmat[0:nh, nh:2 * nh], 0.0).astype(BF16)
    nrb = jnp.where(incl, gmat[nh:2 * nh, 0:nh], 0.0).astype(BF16)
    nrk = jnp.where(incl, gmat[nh:2 * nh, nh:2 * nh], 0.0).astype(BF16)
    yield

    ri, ci = _iota((nh, nh), 0), _iota((nh, nh), 1)

    def same_block(size):
        sh = size.bit_length() - 1
        return lax.shift_right_logical(ri, sh) == lax.shift_right_logical(ci, sh)

    base = 8
    m = jnp.where(same_block(base), nab, 0.0)
    t_inv = jnp.where(ri == ci, 1.0, 0.0) + m
    m = m.astype(BF16)
    for _ in range(base.bit_length() - 2):
        m = _dot(m, m).astype(BF16)
        yield
        t_inv = t_inv + _dot(t_inv.astype(BF16), m)
        yield
    size = base
    while size < L:
        off = jnp.where(same_block(2 * size), jnp.where(same_block(size), 0.0, nab), 0.0).astype(BF16)
        tb = t_inv.astype(BF16)
        half = _dot(tb, off).astype(BF16)
        yield
        t_inv = t_inv + _dot(half, tb)
        yield
        size *= 2

    wst = w_scr[...]
    sw = _dot_nt(ar, wst.astype(BF16))
    rhs = (sw[0:nh] + _dot(nak, vx)).astype(BF16)
    yield
    x = _dot(t_inv.astype(BF16), rhs)
    yield
    ux = x.astype(BF16)
    yx = sw[nh:2 * nh] + _dot(nrb, ux) + _dot(nrk, vx)
    y = _head_collapse(yx)
    u = _head_collapse(x)
    yield

    upd = _dot_tn(jnp.concatenate([u, xv], axis=0).astype(BF16),
                  jnp.concatenate([beta * e_end, k * e_end], axis=0).astype(BF16))
    w_scr[...] = wst * jnp.exp(c_end) + jnp.where(_block_diag_mask(), upd, 0.0)
    yield

    mean = _head_sums(y) * (1.0 / HEAD)
    d = y - mean
    var = _head_sums(d * d) * (1.0 / HEAD)
    yn = d * lax.rsqrt(var + RWKV_GN_EPS) * lng_ref[...] + lnb_ref[...]
    bonus = _head_sums(xr * k * rk_ref[...]) * xv
    y_ref[...] = ((yn + bonus) * g).astype(y_ref.dtype)


RWKV_SEQS_PER_STEP = 8


def _hgrn_rwkv_chunk_kernel(*refs, layer, rwkv_steps, hgrn_steps):
    nr, nh = RWKV_SEQS_PER_STEP, HGRN_SEQS_PER_STEP
    it = iter(refs)
    take = lambda n: [next(it) for _ in range(n)]
    r_seq, r_par, h_seq = take(4 * nr), take(11), take(4 * nh)
    lbl_ref, ng_ref = take(2)
    yr_ref, str_ref, shr_ref, yh_ref, sth_ref = take(5)
    wr_scr, prev_scr, wh_scr, k_scr, b_scr, v_scr, p_scr, o_scr = take(8)
    gw = GROUP_WIDTH
    g = pl.program_id(0)
    cr, ch = g % rwkv_steps, g % hgrn_steps

    @pl.when(cr == 0)
    def _():
        wr_scr[...] = jnp.zeros_like(wr_scr)
        prev_scr[...] = jnp.zeros_like(prev_scr)

    @pl.when(ch == 0)
    def _():
        wh_scr[...] = jnp.zeros_like(wh_scr)
        p_scr[...] = jnp.zeros_like(p_scr)

    rwkv = [_rwkv_chunk_one(*r_seq[4 * s:4 * s + 4], *r_par, yr_ref.at[:, s * gw:(s + 1) * gw], wr_scr.at[s],
                            prev_scr.at[s]) for s in range(nr)]
    hgrn = [_hgrn_chunk_one(*h_seq[4 * s:4 * s + 4], lbl_ref, ng_ref, yh_ref.at[:, s * gw:(s + 1) * gw],
                            wh_scr.at[s], k_scr.at[s], b_scr.at[s], v_scr.at[s], p_scr.at[s], o_scr.at[s],
                            layer=layer) for s in range(nh)]
    order = []
    for s in range(max(nr, nh)):
        order += rwkv[s:s + 1] + hgrn[s:s + 1]
    _run_staged(order)

    def head_blocks(w):
        return [w[h * HEAD:(h + 1) * HEAD, h * HEAD:(h + 1) * HEAD] for h in range(gw // HEAD)]

    @pl.when(cr == rwkv_steps - 1)
    def _():
        for s in range(nr):
            for h, blk in enumerate(head_blocks(wr_scr[s])):
                str_ref[s, h] = blk
        shr_ref[...] = prev_scr[...]

    @pl.when(ch == hgrn_steps - 1)
    def _():
        for s in range(nh):
            for h, blk in enumerate(head_blocks(wh_scr[s])):
                sth_ref[s, h] = blk.T


def hgrn_rwkv_chunk_mixer(proj2, lb_logits, norm_g, mu, w0, a0, k_k, k_a, r_k, ln_g, ln_b, w2p, a2p, g2p, *,
                          bsz, t_len, layer):
    nr, nh, lr, lh, n = RWKV_SEQS_PER_STEP, HGRN_SEQS_PER_STEP, RWKV_CHUNK, HGRN_CHUNK, HGRN_SUB
    gw = GROUP_WIDTH
    ncols = PROJ_WIDTH // gw
    heads = (gw // HEAD, HEAD, HEAD)
    rwkv_steps, hgrn_steps = t_len // lr, t_len // lh
    steps = (bsz // nr) * rwkv_steps
    assert (bsz // nh) * hgrn_steps == steps
    const = lambda shape: pl.BlockSpec(shape, lambda g: (0,) * len(shape))
    r_specs = [pl.BlockSpec((lr, gw), functools.partial(
                   lambda g, s, col: (g % rwkv_steps, ((g // rwkv_steps) * nr + s) * ncols + col), s=s, col=col))
               for s in range(nr) for col in (COL_R, COL_K, COL_V, COL_LORA)]
    h_specs = [pl.BlockSpec((lh, gw), functools.partial(
                   lambda g, s, col: (g % hgrn_steps, ((g // hgrn_steps) * nh + s) * ncols + col), s=s, col=col))
               for s in range(nh) for col in (COL_Q, COL_F, COL_I, COL_G)]
    vec, mat = const((1, gw)), const((gw, gw))
    tile = pltpu.VMEM((nh, lh, gw), F32)
    yr, st_r, sh_r, yh, st_h = pl.pallas_call(
        functools.partial(_hgrn_rwkv_chunk_kernel, layer=layer, rwkv_steps=rwkv_steps, hgrn_steps=hgrn_steps),
        grid=(steps,),
        in_specs=r_specs + [const((1, RWKV_PROJ)), vec, vec, vec, vec, vec, vec, vec, mat, mat, mat]
                 + h_specs + [const((DEPTH, gw)), vec],
        out_specs=[pl.BlockSpec((lr, nr * gw), lambda g: (g % rwkv_steps, g // rwkv_steps)),
                   pl.BlockSpec((nr,) + heads, lambda g: (g // rwkv_steps, 0, 0, 0)),
                   pl.BlockSpec((nr, 1, RWKV_PROJ), lambda g: (g // rwkv_steps, 0, 0)),
                   pl.BlockSpec((lh, nh * gw), lambda g: (g % hgrn_steps, g // hgrn_steps)),
                   pl.BlockSpec((nh,) + heads, lambda g: (g // hgrn_steps, 0, 0, 0))],
        out_shape=[jax.ShapeDtypeStruct((t_len, bsz * gw), BF16), jax.ShapeDtypeStruct((bsz,) + heads, F32),
                   jax.ShapeDtypeStruct((bsz, 1, RWKV_PROJ), F32),
                   jax.ShapeDtypeStruct((t_len, bsz * gw), BF16), jax.ShapeDtypeStruct((bsz,) + heads, F32)],
        scratch_shapes=[pltpu.VMEM((nr, gw, gw), F32), pltpu.VMEM((nr, 1, RWKV_PROJ), F32),
                        pltpu.VMEM((nh, gw, gw), F32), tile, tile, tile, pltpu.VMEM((nh, n * n, gw), F32), tile],
        compiler_params=_params(("arbitrary",)),
        name="hgrn_rwkv_chunk_mixer",
    )(*([proj2] * (4 * nr)), mu, w0, a0, k_k, k_a, r_k, ln_g, ln_b, w2p, a2p, g2p,
      *([proj2] * (4 * nh)), lb_logits, norm_g)
    return yh, st_h, yr, st_r, sh_r


S5_CHUNK = 64


def _s5_seq_kernel(*refs, nseq):
    u_refs = refs[:nseq]
    lr_ref, li_ref, ldt_ref, bre_ref, bim_ref, ccat_ref, d_ref, gw_ref, gb_ref = refs[nseq:nseq + 9]
    y_ref, hr_out, hi_out, h_scr, bu_scr, bb_scr, ab_scr, perm_scr, u_scr = refs[nseq + 9:]
    tc = S5_CHUNK
    n = nseq * tc
    c = pl.program_id(0)

    @pl.when(c == 0)
    def _():
        h_scr[...] = jnp.zeros_like(h_scr)
        lr, li = lr_ref[...], li_ref[...]
        dt = jnp.exp(ldt_ref[...])
        mag = jnp.exp(lr * dt)
        ab_re, ab_im = mag * jnp.cos(li * dt), mag * jnp.sin(li * dt)
        den = lr * lr + li * li
        zr, zi = ab_re - 1.0, ab_im
        cr = (zr * lr + zi * li) / den
        ci = (zi * lr - zr * li) / den
        bre, bim = bre_ref[...], bim_ref[...]
        bb_scr[:, 0:SSM_FLAT] = (cr * bre - ci * bim).astype(BF16)
        bb_scr[:, SSM_FLAT:2 * SSM_FLAT] = (cr * bim + ci * bre).astype(BF16)
        ab_scr[0] = jnp.broadcast_to(ab_re, (nseq, SSM_FLAT))
        ab_scr[1] = jnp.broadcast_to(ab_im, (nseq, SSM_FLAT))
        ri, cj = _iota((n, n), 0), _iota((n, n), 1)
        lseq, lt = nseq.bit_length() - 1, tc.bit_length() - 1
        perm_scr[0] = jnp.where(cj == (ri & (nseq - 1)) * tc + lax.shift_right_logical(ri, lseq), 1.0, 0.0).astype(BF16)
        perm_scr[1] = jnp.where(cj == (ri & (tc - 1)) * nseq + lax.shift_right_logical(ri, lt), 1.0, 0.0).astype(BF16)

        bu_scr[1] = jnp.zeros(bu_scr.shape[1:], F32)
        u_scr[1] = jnp.zeros(u_scr.shape[1:], F32)

    last = pl.num_programs(0) - 1

    def pipeline(slot):

        def scan_chunk():
            u = jnp.concatenate([r[...] for r in u_refs], axis=0)
            u_scr[slot] = u
            u_t = _dot(perm_scr[0], u.astype(BF16)).astype(BF16)
            bu_scr[slot] = _dot(u_t, bb_scr[...])
            yield
            ar, ai = ab_scr[0], ab_scr[1]
            hr0, hi0 = h_scr[0], h_scr[1]
            hr, hi = hr0, hi0
            for t in range(tc):
                rows = slice(t * nseq, (t + 1) * nseq)
                hr, hi = (ar * hr - ai * hi + bu_scr[slot, rows, 0:SSM_FLAT],
                          ar * hi + ai * hr + bu_scr[slot, rows, SSM_FLAT:2 * SSM_FLAT])
                bu_scr[slot, rows, 0:SSM_FLAT] = hr
                bu_scr[slot, rows, SSM_FLAT:2 * SSM_FLAT] = hi
                if t % 8 == 7:
                    yield
            h_scr[0] = jnp.where(c < last, hr, hr0)
            h_scr[1] = jnp.where(c < last, hi, hi0)

        def emit_chunk():
            prev = 1 - slot
            y_t = _dot_nt(bu_scr[prev].astype(BF16), ccat_ref[...])
            yield
            hi_p = y_t.astype(BF16)
            rest = y_t - hi_p.astype(F32)
            mid_p = rest.astype(BF16)
            lo_p = (rest - mid_p.astype(F32)).astype(BF16)
            to_seq = perm_scr[1]
            y = _dot(to_seq, hi_p) + _dot(to_seq, mid_p) + _dot(to_seq, lo_p) + d_ref[...] * u_scr[prev]
            yield
            z = jax.nn.gelu(y)
            gate = _dot(z.astype(BF16), gw_ref[...])
            yield
            out = z * jax.nn.sigmoid(gate + gb_ref[...])
            for b in range(nseq):
                y_ref[:, b * GROUP_WIDTH:(b + 1) * GROUP_WIDTH] = out[b * tc:(b + 1) * tc].astype(y_ref.dtype)

        _run_staged([scan_chunk(), emit_chunk()])

    for parity in range(2):
        pl.when(c % 2 == parity)(functools.partial(pipeline, parity))

    @pl.when(c == last)
    def _():
        hr_out[...] = h_scr[0]
        hi_out[...] = h_scr[1]


def s5_seq_mixer(proj2, lam_re, lam_im, log_dt, b_re_bd, b_im_bd, c_cat, d_skip, glu_w, glu_b, *, bsz, t_len):
    tc = S5_CHUNK
    ncols = PROJ_WIDTH // GROUP_WIDTH
    full = lambda shape: pl.BlockSpec(shape, lambda c: (0,) * len(shape))
    st = jax.ShapeDtypeStruct((bsz, SSM_FLAT), F32)
    nchunks = t_len // tc
    u_specs = [pl.BlockSpec((tc, GROUP_WIDTH), functools.partial(
                   lambda c, b: (jnp.minimum(c, nchunks - 1), b * ncols + COL_SSM), b=b)) for b in range(bsz)]
    return pl.pallas_call(
        functools.partial(_s5_seq_kernel, nseq=bsz),
        grid=(nchunks + 1,),
        in_specs=u_specs + [full((1, SSM_FLAT)), full((1, SSM_FLAT)), full((1, SSM_FLAT)),
                            full((GROUP_WIDTH, SSM_FLAT)), full((GROUP_WIDTH, SSM_FLAT)),
                            full((GROUP_WIDTH, 2 * SSM_FLAT)), full((1, GROUP_WIDTH)),
                            full((GROUP_WIDTH, GROUP_WIDTH)), full((1, GROUP_WIDTH))],
        out_specs=[pl.BlockSpec((tc, bsz * GROUP_WIDTH), lambda c: (jnp.maximum(c - 1, 0), 0)),
                   full((bsz, SSM_FLAT)), full((bsz, SSM_FLAT))],
        out_shape=[jax.ShapeDtypeStruct((t_len, bsz * GROUP_WIDTH), BF16), st, st],
        scratch_shapes=[pltpu.VMEM((2, bsz, SSM_FLAT), F32), pltpu.VMEM((2, bsz * tc, 2 * SSM_FLAT), F32),
                        pltpu.VMEM((GROUP_WIDTH, 2 * SSM_FLAT), BF16), pltpu.VMEM((2, bsz, SSM_FLAT), F32),
                        pltpu.VMEM((2, bsz * tc, bsz * tc), BF16), pltpu.VMEM((2, bsz * tc, GROUP_WIDTH), F32)],
        compiler_params=_params(("arbitrary",)),
        name="s5_seq_mixer",
    )(*([proj2] * bsz), lam_re, lam_im, log_dt, b_re_bd, b_im_bd, c_cat, d_skip, glu_w, glu_b)


POOL_CHUNK = 1024
POOL_HIST = 16


def _pool_seq_kernel(u_ref, w_ref, sc_ref, y_ref, nbuf_ref, ext_scr):
    L, hist = u_ref.shape[0], POOL_HIST
    c = pl.program_id(1)

    @pl.when(c == 0)
    def _():
        ext_scr[0:hist] = jnp.zeros((hist, GROUP_WIDTH), F32)

    u = u_ref[...]
    ext_scr[hist:hist + L] = u
    e = ext_scr[...]
    a2 = e + pltpu.roll(e, 1, axis=0)
    a4 = a2 + pltpu.roll(a2, 2, axis=0)
    a8 = a4 + pltpu.roll(a4, 4, axis=0)
    a16 = a8 + pltpu.roll(a8, 8, axis=0)
    sums = (a2[hist:], a4[hist:], a8[hist:], a16[hist:])

    shape = (L, GROUP_WIDTH)
    pos = _iota(shape, 0) + c * L
    lane = _iota(shape, 1)
    pooled = None
    for gi in reversed(range(len(POOL_WINDOWS))):
        win = POOL_WINDOWS[gi]
        mean = sums[gi] / jnp.minimum(pos + 1, win).astype(F32)
        pooled = mean if pooled is None else jnp.where(lane < (gi + 1) * HEAD, mean, pooled)
    y_ref[...] = (_dot((pooled - u).astype(BF16), w_ref[...]) * sc_ref[...]).astype(y_ref.dtype)

    nb = ext_scr[L:L + hist]
    ext_scr[0:hist] = nb

    @pl.when(c == pl.num_programs(1) - 1)
    def _():
        nbuf_ref[...] = nb


def pool_seq_mixer(proj2, w_bd, scale, *, bsz, t_len):
    L, hist = min(POOL_CHUNK, t_len), POOL_HIST
    ncols = PROJ_WIDTH // GROUP_WIDTH
    return pl.pallas_call(
        _pool_seq_kernel,
        grid=(bsz, t_len // L),
        in_specs=[_seq_col_spec(L, COL_POOL, ncols),
                  pl.BlockSpec((GROUP_WIDTH, GROUP_WIDTH), lambda b, c: (0, 0)),
                  pl.BlockSpec((1, GROUP_WIDTH), lambda b, c: (0, 0))],
        out_specs=[_seq_col_spec(L, 0, 1), pl.BlockSpec((hist, GROUP_WIDTH), lambda b, c: (b, 0))],
        out_shape=[jax.ShapeDtypeStruct((t_len, bsz * GROUP_WIDTH), BF16),
                   jax.ShapeDtypeStruct((bsz * hist, GROUP_WIDTH), F32)],
        scratch_shapes=[pltpu.VMEM((hist + L, GROUP_WIDTH), F32)],
        compiler_params=_params(("parallel", "arbitrary")),
        name="pool_seq_mixer",
    )(proj2, w_bd, scale)


def _mix_mlp_kernel(h_ref, ya_ref, yb_ref, yc_ref, yd_ref, wo_ref, g2_ref, wu_ref, wd_ref, gf_ref, o_ref,
                    h1_scr, xn_scr, acc_scr, *, final_norm):
    j = pl.program_id(1)
    gw = GROUP_WIDTH

    @pl.when(j == 0)
    def _():
        mix = (_dot(ya_ref[...], wo_ref[0:gw]) + _dot(yb_ref[...], wo_ref[gw:2 * gw])
               + _dot(yc_ref[...], wo_ref[2 * gw:3 * gw]) + _dot(yd_ref[...], wo_ref[3 * gw:4 * gw]))
        h1 = h_ref[...] + mix
        h1_scr[...] = h1
        xn_scr[...] = _rms(h1, g2_ref[...]).astype(BF16)
        acc_scr[...] = jnp.zeros_like(acc_scr)

    up = _dot(xn_scr[...], wu_ref[...])
    act = jnp.square(jnp.maximum(up, 0.0)).astype(BF16)
    acc_scr[...] += _dot(act, wd_ref[...])

    @pl.when(j == pl.num_programs(1) - 1)
    def _():
        out = h1_scr[...] + acc_scr[...]
        if final_norm:
            out = _rms(out, gf_ref[...])
        o_ref[...] = out


def mix_mlp(h, ys, w_out, g2, w_up, w_down, g_final, *, layer, final_norm, nseq=0):
    n = h.shape[0]
    tm = min(512, n // max(nseq, 1))
    tf = 2048
    nt = n // tm // max(nseq, 1)
    row = lambda w: pl.BlockSpec((tm, w), lambda i, j: (i, 0))
    mix = pl.BlockSpec((tm, GROUP_WIDTH), _tile_map(nseq, nt))
    return pl.pallas_call(
        functools.partial(_mix_mlp_kernel, final_norm=final_norm),
        grid=(n // tm, D_FF // tf),
        in_specs=[row(D_MODEL), mix, mix, mix, mix,
                  pl.BlockSpec((None, D_MODEL, D_MODEL), lambda i, j: (layer, 0, 0)),
                  pl.BlockSpec((1, D_MODEL), lambda i, j: (0, 0)),
                  pl.BlockSpec((None, D_MODEL, tf), lambda i, j: (layer, 0, j)),
                  pl.BlockSpec((None, tf, D_MODEL), lambda i, j: (layer, j, 0)),
                  pl.BlockSpec((1, D_MODEL), lambda i, j: (0, 0))],
        out_specs=row(D_MODEL),
        out_shape=jax.ShapeDtypeStruct((n, D_MODEL), F32),
        scratch_shapes=[pltpu.VMEM((tm, D_MODEL), F32), pltpu.VMEM((tm, D_MODEL), BF16),
                        pltpu.VMEM((tm, D_MODEL), F32)],
        compiler_params=_params(("parallel", "arbitrary")),
        name="mix_mlp",
    )(h, *ys, w_out, g2, w_up, w_down, g_final)


def _block_diag(blocks):
    g, r, c = blocks.shape
    tiled = jnp.tile(blocks.reshape(g * r, c), (1, g))
    row_blk = lax.broadcasted_iota(jnp.int32, tiled.shape, 0) // r
    col_blk = lax.broadcasted_iota(jnp.int32, tiled.shape, 1) // c
    return jnp.where(row_blk == col_blk, tiled, 0.0)


def _pad_rows(w, start):
    rows = GROUP_WIDTH - start - w.shape[0]
    return jnp.concatenate([jnp.zeros((start, GROUP_WIDTH), w.dtype), w, jnp.zeros((rows, GROUP_WIDTH), w.dtype)])


def _layer_params(l, P):
    row = lambda a: a.reshape(1, -1)
    q = {}
    q["norm1_g"] = row(P["norm1_g"][l])
    q["lam_re"] = row(P["ssm_lambda_re"][l])
    q["lam_im"] = row(P["ssm_lambda_im"][l])
    q["log_dt"] = row(jnp.repeat(P["ssm_log_dt"][l], SSM_STATE))
    q["b_re"] = _block_diag(P["ssm_b_re"][l].transpose(0, 2, 1))
    q["b_im"] = _block_diag(P["ssm_b_im"][l].transpose(0, 2, 1))
    q["c_cat"] = jnp.concatenate([_block_diag(P["ssm_c_re"][l]), -_block_diag(P["ssm_c_im"][l])],
                                 axis=1).astype(BF16)
    q["ssm_d"] = row(P["ssm_d"][l])
    q["glu_w"] = P["ssm_glu_w"][l].astype(BF16)
    q["glu_b"] = row(P["ssm_glu_b"][l])
    q["hgrn_norm_g"] = row(P["hgrn_norm_g"][l])
    q["mu"] = row(P["rwkv_mu"][l])
    for name in ("w0", "a0", "k_k", "k_a", "r_k", "ln_g", "ln_b"):
        q[name] = row(P["rwkv_" + name][l])
    q["w2p"] = _pad_rows(P["rwkv_w2"][l], 0).astype(BF16)
    q["a2p"] = _pad_rows(P["rwkv_a2"][l], DECAY_LORA).astype(BF16)
    q["g2p"] = _pad_rows(P["rwkv_g2"][l], DECAY_LORA + AAA_LORA).astype(BF16)
    q["pool_w"] = _block_diag(P["pool_w"][l]).astype(BF16)
    q["pool_scale"] = row(P["pool_scale"][l])
    q["norm2_g"] = row(P["norm2_g"][l])
    return q


def _trunk_fresh(x_rows, bsz, t_len, layer_params, P):
    h = x_rows
    new = [[] for _ in range(6)]
    g_final = P["norm_f_g"].reshape(1, -1)
    for l in range(DEPTH):
        q = layer_params[l]
        proj2 = rms_proj(h, q["norm1_g"], P["w_in"], layer=l, nseq=bsz)
        y_a, s_re, s_im = s5_seq_mixer(proj2, q["lam_re"], q["lam_im"], q["log_dt"], q["b_re"], q["b_im"],
                                       q["c_cat"], q["ssm_d"], q["glu_w"], q["glu_b"], bsz=bsz, t_len=t_len)
        y_b, s_hg, y_c, s_wkv, s_sh = hgrn_rwkv_chunk_mixer(
            proj2, P["hgrn_lb_logits"], q["hgrn_norm_g"], q["mu"], q["w0"], q["a0"], q["k_k"], q["k_a"], q["r_k"],
            q["ln_g"], q["ln_b"], q["w2p"], q["a2p"], q["g2p"], bsz=bsz, t_len=t_len, layer=l)
        y_d, s_pool = pool_seq_mixer(proj2, q["pool_w"], q["pool_scale"], bsz=bsz, t_len=t_len)
        h = mix_mlp(h, (y_a, y_b, y_c, y_d), P["w_out"], q["norm2_g"], P["mlp_up"], P["mlp_down"], g_final,
                    layer=l, final_norm=(l == DEPTH - 1), nseq=bsz)
        s_pool = s_pool.reshape(bsz, POOL_HIST, GROUP_WIDTH)[:, POOL_HIST - POOL_BUF:]
        for lst, s in zip(new, (s_re, s_im, s_hg, s_wkv, s_sh, s_pool)):
            lst.append(s)
    return h, new


def _trunk_carry(x_rows, states, pos0, t_len, layer_params, P):
    ssm_re0, ssm_im0, hgrn0, wkv0, shift0, pool0 = states
    h = x_rows
    new = [[] for _ in range(6)]
    g_final = P["norm_f_g"].reshape(1, -1)
    tc = t_len
    s_hg = s_wkv = None
    for l in range(DEPTH):
        q = layer_params[l]
        proj = rms_proj(h, q["norm1_g"], P["w_in"], layer=l)
        y_a, s_re, s_im = s5_mixer(proj, ssm_re0[l], ssm_im0[l], q["lam_re"], q["lam_im"], q["log_dt"], q["b_re"],
                                   q["b_im"], q["c_cat"], q["ssm_d"], q["glu_w"], q["glu_b"], t_len=t_len, tc=tc)
        y_b, s_hg = hgrn_mixer(proj, hgrn0, s_hg, P["hgrn_lb_logits"], q["hgrn_norm_g"], t_len=t_len, tc=tc,
                               layer=l)
        y_c, s_wkv, s_sh = rwkv_mixer(proj, shift0[l], wkv0, s_wkv, q["mu"], q["w0"], q["a0"], q["k_k"], q["k_a"],
                                      q["r_k"], q["ln_g"], q["ln_b"], q["w2p"], q["a2p"], q["g2p"],
                                      t_len=t_len, tc=tc, layer=l)
        y_d, s_pool = pool_mixer(proj, pool0[l], q["pool_w"], q["pool_scale"], t_len=t_len, tc=tc, pos0=pos0)
        h = mix_mlp(h, (y_a, y_b, y_c, y_d), P["w_out"], q["norm2_g"], P["mlp_up"], P["mlp_down"], g_final,
                    layer=l, final_norm=(l == DEPTH - 1))
        for i, s in ((0, s_re), (1, s_im), (4, s_sh), (5, s_pool.transpose(1, 0, 2))):
            new[i].append(s)
    new[2], new[3] = s_hg, s_wkv
    return h, new


def _states_out(new, bsz):
    s_re, s_im, s_hg, s_wkv, s_sh, s_pool = new
    return (jnp.stack([s.reshape(bsz, SSM_GROUPS, SSM_STATE) for s in s_re]),
            jnp.stack([s.reshape(bsz, SSM_GROUPS, SSM_STATE) for s in s_im]),
            jnp.stack(s_hg) if isinstance(s_hg, list) else s_hg,
            jnp.stack(s_wkv) if isinstance(s_wkv, list) else s_wkv,
            jnp.stack([s.reshape(bsz, 1, RWKV_PROJ) for s in s_sh]),
            jnp.stack(s_pool))


def kernel(x_prompt, x_sample, state_ssm_re, state_ssm_im, state_hgrn, state_wkv, state_shift, state_pool, norm1_g, w_in, ssm_lambda_re, ssm_lambda_im, ssm_log_dt, ssm_b_re, ssm_b_im, ssm_c_re, ssm_c_im, ssm_d, ssm_glu_w, ssm_glu_b, hgrn_lb_logits, hgrn_norm_g, rwkv_mu, rwkv_w0, rwkv_w2, rwkv_a0, rwkv_a2, rwkv_g2, rwkv_k_k, rwkv_k_a, rwkv_r_k, rwkv_ln_g, rwkv_ln_b, pool_w, pool_scale, w_out, norm2_g, mlp_up, mlp_down, norm_f_g):
    P = dict(norm1_g=norm1_g, w_in=w_in, ssm_lambda_re=ssm_lambda_re, ssm_lambda_im=ssm_lambda_im,
             ssm_log_dt=ssm_log_dt, ssm_b_re=ssm_b_re, ssm_b_im=ssm_b_im, ssm_c_re=ssm_c_re, ssm_c_im=ssm_c_im,
             ssm_d=ssm_d, ssm_glu_w=ssm_glu_w, ssm_glu_b=ssm_glu_b, hgrn_lb_logits=hgrn_lb_logits,
             hgrn_norm_g=hgrn_norm_g, rwkv_mu=rwkv_mu, rwkv_w0=rwkv_w0, rwkv_w2=rwkv_w2, rwkv_a0=rwkv_a0,
             rwkv_a2=rwkv_a2, rwkv_g2=rwkv_g2, rwkv_k_k=rwkv_k_k, rwkv_k_a=rwkv_k_a, rwkv_r_k=rwkv_r_k,
             rwkv_ln_g=rwkv_ln_g, rwkv_ln_b=rwkv_ln_b, pool_w=pool_w, pool_scale=pool_scale, w_out=w_out,
             norm2_g=norm2_g, mlp_up=mlp_up, mlp_down=mlp_down, norm_f_g=norm_f_g)
    layer_params = [_layer_params(l, P) for l in range(DEPTH)]
    for name in ("w_in", "w_out", "mlp_up", "mlp_down"):
        P[name] = P[name].astype(BF16)

    bp, t_p, _ = x_prompt.shape
    yp, new_p = _trunk_fresh(x_prompt.reshape(bp * t_p, D_MODEL), bp, t_p, layer_params, P)
    y_prompt = yp.reshape(bp, t_p, D_MODEL)

    bs, t_s, _ = x_sample.shape
    nblk = bs // SEQ_BLK
    xs = x_sample.reshape(nblk, SEQ_BLK, t_s, D_MODEL).transpose(0, 2, 1, 3).reshape(bs * t_s, D_MODEL)
    st_s = ([state_ssm_re[l].reshape(bs, SSM_FLAT) for l in range(DEPTH)],
            [state_ssm_im[l].reshape(bs, SSM_FLAT) for l in range(DEPTH)],
            state_hgrn, state_wkv,
            [state_shift[l].reshape(bs, RWKV_PROJ) for l in range(DEPTH)],
            [state_pool[l].transpose(1, 0, 2) for l in range(DEPTH)])
    ys, new_s = _trunk_carry(xs, st_s, PAST_LEN, t_s, layer_params, P)
    y_sample = ys.reshape(nblk, t_s, SEQ_BLK, D_MODEL).transpose(0, 2, 1, 3).reshape(bs, t_s, D_MODEL)

    return (y_prompt, y_sample) + _states_out(new_p, bp) + _states_out(new_s, bs)
```
